```python
import jax, jax.numpy as jnp
from jax import lax
import numpy as np

D_MODEL = 2048
BATCH = 4
SEQ = 8192
DEPTH = 1

MIX_WIDTH = D_MODEL
NSA_HEAD_DIM = 64
NSA_HEADS = (MIX_WIDTH // 2) // NSA_HEAD_DIM
NSA_KV_HEADS = max(1, NSA_HEADS // 4)
NSA_GROUP = NSA_HEADS // NSA_KV_HEADS
NSA_WIDTH = NSA_HEADS * NSA_HEAD_DIM
NSA_KV_WIDTH = NSA_KV_HEADS * NSA_HEAD_DIM
CMP_LEN = 32
CMP_STRIDE = 16
CMP_HIDDEN = 128
SLC_BLOCK = 64
SLC_TOPK = 16
SLC_LOCAL = 2
WINDOW = 512
NSA_QBLOCK = 64
SLC_FORCE_SCORE = 1e4
NEG_INF = -1e30
RET_HEADS = 8
RET_WIDTH = MIX_WIDTH - NSA_WIDTH
RET_V_DIM = RET_WIDTH // RET_HEADS
RET_QK_DIM = RET_V_DIM
RET_CHUNK = 128
IN_SIZES = (NSA_WIDTH,) + (NSA_KV_WIDTH,) * 6 + (3 * NSA_HEADS, RET_HEADS * RET_QK_DIM, RET_HEADS * RET_QK_DIM, RET_WIDTH, RET_WIDTH)
IN_WIDTH = NSA_WIDTH + 6 * NSA_KV_WIDTH + 3 * NSA_HEADS + 2 * RET_HEADS * RET_QK_DIM + 2 * RET_WIDTH
PEER_HEADS = 8
PEER_N_KEYS = 128
PEER_EXPERTS = PEER_N_KEYS * PEER_N_KEYS
PEER_TOPK = 16
PEER_QUERY_DIM = 256
PEER_CHUNK = 128
EPS = 1e-6

kernel_name = "hymba_nsa_retention_peer_adaln"

F32 = jnp.float32


def rmsnorm(x, g):
    xf = x.astype(F32)
    y = xf * lax.rsqrt(jnp.mean(xf * xf, axis=-1, keepdims=True) + EPS)
    return (y * g.astype(F32)).astype(x.dtype)


def alibi_slopes(n):
    return jnp.exp2(-8.0 * (jnp.arange(n, dtype=F32) + 1.0) / n)


def split_cols(a, sizes):
    idx = []
    acc = 0
    for s in sizes[:-1]:
        acc += s
        idx.append(acc)
    return jnp.split(a, idx, axis=-1)


def nsa_compress(k, pe, w1, w2, idx):
    B, n_cmp, L = k.shape[0], idx.shape[0], idx.shape[1]
    blk = k[:, idx] + pe[:, None, :]
    blk = jnp.swapaxes(blk, 2, 3).reshape(B, n_cmp, NSA_KV_HEADS, L * NSA_HEAD_DIM)
    out = jax.nn.gelu(blk @ w1) @ w2
    return jnp.swapaxes(out, 1, 2)


def nsa_attention(q, kc, vc, ks, vs, kw, vw, gates, pe_k, pe_v, ck_w1, ck_w2, cv_w1, cv_w2):
    B, S = q.shape[0], q.shape[1]
    G, R, dk, QB = NSA_KV_HEADS, NSA_GROUP, NSA_HEAD_DIM, NSA_QBLOCK
    scale = dk ** -0.5
    dt = q.dtype
    n_cmp = (S - CMP_LEN) // CMP_STRIDE + 1
    cmp_idx = np.arange(n_cmp)[:, None] * CMP_STRIDE + np.arange(CMP_LEN)[None, :]
    cmp_end = jnp.asarray(cmp_idx[:, -1], jnp.int32)
    Kc = nsa_compress(kc, pe_k, ck_w1, ck_w2, cmp_idx)
    Vc = nsa_compress(vc, pe_v, cv_w1, cv_w2, cmp_idx)
    n_slc = S // SLC_BLOCK
    top_n = min(SLC_TOPK, n_slc)
    slc_start = np.arange(n_slc) * SLC_BLOCK
    overlap = jnp.asarray(((cmp_idx[:, :1] < slc_start[None, :] + SLC_BLOCK)
                           & (cmp_idx[:, -1:] >= slc_start[None, :])).astype(np.float32))
    Ks = ks.reshape(B, n_slc, SLC_BLOCK, G, dk).transpose(0, 3, 1, 2, 4)
    Vs = vs.reshape(B, n_slc, SLC_BLOCK, G, dk).transpose(0, 3, 1, 2, 4)
    Kw = jnp.pad(kw, ((0, 0), (WINDOW, 0), (0, 0), (0, 0)))
    Vw = jnp.pad(vw, ((0, 0), (WINDOW, 0), (0, 0), (0, 0)))
    slopes = alibi_slopes(NSA_HEADS).reshape(G, R)[:, :, None, None]
    nqb = S // QB
    qb = q.reshape(B, nqb, QB, G, R, dk).transpose(1, 0, 3, 4, 2, 5)
    gb = gates.reshape(B, nqb, QB, G, R, 3).transpose(1, 0, 3, 4, 2, 5)
    b_ix = jnp.arange(B)[:, None, None, None]
    g_ix = jnp.arange(G)[None, :, None, None]
    blk_ids = jnp.arange(n_slc)
    in_blk = jnp.arange(SLC_BLOCK)
    win_off = jnp.arange(WINDOW + QB)

    def query_block(args):
        i, qi, gi = args
        t = i * QB + jnp.arange(QB)
        d_c = t[:, None] - cmp_end[None, :]
        valid_c = d_c >= 0
        s_c = jnp.einsum('bgrqd,bgnd->bgrqn', qi, Kc).astype(F32) * scale - slopes * d_c.astype(F32)
        p_c = jnp.where(valid_c, jax.nn.softmax(jnp.where(valid_c, s_c, NEG_INF), axis=-1), 0.0)
        o_c = jnp.einsum('bgrqn,bgnd->bgrqd', p_c.astype(dt), Vc)
        imp = jnp.einsum('bgrqn,nm->bgqm', p_c, overlap)
        back = (t // SLC_BLOCK)[:, None] - blk_ids[None, :]
        valid_s = back >= 0
        forced = valid_s & ((blk_ids[None, :] == 0) | (back < SLC_LOCAL))
        imp = jnp.where(forced, SLC_FORCE_SCORE, jnp.where(valid_s, imp, -1.0))
        _, sel = lax.top_k(imp, top_n)
        K_sel = Ks[b_ix, g_ix, sel].reshape(B, G, QB, top_n * SLC_BLOCK, dk)
        V_sel = Vs[b_ix, g_ix, sel].reshape(B, G, QB, top_n * SLC_BLOCK, dk)
        pos_s = (sel[..., None] * SLC_BLOCK + in_blk).reshape(B, G, QB, top_n * SLC_BLOCK)
        d_s = (t[None, None, :, None] - pos_s)[:, :, None]
        s_s = jnp.einsum('bgrqd,bgqkd->bgrqk', qi, K_sel).astype(F32) * scale - slopes * d_s.astype(F32)
        p_s = jax.nn.softmax(jnp.where(d_s >= 0, s_s, NEG_INF), axis=-1)
        o_s = jnp.einsum('bgrqk,bgqkd->bgrqd', p_s.astype(dt), V_sel)
        K_w = lax.dynamic_slice_in_dim(Kw, i * QB, WINDOW + QB, axis=1)
        V_w = lax.dynamic_slice_in_dim(Vw, i * QB, WINDOW + QB, axis=1)
        pos_w = i * QB - WINDOW + win_off
        d_w = t[:, None] - pos_w[None, :]
        valid_w = (d_w >= 0) & (d_w < WINDOW) & (pos_w[None, :] >= 0)
        s_w = jnp.einsum('bgrqd,bkgd->bgrqk', qi, K_w).astype(F32) * scale - slopes * d_w.astype(F32)
        p_w = jax.nn.softmax(jnp.where(valid_w, s_w, NEG_INF), axis=-1)
        o_w = jnp.einsum('bgrqk,bkgd->bgrqd', p_w.astype(dt), V_w)
        gw = jax.nn.sigmoid(gi.astype(F32))
        o = gw[..., 0:1] * o_c + gw[..., 1:2] * o_s + gw[..., 2:3] * o_w
        return o.astype(dt)

    out = lax.map(query_block, (jnp.arange(nqb), qb, gb))
    return out.transpose(1, 0, 4, 2, 3, 5).reshape(B, S, NSA_HEADS, dk)


def retention(q, k, v, gate, g_out):
    B, S = q.shape[0], q.shape[1]
    H, dk, dv, C = RET_HEADS, RET_QK_DIM, RET_V_DIM, RET_CHUNK
    nC = S // C
    dt = q.dtype
    lg = jnp.log1p(-jnp.exp2(-5.0 - jnp.arange(H, dtype=F32)))
    pos = jnp.arange(C, dtype=F32)
    diff = pos[:, None] - pos[None, :]
    intra_decay = jnp.where(diff >= 0, jnp.exp(lg[:, None, None] * jnp.maximum(diff, 0.0)), 0.0)
    k_decay = jnp.exp(lg[:, None] * (C - 1.0 - pos))
    q_decay = jnp.exp(lg[:, None] * (pos + 1.0))
    chunk_decay = jnp.exp(lg * C)

    def to_chunks(a):
        return a.reshape(B, nC, C, H, a.shape[-1]).transpose(0, 3, 1, 2, 4).astype(F32)

    qc = to_chunks(q)
    kc = to_chunks(k) * (dk ** -0.5)
    vc = to_chunks(v)
    att = jnp.einsum('bhncd,bhnsd->bhncs', qc, kc) * intra_decay[:, None]
    o_intra = jnp.einsum('bhncs,bhnse->bhnce', att, vc)
    kv = jnp.einsum('bhncd,bhnce->nbhde', kc * k_decay[:, None, :, None], vc)

    def step(state, kv_n):
        return state * chunk_decay[None, :, None, None] + kv_n, state

    _, r_prev = lax.scan(step, jnp.zeros((B, H, dk, dv), F32), kv)
    o_cross = jnp.einsum('bhncd,nbhde->bhnce', qc * q_decay[:, None, :, None], r_prev)
    o = (o_intra + o_cross).transpose(0, 2, 3, 1, 4).reshape(B, S, H, dv)
    mu = jnp.mean(o, axis=-1, keepdims=True)
    var = jnp.mean(jnp.square(o - mu), axis=-1, keepdims=True)
    y = (o - mu) * lax.rsqrt(var + EPS) * g_out.astype(F32)
    return (jax.nn.silu(gate.astype(F32)) * y.reshape(B, S, H * dv)).astype(dt)


def peer_ffn(h, w_q, sub_keys, w_u, w_v):
    B, S, D = h.shape
    T = B * S
    K = PEER_TOPK
    dt = h.dtype
    hf = h.reshape(T, D)
    q = (hf @ w_q).reshape(T, PEER_HEADS, 2, PEER_QUERY_DIM // 2)
    s_half = jnp.einsum('thpd,hpkd->thpk', q, sub_keys).astype(F32)
    v_half, i_half = lax.top_k(s_half, K)
    cand_s = (v_half[:, :, 0, :, None] + v_half[:, :, 1, None, :]).reshape(T, PEER_HEADS, K * K)
    cand_i = (i_half[:, :, 0, :, None] * PEER_N_KEYS + i_half[:, :, 1, None, :]).reshape(T, PEER_HEADS, K * K)
    top_s, pick = lax.top_k(cand_s, K)
    experts = jnp.take_along_axis(cand_i, pick, axis=-1)
    gates = jax.nn.softmax(top_s, axis=-1)
    n_chunks = T // PEER_CHUNK
    E = PEER_HEADS * K

    def chunk(args):
        hc, ec, gc = args
        a = jnp.einsum('cd,ced->ce', hc, w_u[ec]).astype(F32)
        coef = (gc * jax.nn.gelu(a)).astype(dt)
        return jnp.einsum('ce,ced->cd', coef, w_v[ec])

    out = lax.map(chunk, (hf.reshape(n_chunks, PEER_CHUNK, D),
                          experts.reshape(n_chunks, PEER_CHUNK, E),
                          gates.reshape(n_chunks, PEER_CHUNK, E)))
    return out.reshape(B, S, D).astype(dt)


def setup_inputs(seed: int = 0) -> dict:
    key = jax.random.key(seed)
    ks = jax.random.split(key, 24)
    L = DEPTH

    def nrm(k, shape, s):
        return jax.random.normal(k, shape, F32) * s

    return {
        "x": nrm(ks[0], (BATCH, SEQ, D_MODEL), 1.0),
        "c": nrm(ks[1], (BATCH, D_MODEL), 1.0),
        "w_ada": nrm(ks[2], (L, D_MODEL, 6 * D_MODEL), 0.5 * D_MODEL ** -0.5),
        "b_ada": nrm(ks[3], (L, 6 * D_MODEL), 0.01),
        "g_norm_mix": 1.0 + nrm(ks[4], (L, D_MODEL), 0.02),
        "g_norm_ffn": 1.0 + nrm(ks[5], (L, D_MODEL), 0.02),
        "g_norm_final": 1.0 + nrm(ks[6], (D_MODEL,), 0.02),
        "w_in": nrm(ks[7], (L, D_MODEL, IN_WIDTH), D_MODEL ** -0.5),
        "cmp_pe_k": nrm(ks[8], (L, CMP_LEN, NSA_HEAD_DIM), 0.1),
        "cmp_pe_v": nrm(ks[9], (L, CMP_LEN, NSA_HEAD_DIM), 0.1),
        "cmp_k_w1": nrm(ks[10], (L, CMP_LEN * NSA_HEAD_DIM, CMP_HIDDEN), (CMP_LEN * NSA_HEAD_DIM) ** -0.5),
        "cmp_k_w2": nrm(ks[11], (L, CMP_HIDDEN, NSA_HEAD_DIM), CMP_HIDDEN ** -0.5),
        "cmp_v_w1": nrm(ks[12], (L, CMP_LEN * NSA_HEAD_DIM, CMP_HIDDEN), (CMP_LEN * NSA_HEAD_DIM) ** -0.5),
        "cmp_v_w2": nrm(ks[13], (L, CMP_HIDDEN, NSA_HEAD_DIM), CMP_HIDDEN ** -0.5),
        "g_nsa_out": 1.0 + nrm(ks[14], (L, NSA_HEADS, NSA_HEAD_DIM), 0.02),
        "g_ret_out": 1.0 + nrm(ks[15], (L, RET_HEADS, RET_V_DIM), 0.02),
        "w_out": nrm(ks[16], (L, MIX_WIDTH, D_MODEL), MIX_WIDTH ** -0.5),
        "peer_w_q": nrm(ks[17], (L, D_MODEL, PEER_HEADS * PEER_QUERY_DIM), D_MODEL ** -0.5),
        "peer_sub_keys": nrm(ks[18], (L, PEER_HEADS, 2, PEER_N_KEYS, PEER_QUERY_DIM // 2), (PEER_QUERY_DIM // 2) ** -0.5),
        "peer_u": nrm(ks[19], (L, PEER_EXPERTS, D_MODEL), D_MODEL ** -0.5),
        "peer_v": nrm(ks[20], (L, PEER_EXPERTS, D_MODEL), 0.25),
    }


def reference(x, c, w_ada, b_ada, g_norm_mix, g_norm_ffn, g_norm_final, w_in, cmp_pe_k, cmp_pe_v,
              cmp_k_w1, cmp_k_w2, cmp_v_w1, cmp_v_w2, g_nsa_out, g_ret_out, w_out,
              peer_w_q, peer_sub_keys, peer_u, peer_v):
    B, S, D = x.shape
    c_act = jax.nn.silu(c)
    for l in range(DEPTH):
        mod = (c_act @ w_ada[l] + b_ada[l]).reshape(B, 6, 1, D)
        shift1, scale1, gate1, shift2, scale2, gate2 = jnp.moveaxis(mod, 1, 0)
        h = rmsnorm(x, g_norm_mix[l]) * (1.0 + scale1) + shift1
        q_a, k_c, v_c, k_s, v_s, k_w, v_w, g_a, q_r, k_r, v_r, g_r = split_cols(h @ w_in[l], IN_SIZES)
        o_nsa = nsa_attention(q_a.reshape(B, S, NSA_HEADS, NSA_HEAD_DIM),
                              k_c.reshape(B, S, NSA_KV_HEADS, NSA_HEAD_DIM), v_c.reshape(B, S, NSA_KV_HEADS, NSA_HEAD_DIM),
                              k_s.reshape(B, S, NSA_KV_HEADS, NSA_HEAD_DIM), v_s.reshape(B, S, NSA_KV_HEADS, NSA_HEAD_DIM),
                              k_w.reshape(B, S, NSA_KV_HEADS, NSA_HEAD_DIM), v_w.reshape(B, S, NSA_KV_HEADS, NSA_HEAD_DIM),
                              g_a, cmp_pe_k[l], cmp_pe_v[l], cmp_k_w1[l], cmp_k_w2[l], cmp_v_w1[l], cmp_v_w2[l])
        o_nsa = rmsnorm(o_nsa, g_nsa_out[l]).reshape(B, S, NSA_WIDTH)
        o_ret = retention(q_r.reshape(B, S, RET_HEADS, RET_QK_DIM), k_r.reshape(B, S, RET_HEADS, RET_QK_DIM),
                          v_r.reshape(B, S, RET_HEADS, RET_V_DIM), g_r, g_ret_out[l])
        x = x + gate1 * (jnp.concatenate([o_nsa, o_ret], axis=-1) @ w_out[l])
        h2 = rmsnorm(x, g_norm_ffn[l]) * (1.0 + scale2) + shift2
        x = x + gate2 * peer_ffn(h2, peer_w_q[l], peer_sub_keys[l], peer_u[l], peer_v[l])
    return rmsnorm(x, g_norm_final)
```

```python
import functools
import math

import numpy as np
import jax
import jax.numpy as jnp
from jax import lax
from jax.experimental import pallas as pl
from jax.experimental.pallas import tpu as pltpu

F32 = jnp.float32
BF16 = jnp.bfloat16

D_MODEL = 2048
N_HEADS = 16
HEAD_DIM = 64
KV_GROUPS = 4
GROUP = 4
CMP_LEN = 32
CMP_STRIDE = 16
CMP_HIDDEN = 128
SLC_BLOCK = 64
SLC_TOPK = 16
WINDOW = 512
FORCE_SCORE = 1e4
NEG = -1e30
RET_HEADS = 8
RET_DIM = 128
RET_CHUNK = 128
PEER_HEADS = 8
PEER_KEYS = 128
PEER_EXPERTS = PEER_KEYS * PEER_KEYS
PEER_TOPK = 16
EPS = 1e-6
LOG2E = 1.4426950408889634

LANES = 128
TQ = 128
TK_SEL = 256
TM_PROJ = 256
TM_PEER = 512
TE_PEER = 512
TM_SEL = 256
RET_TILE = 512

_NT = (((1,), (1,)), ((), ()))
_TN = (((0,), (0,)), ((), ()))


def _params(sem, vmem_mb):
    return pltpu.CompilerParams(dimension_semantics=sem, vmem_limit_bytes=vmem_mb * 1024 * 1024)


def _resident(shape, index_map):
    return pl.BlockSpec(shape, index_map, pipeline_mode=pl.Buffered(1))


def _gelu(x):
    return jax.nn.gelu(x)


def _adaln_body(c_ref, w_ref, b_ref, o_ref):
    c = c_ref[...]
    act = (c * jax.nn.sigmoid(c)).astype(BF16)
    o_ref[...] = jnp.dot(act, w_ref[...].astype(BF16), preferred_element_type=F32) + b_ref[...]


def _adaln(c_pad, w, b):
    n = w.shape[1]
    tn = 1536
    return pl.pallas_call(
        _adaln_body,
        grid=(n // tn,),
        in_specs=[pl.BlockSpec((8, D_MODEL), lambda j: (0, 0)),
                  pl.BlockSpec((D_MODEL, tn), lambda j: (0, j)),
                  pl.BlockSpec((1, tn), lambda j: (0, j))],
        out_specs=pl.BlockSpec((8, tn), lambda j: (0, j)),
        out_shape=jax.ShapeDtypeStruct((8, n), F32),
        compiler_params=_params(("arbitrary",), 40),
        name="adaln",
    )(c_pad, w, b)


STD_COLS = 512 * 3 + 1024 * 4
TR_ROWS = 1024 + 256 + 256 + 64


def _inproj_body(x_ref, sc_ref, sh_ref, gn_ref, wstd_ref, wt_ref,
                 cv_ref, ks_ref, kw_ref, qr_ref, kr_ref, vr_ref, gr_ref,
                 qt_ref, vst_ref, vwt_ref, gt_ref, *, tiles_per_seq):
    tm = TM_PROJ
    i = pl.program_id(0)
    x = x_ref[...]
    ms = jnp.mean(x * x, axis=-1, keepdims=True)
    h = x * lax.rsqrt(ms + EPS) * gn_ref[...]
    h = h * (1.0 + sc_ref[0]) + sh_ref[0]
    hb = h.astype(BF16)

    def std(a, b):
        return jnp.dot(hb, wstd_ref[:, a:b], preferred_element_type=F32)

    y = std(0, 512)
    for g in range(KV_GROUPS):
        cv_ref[g] = y[:, g * LANES:(g + 1) * LANES].astype(BF16)

    t = (i % tiles_per_seq) * tm + lax.broadcasted_iota(jnp.int32, (tm, LANES), 0)
    lane = lax.broadcasted_iota(jnp.int32, (tm, LANES), 1)
    pos_hi = ((t >> 6) << 6).astype(F32)
    pos_lo = (t & 63).astype(F32)
    aug = jnp.where((lane == 64) | (lane == 66), pos_hi,
                    jnp.where((lane == 65) | (lane == 67), pos_lo, 0.0))
    for ref, off in ((ks_ref, 512), (kw_ref, 1024)):
        y = std(off, off + 512)
        for g in range(KV_GROUPS):
            ref[:, g * LANES:(g + 1) * LANES] = (y[:, g * LANES:(g + 1) * LANES] + aug).astype(BF16)

    for ref, off in ((qr_ref, 1536), (kr_ref, 2560), (vr_ref, 3584), (gr_ref, 4608)):
        ref[...] = std(off, off + 1024).astype(BF16)

    def tr(a, b):
        return lax.dot_general(wt_ref[a:b, :], hb, _NT, preferred_element_type=F32)

    qt = tr(0, 1024) * (HEAD_DIM ** -0.5 * LOG2E)
    vst = tr(1024, 1280)
    vwt = tr(1280, 1536)
    gt = tr(1536, 1600)
    for c in range(tm // LANES):
        sl = slice(c * LANES, (c + 1) * LANES)
        qt_ref[c] = qt[:, sl].astype(BF16)
        vst_ref[c] = vst[:, sl].astype(BF16)
        vwt_ref[c] = vwt[:, sl].astype(BF16)
        gt_ref[c] = gt[:, sl]


def _inproj(x2, scale1, shift1, g_mix, w_std, w_tr, seq):
    t_tokens = x2.shape[0]
    tm = TM_PROJ
    tps = seq // tm
    nt = t_tokens // tm
    c = tm // LANES
    row = lambda i: (i, 0)
    per_b = lambda i: (i // tps, 0, 0)
    out_shape = (
        jax.ShapeDtypeStruct((KV_GROUPS, t_tokens, LANES), BF16),
        jax.ShapeDtypeStruct((t_tokens, 512), BF16),
        jax.ShapeDtypeStruct((t_tokens, 512), BF16),
        jax.ShapeDtypeStruct((t_tokens, 1024), BF16),
        jax.ShapeDtypeStruct((t_tokens, 1024), BF16),
        jax.ShapeDtypeStruct((t_tokens, 1024), BF16),
        jax.ShapeDtypeStruct((t_tokens, 1024), BF16),
        jax.ShapeDtypeStruct((t_tokens // LANES, 1024, LANES), BF16),
        jax.ShapeDtypeStruct((t_tokens // LANES, 256, LANES), BF16),
        jax.ShapeDtypeStruct((t_tokens // LANES, 256, LANES), BF16),
        jax.ShapeDtypeStruct((t_tokens // LANES, 64, LANES), F32),
    )
    out_specs = (
        pl.BlockSpec((KV_GROUPS, tm, LANES), lambda i: (0, i, 0)),
        pl.BlockSpec((tm, 512), row),
        pl.BlockSpec((tm, 512), row),
        pl.BlockSpec((tm, 1024), row),
        pl.BlockSpec((tm, 1024), row),
        pl.BlockSpec((tm, 1024), row),
        pl.BlockSpec((tm, 1024), row),
        pl.BlockSpec((c, 1024, LANES), lambda i: (i, 0, 0)),
        pl.BlockSpec((c, 256, LANES), lambda i: (i, 0, 0)),
        pl.BlockSpec((c, 256, LANES), lambda i: (i, 0, 0)),
        pl.BlockSpec((c, 64, LANES), lambda i: (i, 0, 0)),
    )
    return pl.pallas_call(
        functools.partial(_inproj_body, tiles_per_seq=tps),
        grid=(nt,),
        in_specs=[pl.BlockSpec((tm, D_MODEL), row),
                  pl.BlockSpec((1, 1, D_MODEL), per_b),
                  pl.BlockSpec((1, 1, D_MODEL), per_b),
                  _resident((1, D_MODEL), lambda i: (0, 0)),
                  _resident((D_MODEL, STD_COLS), lambda i: (0, 0)),
                  _resident((TR_ROWS, D_MODEL), lambda i: (0, 0))],
        out_specs=out_specs,
        out_shape=out_shape,
        compiler_params=_params(("parallel",), 56),
        name="inproj",
    )(x2, scale1, shift1, g_mix, w_std, w_tr)


def _compress_body(x_ref, wa_ref, wb_ref, pea_ref, peb_ref, w2k_ref, w2vt_ref, kcp_ref, vct_ref):
    x = x_ref[0, 0]
    n_rows = x.shape[0]
    p = jnp.dot(x, wa_ref[...], preferred_element_type=F32)
    q = jnp.dot(x, wb_ref[...], preferred_element_type=F32)
    pe = (jnp.dot(pea_ref[...], wa_ref[...], preferred_element_type=F32)
          + jnp.dot(peb_ref[...], wb_ref[...], preferred_element_type=F32))[0:1, :]
    pre = p + pltpu.roll(q, n_rows - 1, 0) + pe
    hid = _gelu(pre).astype(BF16)
    kc = jnp.dot(hid, w2k_ref[...], preferred_element_type=F32)
    n = lax.broadcasted_iota(jnp.int32, (n_rows, LANES), 0)
    lane = lax.broadcasted_iota(jnp.int32, (n_rows, LANES), 1)
    ce = n * CMP_STRIDE + (CMP_LEN - 1)
    ce_hi = ((ce >> 6) << 6).astype(F32)
    ce_lo = (ce & 63).astype(F32)
    aug = jnp.where((lane == 64) | (lane == 66), ce_hi,
                    jnp.where((lane == 65) | (lane == 67), ce_lo, 0.0))
    kcp_ref[0] = (kc + aug).astype(BF16)
    vct_ref[0] = lax.dot_general(w2vt_ref[...], hid, _NT, preferred_element_type=F32).astype(BF16)


def _compress(cv4, wa, wb, pea, peb, w2k, w2vt):
    g_, b_, n_rows, _ = cv4.shape
    const2 = lambda n: (0, 0)
    return pl.pallas_call(
        _compress_body,
        grid=(b_ * g_,),
        in_specs=[pl.BlockSpec((1, 1, n_rows, 2048), lambda n: (n % KV_GROUPS, n // KV_GROUPS, 0, 0)),
                  pl.BlockSpec((2048, 256), const2),
                  pl.BlockSpec((2048, 256), const2),
                  pl.BlockSpec((8, 2048), const2),
                  pl.BlockSpec((8, 2048), const2),
                  pl.BlockSpec((256, LANES), const2),
                  pl.BlockSpec((64, 256), const2)],
        out_specs=(pl.BlockSpec((1, n_rows, LANES), lambda n: (n, 0, 0)),
                   pl.BlockSpec((1, 64, n_rows), lambda n: (n, 0, 0))),
        out_shape=(jax.ShapeDtypeStruct((b_ * g_, n_rows, LANES), BF16),
                   jax.ShapeDtypeStruct((b_ * g_, 64, n_rows), BF16)),
        compiler_params=_params(("parallel",), 32),
        name="nsa_compress",
    )(cv4, wa, wb, pea, peb, w2k, w2vt)


def _online_update(carry, s, vt):
    m_i, l_i, acc = carry
    m_new = jnp.maximum(m_i, jnp.max(s, axis=0, keepdims=True))
    alpha = jnp.exp2(m_i - m_new)
    p = jnp.exp2(s - m_new)
    l_new = alpha * l_i + jnp.sum(p, axis=0, keepdims=True)
    acc = acc * alpha + jnp.dot(vt, p.astype(BF16), preferred_element_type=F32)
    return m_new, l_new, acc


def _nsa_body(qt_ref, gt_ref, kcp_ref, vct_ref, ks_ref, kw_ref, vst_ref, vwt_ref,
              qaug_ref, gout_ref, ovl_ref, o_ref, selb_ref, *, n_cmp):
    qi = pl.program_id(2)
    t0 = qi * TQ
    wq = GROUP * TQ

    qp = jnp.concatenate(
        [jnp.concatenate([qt_ref[0, r * HEAD_DIM:(r + 1) * HEAD_DIM, :], qaug_ref[r]], axis=0)
         for r in range(GROUP)], axis=1)

    s = jnp.dot(kcp_ref[0], qp, preferred_element_type=F32)
    n_io = lax.broadcasted_iota(jnp.int32, (n_cmp, wq), 0)
    tl = lax.broadcasted_iota(jnp.int32, (n_cmp, wq), 1) & (TQ - 1)
    valid = (n_io * CMP_STRIDE + (CMP_LEN - 1)) <= (t0 + tl)
    s = jnp.where(valid, s, NEG)
    m = jnp.max(s, axis=0, keepdims=True)
    p = jnp.where(valid, jnp.exp2(s - m), 0.0)
    l = jnp.sum(p, axis=0, keepdims=True)
    pn = p * (1.0 / jnp.maximum(l, 1e-30))
    o_c = jnp.dot(vct_ref[0], pn.astype(BF16), preferred_element_type=F32)

    ps = pn[:, 0:TQ] + pn[:, TQ:2 * TQ] + pn[:, 2 * TQ:3 * TQ] + pn[:, 3 * TQ:4 * TQ]
    hi = ps.astype(BF16)
    r1 = ps - hi.astype(F32)
    mid = r1.astype(BF16)
    lo = (r1 - mid.astype(F32)).astype(BF16)
    ovl = ovl_ref[...]
    imp = (jnp.dot(ovl, hi, preferred_element_type=F32)
           + jnp.dot(ovl, mid, preferred_element_type=F32)
           + jnp.dot(ovl, lo, preferred_element_type=F32))

    n_slc = imp.shape[0]
    m_io = lax.broadcasted_iota(jnp.int32, (n_slc, TQ), 0)
    q_io = lax.broadcasted_iota(jnp.int32, (n_slc, TQ), 1)
    back = ((t0 + q_io) >> 6) - m_io
    valid_s = back >= 0
    forced = valid_s & ((m_io == 0) | (back < 2))
    w = jnp.where(forced, FORCE_SCORE, jnp.where(valid_s, imp, -1.0))

    def pick(_, carry):
        w, selb = carry
        mx = jnp.max(w, axis=0, keepdims=True)
        idx = jnp.min(jnp.where(w == mx, m_io, n_slc), axis=0, keepdims=True)
        hit = m_io == idx
        return jnp.where(hit, -jnp.inf, w), jnp.where(hit, 0.0, selb)

    _, selb = lax.fori_loop(0, SLC_TOPK, pick, (w, jnp.full((n_slc, TQ), NEG, F32)))
    for mblk in range(n_slc):
        selb_ref[mblk] = jnp.broadcast_to(selb[mblk:mblk + 1, :], (8, TQ))

    kr128 = lax.broadcasted_iota(jnp.int32, (TQ, wq), 0)
    tl128 = lax.broadcasted_iota(jnp.int32, (TQ, wq), 1) & (TQ - 1)
    init = (jnp.full((1, wq), NEG, F32), jnp.zeros((1, wq), F32), jnp.zeros((HEAD_DIM, wq), F32))

    def win_step(j, carry):
        k0 = pl.multiple_of(t0 - WINDOW + j * TQ, TQ)
        s = jnp.dot(kw_ref[pl.ds(k0, TQ), :], qp, preferred_element_type=F32)
        d = (t0 - k0) + tl128 - kr128
        s = jnp.where((d >= 0) & (d < WINDOW), s, NEG)
        return _online_update(carry, s, vwt_ref[k0 // TQ])

    n_win = WINDOW // TQ + 1
    _, l_w, acc_w = lax.fori_loop(jnp.maximum(0, n_win - 1 - qi), n_win, win_step, init)

    def sel_scores(j):
        k0 = pl.multiple_of(j * TK_SEL, TK_SEL)
        s = jnp.dot(ks_ref[pl.ds(k0, TK_SEL), :], qp, preferred_element_type=F32)
        per_blk = TK_SEL // SLC_BLOCK
        bias = jnp.concatenate(
            [jnp.tile(selb_ref[j * per_blk + u], (SLC_BLOCK // 8, 1)) for u in range(per_blk)], axis=0)
        bias = jnp.concatenate([bias] * GROUP, axis=1)
        vt = jnp.concatenate([vst_ref[2 * j], vst_ref[2 * j + 1]], axis=1)
        return s + bias, vt

    def sel_step(j, carry):
        s, vt = sel_scores(j)
        return _online_update(carry, s, vt)

    jd = qi // (TK_SEL // TQ)
    carry = lax.fori_loop(0, jd, sel_step, init)
    s, vt = sel_scores(jd)
    kr256 = lax.broadcasted_iota(jnp.int32, (TK_SEL, wq), 0)
    tl256 = lax.broadcasted_iota(jnp.int32, (TK_SEL, wq), 1) & (TQ - 1)
    s = jnp.where(kr256 <= (t0 - jd * TK_SEL) + tl256, s, NEG)
    _, l_s, acc_s = _online_update(carry, s, vt)

    o_s = acc_s * (1.0 / l_s)
    o_w = acc_w * (1.0 / l_w)
    gw = jax.nn.sigmoid(gt_ref[0])
    outs = []
    for r in range(GROUP):
        sl = slice(r * TQ, (r + 1) * TQ)
        o = (gw[r:r + 1, :] * o_c[:, sl] + gw[GROUP + r:GROUP + r + 1, :] * o_s[:, sl]
             + gw[2 * GROUP + r:2 * GROUP + r + 1, :] * o_w[:, sl])
        ms = jnp.mean(o * o, axis=0, keepdims=True)
        outs.append(o * lax.rsqrt(ms + EPS) * gout_ref[r])
    o_ref[...] = jnp.concatenate(outs, axis=0).T.astype(BF16)


def _nsa(qt, gt, kcp, vct, ks, kw, vst, vwt, qaug, gout_b, ovl_t, batch, seq):
    nq = seq // TQ
    n_cmp = kcp.shape[1]
    n_slc = seq // SLC_BLOCK
    t_tokens = batch * seq
    per_b_chunks = seq // LANES
    return pl.pallas_call(
        functools.partial(_nsa_body, n_cmp=n_cmp),
        grid=(batch, KV_GROUPS, nq),
        in_specs=[
            pl.BlockSpec((1, 256, LANES), lambda b, g, q: (b * nq + q, g, 0)),
            pl.BlockSpec((1, 16, LANES), lambda b, g, q: (b * nq + q, g, 0)),
            pl.BlockSpec((1, n_cmp, LANES), lambda b, g, q: (b * KV_GROUPS + g, 0, 0)),
            pl.BlockSpec((1, 64, n_cmp), lambda b, g, q: (b * KV_GROUPS + g, 0, 0)),
            pl.BlockSpec((seq, LANES), lambda b, g, q: (b, g)),
            pl.BlockSpec((seq, LANES), lambda b, g, q: (b, g)),
            pl.BlockSpec((per_b_chunks, 64, LANES), lambda b, g, q: (b, g, 0)),
            pl.BlockSpec((per_b_chunks, 64, LANES), lambda b, g, q: (b, g, 0)),
            pl.BlockSpec((GROUP, 64, LANES), lambda b, g, q: (g, 0, 0)),
            pl.BlockSpec((GROUP, 64, LANES), lambda b, g, q: (g, 0, 0)),
            pl.BlockSpec((n_slc, n_cmp), lambda b, g, q: (0, 0)),
        ],
        out_specs=pl.BlockSpec((TQ, 256), lambda b, g, q: (b * nq + q, g)),
        out_shape=jax.ShapeDtypeStruct((t_tokens, 1024), BF16),
        scratch_shapes=[pltpu.VMEM((n_slc, 8, TQ), F32)],
        compiler_params=_params(("parallel", "parallel", "arbitrary"), 40),
        name="nsa_attention",
    )(qt, gt, kcp, vct, ks, kw, vst, vwt, qaug, gout_b, ovl_t)


def _ret_body(q_ref, k_ref, v_ref, g_ref, dm_ref, kd_ref, qd_ref, cd_ref, go_ref, o_ref, st_ref):
    @pl.when(pl.program_id(2) == 0)
    def _():
        st_ref[...] = jnp.zeros_like(st_ref)

    c_ = RET_CHUNK
    for c in range(RET_TILE // c_):
        sl = slice(c * c_, (c + 1) * c_)
        q = q_ref[sl, :]
        k = k_ref[sl, :]
        v = v_ref[sl, :]
        att = lax.dot_general(q, k, _NT, preferred_element_type=F32) * dm_ref[0]
        state = st_ref[...]
        o = (jnp.dot(att.astype(BF16), v, preferred_element_type=F32)
             + qd_ref[0] * jnp.dot(q, state.astype(BF16), preferred_element_type=F32))
        kdec = (k.astype(F32) * kd_ref[0]).astype(BF16)
        kv = lax.dot_general(kdec, v, _TN, preferred_element_type=F32)
        st_ref[...] = state * cd_ref[0] + kv
        mu = jnp.mean(o, axis=-1, keepdims=True)
        oc = o - mu
        var = jnp.mean(oc * oc, axis=-1, keepdims=True)
        y = oc * lax.rsqrt(var + EPS) * go_ref[0, 0:1, :]
        gate = g_ref[sl, :].astype(F32)
        o_ref[sl, :] = (gate * jax.nn.sigmoid(gate) * y).astype(BF16)


def _retention(q_r, k_r, v_r, g_r, dm, kd, qd, cd, go, batch, seq):
    t_tokens = batch * seq
    nc = seq // RET_TILE
    tok = lambda b, h, c: (b * nc + c, h)
    per_h = lambda b, h, c: (h, 0, 0)
    sq = (1, RET_DIM, RET_DIM)
    return pl.pallas_call(
        _ret_body,
        grid=(batch, RET_HEADS, nc),
        in_specs=[pl.BlockSpec((RET_TILE, RET_DIM), tok)] * 4
        + [pl.BlockSpec(sq, per_h)] * 4 + [pl.BlockSpec((1, 8, RET_DIM), per_h)],
        out_specs=pl.BlockSpec((RET_TILE, RET_DIM), tok),
        out_shape=jax.ShapeDtypeStruct((t_tokens, RET_HEADS * RET_DIM), BF16),
        scratch_shapes=[pltpu.VMEM((RET_DIM, RET_DIM), F32)],
        compiler_params=_params(("parallel", "parallel", "arbitrary"), 32),
        name="retention",
    )(q_r, k_r, v_r, g_r, dm, kd, qd, cd, go)


def _mid_body(on_ref, or_ref, x_ref, g1_ref, sc_ref, sh_ref, gn_ref, wo_ref, wqt_ref, sk_ref,
              x1_ref, h2_ref, st_ref):
    acc = (jnp.dot(on_ref[...], wo_ref[0:1024, :], preferred_element_type=F32)
           + jnp.dot(or_ref[...], wo_ref[1024:2048, :], preferred_element_type=F32))
    x1 = x_ref[...] + g1_ref[0] * acc
    x1_ref[...] = x1
    ms = jnp.mean(x1 * x1, axis=-1, keepdims=True)
    h2 = x1 * lax.rsqrt(ms + EPS) * gn_ref[...]
    h2 = (h2 * (1.0 + sc_ref[0]) + sh_ref[0]).astype(BF16)
    h2_ref[...] = h2
    qt = lax.dot_general(wqt_ref[...], h2, _NT, preferred_element_type=F32).astype(BF16)
    for hp in range(2 * PEER_HEADS):
        st_ref[hp] = jnp.dot(sk_ref[hp], qt[hp * 128:(hp + 1) * 128, :], preferred_element_type=F32)


def _mid(o_nsa, o_ret, x2, gate1, scale2, shift2, g_ffn, w_out, wq_t, sub_keys, seq):
    t_tokens = x2.shape[0]
    tm = TM_PROJ
    tps = seq // tm
    row = lambda i: (i, 0)
    per_b = lambda i: (i // tps, 0, 0)
    return pl.pallas_call(
        _mid_body,
        grid=(t_tokens // tm,),
        in_specs=[pl.BlockSpec((tm, 1024), row),
                  pl.BlockSpec((tm, 1024), row),
                  pl.BlockSpec((tm, D_MODEL), row),
                  pl.BlockSpec((1, 1, D_MODEL), per_b),
                  pl.BlockSpec((1, 1, D_MODEL), per_b),
                  pl.BlockSpec((1, 1, D_MODEL), per_b),
                  _resident((1, D_MODEL), lambda i: (0, 0)),
                  _resident((D_MODEL, D_MODEL), lambda i: (0, 0)),
                  _resident((D_MODEL, D_MODEL), lambda i: (0, 0)),
                  _resident((2 * PEER_HEADS, PEER_KEYS, 128), lambda i: (0, 0, 0))],
        out_specs=(pl.BlockSpec((tm, D_MODEL), row),
                   pl.BlockSpec((tm, D_MODEL), row),
                   pl.BlockSpec((2 * PEER_HEADS, PEER_KEYS, tm), lambda i: (0, 0, i))),
        out_shape=(jax.ShapeDtypeStruct((t_tokens, D_MODEL), F32),
                   jax.ShapeDtypeStruct((t_tokens, D_MODEL), BF16),
                   jax.ShapeDtypeStruct((2 * PEER_HEADS, PEER_KEYS, t_tokens), F32)),
        compiler_params=_params(("parallel",), 48),
        name="outproj_peerq",
    )(o_nsa, o_ret, x2, gate1, scale2, shift2, g_ffn, w_out, wq_t, sub_keys)


def _top16(s):
    n_rows, n = s.shape
    io = lax.broadcasted_iota(jnp.int32, (n_rows, n), 0)
    a_io = lax.broadcasted_iota(jnp.int32, (PEER_TOPK, n), 0)
    rank = jnp.full((n_rows, n), PEER_TOPK, jnp.int32)
    vals = jnp.zeros((PEER_TOPK, n), F32)
    for a in range(PEER_TOPK):
        mx = jnp.max(s, axis=0, keepdims=True)
        idx = jnp.min(jnp.where(s == mx, io, n_rows), axis=0, keepdims=True)
        hit = io == idx
        rank = jnp.where(hit, a, rank)
        s = jnp.where(hit, -jnp.inf, s)
        vals = jnp.where(a_io == a, mx, vals)
    return vals, rank


def _peer_select_body(s_ref, l_ref, w1_ref, r2_ref, w2_ref):
    s1 = s_ref[0]
    s2 = s_ref[1]
    n = s1.shape[1]
    v1, rank1 = _top16(s1)
    v2, rank2 = _top16(s2)
    a_io = lax.broadcasted_iota(jnp.int32, (PEER_TOPK, n), 0)
    cnt = jnp.zeros((PEER_TOPK, n), jnp.int32)
    cur = v1 + v2[0:1, :]
    top = v1[0:1, :] + v2[0:1, :]
    z = jnp.zeros((1, n), F32)
    for _ in range(PEER_TOPK):
        mx = jnp.max(cur, axis=0, keepdims=True)
        aidx = jnp.min(jnp.where(cur == mx, a_io, PEER_TOPK), axis=0, keepdims=True)
        hit = a_io == aidx
        cnt = cnt + hit.astype(jnp.int32)
        nxt = jnp.sum(jnp.where(hit, cnt, 0), axis=0, keepdims=True)
        nv = jnp.max(jnp.where(a_io == nxt, v2, -jnp.inf), axis=0, keepdims=True)
        cur = jnp.where(hit, v1 + nv, cur)
        z = z + jnp.exp(mx - top)
    cnt_f = cnt.astype(F32)
    lrow = jnp.zeros(s1.shape, F32)
    for a in range(PEER_TOPK):
        lrow = jnp.where(rank1 == a, cnt_f[a:a + 1, :], lrow)
    l_ref[0] = lrow
    w1_ref[0] = jnp.exp(s1 - v1[0:1, :])
    r2_ref[0] = rank2.astype(F32)
    w2_ref[0] = jnp.exp(s2 - v2[0:1, :]) * (1.0 / z)


def _peer_select(st):
    t_tokens = st.shape[2]
    tm = TM_SEL
    shp = jax.ShapeDtypeStruct((PEER_HEADS, PEER_KEYS, t_tokens), F32)
    spec = pl.BlockSpec((1, PEER_KEYS, tm), lambda i, h: (h, 0, i))
    return pl.pallas_call(
        _peer_select_body,
        grid=(t_tokens // tm, PEER_HEADS),
        in_specs=[pl.BlockSpec((2, PEER_KEYS, tm), lambda i, h: (h, 0, i))],
        out_specs=(spec, spec, spec, spec),
        out_shape=(shp, shp, shp, shp),
        compiler_params=_params(("parallel", "parallel"), 32),
        name="peer_select",
    )(st)


def _transpose_body(v_ref, o_ref):
    o_ref[...] = v_ref[...].T.astype(BF16)


def _transpose_bf16(v):
    n, d = v.shape
    tn = 512
    return pl.pallas_call(
        _transpose_body,
        grid=(n // tn,),
        in_specs=[pl.BlockSpec((tn, d), lambda i: (i, 0))],
        out_specs=pl.BlockSpec((d, tn), lambda i: (0, i)),
        out_shape=jax.ShapeDtypeStruct((d, n), BF16),
        compiler_params=_params(("parallel",), 32),
        name="transpose_v",
    )(v)


def _peer_expert_body(h2_ref, u_ref, vt_ref, l_ref, w1_ref, r2_ref, w2_ref, o_ref):
    e = pl.program_id(1)
    a_t = lax.dot_general(u_ref[...], h2_ref[...], _NT, preferred_element_type=F32)
    act = _gelu(a_t)
    blocks = []
    for ii in range(TE_PEER // PEER_KEYS):
        coef = None
        for h in range(PEER_HEADS):
            lrow = l_ref[h, 0, ii:ii + 1, :]
            w1row = w1_ref[h, 0, ii:ii + 1, :]
            term = jnp.where(r2_ref[h] < lrow, w2_ref[h] * w1row, 0.0)
            coef = term if coef is None else coef + term
        blocks.append(coef)
    coef_t = (jnp.concatenate(blocks, axis=0) * act).astype(BF16)
    contrib = jnp.dot(vt_ref[...], coef_t, preferred_element_type=F32)

    @pl.when(e == 0)
    def _():
        o_ref[...] = contrib

    @pl.when(e > 0)
    def _():
        o_ref[...] += contrib


def _peer_expert(h2, u_b, v_t, lrow, w1, r2, w2):
    t_tokens = h2.shape[0]
    tm, te = TM_PEER, TE_PEER
    ipb = te // PEER_KEYS
    l4 = lrow.reshape(PEER_HEADS, PEER_KEYS // ipb, ipb, t_tokens)
    w14 = w1.reshape(PEER_HEADS, PEER_KEYS // ipb, ipb, t_tokens)
    row_spec = pl.BlockSpec((PEER_HEADS, 1, ipb, tm), lambda i, e: (0, e, 0, i))
    full_spec = pl.BlockSpec((PEER_HEADS, PEER_KEYS, tm), lambda i, e: (0, 0, i))
    return pl.pallas_call(
        _peer_expert_body,
        grid=(t_tokens // tm, PEER_EXPERTS // te),
        in_specs=[pl.BlockSpec((tm, D_MODEL), lambda i, e: (i, 0)),
                  pl.BlockSpec((te, D_MODEL), lambda i, e: (e, 0)),
                  pl.BlockSpec((D_MODEL, te), lambda i, e: (0, e)),
                  row_spec, row_spec, full_spec, full_spec],
        out_specs=pl.BlockSpec((D_MODEL, tm), lambda i, e: (0, i)),
        out_shape=jax.ShapeDtypeStruct((D_MODEL, t_tokens), F32),
        compiler_params=_params(("parallel", "arbitrary"), 48),
        name="peer_experts",
    )(h2, u_b, v_t, l4, w14, r2, w2)


def _final_body(x1_ref, pt_ref, g2_ref, gn_ref, o_ref, *, apply_norm):
    y = x1_ref[...] + g2_ref[0] * pt_ref[...].T
    if apply_norm:
        ms = jnp.mean(y * y, axis=-1, keepdims=True)
        y = y * lax.rsqrt(ms + EPS) * gn_ref[...]
    o_ref[...] = y


def _final(x1, peer_t, gate2, g_final, seq, apply_norm):
    t_tokens = x1.shape[0]
    tm = TM_PROJ
    tps = seq // tm
    return pl.pallas_call(
        functools.partial(_final_body, apply_norm=apply_norm),
        grid=(t_tokens // tm,),
        in_specs=[pl.BlockSpec((tm, D_MODEL), lambda i: (i, 0)),
                  pl.BlockSpec((D_MODEL, tm), lambda i: (0, i)),
                  pl.BlockSpec((1, 1, D_MODEL), lambda i: (i // tps, 0, 0)),
                  pl.BlockSpec((1, D_MODEL), lambda i: (0, 0))],
        out_specs=pl.BlockSpec((tm, D_MODEL), lambda i: (i, 0)),
        out_shape=jax.ShapeDtypeStruct((t_tokens, D_MODEL), F32),
        compiler_params=_params(("parallel",), 32),
        name="final_norm",
    )(x1, peer_t, gate2, g_final)


def _split_cols(a, sizes):
    out, acc = [], 0
    for s in sizes:
        out.append(a[:, acc:acc + s])
        acc += s
    return out


def _inproj_weights(w_in):
    kvw = KV_GROUPS * HEAD_DIM
    sizes = (1024,) + (kvw,) * 6 + (3 * N_HEADS, 1024, 1024, 1024, 1024)
    q_a, k_c, v_c, k_s, v_s, k_w, v_w, g_a, q_r, k_r, v_r, g_r = _split_cols(w_in, sizes)
    d = w_in.shape[0]

    def grp(a, g):
        return a[:, g * HEAD_DIM:(g + 1) * HEAD_DIM]

    zeros = jnp.zeros((d, HEAD_DIM), w_in.dtype)
    cv = [jnp.concatenate([grp(k_c, g), grp(v_c, g)], axis=1) for g in range(KV_GROUPS)]
    ksp = [jnp.concatenate([grp(k_s, g), zeros], axis=1) for g in range(KV_GROUPS)]
    kwp = [jnp.concatenate([grp(k_w, g), zeros], axis=1) for g in range(KV_GROUPS)]
    w_std = jnp.concatenate(cv + ksp + kwp + [q_r, k_r, v_r, g_r], axis=1).astype(BF16)
    gcols = []
    for g in range(KV_GROUPS):
        for br in range(3):
            for r in range(GROUP):
                c = (g * GROUP + r) * 3 + br
                gcols.append(g_a[:, c:c + 1])
        gcols.append(jnp.zeros((d, 4), w_in.dtype))
    w_tr = jnp.concatenate([q_a, v_s, v_w] + gcols, axis=1).T.astype(BF16)
    return w_std, w_tr


def _compress_weights(pe_k, pe_v, k_w1, k_w2, v_w1, v_w2):
    half = CMP_LEN // 2

    def w1_half(w1k, w1v, lo):
        a = w1k.reshape(CMP_LEN, HEAD_DIM, CMP_HIDDEN)[lo:lo + half]
        b = w1v.reshape(CMP_LEN, HEAD_DIM, CMP_HIDDEN)[lo:lo + half]
        za = jnp.zeros_like(a)
        top = jnp.concatenate([a, za], axis=2)
        bot = jnp.concatenate([za, b], axis=2)
        return jnp.concatenate([top, bot], axis=1).reshape(half * 2 * HEAD_DIM, 2 * CMP_HIDDEN).astype(BF16)

    def pe_half(lo):
        row = jnp.concatenate([pe_k[lo:lo + half], pe_v[lo:lo + half]], axis=1).reshape(1, -1)
        return jnp.broadcast_to(row, (8, row.shape[1])).astype(BF16)

    wa = w1_half(k_w1, v_w1, 0)
    wb = w1_half(k_w1, v_w1, half)
    w2k = jnp.zeros((2 * CMP_HIDDEN, LANES), F32).at[:CMP_HIDDEN, :HEAD_DIM].set(k_w2).astype(BF16)
    w2vt = jnp.zeros((HEAD_DIM, 2 * CMP_HIDDEN), F32).at[:, CMP_HIDDEN:].set(v_w2.T).astype(BF16)
    return wa, wb, pe_half(0), pe_half(half), w2k, w2vt


def _nsa_constants(seq):
    slopes = jnp.exp2(-8.0 * (jnp.arange(N_HEADS, dtype=F32) + 1.0) / N_HEADS) * LOG2E
    s_hi = slopes.astype(BF16)
    s_lo = (slopes - s_hi.astype(F32)).astype(BF16)
    rows = jnp.zeros((N_HEADS, HEAD_DIM), BF16)
    rows = rows.at[:, 0].set(s_hi).at[:, 1].set(s_hi).at[:, 2].set(s_lo).at[:, 3].set(s_lo)
    qaug = jnp.broadcast_to(rows[:, :, None], (N_HEADS, HEAD_DIM, LANES))
    n_rows = seq // CMP_STRIDE
    n_slc = seq // SLC_BLOCK
    start = np.arange(n_rows)[:, None] * CMP_STRIDE
    end = start + CMP_LEN - 1
    blk = np.arange(n_slc)[None, :] * SLC_BLOCK
    ovl = ((start < blk + SLC_BLOCK) & (end >= blk)).astype(np.float32)
    return qaug, jnp.asarray(ovl.T, BF16)


def _retention_constants():
    h, c = RET_HEADS, RET_CHUNK
    lg = jnp.log1p(-jnp.exp2(-5.0 - jnp.arange(h, dtype=F32)))
    pos = jnp.arange(c, dtype=F32)
    diff = pos[:, None] - pos[None, :]
    scale = RET_DIM ** -0.5
    dm = jnp.where(diff >= 0, jnp.exp(lg[:, None, None] * jnp.maximum(diff, 0.0)), 0.0) * scale
    k_decay = jnp.exp(lg[:, None] * (c - 1.0 - pos)) * scale
    q_decay = jnp.exp(lg[:, None] * (pos + 1.0))
    chunk_decay = jnp.exp(lg * c)
    kd = jnp.broadcast_to(k_decay[:, :, None], (h, c, RET_DIM))
    qd = jnp.broadcast_to(q_decay[:, :, None], (h, c, RET_DIM))
    cd = jnp.broadcast_to(chunk_decay[:, None, None], (h, RET_DIM, RET_DIM))
    return dm, kd, qd, cd


def kernel(x, c, w_ada, b_ada, g_norm_mix, g_norm_ffn, g_norm_final, w_in, cmp_pe_k, cmp_pe_v,
           cmp_k_w1, cmp_k_w2, cmp_v_w1, cmp_v_w2, g_nsa_out, g_ret_out, w_out,
           peer_w_q, peer_sub_keys, peer_u, peer_v):
    batch, seq, d = x.shape
    depth = w_ada.shape[0]
    t_tokens = batch * seq
    xf = x.reshape(t_tokens, d)
    c_pad = jnp.zeros((8, d), F32).at[:batch].set(c)
    qaug, ovl_t = _nsa_constants(seq)
    dm, kd, qd, cd = _retention_constants()

    for l in range(depth):
        mod = _adaln(c_pad, w_ada[l], b_ada[l][None, :])[:batch].reshape(batch, 6, 1, d)
        shift1, scale1, gate1, shift2, scale2, gate2 = (mod[:, k] for k in range(6))

        w_std, w_tr = _inproj_weights(w_in[l])
        (cv, ks, kw, q_r, k_r, v_r, g_r, qt, vst, vwt, gt) = _inproj(
            xf, scale1, shift1, g_norm_mix[l][None, :], w_std, w_tr, seq)

        cv4 = cv.reshape(KV_GROUPS, batch, seq // CMP_STRIDE, CMP_STRIDE * LANES)
        kcp, vct = _compress(cv4, *_compress_weights(cmp_pe_k[l], cmp_pe_v[l], cmp_k_w1[l], cmp_k_w2[l],
                                                     cmp_v_w1[l], cmp_v_w2[l]))
        gout_b = jnp.broadcast_to(g_nsa_out[l][:, :, None], (N_HEADS, HEAD_DIM, LANES))
        o_nsa = _nsa(qt, gt, kcp, vct, ks, kw, vst, vwt, qaug, gout_b, ovl_t, batch, seq)

        go = jnp.broadcast_to(g_ret_out[l][:, None, :], (RET_HEADS, 8, RET_DIM))
        o_ret = _retention(q_r, k_r, v_r, g_r, dm, kd, qd, cd, go, batch, seq)

        sub_keys = peer_sub_keys[l].reshape(2 * PEER_HEADS, PEER_KEYS, -1).astype(BF16)
        x1, h2, st = _mid(o_nsa, o_ret, xf, gate1, scale2, shift2, g_norm_ffn[l][None, :],
                          w_out[l].astype(BF16), peer_w_q[l].T.astype(BF16), sub_keys, seq)

        lrow, w1, r2, w2 = _peer_select(st)
        peer_t = _peer_expert(h2, peer_u[l].astype(BF16), _transpose_bf16(peer_v[l]), lrow, w1, r2, w2)
        xf = _final(x1, peer_t, gate2, g_norm_final[None, :], seq, apply_norm=(l == depth - 1))
    return xf.reshape(batch, seq, d)
```

```python
import functools
import math

import numpy as np
import jax
import jax.numpy as jnp
from jax import lax
from jax.experimental import pallas as pl
from jax.experimental.pallas import tpu as pltpu

F32 = jnp.float32
BF16 = jnp.bfloat16

D_MODEL = 2048
N_HEADS = 16
HEAD_DIM = 64
KV_GROUPS = 4
GROUP = 4
CMP_LEN = 32
CMP_STRIDE = 16
CMP_HIDDEN = 128
SLC_BLOCK = 64
SLC_TOPK = 16
WINDOW = 512
FORCE_SCORE = 1e4
NEG = -1e30
RET_HEADS = 8
RET_DIM = 128
RET_CHUNK = 128
PEER_HEADS = 8
PEER_KEYS = 128
PEER_EXPERTS = PEER_KEYS * PEER_KEYS
PEER_TOPK = 16
EPS = 1e-6
LOG2E = 1.4426950408889634

LANES = 128
TQ = 128
TK_SEL = 256
TM_PROJ = 256
TM_PEER = 512
TE_PEER = 512
TM_SEL = 256
RET_TILE = 512

_NT = (((1,), (1,)), ((), ()))
_TN = (((0,), (0,)), ((), ()))


def _params(sem, vmem_mb):
    return pltpu.CompilerParams(dimension_semantics=sem, vmem_limit_bytes=vmem_mb * 1024 * 1024)


def _resident(shape, index_map):
    return pl.BlockSpec(shape, index_map, pipeline_mode=pl.Buffered(1))


def _gelu(x):
    return jax.nn.gelu(x)


def _adaln_body(c_ref, w_ref, b_ref, o_ref):
    c = c_ref[...]
    act = (c * jax.nn.sigmoid(c)).astype(BF16)
    o_ref[...] = jnp.dot(act, w_ref[...].astype(BF16), preferred_element_type=F32) + b_ref[...]


def _adaln(c_pad, w, b):
    n = w.shape[1]
    tn = 1536
    return pl.pallas_call(
        _adaln_body,
        grid=(n // tn,),
        in_specs=[pl.BlockSpec((8, D_MODEL), lambda j: (0, 0)),
                  pl.BlockSpec((D_MODEL, tn), lambda j: (0, j)),
                  pl.BlockSpec((1, tn), lambda j: (0, j))],
        out_specs=pl.BlockSpec((8, tn), lambda j: (0, j)),
        out_shape=jax.ShapeDtypeStruct((8, n), F32),
        compiler_params=_params(("arbitrary",), 40),
        name="adaln",
    )(c_pad, w, b)


STD_COLS = 512 * 3 + 1024 * 4
TR_ROWS = 1024 + 256 + 256 + 64


def _inproj_body(x_ref, sc_ref, sh_ref, gn_ref, wstd_ref, wt_ref,
                 cv_ref, ks_ref, kw_ref, qr_ref, kr_ref, vr_ref, gr_ref,
                 qt_ref, vst_ref, vwt_ref, gt_ref, *, tiles_per_seq):
    tm = TM_PROJ
    i = pl.program_id(0)
    x = x_ref[...]
    ms = jnp.mean(x * x, axis=-1, keepdims=True)
    h = x * lax.rsqrt(ms + EPS) * gn_ref[...]
    h = h * (1.0 + sc_ref[0]) + sh_ref[0]
    hb = h.astype(BF16)

    def std(a, b):
        return jnp.dot(hb, wstd_ref[:, a:b], preferred_element_type=F32)

    y = std(0, 512)
    for g in range(KV_GROUPS):
        cv_ref[g] = y[:, g * LANES:(g + 1) * LANES].astype(BF16)

    t = (i % tiles_per_seq) * tm + lax.broadcasted_iota(jnp.int32, (tm, LANES), 0)
    lane = lax.broadcasted_iota(jnp.int32, (tm, LANES), 1)
    pos_hi = ((t >> 6) << 6).astype(F32)
    pos_lo = (t & 63).astype(F32)
    aug = jnp.where((lane == 64) | (lane == 66), pos_hi,
                    jnp.where((lane == 65) | (lane == 67), pos_lo, 0.0))
    for ref, off in ((ks_ref, 512), (kw_ref, 1024)):
        y = std(off, off + 512)
        for g in range(KV_GROUPS):
            ref[:, g * LANES:(g + 1) * LANES] = (y[:, g * LANES:(g + 1) * LANES] + aug).astype(BF16)

    for ref, off in ((qr_ref, 1536), (kr_ref, 2560), (vr_ref, 3584), (gr_ref, 4608)):
        ref[...] = std(off, off + 1024).astype(BF16)

    def tr(a, b):
        return lax.dot_general(wt_ref[a:b, :], hb, _NT, preferred_element_type=F32)

    qt = tr(0, 1024) * (HEAD_DIM ** -0.5 * LOG2E)
    vst = tr(1024, 1280)
    vwt = tr(1280, 1536)
    gt = tr(1536, 1600)
    for c in range(tm // LANES):
        sl = slice(c * LANES, (c + 1) * LANES)
        qt_ref[c] = qt[:, sl].astype(BF16)
        vst_ref[c] = vst[:, sl].astype(BF16)
        vwt_ref[c] = vwt[:, sl].astype(BF16)
        gt_ref[c] = gt[:, sl]


def _inproj(x2, scale1, shift1, g_mix, w_std, w_tr, seq):
    t_tokens = x2.shape[0]
    tm = TM_PROJ
    tps = seq // tm
    nt = t_tokens // tm
    c = tm // LANES
    row = lambda i: (i, 0)
    per_b = lambda i: (i // tps, 0, 0)
    out_shape = (
        jax.ShapeDtypeStruct((KV_GROUPS, t_tokens, LANES), BF16),
        jax.ShapeDtypeStruct((t_tokens, 512), BF16),
        jax.ShapeDtypeStruct((t_tokens, 512), BF16),
        jax.ShapeDtypeStruct((t_tokens, 1024), BF16),
        jax.ShapeDtypeStruct((t_tokens, 1024), BF16),
        jax.ShapeDtypeStruct((t_tokens, 1024), BF16),
        jax.ShapeDtypeStruct((t_tokens, 1024), BF16),
        jax.ShapeDtypeStruct((t_tokens // LANES, 1024, LANES), BF16),
        jax.ShapeDtypeStruct((t_tokens // LANES, 256, LANES), BF16),
        jax.ShapeDtypeStruct((t_tokens // LANES, 256, LANES), BF16),
        jax.ShapeDtypeStruct((t_tokens // LANES, 64, LANES), F32),
    )
    out_specs = (
        pl.BlockSpec((KV_GROUPS, tm, LANES), lambda i: (0, i, 0)),
        pl.BlockSpec((tm, 512), row),
        pl.BlockSpec((tm, 512), row),
        pl.BlockSpec((tm, 1024), row),
        pl.BlockSpec((tm, 1024), row),
        pl.BlockSpec((tm, 1024), row),
        pl.BlockSpec((tm, 1024), row),
        pl.BlockSpec((c, 1024, LANES), lambda i: (i, 0, 0)),
        pl.BlockSpec((c, 256, LANES), lambda i: (i, 0, 0)),
        pl.BlockSpec((c, 256, LANES), lambda i: (i, 0, 0)),
        pl.BlockSpec((c, 64, LANES), lambda i: (i, 0, 0)),
    )
    return pl.pallas_call(
        functools.partial(_inproj_body, tiles_per_seq=tps),
        grid=(nt,),
        in_specs=[pl.BlockSpec((tm, D_MODEL), row),
                  pl.BlockSpec((1, 1, D_MODEL), per_b),
                  pl.BlockSpec((1, 1, D_MODEL), per_b),
                  _resident((1, D_MODEL), lambda i: (0, 0)),
                  _resident((D_MODEL, STD_COLS), lambda i: (0, 0)),
                  _resident((TR_ROWS, D_MODEL), lambda i: (0, 0))],
        out_specs=out_specs,
        out_shape=out_shape,
        compiler_params=_params(("parallel",), 56),
        name="inproj",
    )(x2, scale1, shift1, g_mix, w_std, w_tr)


def _compress_body(x_ref, wa_ref, wb_ref, pea_ref, peb_ref, w2k_ref, w2vt_ref, kcp_ref, vct_ref):
    x = x_ref[0, 0]
    n_rows = x.shape[0]
    p = jnp.dot(x, wa_ref[...], preferred_element_type=F32)
    q = jnp.dot(x, wb_ref[...], preferred_element_type=F32)
    pe = (jnp.dot(pea_ref[...], wa_ref[...], preferred_element_type=F32)
          + jnp.dot(peb_ref[...], wb_ref[...], preferred_element_type=F32))[0:1, :]
    pre = p + pltpu.roll(q, n_rows - 1, 0) + pe
    hid = _gelu(pre).astype(BF16)
    kc = jnp.dot(hid, w2k_ref[...], preferred_element_type=F32)
    n = lax.broadcasted_iota(jnp.int32, (n_rows, LANES), 0)
    lane = lax.broadcasted_iota(jnp.int32, (n_rows, LANES), 1)
    ce = n * CMP_STRIDE + (CMP_LEN - 1)
    ce_hi = ((ce >> 6) << 6).astype(F32)
    ce_lo = (ce & 63).astype(F32)
    aug = jnp.where((lane == 64) | (lane == 66), ce_hi,
                    jnp.where((lane == 65) | (lane == 67), ce_lo, 0.0))
    kcp_ref[0] = (kc + aug).astype(BF16)
    vct_ref[0] = lax.dot_general(w2vt_ref[...], hid, _NT, preferred_element_type=F32).astype(BF16)


def _compress(cv4, wa, wb, pea, peb, w2k, w2vt):
    g_, b_, n_rows, _ = cv4.shape
    const2 = lambda n: (0, 0)
    return pl.pallas_call(
        _compress_body,
        grid=(b_ * g_,),
        in_specs=[pl.BlockSpec((1, 1, n_rows, 2048), lambda n: (n % KV_GROUPS, n // KV_GROUPS, 0, 0)),
                  pl.BlockSpec((2048, 256), const2),
                  pl.BlockSpec((2048, 256), const2),
                  pl.BlockSpec((8, 2048), const2),
                  pl.BlockSpec((8, 2048), const2),
                  pl.BlockSpec((256, LANES), const2),
                  pl.BlockSpec((64, 256), const2)],
        out_specs=(pl.BlockSpec((1, n_rows, LANES), lambda n: (n, 0, 0)),
                   pl.BlockSpec((1, 64, n_rows), lambda n: (n, 0, 0))),
        out_shape=(jax.ShapeDtypeStruct((b_ * g_, n_rows, LANES), BF16),
                   jax.ShapeDtypeStruct((b_ * g_, 64, n_rows), BF16)),
        compiler_params=_params(("parallel",), 32),
        name="nsa_compress",
    )(cv4, wa, wb, pea, peb, w2k, w2vt)


def _softmax_step(state, s, pv_prev):
    m_i, l_i, acc = state
    m_new = jnp.maximum(m_i, jnp.max(s, axis=0, keepdims=True))
    alpha = jnp.exp2(m_i - m_new)
    p = jnp.exp2(s - m_new)
    l_new = alpha * l_i + jnp.sum(p, axis=0, keepdims=True)
    return (m_new, l_new, (acc + pv_prev) * alpha), p.astype(BF16)


def _nsa_body(qt_ref, gt_ref, kcp_ref, vct_ref, ks_ref, kw_ref, vst_ref, vwt_ref,
              qaug_ref, gout_ref, ovl_ref, o_ref, selb_ref, sa_ref, sb_ref, pa_ref, pb_ref, *, n_cmp):
    qi = pl.program_id(2)
    t0 = qi * TQ
    wq = GROUP * TQ

    qp = jnp.concatenate(
        [jnp.concatenate([qt_ref[0, r * HEAD_DIM:(r + 1) * HEAD_DIM, :], qaug_ref[r]], axis=0)
         for r in range(GROUP)], axis=1)

    s = jnp.dot(kcp_ref[0], qp, preferred_element_type=F32)
    n_io = lax.broadcasted_iota(jnp.int32, (n_cmp, wq), 0)
    tl = lax.broadcasted_iota(jnp.int32, (n_cmp, wq), 1) & (TQ - 1)
    valid = (n_io * CMP_STRIDE + (CMP_LEN - 1)) <= (t0 + tl)
    s = jnp.where(valid, s, NEG)
    m = jnp.max(s, axis=0, keepdims=True)
    p = jnp.where(valid, jnp.exp2(s - m), 0.0)
    l = jnp.sum(p, axis=0, keepdims=True)
    pn = p * (1.0 / jnp.maximum(l, 1e-30))
    o_c = jnp.dot(vct_ref[0], pn.astype(BF16), preferred_element_type=F32)

    ps = pn[:, 0:TQ] + pn[:, TQ:2 * TQ] + pn[:, 2 * TQ:3 * TQ] + pn[:, 3 * TQ:4 * TQ]
    hi = ps.astype(BF16)
    r1 = ps - hi.astype(F32)
    mid = r1.astype(BF16)
    lo = (r1 - mid.astype(F32)).astype(BF16)
    ovl = ovl_ref[...]
    imp = (jnp.dot(ovl, hi, preferred_element_type=F32)
           + jnp.dot(ovl, mid, preferred_element_type=F32)
           + jnp.dot(ovl, lo, preferred_element_type=F32))

    n_slc = imp.shape[0]
    m_io = lax.broadcasted_iota(jnp.int32, (n_slc, TQ), 0)
    q_io = lax.broadcasted_iota(jnp.int32, (n_slc, TQ), 1)
    back = ((t0 + q_io) >> 6) - m_io
    valid_s = back >= 0
    forced = valid_s & ((m_io == 0) | (back < 2))
    w = jnp.where(forced, FORCE_SCORE, jnp.where(valid_s, imp, -1.0))

    def pick(_, carry):
        w, selb = carry
        mx = jnp.max(w, axis=0, keepdims=True)
        idx = jnp.min(jnp.where(w == mx, m_io, n_slc), axis=0, keepdims=True)
        hit = m_io == idx
        return jnp.where(hit, -jnp.inf, w), jnp.where(hit, 0.0, selb)

    _, selb = lax.fori_loop(0, SLC_TOPK, pick, (w, jnp.full((n_slc, TQ), NEG, F32)))
    for mblk in range(n_slc):
        selb_ref[mblk] = jnp.broadcast_to(selb[mblk:mblk + 1, :], (8, TQ))

    lane_q = lax.broadcasted_iota(jnp.int32, (1, wq), 1) & (TQ - 1)

    n_win = WINDOW // TQ + 1
    k_lo = pl.multiple_of(jnp.maximum(t0 - WINDOW, 0), TQ)
    s = jnp.dot(kw_ref[pl.ds(k_lo, n_win * TQ), :], qp, preferred_element_type=F32)
    d = (t0 - k_lo) + lane_q - lax.broadcasted_iota(jnp.int32, (n_win * TQ, wq), 0)
    s = jnp.where((d >= 0) & (d < WINDOW), s, NEG)
    p = jnp.exp2(s - jnp.max(s, axis=0, keepdims=True))
    l_w = jnp.sum(p, axis=0, keepdims=True)
    vt_w = jnp.concatenate([vwt_ref[k_lo // TQ + u] for u in range(n_win)], axis=1)
    o_w = jnp.dot(vt_w, p.astype(BF16), preferred_element_type=F32) * (1.0 / l_w)

    per_blk = TK_SEL // SLC_BLOCK

    def scores(j):
        k0 = pl.multiple_of(j * TK_SEL, TK_SEL)
        bias = jnp.concatenate(
            [jnp.tile(selb_ref[j * per_blk + u], (SLC_BLOCK // 8, 1)) for u in range(per_blk)], axis=0)
        bias = jnp.concatenate([bias] * GROUP, axis=1)
        return jnp.dot(ks_ref[pl.ds(k0, TK_SEL), :], qp, preferred_element_type=F32) + bias

    def vt_tile(j):
        return jnp.concatenate([vst_ref[2 * j], vst_ref[2 * j + 1]], axis=1)

    def causal(j):
        kr = lax.broadcasted_iota(jnp.int32, (TK_SEL, wq), 0)
        return kr <= (t0 - j * TK_SEL) + lane_q

    def pv_dot(j, p_ref):
        return jnp.dot(vt_tile(j), p_ref[...], preferred_element_type=F32)

    jd = qi // (TK_SEL // TQ)
    state = (jnp.full((1, wq), NEG, F32), jnp.zeros((1, wq), F32), jnp.zeros((HEAD_DIM, wq), F32))
    sa_ref[...] = scores(0)
    pb_ref[...] = jnp.zeros_like(pb_ref)

    def pair(u, state):
        a = 2 * u
        sb_ref[...] = scores(a + 1)
        state, p = _softmax_step(state, sa_ref[...], pv_dot(jnp.maximum(a - 1, 0), pb_ref))
        pa_ref[...] = p
        sa_ref[...] = scores(a + 2)
        state, p = _softmax_step(state, sb_ref[...], pv_dot(a, pa_ref))
        pb_ref[...] = p
        return state

    state = lax.fori_loop(0, jd // 2, pair, state)
    x = 2 * (jd // 2)
    y = jnp.minimum(x + 1, jd)
    sb_ref[...] = scores(y)
    state, p = _softmax_step(state, jnp.where(causal(x), sa_ref[...], NEG), pv_dot(jnp.maximum(x - 1, 0), pb_ref))
    pa_ref[...] = p
    (_, l_s, acc_s), p = _softmax_step(state, jnp.where(causal(x + 1), sb_ref[...], NEG), pv_dot(x, pa_ref))
    acc_s = acc_s + jnp.dot(vt_tile(y), p, preferred_element_type=F32)

    o_s = acc_s * (1.0 / l_s)
    gw = jax.nn.sigmoid(gt_ref[0])
    outs = []
    for r in range(GROUP):
        sl = slice(r * TQ, (r + 1) * TQ)
        o = (gw[r:r + 1, :] * o_c[:, sl] + gw[GROUP + r:GROUP + r + 1, :] * o_s[:, sl]
             + gw[2 * GROUP + r:2 * GROUP + r + 1, :] * o_w[:, sl])
        ms = jnp.mean(o * o, axis=0, keepdims=True)
        outs.append(o * lax.rsqrt(ms + EPS) * gout_ref[r])
    o_ref[...] = jnp.concatenate(outs, axis=0).T.astype(BF16)


def _nsa(qt, gt, kcp, vct, ks, kw, vst, vwt, qaug, gout_b, ovl_t, batch, seq):
    nq = seq // TQ
    n_cmp = kcp.shape[1]
    n_slc = seq // SLC_BLOCK
    t_tokens = batch * seq
    per_b_chunks = seq // LANES
    return pl.pallas_call(
        functools.partial(_nsa_body, n_cmp=n_cmp),
        grid=(batch, KV_GROUPS, nq),
        in_specs=[
            pl.BlockSpec((1, 256, LANES), lambda b, g, q: (b * nq + q, g, 0)),
            pl.BlockSpec((1, 16, LANES), lambda b, g, q: (b * nq + q, g, 0)),
            pl.BlockSpec((1, n_cmp, LANES), lambda b, g, q: (b * KV_GROUPS + g, 0, 0)),
            pl.BlockSpec((1, 64, n_cmp), lambda b, g, q: (b * KV_GROUPS + g, 0, 0)),
            pl.BlockSpec((seq, LANES), lambda b, g, q: (b, g)),
            pl.BlockSpec((seq, LANES), lambda b, g, q: (b, g)),
            pl.BlockSpec((per_b_chunks, 64, LANES), lambda b, g, q: (b, g, 0)),
            pl.BlockSpec((per_b_chunks, 64, LANES), lambda b, g, q: (b, g, 0)),
            pl.BlockSpec((GROUP, 64, LANES), lambda b, g, q: (g, 0, 0)),
            pl.BlockSpec((GROUP, 64, LANES), lambda b, g, q: (g, 0, 0)),
            pl.BlockSpec((n_slc, n_cmp), lambda b, g, q: (0, 0)),
        ],
        out_specs=pl.BlockSpec((TQ, 256), lambda b, g, q: (b * nq + q, g)),
        out_shape=jax.ShapeDtypeStruct((t_tokens, 1024), BF16),
        scratch_shapes=[pltpu.VMEM((n_slc, 8, TQ), F32),
                        pltpu.VMEM((TK_SEL, GROUP * TQ), F32), pltpu.VMEM((TK_SEL, GROUP * TQ), F32),
                        pltpu.VMEM((TK_SEL, GROUP * TQ), BF16), pltpu.VMEM((TK_SEL, GROUP * TQ), BF16)],
        compiler_params=_params(("parallel", "parallel", "arbitrary"), 40),
        name="nsa_attention",
    )(qt, gt, kcp, vct, ks, kw, vst, vwt, qaug, gout_b, ovl_t)


def _ret_body(q_ref, k_ref, v_ref, g_ref, dm_ref, kd_ref, qd_ref, cd_ref, go_ref, o_ref, st_ref):
    @pl.when(pl.program_id(2) == 0)
    def _():
        st_ref[...] = jnp.zeros_like(st_ref)

    c_ = RET_CHUNK
    for c in range(RET_TILE // c_):
        sl = slice(c * c_, (c + 1) * c_)
        q = q_ref[sl, :]
        k = k_ref[sl, :]
        v = v_ref[sl, :]
        att = lax.dot_general(q, k, _NT, preferred_element_type=F32) * dm_ref[0]
        state = st_ref[...]
        o = (jnp.dot(att.astype(BF16), v, preferred_element_type=F32)
             + qd_ref[0] * jnp.dot(q, state.astype(BF16), preferred_element_type=F32))
        kdec = (k.astype(F32) * kd_ref[0]).astype(BF16)
        kv = lax.dot_general(kdec, v, _TN, preferred_element_type=F32)
        st_ref[...] = state * cd_ref[0] + kv
        mu = jnp.mean(o, axis=-1, keepdims=True)
        oc = o - mu
        var = jnp.mean(oc * oc, axis=-1, keepdims=True)
        y = oc * lax.rsqrt(var + EPS) * go_ref[0, 0:1, :]
        gate = g_ref[sl, :].astype(F32)
        o_ref[sl, :] = (gate * jax.nn.sigmoid(gate) * y).astype(BF16)


def _retention(q_r, k_r, v_r, g_r, dm, kd, qd, cd, go, batch, seq):
    t_tokens = batch * seq
    nc = seq // RET_TILE
    tok = lambda b, h, c: (b * nc + c, h)
    per_h = lambda b, h, c: (h, 0, 0)
    sq = (1, RET_DIM, RET_DIM)
    return pl.pallas_call(
        _ret_body,
        grid=(batch, RET_HEADS, nc),
        in_specs=[pl.BlockSpec((RET_TILE, RET_DIM), tok)] * 4
        + [pl.BlockSpec(sq, per_h)] * 4 + [pl.BlockSpec((1, 8, RET_DIM), per_h)],
        out_specs=pl.BlockSpec((RET_TILE, RET_DIM), tok),
        out_shape=jax.ShapeDtypeStruct((t_tokens, RET_HEADS * RET_DIM), BF16),
        scratch_shapes=[pltpu.VMEM((RET_DIM, RET_DIM), F32)],
        compiler_params=_params(("parallel", "parallel", "arbitrary"), 32),
        name="retention",
    )(q_r, k_r, v_r, g_r, dm, kd, qd, cd, go)


def _mid_body(on_ref, or_ref, x_ref, g1_ref, sc_ref, sh_ref, gn_ref, wo_ref, wqt_ref, sk_ref,
              x1_ref, h2_ref, st_ref):
    acc = (jnp.dot(on_ref[...], wo_ref[0:1024, :], preferred_element_type=F32)
           + jnp.dot(or_ref[...], wo_ref[1024:2048, :], preferred_element_type=F32))
    x1 = x_ref[...] + g1_ref[0] * acc
    x1_ref[...] = x1
    ms = jnp.mean(x1 * x1, axis=-1, keepdims=True)
    h2 = x1 * lax.rsqrt(ms + EPS) * gn_ref[...]
    h2 = (h2 * (1.0 + sc_ref[0]) + sh_ref[0]).astype(BF16)
    h2_ref[...] = h2
    qt = lax.dot_general(wqt_ref[...], h2, _NT, preferred_element_type=F32).astype(BF16)
    for hp in range(2 * PEER_HEADS):
        st_ref[hp] = jnp.dot(sk_ref[hp], qt[hp * 128:(hp + 1) * 128, :], preferred_element_type=F32)


def _mid(o_nsa, o_ret, x2, gate1, scale2, shift2, g_ffn, w_out, wq_t, sub_keys, seq):
    t_tokens = x2.shape[0]
    tm = TM_PROJ
    tps = seq // tm
    row = lambda i: (i, 0)
    per_b = lambda i: (i // tps, 0, 0)
    return pl.pallas_call(
        _mid_body,
        grid=(t_tokens // tm,),
        in_specs=[pl.BlockSpec((tm, 1024), row),
                  pl.BlockSpec((tm, 1024), row),
                  pl.BlockSpec((tm, D_MODEL), row),
                  pl.BlockSpec((1, 1, D_MODEL), per_b),
                  pl.BlockSpec((1, 1, D_MODEL), per_b),
                  pl.BlockSpec((1, 1, D_MODEL), per_b),
                  _resident((1, D_MODEL), lambda i: (0, 0)),
                  _resident((D_MODEL, D_MODEL), lambda i: (0, 0)),
                  _resident((D_MODEL, D_MODEL), lambda i: (0, 0)),
                  _resident((2 * PEER_HEADS, PEER_KEYS, 128), lambda i: (0, 0, 0))],
        out_specs=(pl.BlockSpec((tm, D_MODEL), row),
                   pl.BlockSpec((tm, D_MODEL), row),
                   pl.BlockSpec((2 * PEER_HEADS, PEER_KEYS, tm), lambda i: (0, 0, i))),
        out_shape=(jax.ShapeDtypeStruct((t_tokens, D_MODEL), F32),
                   jax.ShapeDtypeStruct((t_tokens, D_MODEL), BF16),
                   jax.ShapeDtypeStruct((2 * PEER_HEADS, PEER_KEYS, t_tokens), F32)),
        compiler_params=_params(("parallel",), 48),
        name="outproj_peerq",
    )(o_nsa, o_ret, x2, gate1, scale2, shift2, g_ffn, w_out, wq_t, sub_keys)


def _top16(s):
    n_rows, n = s.shape
    io = lax.broadcasted_iota(jnp.int32, (n_rows, n), 0)
    a_io = lax.broadcasted_iota(jnp.int32, (PEER_TOPK, n), 0)
    rank = jnp.full((n_rows, n), PEER_TOPK, jnp.int32)
    vals = jnp.zeros((PEER_TOPK, n), F32)
    for a in range(PEER_TOPK):
        mx = jnp.max(s, axis=0, keepdims=True)
        idx = jnp.min(jnp.where(s == mx, io, n_rows), axis=0, keepdims=True)
        hit = io == idx
        rank = jnp.where(hit, a, rank)
        s = jnp.where(hit, -jnp.inf, s)
        vals = jnp.where(a_io == a, mx, vals)
    return vals, rank


def _peer_select_body(s_ref, l_ref, w1_ref, r2_ref, w2_ref):
    s1 = s_ref[0]
    s2 = s_ref[1]
    n = s1.shape[1]
    v1, rank1 = _top16(s1)
    v2, rank2 = _top16(s2)
    a_io = lax.broadcasted_iota(jnp.int32, (PEER_TOPK, n), 0)
    cnt = jnp.zeros((PEER_TOPK, n), jnp.int32)
    cur = v1 + v2[0:1, :]
    top = v1[0:1, :] + v2[0:1, :]
    z = jnp.zeros((1, n), F32)
    for _ in range(PEER_TOPK):
        mx = jnp.max(cur, axis=0, keepdims=True)
        aidx = jnp.min(jnp.where(cur == mx, a_io, PEER_TOPK), axis=0, keepdims=True)
        hit = a_io == aidx
        cnt = cnt + hit.astype(jnp.int32)
        nxt = jnp.sum(jnp.where(hit, cnt, 0), axis=0, keepdims=True)
        nv = jnp.max(jnp.where(a_io == nxt, v2, -jnp.inf), axis=0, keepdims=True)
        cur = jnp.where(hit, v1 + nv, cur)
        z = z + jnp.exp(mx - top)
    cnt_f = cnt.astype(F32)
    lrow = jnp.zeros(s1.shape, F32)
    for a in range(PEER_TOPK):
        lrow = jnp.where(rank1 == a, cnt_f[a:a + 1, :], lrow)
    l_ref[0] = lrow
    w1_ref[0] = jnp.exp(s1 - v1[0:1, :])
    r2_ref[0] = rank2.astype(F32)
    w2_ref[0] = jnp.exp(s2 - v2[0:1, :]) * (1.0 / z)


def _peer_select(st):
    t_tokens = st.shape[2]
    tm = TM_SEL
    shp = jax.ShapeDtypeStruct((PEER_HEADS, PEER_KEYS, t_tokens), F32)
    spec = pl.BlockSpec((1, PEER_KEYS, tm), lambda i, h: (h, 0, i))
    return pl.pallas_call(
        _peer_select_body,
        grid=(t_tokens // tm, PEER_HEADS),
        in_specs=[pl.BlockSpec((2, PEER_KEYS, tm), lambda i, h: (h, 0, i))],
        out_specs=(spec, spec, spec, spec),
        out_shape=(shp, shp, shp, shp),
        compiler_params=_params(("parallel", "parallel"), 32),
        name="peer_select",
    )(st)


def _transpose_body(v_ref, o_ref):
    o_ref[...] = v_ref[...].T.astype(BF16)


def _transpose_bf16(v):
    n, d = v.shape
    tn = 512
    return pl.pallas_call(
        _transpose_body,
        grid=(n // tn,),
        in_specs=[pl.BlockSpec((tn, d), lambda i: (i, 0))],
        out_specs=pl.BlockSpec((d, tn), lambda i: (0, i)),
        out_shape=jax.ShapeDtypeStruct((d, n), BF16),
        compiler_params=_params(("parallel",), 32),
        name="transpose_v",
    )(v)


def _peer_expert_body(h2_ref, u_ref, vt_ref, l_ref, w1_ref, r2_ref, w2_ref, o_ref):
    @pl.when(pl.program_id(1) == 0)
    def _():
        o_ref[...] = jnp.zeros_like(o_ref)

    sw = 256
    strips = [slice(c * sw, (c + 1) * sw) for c in range(TM_PEER // sw)]
    scores = [lax.dot_general(u_ref[...], h2_ref[ls, :], _NT, preferred_element_type=F32)
              for ls in strips]
    for ls, a_t in zip(strips, scores):
        act = _gelu(a_t)
        blocks = []
        for ii in range(TE_PEER // PEER_KEYS):
            coef = None
            for h in range(PEER_HEADS):
                lrow = l_ref[h, 0, ii:ii + 1, ls]
                w1row = w1_ref[h, 0, ii:ii + 1, ls]
                term = jnp.where(r2_ref[h, :, ls] < lrow, w2_ref[h, :, ls] * w1row, 0.0)
                coef = term if coef is None else coef + term
            blocks.append(coef)
        coef_t = (jnp.concatenate(blocks, axis=0) * act).astype(BF16)
        o_ref[:, ls] += jnp.dot(vt_ref[...], coef_t, preferred_element_type=F32)


def _peer_expert(h2, u_b, v_t, lrow, w1, r2, w2):
    t_tokens = h2.shape[0]
    tm, te = TM_PEER, TE_PEER
    ipb = te // PEER_KEYS
    l4 = lrow.reshape(PEER_HEADS, PEER_KEYS // ipb, ipb, t_tokens)
    w14 = w1.reshape(PEER_HEADS, PEER_KEYS // ipb, ipb, t_tokens)
    row_spec = pl.BlockSpec((PEER_HEADS, 1, ipb, tm), lambda i, e: (0, e, 0, i))
    full_spec = pl.BlockSpec((PEER_HEADS, PEER_KEYS, tm), lambda i, e: (0, 0, i))
    return pl.pallas_call(
        _peer_expert_body,
        grid=(t_tokens // tm, PEER_EXPERTS // te),
        in_specs=[pl.BlockSpec((tm, D_MODEL), lambda i, e: (i, 0)),
                  pl.BlockSpec((te, D_MODEL), lambda i, e: (e, 0)),
                  pl.BlockSpec((D_MODEL, te), lambda i, e: (0, e)),
                  row_spec, row_spec, full_spec, full_spec],
        out_specs=pl.BlockSpec((D_MODEL, tm), lambda i, e: (0, i)),
        out_shape=jax.ShapeDtypeStruct((D_MODEL, t_tokens), F32),
        compiler_params=_params(("parallel", "arbitrary"), 48),
        name="peer_experts",
    )(h2, u_b, v_t, l4, w14, r2, w2)


def _final_body(x1_ref, pt_ref, g2_ref, gn_ref, o_ref, *, apply_norm):
    y = x1_ref[...] + g2_ref[0] * pt_ref[...].T
    if apply_norm:
        ms = jnp.mean(y * y, axis=-1, keepdims=True)
        y = y * lax.rsqrt(ms + EPS) * gn_ref[...]
    o_ref[...] = y


def _final(x1, peer_t, gate2, g_final, seq, apply_norm):
    t_tokens = x1.shape[0]
    tm = TM_PROJ
    tps = seq // tm
    return pl.pallas_call(
        functools.partial(_final_body, apply_norm=apply_norm),
        grid=(t_tokens // tm,),
        in_specs=[pl.BlockSpec((tm, D_MODEL), lambda i: (i, 0)),
                  pl.BlockSpec((D_MODEL, tm), lambda i: (0, i)),
                  pl.BlockSpec((1, 1, D_MODEL), lambda i: (i // tps, 0, 0)),
                  pl.BlockSpec((1, D_MODEL), lambda i: (0, 0))],
        out_specs=pl.BlockSpec((tm, D_MODEL), lambda i: (i, 0)),
        out_shape=jax.ShapeDtypeStruct((t_tokens, D_MODEL), F32),
        compiler_params=_params(("parallel",), 32),
        name="final_norm",
    )(x1, peer_t, gate2, g_final)


def _split_cols(a, sizes):
    out, acc = [], 0
    for s in sizes:
        out.append(a[:, acc:acc + s])
        acc += s
    return out


def _inproj_weights(w_in):
    kvw = KV_GROUPS * HEAD_DIM
    sizes = (1024,) + (kvw,) * 6 + (3 * N_HEADS, 1024, 1024, 1024, 1024)
    q_a, k_c, v_c, k_s, v_s, k_w, v_w, g_a, q_r, k_r, v_r, g_r = _split_cols(w_in, sizes)
    d = w_in.shape[0]

    def grp(a, g):
        return a[:, g * HEAD_DIM:(g + 1) * HEAD_DIM]

    zeros = jnp.zeros((d, HEAD_DIM), w_in.dtype)
    cv = [jnp.concatenate([grp(k_c, g), grp(v_c, g)], axis=1) for g in range(KV_GROUPS)]
    ksp = [jnp.concatenate([grp(k_s, g), zeros], axis=1) for g in range(KV_GROUPS)]
    kwp = [jnp.concatenate([grp(k_w, g), zeros], axis=1) for g in range(KV_GROUPS)]
    w_std = jnp.concatenate(cv + ksp + kwp + [q_r, k_r, v_r, g_r], axis=1).astype(BF16)
    gcols = []
    for g in range(KV_GROUPS):
        for br in range(3):
            for r in range(GROUP):
                c = (g * GROUP + r) * 3 + br
                gcols.append(g_a[:, c:c + 1])
        gcols.append(jnp.zeros((d, 4), w_in.dtype))
    w_tr = jnp.concatenate([q_a, v_s, v_w] + gcols, axis=1).T.astype(BF16)
    return w_std, w_tr


def _compress_weights(pe_k, pe_v, k_w1, k_w2, v_w1, v_w2):
    half = CMP_LEN // 2

    def w1_half(w1k, w1v, lo):
        a = w1k.reshape(CMP_LEN, HEAD_DIM, CMP_HIDDEN)[lo:lo + half]
        b = w1v.reshape(CMP_LEN, HEAD_DIM, CMP_HIDDEN)[lo:lo + half]
        za = jnp.zeros_like(a)
        top = jnp.concatenate([a, za], axis=2)
        bot = jnp.concatenate([za, b], axis=2)
        return jnp.concatenate([top, bot], axis=1).reshape(half * 2 * HEAD_DIM, 2 * CMP_HIDDEN).astype(BF16)

    def pe_half(lo):
        row = jnp.concatenate([pe_k[lo:lo + half], pe_v[lo:lo + half]], axis=1).reshape(1, -1)
        return jnp.broadcast_to(row, (8, row.shape[1])).astype(BF16)

    wa = w1_half(k_w1, v_w1, 0)
    wb = w1_half(k_w1, v_w1, half)
    w2k = jnp.zeros((2 * CMP_HIDDEN, LANES), F32).at[:CMP_HIDDEN, :HEAD_DIM].set(k_w2).astype(BF16)
    w2vt = jnp.zeros((HEAD_DIM, 2 * CMP_HIDDEN), F32).at[:, CMP_HIDDEN:].set(v_w2.T).astype(BF16)
    return wa, wb, pe_half(0), pe_half(half), w2k, w2vt


def _nsa_constants(seq):
    slopes = jnp.exp2(-8.0 * (jnp.arange(N_HEADS, dtype=F32) + 1.0) / N_HEADS) * LOG2E
    s_hi = slopes.astype(BF16)
    s_lo = (slopes - s_hi.astype(F32)).astype(BF16)
    rows = jnp.zeros((N_HEADS, HEAD_DIM), BF16)
    rows = rows.at[:, 0].set(s_hi).at[:, 1].set(s_hi).at[:, 2].set(s_lo).at[:, 3].set(s_lo)
    qaug = jnp.broadcast_to(rows[:, :, None], (N_HEADS, HEAD_DIM, LANES))
    n_rows = seq // CMP_STRIDE
    n_slc = seq // SLC_BLOCK
    start = np.arange(n_rows)[:, None] * CMP_STRIDE
    end = start + CMP_LEN - 1
    blk = np.arange(n_slc)[None, :] * SLC_BLOCK
    ovl = ((start < blk + SLC_BLOCK) & (end >= blk)).astype(np.float32)
    return qaug, jnp.asarray(ovl.T, BF16)


def _retention_constants():
    h, c = RET_HEADS, RET_CHUNK
    lg = jnp.log1p(-jnp.exp2(-5.0 - jnp.arange(h, dtype=F32)))
    pos = jnp.arange(c, dtype=F32)
    diff = pos[:, None] - pos[None, :]
    scale = RET_DIM ** -0.5
    dm = jnp.where(diff >= 0, jnp.exp(lg[:, None, None] * jnp.maximum(diff, 0.0)), 0.0) * scale
    k_decay = jnp.exp(lg[:, None] * (c - 1.0 - pos)) * scale
    q_decay = jnp.exp(lg[:, None] * (pos + 1.0))
    chunk_decay = jnp.exp(lg * c)
    kd = jnp.broadcast_to(k_decay[:, :, None], (h, c, RET_DIM))
    qd = jnp.broadcast_to(q_decay[:, :, None], (h, c, RET_DIM))
    cd = jnp.broadcast_to(chunk_decay[:, None, None], (h, RET_DIM, RET_DIM))
    return dm, kd, qd, cd


def kernel(x, c, w_ada, b_ada, g_norm_mix, g_norm_ffn, g_norm_final, w_in, cmp_pe_k, cmp_pe_v,
           cmp_k_w1, cmp_k_w2, cmp_v_w1, cmp_v_w2, g_nsa_out, g_ret_out, w_out,
           peer_w_q, peer_sub_keys, peer_u, peer_v):
    batch, seq, d = x.shape
    depth = w_ada.shape[0]
    t_tokens = batch * seq
    xf = x.reshape(t_tokens, d)
    c_pad = jnp.zeros((8, d), F32).at[:batch].set(c)
    qaug, ovl_t = _nsa_constants(seq)
    dm, kd, qd, cd = _retention_constants()

    for l in range(depth):
        mod = _adaln(c_pad, w_ada[l], b_ada[l][None, :])[:batch].reshape(batch, 6, 1, d)
        shift1, scale1, gate1, shift2, scale2, gate2 = (mod[:, k] for k in range(6))

        w_std, w_tr = _inproj_weights(w_in[l])
        (cv, ks, kw, q_r, k_r, v_r, g_r, qt, vst, vwt, gt) = _inproj(
            xf, scale1, shift1, g_norm_mix[l][None, :], w_std, w_tr, seq)

        cv4 = cv.reshape(KV_GROUPS, batch, seq // CMP_STRIDE, CMP_STRIDE * LANES)
        kcp, vct = _compress(cv4, *_compress_weights(cmp_pe_k[l], cmp_pe_v[l], cmp_k_w1[l], cmp_k_w2[l],
                                                     cmp_v_w1[l], cmp_v_w2[l]))
        gout_b = jnp.broadcast_to(g_nsa_out[l][:, :, None], (N_HEADS, HEAD_DIM, LANES))
        o_nsa = _nsa(qt, gt, kcp, vct, ks, kw, vst, vwt, qaug, gout_b, ovl_t, batch, seq)

        go = jnp.broadcast_to(g_ret_out[l][:, None, :], (RET_HEADS, 8, RET_DIM))
        o_ret = _retention(q_r, k_r, v_r, g_r, dm, kd, qd, cd, go, batch, seq)

        sub_keys = peer_sub_keys[l].reshape(2 * PEER_HEADS, PEER_KEYS, -1).astype(BF16)
        x1, h2, st = _mid(o_nsa, o_ret, xf, gate1, scale2, shift2, g_norm_ffn[l][None, :],
                          w_out[l].astype(BF16), peer_w_q[l].T.astype(BF16), sub_keys, seq)

        lrow, w1, r2, w2 = _peer_select(st)
        peer_t = _peer_expert(h2, peer_u[l].astype(BF16), _transpose_bf16(peer_v[l]), lrow, w1, r2, w2)
        xf = _final(x1, peer_t, gate2, g_norm_final[None, :], seq, apply_norm=(l == depth - 1))
    return xf.reshape(batch, seq, d)
```

```python
import functools
import math

import numpy as np
import jax
import jax.numpy as jnp
from jax import lax
from jax.experimental import pallas as pl
from jax.experimental.pallas import tpu as pltpu

F32 = jnp.float32
BF16 = jnp.bfloat16

D_MODEL = 2048
N_HEADS = 16
HEAD_DIM = 64
KV_GROUPS = 4
GROUP = 4
CMP_LEN = 32
CMP_STRIDE = 16
CMP_HIDDEN = 128
SLC_BLOCK = 64
SLC_TOPK = 16
WINDOW = 512
FORCE_SCORE = 1e4
NEG = -1e30
RET_HEADS = 8
RET_DIM = 128
RET_CHUNK = 128
PEER_HEADS = 8
PEER_KEYS = 128
PEER_EXPERTS = PEER_KEYS * PEER_KEYS
PEER_TOPK = 16
EPS = 1e-6
LOG2E = 1.4426950408889634

LANES = 128
TQ = 128
TK_SEL = 256
TM_PROJ = 256
TM_PEER = 512
TE_PEER = 512
TM_SEL = 256
RET_TILE = 512

_NT = (((1,), (1,)), ((), ()))
_TN = (((0,), (0,)), ((), ()))


def _params(sem, vmem_mb):
    return pltpu.CompilerParams(dimension_semantics=sem, vmem_limit_bytes=vmem_mb * 1024 * 1024)


def _resident(shape, index_map):
    return pl.BlockSpec(shape, index_map, pipeline_mode=pl.Buffered(1))


def _gelu(x):
    return jax.nn.gelu(x)


def _adaln_body(c_ref, w_ref, b_ref, o_ref):
    c = c_ref[...]
    act = (c * jax.nn.sigmoid(c)).astype(BF16)
    o_ref[...] = jnp.dot(act, w_ref[...].astype(BF16), preferred_element_type=F32) + b_ref[...]


def _adaln(c_pad, w, b):
    n = w.shape[1]
    tn = 1536
    return pl.pallas_call(
        _adaln_body,
        grid=(n // tn,),
        in_specs=[pl.BlockSpec((8, D_MODEL), lambda j: (0, 0)),
                  pl.BlockSpec((D_MODEL, tn), lambda j: (0, j)),
                  pl.BlockSpec((1, tn), lambda j: (0, j))],
        out_specs=pl.BlockSpec((8, tn), lambda j: (0, j)),
        out_shape=jax.ShapeDtypeStruct((8, n), F32),
        compiler_params=_params(("arbitrary",), 40),
        name="adaln",
    )(c_pad, w, b)


STD_COLS = 512 * 3 + 1024 * 4
TR_ROWS = 1024 + 256 + 256 + 64


def _inproj_body(x_ref, sc_ref, sh_ref, gn_ref, wstd_ref, wt_ref,
                 cv_ref, ks_ref, kw_ref, qr_ref, kr_ref, vr_ref, gr_ref,
                 qt_ref, vst_ref, vwt_ref, gt_ref, *, tiles_per_seq):
    tm = TM_PROJ
    i = pl.program_id(0)
    x = x_ref[...]
    ms = jnp.mean(x * x, axis=-1, keepdims=True)
    h = x * lax.rsqrt(ms + EPS) * gn_ref[...]
    h = h * (1.0 + sc_ref[0]) + sh_ref[0]
    hb = h.astype(BF16)

    def std(a, b):
        return jnp.dot(hb, wstd_ref[:, a:b], preferred_element_type=F32)

    y = std(0, 512)
    for g in range(KV_GROUPS):
        cv_ref[g] = y[:, g * LANES:(g + 1) * LANES].astype(BF16)

    t = (i % tiles_per_seq) * tm + lax.broadcasted_iota(jnp.int32, (tm, LANES), 0)
    lane = lax.broadcasted_iota(jnp.int32, (tm, LANES), 1)
    pos_hi = ((t >> 6) << 6).astype(F32)
    pos_lo = (t & 63).astype(F32)
    aug = jnp.where((lane == 64) | (lane == 66), pos_hi,
                    jnp.where((lane == 65) | (lane == 67), pos_lo, 0.0))
    for ref, off in ((ks_ref, 512), (kw_ref, 1024)):
        y = std(off, off + 512)
        for g in range(KV_GROUPS):
            ref[:, g * LANES:(g + 1) * LANES] = (y[:, g * LANES:(g + 1) * LANES] + aug).astype(BF16)

    for ref, off in ((qr_ref, 1536), (kr_ref, 2560), (vr_ref, 3584), (gr_ref, 4608)):
        ref[...] = std(off, off + 1024).astype(BF16)

    def tr(a, b):
        return lax.dot_general(wt_ref[a:b, :], hb, _NT, preferred_element_type=F32)

    qt = tr(0, 1024) * (HEAD_DIM ** -0.5 * LOG2E)
    vst = tr(1024, 1280)
    vwt = tr(1280, 1536)
    gt = tr(1536, 1600)
    for c in range(tm // LANES):
        sl = slice(c * LANES, (c + 1) * LANES)
        qt_ref[c] = qt[:, sl].astype(BF16)
        vst_ref[c] = vst[:, sl].astype(BF16)
        vwt_ref[c] = vwt[:, sl].astype(BF16)
        gt_ref[c] = gt[:, sl]


def _inproj(x2, scale1, shift1, g_mix, w_std, w_tr, seq):
    t_tokens = x2.shape[0]
    tm = TM_PROJ
    tps = seq // tm
    nt = t_tokens // tm
    c = tm // LANES
    row = lambda i: (i, 0)
    per_b = lambda i: (i // tps, 0, 0)
    out_shape = (
        jax.ShapeDtypeStruct((KV_GROUPS, t_tokens, LANES), BF16),
        jax.ShapeDtypeStruct((t_tokens, 512), BF16),
        jax.ShapeDtypeStruct((t_tokens, 512), BF16),
        jax.ShapeDtypeStruct((t_tokens, 1024), BF16),
        jax.ShapeDtypeStruct((t_tokens, 1024), BF16),
        jax.ShapeDtypeStruct((t_tokens, 1024), BF16),
        jax.ShapeDtypeStruct((t_tokens, 1024), BF16),
        jax.ShapeDtypeStruct((t_tokens // LANES, 1024, LANES), BF16),
        jax.ShapeDtypeStruct((t_tokens // LANES, 256, LANES), BF16),
        jax.ShapeDtypeStruct((t_tokens // LANES, 256, LANES), BF16),
        jax.ShapeDtypeStruct((t_tokens // LANES, 64, LANES), F32),
    )
    out_specs = (
        pl.BlockSpec((KV_GROUPS, tm, LANES), lambda i: (0, i, 0)),
        pl.BlockSpec((tm, 512), row),
        pl.BlockSpec((tm, 512), row),
        pl.BlockSpec((tm, 1024), row),
        pl.BlockSpec((tm, 1024), row),
        pl.BlockSpec((tm, 1024), row),
        pl.BlockSpec((tm, 1024), row),
        pl.BlockSpec((c, 1024, LANES), lambda i: (i, 0, 0)),
        pl.BlockSpec((c, 256, LANES), lambda i: (i, 0, 0)),
        pl.BlockSpec((c, 256, LANES), lambda i: (i, 0, 0)),
        pl.BlockSpec((c, 64, LANES), lambda i: (i, 0, 0)),
    )
    return pl.pallas_call(
        functools.partial(_inproj_body, tiles_per_seq=tps),
        grid=(nt,),
        in_specs=[pl.BlockSpec((tm, D_MODEL), row),
                  pl.BlockSpec((1, 1, D_MODEL), per_b),
                  pl.BlockSpec((1, 1, D_MODEL), per_b),
                  _resident((1, D_MODEL), lambda i: (0, 0)),
                  _resident((D_MODEL, STD_COLS), lambda i: (0, 0)),
                  _resident((TR_ROWS, D_MODEL), lambda i: (0, 0))],
        out_specs=out_specs,
        out_shape=out_shape,
        compiler_params=_params(("parallel",), 56),
        name="inproj",
    )(x2, scale1, shift1, g_mix, w_std, w_tr)


def _compress_body(x_ref, wa_ref, wb_ref, pea_ref, peb_ref, w2k_ref, w2vt_ref, kcp_ref, vct_ref):
    x = x_ref[0, 0]
    n_rows = x.shape[0]
    p = jnp.dot(x, wa_ref[...], preferred_element_type=F32)
    q = jnp.dot(x, wb_ref[...], preferred_element_type=F32)
    pe = (jnp.dot(pea_ref[...], wa_ref[...], preferred_element_type=F32)
          + jnp.dot(peb_ref[...], wb_ref[...], preferred_element_type=F32))[0:1, :]
    pre = p + pltpu.roll(q, n_rows - 1, 0) + pe
    hid = _gelu(pre).astype(BF16)
    kc = jnp.dot(hid, w2k_ref[...], preferred_element_type=F32)
    n = lax.broadcasted_iota(jnp.int32, (n_rows, LANES), 0)
    lane = lax.broadcasted_iota(jnp.int32, (n_rows, LANES), 1)
    ce = n * CMP_STRIDE + (CMP_LEN - 1)
    ce_hi = ((ce >> 6) << 6).astype(F32)
    ce_lo = (ce & 63).astype(F32)
    aug = jnp.where((lane == 64) | (lane == 66), ce_hi,
                    jnp.where((lane == 65) | (lane == 67), ce_lo, 0.0))
    kcp_ref[0] = (kc + aug).astype(BF16)
    vct_ref[0] = lax.dot_general(w2vt_ref[...], hid, _NT, preferred_element_type=F32).astype(BF16)


def _compress(cv4, wa, wb, pea, peb, w2k, w2vt):
    g_, b_, n_rows, _ = cv4.shape
    const2 = lambda n: (0, 0)
    return pl.pallas_call(
        _compress_body,
        grid=(b_ * g_,),
        in_specs=[pl.BlockSpec((1, 1, n_rows, 2048), lambda n: (n % KV_GROUPS, n // KV_GROUPS, 0, 0)),
                  pl.BlockSpec((2048, 256), const2),
                  pl.BlockSpec((2048, 256), const2),
                  pl.BlockSpec((8, 2048), const2),
                  pl.BlockSpec((8, 2048), const2),
                  pl.BlockSpec((256, LANES), const2),
                  pl.BlockSpec((64, 256), const2)],
        out_specs=(pl.BlockSpec((1, n_rows, LANES), lambda n: (n, 0, 0)),
                   pl.BlockSpec((1, 64, n_rows), lambda n: (n, 0, 0))),
        out_shape=(jax.ShapeDtypeStruct((b_ * g_, n_rows, LANES), BF16),
                   jax.ShapeDtypeStruct((b_ * g_, 64, n_rows), BF16)),
        compiler_params=_params(("parallel",), 32),
        name="nsa_compress",
    )(cv4, wa, wb, pea, peb, w2k, w2vt)


def _softmax_step(state, s, pv_prev):
    m_i, l_i, acc = state
    m_new = jnp.maximum(m_i, jnp.max(s, axis=0, keepdims=True))
    alpha = jnp.exp2(m_i - m_new)
    p = jnp.exp2(s - m_new)
    l_new = alpha * l_i + jnp.sum(p, axis=0, keepdims=True)
    return (m_new, l_new, (acc + pv_prev) * alpha), p.astype(BF16)


def _nsa_body(qt_ref, gt_ref, kcp_ref, vct_ref, ks_ref, kw_ref, vst_ref, vwt_ref,
              qaug_ref, gout_ref, ovl_ref, o_ref, selb_ref, sa_ref, sb_ref, pa_ref, pb_ref, *, n_cmp):
    qi = pl.program_id(2)
    t0 = qi * TQ
    wq = GROUP * TQ

    qp = jnp.concatenate(
        [jnp.concatenate([qt_ref[0, r * HEAD_DIM:(r + 1) * HEAD_DIM, :], qaug_ref[r]], axis=0)
         for r in range(GROUP)], axis=1)

    s = jnp.dot(kcp_ref[0], qp, preferred_element_type=F32)
    n_io = lax.broadcasted_iota(jnp.int32, (n_cmp, wq), 0)
    tl = lax.broadcasted_iota(jnp.int32, (n_cmp, wq), 1) & (TQ - 1)
    valid = (n_io * CMP_STRIDE + (CMP_LEN - 1)) <= (t0 + tl)
    s = jnp.where(valid, s, NEG)
    m = jnp.max(s, axis=0, keepdims=True)
    p = jnp.where(valid, jnp.exp2(s - m), 0.0)
    l = jnp.sum(p, axis=0, keepdims=True)
    pn = p * (1.0 / jnp.maximum(l, 1e-30))
    o_c = jnp.dot(vct_ref[0], pn.astype(BF16), preferred_element_type=F32)

    ps = pn[:, 0:TQ] + pn[:, TQ:2 * TQ] + pn[:, 2 * TQ:3 * TQ] + pn[:, 3 * TQ:4 * TQ]
    hi = ps.astype(BF16)
    r1 = ps - hi.astype(F32)
    mid = r1.astype(BF16)
    lo = (r1 - mid.astype(F32)).astype(BF16)
    ovl = ovl_ref[...]
    imp = (jnp.dot(ovl, hi, preferred_element_type=F32)
           + jnp.dot(ovl, mid, preferred_element_type=F32)
           + jnp.dot(ovl, lo, preferred_element_type=F32))

    n_slc = imp.shape[0]
    m_io = lax.broadcasted_iota(jnp.int32, (n_slc, TQ), 0)
    q_io = lax.broadcasted_iota(jnp.int32, (n_slc, TQ), 1)
    back = ((t0 + q_io) >> 6) - m_io
    valid_s = back >= 0
    forced = valid_s & ((m_io == 0) | (back < 2))
    w = jnp.where(forced, FORCE_SCORE, jnp.where(valid_s, imp, -1.0))

    def pick(_, carry):
        w, selb = carry
        mx = jnp.max(w, axis=0, keepdims=True)
        idx = jnp.min(jnp.where(w == mx, m_io, n_slc), axis=0, keepdims=True)
        hit = m_io == idx
        return jnp.where(hit, -jnp.inf, w), jnp.where(hit, 0.0, selb)

    _, selb = lax.fori_loop(0, SLC_TOPK, pick, (w, jnp.full((n_slc, TQ), NEG, F32)))
    for mblk in range(n_slc):
        selb_ref[mblk] = jnp.broadcast_to(selb[mblk:mblk + 1, :], (8, TQ))

    lane_q = lax.broadcasted_iota(jnp.int32, (1, wq), 1) & (TQ - 1)

    n_win = WINDOW // TQ + 1
    k_lo = pl.multiple_of(jnp.maximum(t0 - WINDOW, 0), TQ)
    s = jnp.dot(kw_ref[pl.ds(k_lo, n_win * TQ), :], qp, preferred_element_type=F32)
    d = (t0 - k_lo) + lane_q - lax.broadcasted_iota(jnp.int32, (n_win * TQ, wq), 0)
    s = jnp.where((d >= 0) & (d < WINDOW), s, NEG)
    p = jnp.exp2(s - jnp.max(s, axis=0, keepdims=True))
    l_w = jnp.sum(p, axis=0, keepdims=True)
    vt_w = jnp.concatenate([vwt_ref[k_lo // TQ + u] for u in range(n_win)], axis=1)
    o_w = jnp.dot(vt_w, p.astype(BF16), preferred_element_type=F32) * (1.0 / l_w)

    per_blk = TK_SEL // SLC_BLOCK

    def scores(j):
        k0 = pl.multiple_of(j * TK_SEL, TK_SEL)
        bias = jnp.concatenate(
            [jnp.tile(selb_ref[j * per_blk + u], (SLC_BLOCK // 8, 1)) for u in range(per_blk)], axis=0)
        bias = jnp.concatenate([bias] * GROUP, axis=1)
        return jnp.dot(ks_ref[pl.ds(k0, TK_SEL), :], qp, preferred_element_type=F32) + bias

    def vt_tile(j):
        return jnp.concatenate([vst_ref[2 * j], vst_ref[2 * j + 1]], axis=1)

    def causal(j):
        kr = lax.broadcasted_iota(jnp.int32, (TK_SEL, wq), 0)
        return kr <= (t0 - j * TK_SEL) + lane_q

    def pv_dot(j, p_ref):
        return jnp.dot(vt_tile(j), p_ref[...], preferred_element_type=F32)

    jd = qi // (TK_SEL // TQ)
    state = (jnp.full((1, wq), NEG, F32), jnp.zeros((1, wq), F32), jnp.zeros((HEAD_DIM, wq), F32))
    sa_ref[...] = scores(0)
    pb_ref[...] = jnp.zeros_like(pb_ref)

    def pair(u, state):
        a = 2 * u
        sb_ref[...] = scores(a + 1)
        state, p = _softmax_step(state, sa_ref[...], pv_dot(jnp.maximum(a - 1, 0), pb_ref))
        pa_ref[...] = p
        sa_ref[...] = scores(a + 2)
        state, p = _softmax_step(state, sb_ref[...], pv_dot(a, pa_ref))
        pb_ref[...] = p
        return state

    state = lax.fori_loop(0, jd // 2, pair, state)
    x = 2 * (jd // 2)
    y = jnp.minimum(x + 1, jd)
    sb_ref[...] = scores(y)
    state, p = _softmax_step(state, jnp.where(causal(x), sa_ref[...], NEG), pv_dot(jnp.maximum(x - 1, 0), pb_ref))
    pa_ref[...] = p
    (_, l_s, acc_s), p = _softmax_step(state, jnp.where(causal(x + 1), sb_ref[...], NEG), pv_dot(x, pa_ref))
    acc_s = acc_s + jnp.dot(vt_tile(y), p, preferred_element_type=F32)

    o_s = acc_s * (1.0 / l_s)
    gw = jax.nn.sigmoid(gt_ref[0])
    outs = []
    for r in range(GROUP):
        sl = slice(r * TQ, (r + 1) * TQ)
        o = (gw[r:r + 1, :] * o_c[:, sl] + gw[GROUP + r:GROUP + r + 1, :] * o_s[:, sl]
             + gw[2 * GROUP + r:2 * GROUP + r + 1, :] * o_w[:, sl])
        ms = jnp.mean(o * o, axis=0, keepdims=True)
        outs.append(o * lax.rsqrt(ms + EPS) * gout_ref[r])
    o_ref[...] = jnp.concatenate(outs, axis=0).T.astype(BF16)


def _nsa(qt, gt, kcp, vct, ks, kw, vst, vwt, qaug, gout_b, ovl_t, batch, seq):
    nq = seq // TQ
    n_cmp = kcp.shape[1]
    n_slc = seq // SLC_BLOCK
    t_tokens = batch * seq
    per_b_chunks = seq // LANES
    return pl.pallas_call(
        functools.partial(_nsa_body, n_cmp=n_cmp),
        grid=(batch, KV_GROUPS, nq),
        in_specs=[
            pl.BlockSpec((1, 256, LANES), lambda b, g, q: (b * nq + q, g, 0)),
            pl.BlockSpec((1, 16, LANES), lambda b, g, q: (b * nq + q, g, 0)),
            pl.BlockSpec((1, n_cmp, LANES), lambda b, g, q: (b * KV_GROUPS + g, 0, 0)),
            pl.BlockSpec((1, 64, n_cmp), lambda b, g, q: (b * KV_GROUPS + g, 0, 0)),
            pl.BlockSpec((seq, LANES), lambda b, g, q: (b, g)),
            pl.BlockSpec((seq, LANES), lambda b, g, q: (b, g)),
            pl.BlockSpec((per_b_chunks, 64, LANES), lambda b, g, q: (b, g, 0)),
            pl.BlockSpec((per_b_chunks, 64, LANES), lambda b, g, q: (b, g, 0)),
            pl.BlockSpec((GROUP, 64, LANES), lambda b, g, q: (g, 0, 0)),
            pl.BlockSpec((GROUP, 64, LANES), lambda b, g, q: (g, 0, 0)),
            pl.BlockSpec((n_slc, n_cmp), lambda b, g, q: (0, 0)),
        ],
        out_specs=pl.BlockSpec((TQ, 256), lambda b, g, q: (b * nq + q, g)),
        out_shape=jax.ShapeDtypeStruct((t_tokens, 1024), BF16),
        scratch_shapes=[pltpu.VMEM((n_slc, 8, TQ), F32),
                        pltpu.VMEM((TK_SEL, GROUP * TQ), F32), pltpu.VMEM((TK_SEL, GROUP * TQ), F32),
                        pltpu.VMEM((TK_SEL, GROUP * TQ), BF16), pltpu.VMEM((TK_SEL, GROUP * TQ), BF16)],
        compiler_params=_params(("parallel", "parallel", "arbitrary"), 40),
        name="nsa_attention",
    )(qt, gt, kcp, vct, ks, kw, vst, vwt, qaug, gout_b, ovl_t)


def _ret_body(q_ref, k_ref, v_ref, g_ref, dm_ref, kd_ref, qd_ref, cd_ref, go_ref, o_ref, st_ref):
    @pl.when(pl.program_id(2) == 0)
    def _():
        st_ref[...] = jnp.zeros_like(st_ref)

    c_ = RET_CHUNK
    for c in range(RET_TILE // c_):
        sl = slice(c * c_, (c + 1) * c_)
        q = q_ref[sl, :]
        k = k_ref[sl, :]
        v = v_ref[sl, :]
        att = lax.dot_general(q, k, _NT, preferred_element_type=F32) * dm_ref[0]
        state = st_ref[...]
        o = (jnp.dot(att.astype(BF16), v, preferred_element_type=F32)
             + qd_ref[0] * jnp.dot(q, state.astype(BF16), preferred_element_type=F32))
        kdec = (k.astype(F32) * kd_ref[0]).astype(BF16)
        kv = lax.dot_general(kdec, v, _TN, preferred_element_type=F32)
        st_ref[...] = state * cd_ref[0] + kv
        mu = jnp.mean(o, axis=-1, keepdims=True)
        oc = o - mu
        var = jnp.mean(oc * oc, axis=-1, keepdims=True)
        y = oc * lax.rsqrt(var + EPS) * go_ref[0, 0:1, :]
        gate = g_ref[sl, :].astype(F32)
        o_ref[sl, :] = (gate * jax.nn.sigmoid(gate) * y).astype(BF16)


def _retention(q_r, k_r, v_r, g_r, dm, kd, qd, cd, go, batch, seq):
    t_tokens = batch * seq
    nc = seq // RET_TILE
    tok = lambda b, h, c: (b * nc + c, h)
    per_h = lambda b, h, c: (h, 0, 0)
    sq = (1, RET_DIM, RET_DIM)
    return pl.pallas_call(
        _ret_body,
        grid=(batch, RET_HEADS, nc),
        in_specs=[pl.BlockSpec((RET_TILE, RET_DIM), tok)] * 4
        + [pl.BlockSpec(sq, per_h)] * 4 + [pl.BlockSpec((1, 8, RET_DIM), per_h)],
        out_specs=pl.BlockSpec((RET_TILE, RET_DIM), tok),
        out_shape=jax.ShapeDtypeStruct((t_tokens, RET_HEADS * RET_DIM), BF16),
        scratch_shapes=[pltpu.VMEM((RET_DIM, RET_DIM), F32)],
        compiler_params=_params(("parallel", "parallel", "arbitrary"), 32),
        name="retention",
    )(q_r, k_r, v_r, g_r, dm, kd, qd, cd, go)


def _mid_body(on_ref, or_ref, x_ref, g1_ref, sc_ref, sh_ref, gn_ref, wo_ref, wqt_ref, sk_ref,
              x1_ref, h2t_ref, st_ref):
    acc = (jnp.dot(on_ref[...], wo_ref[0:1024, :], preferred_element_type=F32)
           + jnp.dot(or_ref[...], wo_ref[1024:2048, :], preferred_element_type=F32))
    x1 = x_ref[...] + g1_ref[0] * acc
    x1_ref[...] = x1
    ms = jnp.mean(x1 * x1, axis=-1, keepdims=True)
    h2 = x1 * lax.rsqrt(ms + EPS) * gn_ref[...]
    h2 = h2 * (1.0 + sc_ref[0]) + sh_ref[0]
    h2t_ref[...] = h2.T.astype(BF16)
    h2 = h2.astype(BF16)
    qt = lax.dot_general(wqt_ref[...], h2, _NT, preferred_element_type=F32).astype(BF16)
    for hp in range(2 * PEER_HEADS):
        st_ref[hp] = jnp.dot(sk_ref[hp], qt[hp * 128:(hp + 1) * 128, :], preferred_element_type=F32)


def _mid(o_nsa, o_ret, x2, gate1, scale2, shift2, g_ffn, w_out, wq_t, sub_keys, seq):
    t_tokens = x2.shape[0]
    tm = TM_PROJ
    tps = seq // tm
    row = lambda i: (i, 0)
    per_b = lambda i: (i // tps, 0, 0)
    return pl.pallas_call(
        _mid_body,
        grid=(t_tokens // tm,),
        in_specs=[pl.BlockSpec((tm, 1024), row),
                  pl.BlockSpec((tm, 1024), row),
                  pl.BlockSpec((tm, D_MODEL), row),
                  pl.BlockSpec((1, 1, D_MODEL), per_b),
                  pl.BlockSpec((1, 1, D_MODEL), per_b),
                  pl.BlockSpec((1, 1, D_MODEL), per_b),
                  _resident((1, D_MODEL), lambda i: (0, 0)),
                  _resident((D_MODEL, D_MODEL), lambda i: (0, 0)),
                  _resident((D_MODEL, D_MODEL), lambda i: (0, 0)),
                  _resident((2 * PEER_HEADS, PEER_KEYS, 128), lambda i: (0, 0, 0))],
        out_specs=(pl.BlockSpec((tm, D_MODEL), row),
                   pl.BlockSpec((D_MODEL, tm), lambda i: (0, i)),
                   pl.BlockSpec((2 * PEER_HEADS, PEER_KEYS, tm), lambda i: (0, 0, i))),
        out_shape=(jax.ShapeDtypeStruct((t_tokens, D_MODEL), F32),
                   jax.ShapeDtypeStruct((D_MODEL, t_tokens), BF16),
                   jax.ShapeDtypeStruct((2 * PEER_HEADS, PEER_KEYS, t_tokens), F32)),
        compiler_params=_params(("parallel",), 48),
        name="outproj_peerq",
    )(o_nsa, o_ret, x2, gate1, scale2, shift2, g_ffn, w_out, wq_t, sub_keys)


def _top16(s):
    n_rows, n = s.shape
    io = lax.broadcasted_iota(jnp.int32, (n_rows, n), 0)
    a_io = lax.broadcasted_iota(jnp.int32, (PEER_TOPK, n), 0)
    rank = jnp.full((n_rows, n), PEER_TOPK, jnp.int32)
    vals = jnp.zeros((PEER_TOPK, n), F32)
    for a in range(PEER_TOPK):
        mx = jnp.max(s, axis=0, keepdims=True)
        idx = jnp.min(jnp.where(s == mx, io, n_rows), axis=0, keepdims=True)
        hit = io == idx
        rank = jnp.where(hit, a, rank)
        s = jnp.where(hit, -jnp.inf, s)
        vals = jnp.where(a_io == a, mx, vals)
    return vals, rank


def _peer_select_body(s_ref, l_ref, w1_ref, r2_ref, w2_ref):
    s1 = s_ref[0]
    s2 = s_ref[1]
    n = s1.shape[1]
    v1, rank1 = _top16(s1)
    v2, rank2 = _top16(s2)
    a_io = lax.broadcasted_iota(jnp.int32, (PEER_TOPK, n), 0)
    cnt = jnp.zeros((PEER_TOPK, n), jnp.int32)
    cur = v1 + v2[0:1, :]
    top = v1[0:1, :] + v2[0:1, :]
    z = jnp.zeros((1, n), F32)
    for _ in range(PEER_TOPK):
        mx = jnp.max(cur, axis=0, keepdims=True)
        aidx = jnp.min(jnp.where(cur == mx, a_io, PEER_TOPK), axis=0, keepdims=True)
        hit = a_io == aidx
        cnt = cnt + hit.astype(jnp.int32)
        nxt = jnp.sum(jnp.where(hit, cnt, 0), axis=0, keepdims=True)
        nv = jnp.max(jnp.where(a_io == nxt, v2, -jnp.inf), axis=0, keepdims=True)
        cur = jnp.where(hit, v1 + nv, cur)
        z = z + jnp.exp(mx - top)
    cnt_f = cnt.astype(F32)
    lrow = jnp.zeros(s1.shape, F32)
    for a in range(PEER_TOPK):
        lrow = jnp.where(rank1 == a, cnt_f[a:a + 1, :], lrow)
    l_ref[0] = lrow
    w1_ref[0] = jnp.exp(s1 - v1[0:1, :])
    r2_ref[0] = rank2.astype(F32).astype(BF16)
    w2_ref[0] = (jnp.exp(s2 - v2[0:1, :]) * (1.0 / z)).astype(BF16)


def _peer_select(st):
    t_tokens = st.shape[2]
    tm = TM_SEL
    shp = jax.ShapeDtypeStruct((PEER_HEADS, PEER_KEYS, t_tokens), F32)
    shp_b = jax.ShapeDtypeStruct((PEER_HEADS, PEER_KEYS, t_tokens), BF16)
    spec = pl.BlockSpec((1, PEER_KEYS, tm), lambda i, h: (h, 0, i))
    return pl.pallas_call(
        _peer_select_body,
        grid=(t_tokens // tm, PEER_HEADS),
        in_specs=[pl.BlockSpec((2, PEER_KEYS, tm), lambda i, h: (h, 0, i))],
        out_specs=(spec, spec, spec, spec),
        out_shape=(shp, shp, shp_b, shp_b),
        compiler_params=_params(("parallel", "parallel"), 32),
        name="peer_select",
    )(st)


def _transpose_body(v_ref, o_ref):
    o_ref[...] = v_ref[...].T.astype(BF16)


def _transpose_bf16(v):
    n, d = v.shape
    tn = 512
    return pl.pallas_call(
        _transpose_body,
        grid=(n // tn,),
        in_specs=[pl.BlockSpec((tn, d), lambda i: (i, 0))],
        out_specs=pl.BlockSpec((d, tn), lambda i: (0, i)),
        out_shape=jax.ShapeDtypeStruct((d, n), BF16),
        compiler_params=_params(("parallel",), 32),
        name="transpose_v",
    )(v)


def _peer_expert_body(h2t_ref, ua_ref, ub_ref, vtp_ref, vta_ref, vtl_ref,
                      l_ref, w1_ref, r2_ref, w2_ref, o_ref, c0_ref, c1_ref):
    k = pl.program_id(1)
    sw = 256
    strips = [slice(c * sw, (c + 1) * sw) for c in range(TM_PEER // sw)]
    ipb = TE_PEER // PEER_KEYS
    blk = D_MODEL // ipb

    def accumulate(j, vt_ref, c_ref):
        d0 = pl.multiple_of(j * blk, blk)
        for ls in strips:
            o_ref[pl.ds(d0, blk), ls] += jnp.dot(vt_ref[pl.ds(d0, blk), :], c_ref[:, ls],
                                                 preferred_element_type=F32)

    def coefficients(j, u_ref, row0, c_ref):
        r0 = pl.multiple_of(j * PEER_KEYS, PEER_KEYS)
        for ls in strips:
            coef = None
            for h in range(PEER_HEADS):
                lrow = jnp.broadcast_to(l_ref[h, pl.ds(row0 + j, 1), ls], (16, sw)).astype(BF16)
                w1row = jnp.broadcast_to(w1_ref[h, pl.ds(row0 + j, 1), ls], (16, sw)).astype(BF16)
                lrow = jnp.tile(lrow, (PEER_KEYS // 16, 1))
                w1row = jnp.tile(w1row, (PEER_KEYS // 16, 1))
                term = jnp.where(r2_ref[h, :, ls] < lrow, w2_ref[h, :, ls] * w1row, jnp.zeros((), BF16))
                coef = term if coef is None else coef + term
            a_t = jnp.dot(u_ref[pl.ds(r0, PEER_KEYS), :], h2t_ref[:, ls], preferred_element_type=F32)
            c_ref[pl.ds(r0, PEER_KEYS), ls] = coef * _gelu(a_t).astype(BF16)

    @pl.when(k == 0)
    def _():
        o_ref[...] = jnp.zeros_like(o_ref)
        c1_ref[...] = jnp.zeros_like(c1_ref)

    def half_a(j, carry):
        coefficients(j, ua_ref, 0, c0_ref)
        accumulate(j, vtp_ref, c1_ref)
        return carry

    def half_b(j, carry):
        coefficients(j, ub_ref, ipb, c1_ref)
        accumulate(j, vta_ref, c0_ref)
        return carry

    lax.fori_loop(0, ipb, half_a, 0)
    lax.fori_loop(0, ipb, half_b, 0)

    @pl.when(k == pl.num_programs(1) - 1)
    def _():
        lax.fori_loop(0, ipb, lambda j, c: (accumulate(j, vtl_ref, c1_ref), c)[1], 0)


def _peer_expert(h2t, u_b, v_t, lrow, w1, r2, w2):
    t_tokens = h2t.shape[1]
    tm, te = TM_PEER, TE_PEER
    ipb = te // PEER_KEYS
    ne = PEER_EXPERTS // te
    u_spec = lambda f: pl.BlockSpec((te, D_MODEL), lambda i, k: (f(k), 0))
    vt_spec = lambda f: pl.BlockSpec((D_MODEL, te), lambda i, k: (0, f(k)))
    row_spec = pl.BlockSpec((PEER_HEADS, 2 * ipb, tm), lambda i, k: (0, k, i))
    full_spec = pl.BlockSpec((PEER_HEADS, PEER_KEYS, tm), lambda i, k: (0, 0, i))
    return pl.pallas_call(
        _peer_expert_body,
        grid=(t_tokens // tm, ne // 2),
        in_specs=[pl.BlockSpec((D_MODEL, tm), lambda i, k: (0, i)),
                  u_spec(lambda k: 2 * k),
                  u_spec(lambda k: 2 * k + 1),
                  vt_spec(lambda k: jnp.maximum(2 * k - 1, 0)),
                  vt_spec(lambda k: 2 * k),
                  vt_spec(lambda k: ne - 1),
                  row_spec, row_spec, full_spec, full_spec],
        out_specs=pl.BlockSpec((D_MODEL, tm), lambda i, k: (0, i)),
        out_shape=jax.ShapeDtypeStruct((D_MODEL, t_tokens), F32),
        scratch_shapes=[pltpu.VMEM((te, tm), BF16), pltpu.VMEM((te, tm), BF16)],
        compiler_params=_params(("parallel", "arbitrary"), 56),
        name="peer_experts",
    )(h2t, u_b, u_b, v_t, v_t, v_t, lrow, w1, r2, w2)


def _final_body(x1_ref, pt_ref, g2_ref, gn_ref, o_ref, *, apply_norm):
    y = x1_ref[...] + g2_ref[0] * pt_ref[...].T
    if apply_norm:
        ms = jnp.mean(y * y, axis=-1, keepdims=True)
        y = y * lax.rsqrt(ms + EPS) * gn_ref[...]
    o_ref[...] = y


def _final(x1, peer_t, gate2, g_final, seq, apply_norm):
    t_tokens = x1.shape[0]
    tm = TM_PROJ
    tps = seq // tm
    return pl.pallas_call(
        functools.partial(_final_body, apply_norm=apply_norm),
        grid=(t_tokens // tm,),
        in_specs=[pl.BlockSpec((tm, D_MODEL), lambda i: (i, 0)),
                  pl.BlockSpec((D_MODEL, tm), lambda i: (0, i)),
                  pl.BlockSpec((1, 1, D_MODEL), lambda i: (i // tps, 0, 0)),
                  pl.BlockSpec((1, D_MODEL), lambda i: (0, 0))],
        out_specs=pl.BlockSpec((tm, D_MODEL), lambda i: (i, 0)),
        out_shape=jax.ShapeDtypeStruct((t_tokens, D_MODEL), F32),
        compiler_params=_params(("parallel",), 32),
        name="final_norm",
    )(x1, peer_t, gate2, g_final)


def _split_cols(a, sizes):
    out, acc = [], 0
    for s in sizes:
        out.append(a[:, acc:acc + s])
        acc += s
    return out


def _inproj_weights(w_in):
    kvw = KV_GROUPS * HEAD_DIM
    sizes = (1024,) + (kvw,) * 6 + (3 * N_HEADS, 1024, 1024, 1024, 1024)
    q_a, k_c, v_c, k_s, v_s, k_w, v_w, g_a, q_r, k_r, v_r, g_r = _split_cols(w_in, sizes)
    d = w_in.shape[0]

    def grp(a, g):
        return a[:, g * HEAD_DIM:(g + 1) * HEAD_DIM]

    zeros = jnp.zeros((d, HEAD_DIM), w_in.dtype)
    cv = [jnp.concatenate([grp(k_c, g), grp(v_c, g)], axis=1) for g in range(KV_GROUPS)]
    ksp = [jnp.concatenate([grp(k_s, g), zeros], axis=1) for g in range(KV_GROUPS)]
    kwp = [jnp.concatenate([grp(k_w, g), zeros], axis=1) for g in range(KV_GROUPS)]
    w_std = jnp.concatenate(cv + ksp + kwp + [q_r, k_r, v_r, g_r], axis=1).astype(BF16)
    gcols = []
    for g in range(KV_GROUPS):
        for br in range(3):
            for r in range(GROUP):
                c = (g * GROUP + r) * 3 + br
                gcols.append(g_a[:, c:c + 1])
        gcols.append(jnp.zeros((d, 4), w_in.dtype))
    w_tr = jnp.concatenate([q_a, v_s, v_w] + gcols, axis=1).T.astype(BF16)
    return w_std, w_tr


def _compress_weights(pe_k, pe_v, k_w1, k_w2, v_w1, v_w2):
    half = CMP_LEN // 2

    def w1_half(w1k, w1v, lo):
        a = w1k.reshape(CMP_LEN, HEAD_DIM, CMP_HIDDEN)[lo:lo + half]
        b = w1v.reshape(CMP_LEN, HEAD_DIM, CMP_HIDDEN)[lo:lo + half]
        za = jnp.zeros_like(a)
        top = jnp.concatenate([a, za], axis=2)
        bot = jnp.concatenate([za, b], axis=2)
        return jnp.concatenate([top, bot], axis=1).reshape(half * 2 * HEAD_DIM, 2 * CMP_HIDDEN).astype(BF16)

    def pe_half(lo):
        row = jnp.concatenate([pe_k[lo:lo + half], pe_v[lo:lo + half]], axis=1).reshape(1, -1)
        return jnp.broadcast_to(row, (8, row.shape[1])).astype(BF16)

    wa = w1_half(k_w1, v_w1, 0)
    wb = w1_half(k_w1, v_w1, half)
    w2k = jnp.zeros((2 * CMP_HIDDEN, LANES), F32).at[:CMP_HIDDEN, :HEAD_DIM].set(k_w2).astype(BF16)
    w2vt = jnp.zeros((HEAD_DIM, 2 * CMP_HIDDEN), F32).at[:, CMP_HIDDEN:].set(v_w2.T).astype(BF16)
    return wa, wb, pe_half(0), pe_half(half), w2k, w2vt


def _nsa_constants(seq):
    slopes = jnp.exp2(-8.0 * (jnp.arange(N_HEADS, dtype=F32) + 1.0) / N_HEADS) * LOG2E
    s_hi = slopes.astype(BF16)
    s_lo = (slopes - s_hi.astype(F32)).astype(BF16)
    rows = jnp.zeros((N_HEADS, HEAD_DIM), BF16)
    rows = rows.at[:, 0].set(s_hi).at[:, 1].set(s_hi).at[:, 2].set(s_lo).at[:, 3].set(s_lo)
    qaug = jnp.broadcast_to(rows[:, :, None], (N_HEADS, HEAD_DIM, LANES))
    n_rows = seq // CMP_STRIDE
    n_slc = seq // SLC_BLOCK
    start = np.arange(n_rows)[:, None] * CMP_STRIDE
    end = start + CMP_LEN - 1
    blk = np.arange(n_slc)[None, :] * SLC_BLOCK
    ovl = ((start < blk + SLC_BLOCK) & (end >= blk)).astype(np.float32)
    return qaug, jnp.asarray(ovl.T, BF16)


def _retention_constants():
    h, c = RET_HEADS, RET_CHUNK
    lg = jnp.log1p(-jnp.exp2(-5.0 - jnp.arange(h, dtype=F32)))
    pos = jnp.arange(c, dtype=F32)
    diff = pos[:, None] - pos[None, :]
    scale = RET_DIM ** -0.5
    dm = jnp.where(diff >= 0, jnp.exp(lg[:, None, None] * jnp.maximum(diff, 0.0)), 0.0) * scale
    k_decay = jnp.exp(lg[:, None] * (c - 1.0 - pos)) * scale
    q_decay = jnp.exp(lg[:, None] * (pos + 1.0))
    chunk_decay = jnp.exp(lg * c)
    kd = jnp.broadcast_to(k_decay[:, :, None], (h, c, RET_DIM))
    qd = jnp.broadcast_to(q_decay[:, :, None], (h, c, RET_DIM))
    cd = jnp.broadcast_to(chunk_decay[:, None, None], (h, RET_DIM, RET_DIM))
    return dm, kd, qd, cd


def kernel(x, c, w_ada, b_ada, g_norm_mix, g_norm_ffn, g_norm_final, w_in, cmp_pe_k, cmp_pe_v,
           cmp_k_w1, cmp_k_w2, cmp_v_w1, cmp_v_w2, g_nsa_out, g_ret_out, w_out,
           peer_w_q, peer_sub_keys, peer_u, peer_v):
    batch, seq, d = x.shape
    depth = w_ada.shape[0]
    t_tokens = batch * seq
    xf = x.reshape(t_tokens, d)
    c_pad = jnp.zeros((8, d), F32).at[:batch].set(c)
    qaug, ovl_t = _nsa_constants(seq)
    dm, kd, qd, cd = _retention_constants()

    for l in range(depth):
        mod = _adaln(c_pad, w_ada[l], b_ada[l][None, :])[:batch].reshape(batch, 6, 1, d)
        shift1, scale1, gate1, shift2, scale2, gate2 = (mod[:, k] for k in range(6))

        w_std, w_tr = _inproj_weights(w_in[l])
        (cv, ks, kw, q_r, k_r, v_r, g_r, qt, vst, vwt, gt) = _inproj(
            xf, scale1, shift1, g_norm_mix[l][None, :], w_std, w_tr, seq)

        cv4 = cv.reshape(KV_GROUPS, batch, seq // CMP_STRIDE, CMP_STRIDE * LANES)
        kcp, vct = _compress(cv4, *_compress_weights(cmp_pe_k[l], cmp_pe_v[l], cmp_k_w1[l], cmp_k_w2[l],
                                                     cmp_v_w1[l], cmp_v_w2[l]))
        gout_b = jnp.broadcast_to(g_nsa_out[l][:, :, None], (N_HEADS, HEAD_DIM, LANES))
        o_nsa = _nsa(qt, gt, kcp, vct, ks, kw, vst, vwt, qaug, gout_b, ovl_t, batch, seq)

        go = jnp.broadcast_to(g_ret_out[l][:, None, :], (RET_HEADS, 8, RET_DIM))
        o_ret = _retention(q_r, k_r, v_r, g_r, dm, kd, qd, cd, go, batch, seq)

        sub_keys = peer_sub_keys[l].reshape(2 * PEER_HEADS, PEER_KEYS, -1).astype(BF16)
        x1, h2, st = _mid(o_nsa, o_ret, xf, gate1, scale2, shift2, g_norm_ffn[l][None, :],
                          w_out[l].astype(BF16), peer_w_q[l].T.astype(BF16), sub_keys, seq)

        lrow, w1, r2, w2 = _peer_select(st)
        peer_t = _peer_expert(h2, peer_u[l].astype(BF16), _transpose_bf16(peer_v[l]), lrow, w1, r2, w2)
        xf = _final(x1, peer_t, gate2, g_norm_final[None, :], seq, apply_norm=(l == depth - 1))
    return xf.reshape(batch, seq, d)
```

```python
import functools
import math

import numpy as np
import jax
import jax.numpy as jnp
from jax import lax
from jax.experimental import pallas as pl
from jax.experimental.pallas import tpu as pltpu

F32 = jnp.float32
BF16 = jnp.bfloat16

D_MODEL = 2048
N_HEADS = 16
HEAD_DIM = 64
KV_GROUPS = 4
GROUP = 4
CMP_LEN = 32
CMP_STRIDE = 16
CMP_HIDDEN = 128
SLC_BLOCK = 64
SLC_TOPK = 16
WINDOW = 512
FORCE_SCORE = 1e4
NEG = -1e30
RET_HEADS = 8
RET_DIM = 128
RET_CHUNK = 128
PEER_HEADS = 8
PEER_KEYS = 128
PEER_EXPERTS = PEER_KEYS * PEER_KEYS
PEER_TOPK = 16
EPS = 1e-6
LOG2E = 1.4426950408889634

LANES = 128
TQ = 128
TK_SEL = 256
TM_PROJ = 256
TM_PEER = 512
TE_PEER = 512
TM_SEL = 256
RET_TILE = 512

_NT = (((1,), (1,)), ((), ()))
_TN = (((0,), (0,)), ((), ()))


def _params(sem, vmem_mb):
    return pltpu.CompilerParams(dimension_semantics=sem, vmem_limit_bytes=vmem_mb * 1024 * 1024)


def _resident(shape, index_map):
    return pl.BlockSpec(shape, index_map, pipeline_mode=pl.Buffered(1))


def _gelu(x):
    return jax.nn.gelu(x)


def _adaln_body(c_ref, w_ref, b_ref, o_ref):
    c = c_ref[...]
    act = (c * jax.nn.sigmoid(c)).astype(BF16)
    o_ref[...] = jnp.dot(act, w_ref[...].astype(BF16), preferred_element_type=F32) + b_ref[...]


def _adaln(c_pad, w, b):
    n = w.shape[1]
    tn = 1536
    return pl.pallas_call(
        _adaln_body,
        grid=(n // tn,),
        in_specs=[pl.BlockSpec((8, D_MODEL), lambda j: (0, 0)),
                  pl.BlockSpec((D_MODEL, tn), lambda j: (0, j)),
                  pl.BlockSpec((1, tn), lambda j: (0, j))],
        out_specs=pl.BlockSpec((8, tn), lambda j: (0, j)),
        out_shape=jax.ShapeDtypeStruct((8, n), F32),
        compiler_params=_params(("arbitrary",), 40),
        name="adaln",
    )(c_pad, w, b)


STD_COLS = 512 * 3 + 1024 * 4
TR_ROWS = 1024 + 256 + 256 + 64


def _inproj_body(x_ref, sc_ref, sh_ref, gn_ref, wstd_ref, wt_ref,
                 cv_ref, ks_ref, kw_ref, qr_ref, kr_ref, vr_ref, gr_ref,
                 qt_ref, vst_ref, vwt_ref, gt_ref, *, tiles_per_seq):
    tm = TM_PROJ
    i = pl.program_id(0)
    x = x_ref[...]
    ms = jnp.mean(x * x, axis=-1, keepdims=True)
    h = x * lax.rsqrt(ms + EPS) * gn_ref[...]
    h = h * (1.0 + sc_ref[0]) + sh_ref[0]
    hb = h.astype(BF16)

    def std(a, b):
        return jnp.dot(hb, wstd_ref[:, a:b], preferred_element_type=F32)

    y = std(0, 512)
    for g in range(KV_GROUPS):
        cv_ref[g] = y[:, g * LANES:(g + 1) * LANES].astype(BF16)

    t = (i % tiles_per_seq) * tm + lax.broadcasted_iota(jnp.int32, (tm, LANES), 0)
    lane = lax.broadcasted_iota(jnp.int32, (tm, LANES), 1)
    pos_hi = ((t >> 6) << 6).astype(F32)
    pos_lo = (t & 63).astype(F32)
    aug = jnp.where((lane == 64) | (lane == 66), pos_hi,
                    jnp.where((lane == 65) | (lane == 67), pos_lo, 0.0))
    for ref, off in ((ks_ref, 512), (kw_ref, 1024)):
        y = std(off, off + 512)
        for g in range(KV_GROUPS):
            ref[:, g * LANES:(g + 1) * LANES] = (y[:, g * LANES:(g + 1) * LANES] + aug).astype(BF16)

    for ref, off in ((qr_ref, 1536), (kr_ref, 2560), (vr_ref, 3584), (gr_ref, 4608)):
        ref[...] = std(off, off + 1024).astype(BF16)

    def tr(a, b):
        return lax.dot_general(wt_ref[a:b, :], hb, _NT, preferred_element_type=F32)

    qt = tr(0, 1024) * (HEAD_DIM ** -0.5 * LOG2E)
    vst = tr(1024, 1280)
    vwt = tr(1280, 1536)
    gt = tr(1536, 1600)
    for c in range(tm // LANES):
        sl = slice(c * LANES, (c + 1) * LANES)
        qt_ref[c] = qt[:, sl].astype(BF16)
        vst_ref[c] = vst[:, sl].astype(BF16)
        vwt_ref[c] = vwt[:, sl].astype(BF16)
        gt_ref[c] = gt[:, sl]


def _inproj(x2, scale1, shift1, g_mix, w_std, w_tr, seq):
    t_tokens = x2.shape[0]
    tm = TM_PROJ
    tps = seq // tm
    nt = t_tokens // tm
    c = tm // LANES
    row = lambda i: (i, 0)
    per_b = lambda i: (i // tps, 0, 0)
    out_shape = (
        jax.ShapeDtypeStruct((KV_GROUPS, t_tokens, LANES), BF16),
        jax.ShapeDtypeStruct((t_tokens, 512), BF16),
        jax.ShapeDtypeStruct((t_tokens, 512), BF16),
        jax.ShapeDtypeStruct((t_tokens, 1024), BF16),
        jax.ShapeDtypeStruct((t_tokens, 1024), BF16),
        jax.ShapeDtypeStruct((t_tokens, 1024), BF16),
        jax.ShapeDtypeStruct((t_tokens, 1024), BF16),
        jax.ShapeDtypeStruct((t_tokens // LANES, 1024, LANES), BF16),
        jax.ShapeDtypeStruct((t_tokens // LANES, 256, LANES), BF16),
        jax.ShapeDtypeStruct((t_tokens // LANES, 256, LANES), BF16),
        jax.ShapeDtypeStruct((t_tokens // LANES, 64, LANES), F32),
    )
    out_specs = (
        pl.BlockSpec((KV_GROUPS, tm, LANES), lambda i: (0, i, 0)),
        pl.BlockSpec((tm, 512), row),
        pl.BlockSpec((tm, 512), row),
        pl.BlockSpec((tm, 1024), row),
        pl.BlockSpec((tm, 1024), row),
        pl.BlockSpec((tm, 1024), row),
        pl.BlockSpec((tm, 1024), row),
        pl.BlockSpec((c, 1024, LANES), lambda i: (i, 0, 0)),
        pl.BlockSpec((c, 256, LANES), lambda i: (i, 0, 0)),
        pl.BlockSpec((c, 256, LANES), lambda i: (i, 0, 0)),
        pl.BlockSpec((c, 64, LANES), lambda i: (i, 0, 0)),
    )
    return pl.pallas_call(
        functools.partial(_inproj_body, tiles_per_seq=tps),
        grid=(nt,),
        in_specs=[pl.BlockSpec((tm, D_MODEL), row),
                  pl.BlockSpec((1, 1, D_MODEL), per_b),
                  pl.BlockSpec((1, 1, D_MODEL), per_b),
                  _resident((1, D_MODEL), lambda i: (0, 0)),
                  _resident((D_MODEL, STD_COLS), lambda i: (0, 0)),
                  _resident((TR_ROWS, D_MODEL), lambda i: (0, 0))],
        out_specs=out_specs,
        out_shape=out_shape,
        compiler_params=_params(("parallel",), 56),
        name="inproj",
    )(x2, scale1, shift1, g_mix, w_std, w_tr)


def _compress_body(x_ref, wa_ref, wb_ref, pea_ref, peb_ref, w2k_ref, w2vt_ref, kcp_ref, vct_ref):
    x = x_ref[0, 0]
    n_rows = x.shape[0]
    p = jnp.dot(x, wa_ref[...], preferred_element_type=F32)
    q = jnp.dot(x, wb_ref[...], preferred_element_type=F32)
    pe = (jnp.dot(pea_ref[...], wa_ref[...], preferred_element_type=F32)
          + jnp.dot(peb_ref[...], wb_ref[...], preferred_element_type=F32))[0:1, :]
    pre = p + pltpu.roll(q, n_rows - 1, 0) + pe
    hid = _gelu(pre).astype(BF16)
    kc = jnp.dot(hid, w2k_ref[...], preferred_element_type=F32)
    n = lax.broadcasted_iota(jnp.int32, (n_rows, LANES), 0)
    lane = lax.broadcasted_iota(jnp.int32, (n_rows, LANES), 1)
    ce = n * CMP_STRIDE + (CMP_LEN - 1)
    ce_hi = ((ce >> 6) << 6).astype(F32)
    ce_lo = (ce & 63).astype(F32)
    aug = jnp.where((lane == 64) | (lane == 66), ce_hi,
                    jnp.where((lane == 65) | (lane == 67), ce_lo, 0.0))
    kcp_ref[0] = (kc + aug).astype(BF16)
    vct_ref[0] = lax.dot_general(w2vt_ref[...], hid, _NT, preferred_element_type=F32).astype(BF16)


def _compress(cv4, wa, wb, pea, peb, w2k, w2vt):
    g_, b_, n_rows, _ = cv4.shape
    const2 = lambda n: (0, 0)
    return pl.pallas_call(
        _compress_body,
        grid=(b_ * g_,),
        in_specs=[pl.BlockSpec((1, 1, n_rows, 2048), lambda n: (n % KV_GROUPS, n // KV_GROUPS, 0, 0)),
                  pl.BlockSpec((2048, 256), const2),
                  pl.BlockSpec((2048, 256), const2),
                  pl.BlockSpec((8, 2048), const2),
                  pl.BlockSpec((8, 2048), const2),
                  pl.BlockSpec((256, LANES), const2),
                  pl.BlockSpec((64, 256), const2)],
        out_specs=(pl.BlockSpec((1, n_rows, LANES), lambda n: (n, 0, 0)),
                   pl.BlockSpec((1, 64, n_rows), lambda n: (n, 0, 0))),
        out_shape=(jax.ShapeDtypeStruct((b_ * g_, n_rows, LANES), BF16),
                   jax.ShapeDtypeStruct((b_ * g_, 64, n_rows), BF16)),
        compiler_params=_params(("parallel",), 32),
        name="nsa_compress",
    )(cv4, wa, wb, pea, peb, w2k, w2vt)


def _softmax_step(state, s, pv_prev):
    m_i, l_i, acc = state
    m_new = jnp.maximum(m_i, jnp.max(s, axis=0, keepdims=True))
    alpha = jnp.exp2(m_i - m_new)
    p = jnp.exp2(s - m_new)
    l_new = alpha * l_i + jnp.sum(p, axis=0, keepdims=True)
    return (m_new, l_new, (acc + pv_prev) * alpha), p.astype(BF16)


def _nsa_body(qt_ref, gt_ref, kcp_ref, vct_ref, ks_ref, kw_ref, vst_ref, vwt_ref,
              qaug_ref, gout_ref, ovl_ref, grp_ref, o_ref, selb_ref, sa_ref, sb_ref, pa_ref, pb_ref, *, n_cmp):
    qi = pl.program_id(2)
    t0 = qi * TQ
    wq = GROUP * TQ

    qp = jnp.concatenate(
        [jnp.concatenate([qt_ref[0, r * HEAD_DIM:(r + 1) * HEAD_DIM, :], qaug_ref[r]], axis=0)
         for r in range(GROUP)], axis=1)

    s = jnp.dot(kcp_ref[0], qp, preferred_element_type=F32)
    n_io = lax.broadcasted_iota(jnp.int32, (n_cmp, wq), 0)
    tl = lax.broadcasted_iota(jnp.int32, (n_cmp, wq), 1) & (TQ - 1)
    valid = (n_io * CMP_STRIDE + (CMP_LEN - 1)) <= (t0 + tl)
    s = jnp.where(valid, s, NEG)
    m = jnp.max(s, axis=0, keepdims=True)
    p = jnp.where(valid, jnp.exp2(s - m), 0.0)
    l = jnp.sum(p, axis=0, keepdims=True)
    pn = p * (1.0 / jnp.maximum(l, 1e-30))
    o_c = jnp.dot(vct_ref[0], pn.astype(BF16), preferred_element_type=F32)

    ps = pn[:, 0:TQ] + pn[:, TQ:2 * TQ] + pn[:, 2 * TQ:3 * TQ] + pn[:, 3 * TQ:4 * TQ]
    hi = ps.astype(BF16)
    r1 = ps - hi.astype(F32)
    mid = r1.astype(BF16)
    lo = (r1 - mid.astype(F32)).astype(BF16)
    ovl = ovl_ref[...]
    imp = (jnp.dot(ovl, hi, preferred_element_type=F32)
           + jnp.dot(ovl, mid, preferred_element_type=F32)
           + jnp.dot(ovl, lo, preferred_element_type=F32))

    n_slc = imp.shape[0]
    m_io = lax.broadcasted_iota(jnp.int32, (n_slc, TQ), 0)
    q_io = lax.broadcasted_iota(jnp.int32, (n_slc, TQ), 1)
    back = ((t0 + q_io) >> 6) - m_io
    valid_s = back >= 0
    forced = valid_s & ((m_io == 0) | (back < 2))
    w = jnp.where(forced, FORCE_SCORE, jnp.where(valid_s, imp, -1.0))

    def pick(_, carry):
        w, selb = carry
        mx = jnp.max(w, axis=0, keepdims=True)
        idx = jnp.min(jnp.where(w == mx, m_io, n_slc), axis=0, keepdims=True)
        hit = m_io == idx
        return jnp.where(hit, -jnp.inf, w), jnp.where(hit, 0.0, selb)

    _, selb = lax.fori_loop(0, SLC_TOPK, pick, (w, jnp.full((n_slc, TQ), NEG, F32)))
    for mblk in range(n_slc):
        selb_ref[mblk] = jnp.broadcast_to(selb[mblk:mblk + 1, :], (8, TQ))

    lane_q = lax.broadcasted_iota(jnp.int32, (1, wq), 1) & (TQ - 1)

    n_win = WINDOW // TQ + 1
    k_lo = pl.multiple_of(jnp.maximum(t0 - WINDOW, 0), TQ)
    s = jnp.dot(kw_ref[pl.ds(k_lo, n_win * TQ), :], qp, preferred_element_type=F32)
    d = (t0 - k_lo) + lane_q - lax.broadcasted_iota(jnp.int32, (n_win * TQ, wq), 0)
    s = jnp.where((d >= 0) & (d < WINDOW), s, NEG)
    p = jnp.exp2(s - jnp.max(s, axis=0, keepdims=True))
    l_w = jnp.sum(p, axis=0, keepdims=True)
    vt_w = jnp.concatenate([vwt_ref[k_lo // TQ + u] for u in range(n_win)], axis=1)
    o_w = jnp.dot(vt_w, p.astype(BF16), preferred_element_type=F32) * (1.0 / l_w)

    per_blk = TK_SEL // SLC_BLOCK

    def scores(j):
        k0 = pl.multiple_of(j * TK_SEL, TK_SEL)
        bias = jnp.concatenate(
            [jnp.tile(selb_ref[j * per_blk + u], (SLC_BLOCK // 8, 1)) for u in range(per_blk)], axis=0)
        bias = jnp.concatenate([bias] * GROUP, axis=1)
        return jnp.dot(ks_ref[pl.ds(k0, TK_SEL), :], qp, preferred_element_type=F32) + bias

    def vt_tile(j):
        return jnp.concatenate([vst_ref[2 * j], vst_ref[2 * j + 1]], axis=1)

    def causal(j):
        kr = lax.broadcasted_iota(jnp.int32, (TK_SEL, wq), 0)
        return kr <= (t0 - j * TK_SEL) + lane_q

    def pv_dot(j, p_ref):
        return jnp.dot(vt_tile(j), p_ref[...], preferred_element_type=F32)

    jd = qi // (TK_SEL // TQ)

    sel01 = jnp.where(selb == 0.0, 1.0, 0.0).astype(BF16)
    tile_any = jnp.max(jnp.dot(grp_ref[...], sel01, preferred_element_type=F32), axis=1, keepdims=True)
    j_io = lax.broadcasted_iota(jnp.int32, tile_any.shape, 0)
    q_lo = jnp.max(jnp.where((tile_any == 0.0) & (j_io <= jd), j_io, -1)) + 1
    n_pre = jnp.max(jnp.where((tile_any > 0.0) & (j_io < q_lo), j_io, -1)) + 1
    n_vis = n_pre + jd - q_lo + 1

    def tile_at(pos):
        return jnp.where(pos < n_pre, pos, pos - n_pre + q_lo)

    state = (jnp.full((1, wq), NEG, F32), jnp.zeros((1, wq), F32), jnp.zeros((HEAD_DIM, wq), F32))
    sa_ref[...] = scores(tile_at(0))
    pb_ref[...] = jnp.zeros_like(pb_ref)

    def pair(u, state):
        a = 2 * u
        sb_ref[...] = scores(tile_at(a + 1))
        state, p = _softmax_step(state, sa_ref[...], pv_dot(tile_at(jnp.maximum(a - 1, 0)), pb_ref))
        pa_ref[...] = p
        sa_ref[...] = scores(tile_at(a + 2))
        state, p = _softmax_step(state, sb_ref[...], pv_dot(tile_at(a), pa_ref))
        pb_ref[...] = p
        return state

    n_pair = (n_vis - 1) // 2
    state = lax.fori_loop(0, n_pair, pair, state)
    x = 2 * n_pair
    y = jnp.minimum(x + 1, n_vis - 1)
    tx, ty = tile_at(x), tile_at(y)
    sb_ref[...] = scores(ty)
    state, p = _softmax_step(state, jnp.where(causal(tx), sa_ref[...], NEG),
                             pv_dot(tile_at(jnp.maximum(x - 1, 0)), pb_ref))
    pa_ref[...] = p
    ty_mask = jnp.where(x + 1 < n_vis, ty, jd + 1)
    (_, l_s, acc_s), p = _softmax_step(state, jnp.where(causal(ty_mask), sb_ref[...], NEG), pv_dot(tx, pa_ref))
    acc_s = acc_s + jnp.dot(vt_tile(ty), p, preferred_element_type=F32)

    o_s = acc_s * (1.0 / l_s)
    gw = jax.nn.sigmoid(gt_ref[0])
    outs = []
    for r in range(GROUP):
        sl = slice(r * TQ, (r + 1) * TQ)
        o = (gw[r:r + 1, :] * o_c[:, sl] + gw[GROUP + r:GROUP + r + 1, :] * o_s[:, sl]
             + gw[2 * GROUP + r:2 * GROUP + r + 1, :] * o_w[:, sl])
        ms = jnp.mean(o * o, axis=0, keepdims=True)
        outs.append(o * lax.rsqrt(ms + EPS) * gout_ref[r])
    o_ref[...] = jnp.concatenate(outs, axis=0).T.astype(BF16)


def _nsa(qt, gt, kcp, vct, ks, kw, vst, vwt, qaug, gout_b, ovl_t, grp, batch, seq):
    nq = seq // TQ
    n_cmp = kcp.shape[1]
    n_slc = seq // SLC_BLOCK
    t_tokens = batch * seq
    per_b_chunks = seq // LANES
    return pl.pallas_call(
        functools.partial(_nsa_body, n_cmp=n_cmp),
        grid=(batch, KV_GROUPS, nq),
        in_specs=[
            pl.BlockSpec((1, 256, LANES), lambda b, g, q: (b * nq + q, g, 0)),
            pl.BlockSpec((1, 16, LANES), lambda b, g, q: (b * nq + q, g, 0)),
            pl.BlockSpec((1, n_cmp, LANES), lambda b, g, q: (b * KV_GROUPS + g, 0, 0)),
            pl.BlockSpec((1, 64, n_cmp), lambda b, g, q: (b * KV_GROUPS + g, 0, 0)),
            pl.BlockSpec((seq, LANES), lambda b, g, q: (b, g)),
            pl.BlockSpec((seq, LANES), lambda b, g, q: (b, g)),
            pl.BlockSpec((per_b_chunks, 64, LANES), lambda b, g, q: (b, g, 0)),
            pl.BlockSpec((per_b_chunks, 64, LANES), lambda b, g, q: (b, g, 0)),
            pl.BlockSpec((GROUP, 64, LANES), lambda b, g, q: (g, 0, 0)),
            pl.BlockSpec((GROUP, 64, LANES), lambda b, g, q: (g, 0, 0)),
            pl.BlockSpec((n_slc, n_cmp), lambda b, g, q: (0, 0)),
            pl.BlockSpec(grp.shape, lambda b, g, q: (0, 0)),
        ],
        out_specs=pl.BlockSpec((TQ, 256), lambda b, g, q: (b * nq + q, g)),
        out_shape=jax.ShapeDtypeStruct((t_tokens, 1024), BF16),
        scratch_shapes=[pltpu.VMEM((n_slc, 8, TQ), F32),
                        pltpu.VMEM((TK_SEL, GROUP * TQ), F32), pltpu.VMEM((TK_SEL, GROUP * TQ), F32),
                        pltpu.VMEM((TK_SEL, GROUP * TQ), BF16), pltpu.VMEM((TK_SEL, GROUP * TQ), BF16)],
        compiler_params=_params(("parallel", "parallel", "arbitrary"), 40),
        name="nsa_attention",
    )(qt, gt, kcp, vct, ks, kw, vst, vwt, qaug, gout_b, ovl_t, grp)


def _ret_body(q_ref, k_ref, v_ref, g_ref, dm_ref, kd_ref, qd_ref, cd_ref, go_ref, o_ref, st_ref):
    @pl.when(pl.program_id(2) == 0)
    def _():
        st_ref[...] = jnp.zeros_like(st_ref)

    c_ = RET_CHUNK
    for c in range(RET_TILE // c_):
        sl = slice(c * c_, (c + 1) * c_)
        q = q_ref[sl, :]
        k = k_ref[sl, :]
        v = v_ref[sl, :]
        att = lax.dot_general(q, k, _NT, preferred_element_type=F32) * dm_ref[0]
        state = st_ref[...]
        o = (jnp.dot(att.astype(BF16), v, preferred_element_type=F32)
             + qd_ref[0] * jnp.dot(q, state.astype(BF16), preferred_element_type=F32))
        kdec = (k.astype(F32) * kd_ref[0]).astype(BF16)
        kv = lax.dot_general(kdec, v, _TN, preferred_element_type=F32)
        st_ref[...] = state * cd_ref[0] + kv
        mu = jnp.mean(o, axis=-1, keepdims=True)
        oc = o - mu
        var = jnp.mean(oc * oc, axis=-1, keepdims=True)
        y = oc * lax.rsqrt(var + EPS) * go_ref[0, 0:1, :]
        gate = g_ref[sl, :].astype(F32)
        o_ref[sl, :] = (gate * jax.nn.sigmoid(gate) * y).astype(BF16)


def _retention(q_r, k_r, v_r, g_r, dm, kd, qd, cd, go, batch, seq):
    t_tokens = batch * seq
    nc = seq // RET_TILE
    tok = lambda b, h, c: (b * nc + c, h)
    per_h = lambda b, h, c: (h, 0, 0)
    sq = (1, RET_DIM, RET_DIM)
    return pl.pallas_call(
        _ret_body,
        grid=(batch, RET_HEADS, nc),
        in_specs=[pl.BlockSpec((RET_TILE, RET_DIM), tok)] * 4
        + [pl.BlockSpec(sq, per_h)] * 4 + [pl.BlockSpec((1, 8, RET_DIM), per_h)],
        out_specs=pl.BlockSpec((RET_TILE, RET_DIM), tok),
        out_shape=jax.ShapeDtypeStruct((t_tokens, RET_HEADS * RET_DIM), BF16),
        scratch_shapes=[pltpu.VMEM((RET_DIM, RET_DIM), F32)],
        compiler_params=_params(("parallel", "parallel", "arbitrary"), 32),
        name="retention",
    )(q_r, k_r, v_r, g_r, dm, kd, qd, cd, go)


def _mid_body(on_ref, or_ref, x_ref, g1_ref, sc_ref, sh_ref, gn_ref, wo_ref, wqt_ref, sk_ref,
              x1_ref, h2t_ref, st_ref):
    acc = (jnp.dot(on_ref[...], wo_ref[0:1024, :], preferred_element_type=F32)
           + jnp.dot(or_ref[...], wo_ref[1024:2048, :], preferred_element_type=F32))
    x1 = x_ref[...] + g1_ref[0] * acc
    x1_ref[...] = x1
    ms = jnp.mean(x1 * x1, axis=-1, keepdims=True)
    h2 = x1 * lax.rsqrt(ms + EPS) * gn_ref[...]
    h2 = h2 * (1.0 + sc_ref[0]) + sh_ref[0]
    h2t_ref[...] = h2.T.astype(BF16)
    h2 = h2.astype(BF16)
    qt = lax.dot_general(wqt_ref[...], h2, _NT, preferred_element_type=F32).astype(BF16)
    for hp in range(2 * PEER_HEADS):
        st_ref[hp] = jnp.dot(sk_ref[hp], qt[hp * 128:(hp + 1) * 128, :], preferred_element_type=F32)


def _mid(o_nsa, o_ret, x2, gate1, scale2, shift2, g_ffn, w_out, wq_t, sub_keys, seq):
    t_tokens = x2.shape[0]
    tm = TM_PROJ
    tps = seq // tm
    row = lambda i: (i, 0)
    per_b = lambda i: (i // tps, 0, 0)
    return pl.pallas_call(
        _mid_body,
        grid=(t_tokens // tm,),
        in_specs=[pl.BlockSpec((tm, 1024), row),
                  pl.BlockSpec((tm, 1024), row),
                  pl.BlockSpec((tm, D_MODEL), row),
                  pl.BlockSpec((1, 1, D_MODEL), per_b),
                  pl.BlockSpec((1, 1, D_MODEL), per_b),
                  pl.BlockSpec((1, 1, D_MODEL), per_b),
                  _resident((1, D_MODEL), lambda i: (0, 0)),
                  _resident((D_MODEL, D_MODEL), lambda i: (0, 0)),
                  _resident((D_MODEL, D_MODEL), lambda i: (0, 0)),
                  _resident((2 * PEER_HEADS, PEER_KEYS, 128), lambda i: (0, 0, 0))],
        out_specs=(pl.BlockSpec((tm, D_MODEL), row),
                   pl.BlockSpec((D_MODEL, tm), lambda i: (0, i)),
                   pl.BlockSpec((2 * PEER_HEADS, PEER_KEYS, tm), lambda i: (0, 0, i))),
        out_shape=(jax.ShapeDtypeStruct((t_tokens, D_MODEL), F32),
                   jax.ShapeDtypeStruct((D_MODEL, t_tokens), BF16),
                   jax.ShapeDtypeStruct((2 * PEER_HEADS, PEER_KEYS, t_tokens), F32)),
        compiler_params=_params(("parallel",), 48),
        name="outproj_peerq",
    )(o_nsa, o_ret, x2, gate1, scale2, shift2, g_ffn, w_out, wq_t, sub_keys)


def _top16(s):
    n_rows, n = s.shape
    io = lax.broadcasted_iota(jnp.int32, (n_rows, n), 0)
    a_io = lax.broadcasted_iota(jnp.int32, (PEER_TOPK, n), 0)
    rank = jnp.full((n_rows, n), PEER_TOPK, jnp.int32)
    vals = jnp.zeros((PEER_TOPK, n), F32)
    for a in range(PEER_TOPK):
        mx = jnp.max(s, axis=0, keepdims=True)
        idx = jnp.min(jnp.where(s == mx, io, n_rows), axis=0, keepdims=True)
        hit = io == idx
        rank = jnp.where(hit, a, rank)
        s = jnp.where(hit, -jnp.inf, s)
        vals = jnp.where(a_io == a, mx, vals)
    return vals, rank


def _peer_select_body(s_ref, l_ref, w1_ref, r2_ref, w2_ref):
    s1 = s_ref[0]
    s2 = s_ref[1]
    n = s1.shape[1]
    v1, rank1 = _top16(s1)
    v2, rank2 = _top16(s2)
    a_io = lax.broadcasted_iota(jnp.int32, (PEER_TOPK, n), 0)
    cnt = jnp.zeros((PEER_TOPK, n), jnp.int32)
    cur = v1 + v2[0:1, :]
    top = v1[0:1, :] + v2[0:1, :]
    z = jnp.zeros((1, n), F32)
    for _ in range(PEER_TOPK):
        mx = jnp.max(cur, axis=0, keepdims=True)
        aidx = jnp.min(jnp.where(cur == mx, a_io, PEER_TOPK), axis=0, keepdims=True)
        hit = a_io == aidx
        cnt = cnt + hit.astype(jnp.int32)
        nxt = jnp.sum(jnp.where(hit, cnt, 0), axis=0, keepdims=True)
        nv = jnp.max(jnp.where(a_io == nxt, v2, -jnp.inf), axis=0, keepdims=True)
        cur = jnp.where(hit, v1 + nv, cur)
        z = z + jnp.exp(mx - top)
    cnt_f = cnt.astype(F32)
    lrow = jnp.zeros(s1.shape, F32)
    for a in range(PEER_TOPK):
        lrow = jnp.where(rank1 == a, cnt_f[a:a + 1, :], lrow)
    l_ref[0] = lrow
    w1_ref[0] = jnp.exp(s1 - v1[0:1, :])
    r2_ref[0] = rank2.astype(F32).astype(BF16)
    w2_ref[0] = (jnp.exp(s2 - v2[0:1, :]) * (1.0 / z)).astype(BF16)


def _peer_select(st):
    t_tokens = st.shape[2]
    tm = TM_SEL
    shp = jax.ShapeDtypeStruct((PEER_HEADS, PEER_KEYS, t_tokens), F32)
    shp_b = jax.ShapeDtypeStruct((PEER_HEADS, PEER_KEYS, t_tokens), BF16)
    spec = pl.BlockSpec((1, PEER_KEYS, tm), lambda i, h: (h, 0, i))
    return pl.pallas_call(
        _peer_select_body,
        grid=(t_tokens // tm, PEER_HEADS),
        in_specs=[pl.BlockSpec((2, PEER_KEYS, tm), lambda i, h: (h, 0, i))],
        out_specs=(spec, spec, spec, spec),
        out_shape=(shp, shp, shp_b, shp_b),
        compiler_params=_params(("parallel", "parallel"), 32),
        name="peer_select",
    )(st)


def _transpose_body(v_ref, o_ref):
    o_ref[...] = v_ref[...].T.astype(BF16)


def _transpose_bf16(v):
    n, d = v.shape
    tn = 512
    return pl.pallas_call(
        _transpose_body,
        grid=(n // tn,),
        in_specs=[pl.BlockSpec((tn, d), lambda i: (i, 0))],
        out_specs=pl.BlockSpec((d, tn), lambda i: (0, i)),
        out_shape=jax.ShapeDtypeStruct((d, n), BF16),
        compiler_params=_params(("parallel",), 32),
        name="transpose_v",
    )(v)


def _peer_expert_body(h2t_ref, ua_ref, ub_ref, vtp_ref, vta_ref, vtl_ref,
                      l_ref, w1_ref, r2_ref, w2_ref, o_ref, c0_ref, c1_ref):
    k = pl.program_id(1)
    sw = 256
    strips = [slice(c * sw, (c + 1) * sw) for c in range(TM_PEER // sw)]
    ipb = TE_PEER // PEER_KEYS
    blk = D_MODEL // ipb

    def accumulate(j, vt_ref, c_ref):
        d0 = pl.multiple_of(j * blk, blk)
        for ls in strips:
            o_ref[pl.ds(d0, blk), ls] += jnp.dot(vt_ref[pl.ds(d0, blk), :], c_ref[:, ls],
                                                 preferred_element_type=F32)

    def coefficients(j, u_ref, row0, c_ref):
        r0 = pl.multiple_of(j * PEER_KEYS, PEER_KEYS)
        for ls in strips:
            coef = None
            for h in range(PEER_HEADS):
                lrow = jnp.broadcast_to(l_ref[h, pl.ds(row0 + j, 1), ls], (16, sw)).astype(BF16)
                w1row = jnp.broadcast_to(w1_ref[h, pl.ds(row0 + j, 1), ls], (16, sw)).astype(BF16)
                lrow = jnp.tile(lrow, (PEER_KEYS // 16, 1))
                w1row = jnp.tile(w1row, (PEER_KEYS // 16, 1))
                term = jnp.where(r2_ref[h, :, ls] < lrow, w2_ref[h, :, ls] * w1row, jnp.zeros((), BF16))
                coef = term if coef is None else coef + term
            a_t = jnp.dot(u_ref[pl.ds(r0, PEER_KEYS), :], h2t_ref[:, ls], preferred_element_type=F32)
            c_ref[pl.ds(r0, PEER_KEYS), ls] = coef * _gelu(a_t).astype(BF16)

    @pl.when(k == 0)
    def _():
        o_ref[...] = jnp.zeros_like(o_ref)
        c1_ref[...] = jnp.zeros_like(c1_ref)

    def half_a(j, carry):
        coefficients(j, ua_ref, 0, c0_ref)
        accumulate(j, vtp_ref, c1_ref)
        return carry

    def half_b(j, carry):
        coefficients(j, ub_ref, ipb, c1_ref)
        accumulate(j, vta_ref, c0_ref)
        return carry

    lax.fori_loop(0, ipb, half_a, 0)
    lax.fori_loop(0, ipb, half_b, 0)

    @pl.when(k == pl.num_programs(1) - 1)
    def _():
        lax.fori_loop(0, ipb, lambda j, c: (accumulate(j, vtl_ref, c1_ref), c)[1], 0)


def _peer_expert(h2t, u_b, v_t, lrow, w1, r2, w2):
    t_tokens = h2t.shape[1]
    tm, te = TM_PEER, TE_PEER
    ipb = te // PEER_KEYS
    ne = PEER_EXPERTS // te
    u_spec = lambda f: pl.BlockSpec((te, D_MODEL), lambda i, k: (f(k), 0))
    vt_spec = lambda f: pl.BlockSpec((D_MODEL, te), lambda i, k: (0, f(k)))
    row_spec = pl.BlockSpec((PEER_HEADS, 2 * ipb, tm), lambda i, k: (0, k, i))
    full_spec = pl.BlockSpec((PEER_HEADS, PEER_KEYS, tm), lambda i, k: (0, 0, i))
    return pl.pallas_call(
        _peer_expert_body,
        grid=(t_tokens // tm, ne // 2),
        in_specs=[pl.BlockSpec((D_MODEL, tm), lambda i, k: (0, i)),
                  u_spec(lambda k: 2 * k),
                  u_spec(lambda k: 2 * k + 1),
                  vt_spec(lambda k: jnp.maximum(2 * k - 1, 0)),
                  vt_spec(lambda k: 2 * k),
                  vt_spec(lambda k: ne - 1),
                  row_spec, row_spec, full_spec, full_spec],
        out_specs=pl.BlockSpec((D_MODEL, tm), lambda i, k: (0, i)),
        out_shape=jax.ShapeDtypeStruct((D_MODEL, t_tokens), F32),
        scratch_shapes=[pltpu.VMEM((te, tm), BF16), pltpu.VMEM((te, tm), BF16)],
        compiler_params=_params(("parallel", "arbitrary"), 56),
        name="peer_experts",
    )(h2t, u_b, u_b, v_t, v_t, v_t, lrow, w1, r2, w2)


def _final_body(x1_ref, pt_ref, g2_ref, gn_ref, o_ref, *, apply_norm):
    y = x1_ref[...] + g2_ref[0] * pt_ref[...].T
    if apply_norm:
        ms = jnp.mean(y * y, axis=-1, keepdims=True)
        y = y * lax.rsqrt(ms + EPS) * gn_ref[...]
    o_ref[...] = y


def _final(x1, peer_t, gate2, g_final, seq, apply_norm):
    t_tokens = x1.shape[0]
    tm = TM_PROJ
    tps = seq // tm
    return pl.pallas_call(
        functools.partial(_final_body, apply_norm=apply_norm),
        grid=(t_tokens // tm,),
        in_specs=[pl.BlockSpec((tm, D_MODEL), lambda i: (i, 0)),
                  pl.BlockSpec((D_MODEL, tm), lambda i: (0, i)),
                  pl.BlockSpec((1, 1, D_MODEL), lambda i: (i // tps, 0, 0)),
                  pl.BlockSpec((1, D_MODEL), lambda i: (0, 0))],
        out_specs=pl.BlockSpec((tm, D_MODEL), lambda i: (i, 0)),
        out_shape=jax.ShapeDtypeStruct((t_tokens, D_MODEL), F32),
        compiler_params=_params(("parallel",), 32),
        name="final_norm",
    )(x1, peer_t, gate2, g_final)


def _split_cols(a, sizes):
    out, acc = [], 0
    for s in sizes:
        out.append(a[:, acc:acc + s])
        acc += s
    return out


def _inproj_weights(w_in):
    kvw = KV_GROUPS * HEAD_DIM
    sizes = (1024,) + (kvw,) * 6 + (3 * N_HEADS, 1024, 1024, 1024, 1024)
    q_a, k_c, v_c, k_s, v_s, k_w, v_w, g_a, q_r, k_r, v_r, g_r = _split_cols(w_in, sizes)
    d = w_in.shape[0]

    def grp(a, g):
        return a[:, g * HEAD_DIM:(g + 1) * HEAD_DIM]

    zeros = jnp.zeros((d, HEAD_DIM), w_in.dtype)
    cv = [jnp.concatenate([grp(k_c, g), grp(v_c, g)], axis=1) for g in range(KV_GROUPS)]
    ksp = [jnp.concatenate([grp(k_s, g), zeros], axis=1) for g in range(KV_GROUPS)]
    kwp = [jnp.concatenate([grp(k_w, g), zeros], axis=1) for g in range(KV_GROUPS)]
    w_std = jnp.concatenate(cv + ksp + kwp + [q_r, k_r, v_r, g_r], axis=1).astype(BF16)
    gcols = []
    for g in range(KV_GROUPS):
        for br in range(3):
            for r in range(GROUP):
                c = (g * GROUP + r) * 3 + br
                gcols.append(g_a[:, c:c + 1])
        gcols.append(jnp.zeros((d, 4), w_in.dtype))
    w_tr = jnp.concatenate([q_a, v_s, v_w] + gcols, axis=1).T.astype(BF16)
    return w_std, w_tr


def _compress_weights(pe_k, pe_v, k_w1, k_w2, v_w1, v_w2):
    half = CMP_LEN // 2

    def w1_half(w1k, w1v, lo):
        a = w1k.reshape(CMP_LEN, HEAD_DIM, CMP_HIDDEN)[lo:lo + half]
        b = w1v.reshape(CMP_LEN, HEAD_DIM, CMP_HIDDEN)[lo:lo + half]
        za = jnp.zeros_like(a)
        top = jnp.concatenate([a, za], axis=2)
        bot = jnp.concatenate([za, b], axis=2)
        return jnp.concatenate([top, bot], axis=1).reshape(half * 2 * HEAD_DIM, 2 * CMP_HIDDEN).astype(BF16)

    def pe_half(lo):
        row = jnp.concatenate([pe_k[lo:lo + half], pe_v[lo:lo + half]], axis=1).reshape(1, -1)
        return jnp.broadcast_to(row, (8, row.shape[1])).astype(BF16)

    wa = w1_half(k_w1, v_w1, 0)
    wb = w1_half(k_w1, v_w1, half)
    w2k = jnp.zeros((2 * CMP_HIDDEN, LANES), F32).at[:CMP_HIDDEN, :HEAD_DIM].set(k_w2).astype(BF16)
    w2vt = jnp.zeros((HEAD_DIM, 2 * CMP_HIDDEN), F32).at[:, CMP_HIDDEN:].set(v_w2.T).astype(BF16)
    return wa, wb, pe_half(0), pe_half(half), w2k, w2vt


def _nsa_constants(seq):
    slopes = jnp.exp2(-8.0 * (jnp.arange(N_HEADS, dtype=F32) + 1.0) / N_HEADS) * LOG2E
    s_hi = slopes.astype(BF16)
    s_lo = (slopes - s_hi.astype(F32)).astype(BF16)
    rows = jnp.zeros((N_HEADS, HEAD_DIM), BF16)
    rows = rows.at[:, 0].set(s_hi).at[:, 1].set(s_hi).at[:, 2].set(s_lo).at[:, 3].set(s_lo)
    qaug = jnp.broadcast_to(rows[:, :, None], (N_HEADS, HEAD_DIM, LANES))
    n_rows = seq // CMP_STRIDE
    n_slc = seq // SLC_BLOCK
    start = np.arange(n_rows)[:, None] * CMP_STRIDE
    end = start + CMP_LEN - 1
    blk = np.arange(n_slc)[None, :] * SLC_BLOCK
    ovl = ((start < blk + SLC_BLOCK) & (end >= blk)).astype(np.float32)
    per_tile = TK_SEL // SLC_BLOCK
    grp = (np.arange(n_slc)[None, :] // per_tile == np.arange(n_slc // per_tile)[:, None])
    return qaug, jnp.asarray(ovl.T, BF16), jnp.asarray(grp.astype(np.float32), BF16)


def _retention_constants():
    h, c = RET_HEADS, RET_CHUNK
    lg = jnp.log1p(-jnp.exp2(-5.0 - jnp.arange(h, dtype=F32)))
    pos = jnp.arange(c, dtype=F32)
    diff = pos[:, None] - pos[None, :]
    scale = RET_DIM ** -0.5
    dm = jnp.where(diff >= 0, jnp.exp(lg[:, None, None] * jnp.maximum(diff, 0.0)), 0.0) * scale
    k_decay = jnp.exp(lg[:, None] * (c - 1.0 - pos)) * scale
    q_decay = jnp.exp(lg[:, None] * (pos + 1.0))
    chunk_decay = jnp.exp(lg * c)
    kd = jnp.broadcast_to(k_decay[:, :, None], (h, c, RET_DIM))
    qd = jnp.broadcast_to(q_decay[:, :, None], (h, c, RET_DIM))
    cd = jnp.broadcast_to(chunk_decay[:, None, None], (h, RET_DIM, RET_DIM))
    return dm, kd, qd, cd


def kernel(x, c, w_ada, b_ada, g_norm_mix, g_norm_ffn, g_norm_final, w_in, cmp_pe_k, cmp_pe_v,
           cmp_k_w1, cmp_k_w2, cmp_v_w1, cmp_v_w2, g_nsa_out, g_ret_out, w_out,
           peer_w_q, peer_sub_keys, peer_u, peer_v):
    batch, seq, d = x.shape
    depth = w_ada.shape[0]
    t_tokens = batch * seq
    xf = x.reshape(t_tokens, d)
    c_pad = jnp.zeros((8, d), F32).at[:batch].set(c)
    qaug, ovl_t, grp = _nsa_constants(seq)
    dm, kd, qd, cd = _retention_constants()

    for l in range(depth):
        mod = _adaln(c_pad, w_ada[l], b_ada[l][None, :])[:batch].reshape(batch, 6, 1, d)
        shift1, scale1, gate1, shift2, scale2, gate2 = (mod[:, k] for k in range(6))

        w_std, w_tr = _inproj_weights(w_in[l])
        (cv, ks, kw, q_r, k_r, v_r, g_r, qt, vst, vwt, gt) = _inproj(
            xf, scale1, shift1, g_norm_mix[l][None, :], w_std, w_tr, seq)

        cv4 = cv.reshape(KV_GROUPS, batch, seq // CMP_STRIDE, CMP_STRIDE * LANES)
        kcp, vct = _compress(cv4, *_compress_weights(cmp_pe_k[l], cmp_pe_v[l], cmp_k_w1[l], cmp_k_w2[l],
                                                     cmp_v_w1[l], cmp_v_w2[l]))
        gout_b = jnp.broadcast_to(g_nsa_out[l][:, :, None], (N_HEADS, HEAD_DIM, LANES))
        o_nsa = _nsa(qt, gt, kcp, vct, ks, kw, vst, vwt, qaug, gout_b, ovl_t, grp, batch, seq)

        go = jnp.broadcast_to(g_ret_out[l][:, None, :], (RET_HEADS, 8, RET_DIM))
        o_ret = _retention(q_r, k_r, v_r, g_r, dm, kd, qd, cd, go, batch, seq)

        sub_keys = peer_sub_keys[l].reshape(2 * PEER_HEADS, PEER_KEYS, -1).astype(BF16)
        x1, h2, st = _mid(o_nsa, o_ret, xf, gate1, scale2, shift2, g_norm_ffn[l][None, :],
                          w_out[l].astype(BF16), peer_w_q[l].T.astype(BF16), sub_keys, seq)

        lrow, w1, r2, w2 = _peer_select(st)
        peer_t = _peer_expert(h2, peer_u[l].astype(BF16), _transpose_bf16(peer_v[l]), lrow, w1, r2, w2)
        xf = _final(x1, peer_t, gate2, g_norm_final[None, :], seq, apply_norm=(l == depth - 1))
    return xf.reshape(batch, seq, d)
```

```python
import functools
import math

import numpy as np
import jax
import jax.numpy as jnp
from jax import lax
from jax.experimental import pallas as pl
from jax.experimental.pallas import tpu as pltpu

F32 = jnp.float32
BF16 = jnp.bfloat16

D_MODEL = 2048
N_HEADS = 16
HEAD_DIM = 64
KV_GROUPS = 4
GROUP = 4
CMP_LEN = 32
CMP_STRIDE = 16
CMP_HIDDEN = 128
SLC_BLOCK = 64
SLC_TOPK = 16
WINDOW = 512
FORCE_SCORE = 1e4
NEG = -1e30
RET_HEADS = 8
RET_DIM = 128
RET_CHUNK = 128
PEER_HEADS = 8
PEER_KEYS = 128
PEER_EXPERTS = PEER_KEYS * PEER_KEYS
PEER_TOPK = 16
EPS = 1e-6
LOG2E = 1.4426950408889634

LANES = 128
TQ = 128
TK_SEL = 256
TM_PROJ = 256
TM_PEER = 512
TE_PEER = 1024
TM_SEL = 256
RET_TILE = 512

_NT = (((1,), (1,)), ((), ()))
_TN = (((0,), (0,)), ((), ()))


def _params(sem, vmem_mb):
    return pltpu.CompilerParams(dimension_semantics=sem, vmem_limit_bytes=vmem_mb * 1024 * 1024)


def _resident(shape, index_map):
    return pl.BlockSpec(shape, index_map, pipeline_mode=pl.Buffered(1))


def _gelu(x):
    return jax.nn.gelu(x)


def _adaln_body(c_ref, w_ref, b_ref, o_ref):
    c = c_ref[...]
    act = (c * jax.nn.sigmoid(c)).astype(BF16)
    o_ref[...] = jnp.dot(act, w_ref[...].astype(BF16), preferred_element_type=F32) + b_ref[...]


def _adaln(c_pad, w, b):
    n = w.shape[1]
    tn = 1536
    return pl.pallas_call(
        _adaln_body,
        grid=(n // tn,),
        in_specs=[pl.BlockSpec((8, D_MODEL), lambda j: (0, 0)),
                  pl.BlockSpec((D_MODEL, tn), lambda j: (0, j)),
                  pl.BlockSpec((1, tn), lambda j: (0, j))],
        out_specs=pl.BlockSpec((8, tn), lambda j: (0, j)),
        out_shape=jax.ShapeDtypeStruct((8, n), F32),
        compiler_params=_params(("arbitrary",), 40),
        name="adaln",
    )(c_pad, w, b)


STD_COLS = 512 * 3 + 1024 * 4
TR_ROWS = 1024 + 256 + 256 + 64


def _inproj_body(x_ref, sc_ref, sh_ref, gn_ref, wstd_ref, wt_ref,
                 cv_ref, ks_ref, kw_ref, qr_ref, kr_ref, vr_ref, gr_ref,
                 qt_ref, vst_ref, vwt_ref, gt_ref, *, tiles_per_seq):
    tm = TM_PROJ
    i = pl.program_id(0)
    x = x_ref[...]
    ms = jnp.mean(x * x, axis=-1, keepdims=True)
    h = x * lax.rsqrt(ms + EPS) * gn_ref[...]
    h = h * (1.0 + sc_ref[0]) + sh_ref[0]
    hb = h.astype(BF16)

    def std(a, b):
        return jnp.dot(hb, wstd_ref[:, a:b], preferred_element_type=F32)

    y = std(0, 512)
    for g in range(KV_GROUPS):
        cv_ref[g] = y[:, g * LANES:(g + 1) * LANES].astype(BF16)

    t = (i % tiles_per_seq) * tm + lax.broadcasted_iota(jnp.int32, (tm, LANES), 0)
    lane = lax.broadcasted_iota(jnp.int32, (tm, LANES), 1)
    pos_hi = ((t >> 6) << 6).astype(F32)
    pos_lo = (t & 63).astype(F32)
    aug = jnp.where((lane == 64) | (lane == 66), pos_hi,
                    jnp.where((lane == 65) | (lane == 67), pos_lo, 0.0))
    for ref, off in ((ks_ref, 512), (kw_ref, 1024)):
        y = std(off, off + 512)
        for g in range(KV_GROUPS):
            ref[:, g * LANES:(g + 1) * LANES] = (y[:, g * LANES:(g + 1) * LANES] + aug).astype(BF16)

    for ref, off in ((qr_ref, 1536), (kr_ref, 2560), (vr_ref, 3584), (gr_ref, 4608)):
        ref[...] = std(off, off + 1024).astype(BF16)

    def tr(a, b):
        return lax.dot_general(wt_ref[a:b, :], hb, _NT, preferred_element_type=F32)

    qt = tr(0, 1024) * (HEAD_DIM ** -0.5 * LOG2E)
    vst = tr(1024, 1280)
    vwt = tr(1280, 1536)
    gt = tr(1536, 1600)
    for c in range(tm // LANES):
        sl = slice(c * LANES, (c + 1) * LANES)
        qt_ref[c] = qt[:, sl].astype(BF16)
        vst_ref[c] = vst[:, sl].astype(BF16)
        vwt_ref[c] = vwt[:, sl].astype(BF16)
        gt_ref[c] = gt[:, sl]


def _inproj(x2, scale1, shift1, g_mix, w_std, w_tr, seq):
    t_tokens = x2.shape[0]
    tm = TM_PROJ
    tps = seq // tm
    nt = t_tokens // tm
    c = tm // LANES
    row = lambda i: (i, 0)
    per_b = lambda i: (i // tps, 0, 0)
    out_shape = (
        jax.ShapeDtypeStruct((KV_GROUPS, t_tokens, LANES), BF16),
        jax.ShapeDtypeStruct((t_tokens, 512), BF16),
        jax.ShapeDtypeStruct((t_tokens, 512), BF16),
        jax.ShapeDtypeStruct((t_tokens, 1024), BF16),
        jax.ShapeDtypeStruct((t_tokens, 1024), BF16),
        jax.ShapeDtypeStruct((t_tokens, 1024), BF16),
        jax.ShapeDtypeStruct((t_tokens, 1024), BF16),
        jax.ShapeDtypeStruct((t_tokens // LANES, 1024, LANES), BF16),
        jax.ShapeDtypeStruct((t_tokens // LANES, 256, LANES), BF16),
        jax.ShapeDtypeStruct((t_tokens // LANES, 256, LANES), BF16),
        jax.ShapeDtypeStruct((t_tokens // LANES, 64, LANES), F32),
    )
    out_specs = (
        pl.BlockSpec((KV_GROUPS, tm, LANES), lambda i: (0, i, 0)),
        pl.BlockSpec((tm, 512), row),
        pl.BlockSpec((tm, 512), row),
        pl.BlockSpec((tm, 1024), row),
        pl.BlockSpec((tm, 1024), row),
        pl.BlockSpec((tm, 1024), row),
        pl.BlockSpec((tm, 1024), row),
        pl.BlockSpec((c, 1024, LANES), lambda i: (i, 0, 0)),
        pl.BlockSpec((c, 256, LANES), lambda i: (i, 0, 0)),
        pl.BlockSpec((c, 256, LANES), lambda i: (i, 0, 0)),
        pl.BlockSpec((c, 64, LANES), lambda i: (i, 0, 0)),
    )
    return pl.pallas_call(
        functools.partial(_inproj_body, tiles_per_seq=tps),
        grid=(nt,),
        in_specs=[pl.BlockSpec((tm, D_MODEL), row),
                  pl.BlockSpec((1, 1, D_MODEL), per_b),
                  pl.BlockSpec((1, 1, D_MODEL), per_b),
                  _resident((1, D_MODEL), lambda i: (0, 0)),
                  _resident((D_MODEL, STD_COLS), lambda i: (0, 0)),
                  _resident((TR_ROWS, D_MODEL), lambda i: (0, 0))],
        out_specs=out_specs,
        out_shape=out_shape,
        compiler_params=_params(("parallel",), 56),
        name="inproj",
    )(x2, scale1, shift1, g_mix, w_std, w_tr)


def _compress_body(x_ref, wa_ref, wb_ref, pea_ref, peb_ref, w2k_ref, w2vt_ref, kcp_ref, vct_ref):
    x = x_ref[0, 0]
    n_rows = x.shape[0]
    p = jnp.dot(x, wa_ref[...], preferred_element_type=F32)
    q = jnp.dot(x, wb_ref[...], preferred_element_type=F32)
    pe = (jnp.dot(pea_ref[...], wa_ref[...], preferred_element_type=F32)
          + jnp.dot(peb_ref[...], wb_ref[...], preferred_element_type=F32))[0:1, :]
    pre = p + pltpu.roll(q, n_rows - 1, 0) + pe
    hid = _gelu(pre).astype(BF16)
    kc = jnp.dot(hid, w2k_ref[...], preferred_element_type=F32)
    n = lax.broadcasted_iota(jnp.int32, (n_rows, LANES), 0)
    lane = lax.broadcasted_iota(jnp.int32, (n_rows, LANES), 1)
    ce = n * CMP_STRIDE + (CMP_LEN - 1)
    ce_hi = ((ce >> 6) << 6).astype(F32)
    ce_lo = (ce & 63).astype(F32)
    aug = jnp.where((lane == 64) | (lane == 66), ce_hi,
                    jnp.where((lane == 65) | (lane == 67), ce_lo, 0.0))
    kcp_ref[0] = (kc + aug).astype(BF16)
    vct_ref[0] = lax.dot_general(w2vt_ref[...], hid, _NT, preferred_element_type=F32).astype(BF16)


def _compress(cv4, wa, wb, pea, peb, w2k, w2vt):
    g_, b_, n_rows, _ = cv4.shape
    const2 = lambda n: (0, 0)
    return pl.pallas_call(
        _compress_body,
        grid=(b_ * g_,),
        in_specs=[pl.BlockSpec((1, 1, n_rows, 2048), lambda n: (n % KV_GROUPS, n // KV_GROUPS, 0, 0)),
                  pl.BlockSpec((2048, 256), const2),
                  pl.BlockSpec((2048, 256), const2),
                  pl.BlockSpec((8, 2048), const2),
                  pl.BlockSpec((8, 2048), const2),
                  pl.BlockSpec((256, LANES), const2),
                  pl.BlockSpec((64, 256), const2)],
        out_specs=(pl.BlockSpec((1, n_rows, LANES), lambda n: (n, 0, 0)),
                   pl.BlockSpec((1, 64, n_rows), lambda n: (n, 0, 0))),
        out_shape=(jax.ShapeDtypeStruct((b_ * g_, n_rows, LANES), BF16),
                   jax.ShapeDtypeStruct((b_ * g_, 64, n_rows), BF16)),
        compiler_params=_params(("parallel",), 32),
        name="nsa_compress",
    )(cv4, wa, wb, pea, peb, w2k, w2vt)


def _softmax_step(state, s, pv_prev):
    m_i, l_i, acc = state
    m_new = jnp.maximum(m_i, jnp.max(s, axis=0, keepdims=True))
    alpha = jnp.exp2(m_i - m_new)
    p = jnp.exp2(s - m_new)
    l_new = alpha * l_i + jnp.sum(p, axis=0, keepdims=True)
    return (m_new, l_new, (acc + pv_prev) * alpha), p.astype(BF16)


def _nsa_body(qt_ref, gt_ref, kcp_ref, vct_ref, ks_ref, kw_ref, vst_ref, vwt_ref,
              qaug_ref, gout_ref, ovl_ref, grp_ref, o_ref, selb_ref, sa_ref, sb_ref, pa_ref, pb_ref, *, n_cmp):
    qi = pl.program_id(2)
    t0 = qi * TQ
    wq = GROUP * TQ

    qp = jnp.concatenate(
        [jnp.concatenate([qt_ref[0, r * HEAD_DIM:(r + 1) * HEAD_DIM, :], qaug_ref[r]], axis=0)
         for r in range(GROUP)], axis=1)

    s = jnp.dot(kcp_ref[0], qp, preferred_element_type=F32)
    n_io = lax.broadcasted_iota(jnp.int32, (n_cmp, wq), 0)
    tl = lax.broadcasted_iota(jnp.int32, (n_cmp, wq), 1) & (TQ - 1)
    valid = (n_io * CMP_STRIDE + (CMP_LEN - 1)) <= (t0 + tl)
    s = jnp.where(valid, s, NEG)
    m = jnp.max(s, axis=0, keepdims=True)
    p = jnp.where(valid, jnp.exp2(s - m), 0.0)
    l = jnp.sum(p, axis=0, keepdims=True)
    pn = p * (1.0 / jnp.maximum(l, 1e-30))
    o_c = jnp.dot(vct_ref[0], pn.astype(BF16), preferred_element_type=F32)

    ps = pn[:, 0:TQ] + pn[:, TQ:2 * TQ] + pn[:, 2 * TQ:3 * TQ] + pn[:, 3 * TQ:4 * TQ]
    hi = ps.astype(BF16)
    r1 = ps - hi.astype(F32)
    mid = r1.astype(BF16)
    lo = (r1 - mid.astype(F32)).astype(BF16)
    ovl = ovl_ref[...]
    imp = (jnp.dot(ovl, hi, preferred_element_type=F32)
           + jnp.dot(ovl, mid, preferred_element_type=F32)
           + jnp.dot(ovl, lo, preferred_element_type=F32))

    n_slc = imp.shape[0]
    m_io = lax.broadcasted_iota(jnp.int32, (n_slc, TQ), 0)
    q_io = lax.broadcasted_iota(jnp.int32, (n_slc, TQ), 1)
    back = ((t0 + q_io) >> 6) - m_io
    valid_s = back >= 0
    forced = valid_s & ((m_io == 0) | (back < 2))
    w = jnp.where(forced, FORCE_SCORE, jnp.where(valid_s, imp, -1.0))

    def pick(_, carry):
        w, selb = carry
        mx = jnp.max(w, axis=0, keepdims=True)
        idx = jnp.min(jnp.where(w == mx, m_io, n_slc), axis=0, keepdims=True)
        hit = m_io == idx
        return jnp.where(hit, -jnp.inf, w), jnp.where(hit, 0.0, selb)

    _, selb = lax.fori_loop(0, SLC_TOPK, pick, (w, jnp.full((n_slc, TQ), NEG, F32)))
    for mblk in range(n_slc):
        selb_ref[mblk] = jnp.broadcast_to(selb[mblk:mblk + 1, :], (8, TQ))

    lane_q = lax.broadcasted_iota(jnp.int32, (1, wq), 1) & (TQ - 1)

    n_win = WINDOW // TQ + 1
    k_lo = pl.multiple_of(jnp.maximum(t0 - WINDOW, 0), TQ)
    s = jnp.dot(kw_ref[pl.ds(k_lo, n_win * TQ), :], qp, preferred_element_type=F32)
    d = (t0 - k_lo) + lane_q - lax.broadcasted_iota(jnp.int32, (n_win * TQ, wq), 0)
    s = jnp.where((d >= 0) & (d < WINDOW), s, NEG)
    p = jnp.exp2(s - jnp.max(s, axis=0, keepdims=True))
    l_w = jnp.sum(p, axis=0, keepdims=True)
    vt_w = jnp.concatenate([vwt_ref[k_lo // TQ + u] for u in range(n_win)], axis=1)
    o_w = jnp.dot(vt_w, p.astype(BF16), preferred_element_type=F32) * (1.0 / l_w)

    per_blk = TK_SEL // SLC_BLOCK

    def scores(j):
        k0 = pl.multiple_of(j * TK_SEL, TK_SEL)
        bias = jnp.concatenate(
            [jnp.tile(selb_ref[j * per_blk + u], (SLC_BLOCK // 8, 1)) for u in range(per_blk)], axis=0)
        bias = jnp.concatenate([bias] * GROUP, axis=1)
        return jnp.dot(ks_ref[pl.ds(k0, TK_SEL), :], qp, preferred_element_type=F32) + bias

    def vt_tile(j):
        return jnp.concatenate([vst_ref[2 * j], vst_ref[2 * j + 1]], axis=1)

    def causal(j):
        kr = lax.broadcasted_iota(jnp.int32, (TK_SEL, wq), 0)
        return kr <= (t0 - j * TK_SEL) + lane_q

    def pv_dot(j, p_ref):
        return jnp.dot(vt_tile(j), p_ref[...], preferred_element_type=F32)

    jd = qi // (TK_SEL // TQ)

    sel01 = jnp.where(selb == 0.0, 1.0, 0.0).astype(BF16)
    tile_any = jnp.max(jnp.dot(grp_ref[...], sel01, preferred_element_type=F32), axis=1, keepdims=True)
    j_io = lax.broadcasted_iota(jnp.int32, tile_any.shape, 0)
    q_lo = jnp.max(jnp.where((tile_any == 0.0) & (j_io <= jd), j_io, -1)) + 1
    n_pre = jnp.max(jnp.where((tile_any > 0.0) & (j_io < q_lo), j_io, -1)) + 1
    n_vis = n_pre + jd - q_lo + 1

    def tile_at(pos):
        return jnp.where(pos < n_pre, pos, pos - n_pre + q_lo)

    state = (jnp.full((1, wq), NEG, F32), jnp.zeros((1, wq), F32), jnp.zeros((HEAD_DIM, wq), F32))
    sa_ref[...] = scores(tile_at(0))
    pb_ref[...] = jnp.zeros_like(pb_ref)

    def pair(u, state):
        a = 2 * u
        sb_ref[...] = scores(tile_at(a + 1))
        state, p = _softmax_step(state, sa_ref[...], pv_dot(tile_at(jnp.maximum(a - 1, 0)), pb_ref))
        pa_ref[...] = p
        sa_ref[...] = scores(tile_at(a + 2))
        state, p = _softmax_step(state, sb_ref[...], pv_dot(tile_at(a), pa_ref))
        pb_ref[...] = p
        return state

    n_pair = (n_vis - 1) // 2
    state = lax.fori_loop(0, n_pair, pair, state)
    x = 2 * n_pair
    y = jnp.minimum(x + 1, n_vis - 1)
    tx, ty = tile_at(x), tile_at(y)
    sb_ref[...] = scores(ty)
    state, p = _softmax_step(state, jnp.where(causal(tx), sa_ref[...], NEG),
                             pv_dot(tile_at(jnp.maximum(x - 1, 0)), pb_ref))
    pa_ref[...] = p
    ty_mask = jnp.where(x + 1 < n_vis, ty, jd + 1)
    (_, l_s, acc_s), p = _softmax_step(state, jnp.where(causal(ty_mask), sb_ref[...], NEG), pv_dot(tx, pa_ref))
    acc_s = acc_s + jnp.dot(vt_tile(ty), p, preferred_element_type=F32)

    o_s = acc_s * (1.0 / l_s)
    gw = jax.nn.sigmoid(gt_ref[0])
    outs = []
    for r in range(GROUP):
        sl = slice(r * TQ, (r + 1) * TQ)
        o = (gw[r:r + 1, :] * o_c[:, sl] + gw[GROUP + r:GROUP + r + 1, :] * o_s[:, sl]
             + gw[2 * GROUP + r:2 * GROUP + r + 1, :] * o_w[:, sl])
        ms = jnp.mean(o * o, axis=0, keepdims=True)
        outs.append(o * lax.rsqrt(ms + EPS) * gout_ref[r])
    o_ref[...] = jnp.concatenate(outs, axis=0).T.astype(BF16)


def _nsa(qt, gt, kcp, vct, ks, kw, vst, vwt, qaug, gout_b, ovl_t, grp, batch, seq):
    nq = seq // TQ
    n_cmp = kcp.shape[1]
    n_slc = seq // SLC_BLOCK
    t_tokens = batch * seq
    per_b_chunks = seq // LANES
    return pl.pallas_call(
        functools.partial(_nsa_body, n_cmp=n_cmp),
        grid=(batch, KV_GROUPS, nq),
        in_specs=[
            pl.BlockSpec((1, 256, LANES), lambda b, g, q: (b * nq + q, g, 0)),
            pl.BlockSpec((1, 16, LANES), lambda b, g, q: (b * nq + q, g, 0)),
            pl.BlockSpec((1, n_cmp, LANES), lambda b, g, q: (b * KV_GROUPS + g, 0, 0)),
            pl.BlockSpec((1, 64, n_cmp), lambda b, g, q: (b * KV_GROUPS + g, 0, 0)),
            pl.BlockSpec((seq, LANES), lambda b, g, q: (b, g)),
            pl.BlockSpec((seq, LANES), lambda b, g, q: (b, g)),
            pl.BlockSpec((per_b_chunks, 64, LANES), lambda b, g, q: (b, g, 0)),
            pl.BlockSpec((per_b_chunks, 64, LANES), lambda b, g, q: (b, g, 0)),
            pl.BlockSpec((GROUP, 64, LANES), lambda b, g, q: (g, 0, 0)),
            pl.BlockSpec((GROUP, 64, LANES), lambda b, g, q: (g, 0, 0)),
            pl.BlockSpec((n_slc, n_cmp), lambda b, g, q: (0, 0)),
            pl.BlockSpec(grp.shape, lambda b, g, q: (0, 0)),
        ],
        out_specs=pl.BlockSpec((TQ, 256), lambda b, g, q: (b * nq + q, g)),
        out_shape=jax.ShapeDtypeStruct((t_tokens, 1024), BF16),
        scratch_shapes=[pltpu.VMEM((n_slc, 8, TQ), F32),
                        pltpu.VMEM((TK_SEL, GROUP * TQ), F32), pltpu.VMEM((TK_SEL, GROUP * TQ), F32),
                        pltpu.VMEM((TK_SEL, GROUP * TQ), BF16), pltpu.VMEM((TK_SEL, GROUP * TQ), BF16)],
        compiler_params=_params(("parallel", "parallel", "arbitrary"), 40),
        name="nsa_attention",
    )(qt, gt, kcp, vct, ks, kw, vst, vwt, qaug, gout_b, ovl_t, grp)


def _ret_body(q_ref, k_ref, v_ref, g_ref, dm_ref, kd_ref, qd_ref, cd_ref, go_ref, o_ref, st_ref):
    @pl.when(pl.program_id(2) == 0)
    def _():
        st_ref[...] = jnp.zeros_like(st_ref)

    c_ = RET_CHUNK
    for c in range(RET_TILE // c_):
        sl = slice(c * c_, (c + 1) * c_)
        q = q_ref[sl, :]
        k = k_ref[sl, :]
        v = v_ref[sl, :]
        att = lax.dot_general(q, k, _NT, preferred_element_type=F32) * dm_ref[0]
        state = st_ref[...]
        o = (jnp.dot(att.astype(BF16), v, preferred_element_type=F32)
             + qd_ref[0] * jnp.dot(q, state.astype(BF16), preferred_element_type=F32))
        kdec = (k.astype(F32) * kd_ref[0]).astype(BF16)
        kv = lax.dot_general(kdec, v, _TN, preferred_element_type=F32)
        st_ref[...] = state * cd_ref[0] + kv
        mu = jnp.mean(o, axis=-1, keepdims=True)
        oc = o - mu
        var = jnp.mean(oc * oc, axis=-1, keepdims=True)
        y = oc * lax.rsqrt(var + EPS) * go_ref[0, 0:1, :]
        gate = g_ref[sl, :].astype(F32)
        o_ref[sl, :] = (gate * jax.nn.sigmoid(gate) * y).astype(BF16)


def _retention(q_r, k_r, v_r, g_r, dm, kd, qd, cd, go, batch, seq):
    t_tokens = batch * seq
    nc = seq // RET_TILE
    tok = lambda b, h, c: (b * nc + c, h)
    per_h = lambda b, h, c: (h, 0, 0)
    sq = (1, RET_DIM, RET_DIM)
    return pl.pallas_call(
        _ret_body,
        grid=(batch, RET_HEADS, nc),
        in_specs=[pl.BlockSpec((RET_TILE, RET_DIM), tok)] * 4
        + [pl.BlockSpec(sq, per_h)] * 4 + [pl.BlockSpec((1, 8, RET_DIM), per_h)],
        out_specs=pl.BlockSpec((RET_TILE, RET_DIM), tok),
        out_shape=jax.ShapeDtypeStruct((t_tokens, RET_HEADS * RET_DIM), BF16),
        scratch_shapes=[pltpu.VMEM((RET_DIM, RET_DIM), F32)],
        compiler_params=_params(("parallel", "parallel", "arbitrary"), 32),
        name="retention",
    )(q_r, k_r, v_r, g_r, dm, kd, qd, cd, go)


def _mid_body(on_ref, or_ref, x_ref, g1_ref, sc_ref, sh_ref, gn_ref, wo_ref, wqt_ref, sk_ref,
              x1_ref, h2t_ref, st_ref):
    acc = (jnp.dot(on_ref[...], wo_ref[0:1024, :], preferred_element_type=F32)
           + jnp.dot(or_ref[...], wo_ref[1024:2048, :], preferred_element_type=F32))
    x1 = x_ref[...] + g1_ref[0] * acc
    x1_ref[...] = x1
    ms = jnp.mean(x1 * x1, axis=-1, keepdims=True)
    h2 = x1 * lax.rsqrt(ms + EPS) * gn_ref[...]
    h2 = h2 * (1.0 + sc_ref[0]) + sh_ref[0]
    h2t_ref[...] = h2.T.astype(BF16)
    h2 = h2.astype(BF16)
    qt = lax.dot_general(wqt_ref[...], h2, _NT, preferred_element_type=F32).astype(BF16)
    for hp in range(2 * PEER_HEADS):
        st_ref[hp] = jnp.dot(sk_ref[hp], qt[hp * 128:(hp + 1) * 128, :], preferred_element_type=F32)


def _mid(o_nsa, o_ret, x2, gate1, scale2, shift2, g_ffn, w_out, wq_t, sub_keys, seq):
    t_tokens = x2.shape[0]
    tm = TM_PROJ
    tps = seq // tm
    row = lambda i: (i, 0)
    per_b = lambda i: (i // tps, 0, 0)
    return pl.pallas_call(
        _mid_body,
        grid=(t_tokens // tm,),
        in_specs=[pl.BlockSpec((tm, 1024), row),
                  pl.BlockSpec((tm, 1024), row),
                  pl.BlockSpec((tm, D_MODEL), row),
                  pl.BlockSpec((1, 1, D_MODEL), per_b),
                  pl.BlockSpec((1, 1, D_MODEL), per_b),
                  pl.BlockSpec((1, 1, D_MODEL), per_b),
                  _resident((1, D_MODEL), lambda i: (0, 0)),
                  _resident((D_MODEL, D_MODEL), lambda i: (0, 0)),
                  _resident((D_MODEL, D_MODEL), lambda i: (0, 0)),
                  _resident((2 * PEER_HEADS, PEER_KEYS, 128), lambda i: (0, 0, 0))],
        out_specs=(pl.BlockSpec((tm, D_MODEL), row),
                   pl.BlockSpec((D_MODEL, tm), lambda i: (0, i)),
                   pl.BlockSpec((2 * PEER_HEADS, PEER_KEYS, tm), lambda i: (0, 0, i))),
        out_shape=(jax.ShapeDtypeStruct((t_tokens, D_MODEL), F32),
                   jax.ShapeDtypeStruct((D_MODEL, t_tokens), BF16),
                   jax.ShapeDtypeStruct((2 * PEER_HEADS, PEER_KEYS, t_tokens), F32)),
        compiler_params=_params(("parallel",), 48),
        name="outproj_peerq",
    )(o_nsa, o_ret, x2, gate1, scale2, shift2, g_ffn, w_out, wq_t, sub_keys)


def _top16(s):
    n_rows, n = s.shape
    io = lax.broadcasted_iota(jnp.int32, (n_rows, n), 0)
    a_io = lax.broadcasted_iota(jnp.int32, (PEER_TOPK, n), 0)
    rank = jnp.full((n_rows, n), PEER_TOPK, jnp.int32)
    vals = jnp.zeros((PEER_TOPK, n), F32)
    for a in range(PEER_TOPK):
        mx = jnp.max(s, axis=0, keepdims=True)
        idx = jnp.min(jnp.where(s == mx, io, n_rows), axis=0, keepdims=True)
        hit = io == idx
        rank = jnp.where(hit, a, rank)
        s = jnp.where(hit, -jnp.inf, s)
        vals = jnp.where(a_io == a, mx, vals)
    return vals, rank


def _peer_select_body(s_ref, l_ref, w1_ref, r2_ref, w2_ref):
    s1 = s_ref[0]
    s2 = s_ref[1]
    n = s1.shape[1]
    v1, rank1 = _top16(s1)
    v2, rank2 = _top16(s2)
    a_io = lax.broadcasted_iota(jnp.int32, (PEER_TOPK, n), 0)
    cnt = jnp.zeros((PEER_TOPK, n), jnp.int32)
    cur = v1 + v2[0:1, :]
    top = v1[0:1, :] + v2[0:1, :]
    z = jnp.zeros((1, n), F32)
    for _ in range(PEER_TOPK):
        mx = jnp.max(cur, axis=0, keepdims=True)
        aidx = jnp.min(jnp.where(cur == mx, a_io, PEER_TOPK), axis=0, keepdims=True)
        hit = a_io == aidx
        cnt = cnt + hit.astype(jnp.int32)
        nxt = jnp.sum(jnp.where(hit, cnt, 0), axis=0, keepdims=True)
        nv = jnp.max(jnp.where(a_io == nxt, v2, -jnp.inf), axis=0, keepdims=True)
        cur = jnp.where(hit, v1 + nv, cur)
        z = z + jnp.exp(mx - top)
    cnt_f = cnt.astype(F32)
    lrow = jnp.zeros(s1.shape, F32)
    for a in range(PEER_TOPK):
        lrow = jnp.where(rank1 == a, cnt_f[a:a + 1, :], lrow)
    l_ref[0] = lrow
    w1_ref[0] = jnp.exp(s1 - v1[0:1, :])
    r2_ref[0] = rank2.astype(F32).astype(BF16)
    w2_ref[0] = (jnp.exp(s2 - v2[0:1, :]) * (1.0 / z)).astype(BF16)


def _peer_select(st):
    t_tokens = st.shape[2]
    tm = TM_SEL
    shp = jax.ShapeDtypeStruct((PEER_HEADS, PEER_KEYS, t_tokens), F32)
    shp_b = jax.ShapeDtypeStruct((PEER_HEADS, PEER_KEYS, t_tokens), BF16)
    spec = pl.BlockSpec((1, PEER_KEYS, tm), lambda i, h: (h, 0, i))
    return pl.pallas_call(
        _peer_select_body,
        grid=(t_tokens // tm, PEER_HEADS),
        in_specs=[pl.BlockSpec((2, PEER_KEYS, tm), lambda i, h: (h, 0, i))],
        out_specs=(spec, spec, spec, spec),
        out_shape=(shp, shp, shp_b, shp_b),
        compiler_params=_params(("parallel", "parallel"), 32),
        name="peer_select",
    )(st)


def _transpose_body(v_ref, o_ref):
    o_ref[...] = v_ref[...].T.astype(BF16)


def _transpose_bf16(v):
    n, d = v.shape
    tn = 512
    return pl.pallas_call(
        _transpose_body,
        grid=(n // tn,),
        in_specs=[pl.BlockSpec((tn, d), lambda i: (i, 0))],
        out_specs=pl.BlockSpec((d, tn), lambda i: (0, i)),
        out_shape=jax.ShapeDtypeStruct((d, n), BF16),
        compiler_params=_params(("parallel",), 32),
        name="transpose_v",
    )(v)


def _peer_expert_body(h2t_ref, u_ref, vt_ref, l_ref, w1_ref, r2_ref, w2_ref, o_ref, ce_ref, co_ref,
                      *, steps_per_tile):
    g = pl.program_id(0)
    sw = 256
    strips = [slice(c * sw, (c + 1) * sw) for c in range(TM_PEER // sw)]
    n_piece = TE_PEER // PEER_KEYS
    blk = D_MODEL // n_piece

    @pl.when(g == 0)
    def _():
        co_ref[...] = jnp.zeros_like(co_ref)

    @pl.when((g == 0) | ((g - 1) % steps_per_tile == 0))
    def _():
        o_ref[...] = jnp.zeros_like(o_ref)

    def run(c_new, c_old):
        def piece(j, carry):
            r0 = pl.multiple_of(j * PEER_KEYS, PEER_KEYS)
            d0 = pl.multiple_of(j * blk, blk)
            for ls in strips:
                coef = None
                for h in range(PEER_HEADS):
                    lrow = jnp.broadcast_to(l_ref[h, pl.ds(j, 1), ls], (16, sw)).astype(BF16)
                    w1row = jnp.broadcast_to(w1_ref[h, pl.ds(j, 1), ls], (16, sw)).astype(BF16)
                    lrow = jnp.tile(lrow, (PEER_KEYS // 16, 1))
                    w1row = jnp.tile(w1row, (PEER_KEYS // 16, 1))
                    term = jnp.where(r2_ref[h, :, ls] < lrow, w2_ref[h, :, ls] * w1row, jnp.zeros((), BF16))
                    coef = term if coef is None else coef + term
                a_t = jnp.dot(u_ref[pl.ds(r0, PEER_KEYS), :], h2t_ref[:, ls], preferred_element_type=F32)
                c_new[pl.ds(r0, PEER_KEYS), ls] = coef * _gelu(a_t).astype(BF16)
                o_ref[pl.ds(d0, blk), ls] += jnp.dot(vt_ref[pl.ds(d0, blk), :], c_old[:, ls],
                                                     preferred_element_type=F32)
            return carry

        lax.fori_loop(0, n_piece, piece, 0, unroll=4)

    @pl.when(g % 2 == 0)
    def _():
        run(ce_ref, co_ref)

    @pl.when(g % 2 == 1)
    def _():
        run(co_ref, ce_ref)


def _peer_expert(h2t, u_b, v_t, lrow, w1, r2, w2):
    t_tokens = h2t.shape[1]
    tm, te = TM_PEER, TE_PEER
    n_piece = te // PEER_KEYS
    ne = PEER_EXPERTS // te
    n_steps = (t_tokens // tm) * ne
    cur = lambda g: jnp.minimum(g, n_steps - 1)
    prev = lambda g: jnp.maximum(g - 1, 0)
    row_spec = pl.BlockSpec((PEER_HEADS, n_piece, tm), lambda g: (0, cur(g) % ne, cur(g) // ne))
    full_spec = pl.BlockSpec((PEER_HEADS, PEER_KEYS, tm), lambda g: (0, 0, cur(g) // ne))
    return pl.pallas_call(
        functools.partial(_peer_expert_body, steps_per_tile=ne),
        grid=(n_steps + 1,),
        in_specs=[pl.BlockSpec((D_MODEL, tm), lambda g: (0, cur(g) // ne)),
                  pl.BlockSpec((te, D_MODEL), lambda g: (cur(g) % ne, 0)),
                  pl.BlockSpec((D_MODEL, te), lambda g: (0, prev(g) % ne)),
                  row_spec, row_spec, full_spec, full_spec],
        out_specs=pl.BlockSpec((D_MODEL, tm), lambda g: (0, prev(g) // ne)),
        out_shape=jax.ShapeDtypeStruct((D_MODEL, t_tokens), F32),
        scratch_shapes=[pltpu.VMEM((te, tm), BF16), pltpu.VMEM((te, tm), BF16)],
        compiler_params=_params(("arbitrary",), 52),
        name="peer_experts",
    )(h2t, u_b, v_t, lrow, w1, r2, w2)


def _final_body(x1_ref, pt_ref, g2_ref, gn_ref, o_ref, *, apply_norm):
    y = x1_ref[...] + g2_ref[0] * pt_ref[...].T
    if apply_norm:
        ms = jnp.mean(y * y, axis=-1, keepdims=True)
        y = y * lax.rsqrt(ms + EPS) * gn_ref[...]
    o_ref[...] = y


def _final(x1, peer_t, gate2, g_final, seq, apply_norm):
    t_tokens = x1.shape[0]
    tm = TM_PROJ
    tps = seq // tm
    return pl.pallas_call(
        functools.partial(_final_body, apply_norm=apply_norm),
        grid=(t_tokens // tm,),
        in_specs=[pl.BlockSpec((tm, D_MODEL), lambda i: (i, 0)),
                  pl.BlockSpec((D_MODEL, tm), lambda i: (0, i)),
                  pl.BlockSpec((1, 1, D_MODEL), lambda i: (i // tps, 0, 0)),
                  pl.BlockSpec((1, D_MODEL), lambda i: (0, 0))],
        out_specs=pl.BlockSpec((tm, D_MODEL), lambda i: (i, 0)),
        out_shape=jax.ShapeDtypeStruct((t_tokens, D_MODEL), F32),
        compiler_params=_params(("parallel",), 32),
        name="final_norm",
    )(x1, peer_t, gate2, g_final)


def _split_cols(a, sizes):
    out, acc = [], 0
    for s in sizes:
        out.append(a[:, acc:acc + s])
        acc += s
    return out


def _inproj_weights(w_in):
    kvw = KV_GROUPS * HEAD_DIM
    sizes = (1024,) + (kvw,) * 6 + (3 * N_HEADS, 1024, 1024, 1024, 1024)
    q_a, k_c, v_c, k_s, v_s, k_w, v_w, g_a, q_r, k_r, v_r, g_r = _split_cols(w_in, sizes)
    d = w_in.shape[0]

    def grp(a, g):
        return a[:, g * HEAD_DIM:(g + 1) * HEAD_DIM]

    zeros = jnp.zeros((d, HEAD_DIM), w_in.dtype)
    cv = [jnp.concatenate([grp(k_c, g), grp(v_c, g)], axis=1) for g in range(KV_GROUPS)]
    ksp = [jnp.concatenate([grp(k_s, g), zeros], axis=1) for g in range(KV_GROUPS)]
    kwp = [jnp.concatenate([grp(k_w, g), zeros], axis=1) for g in range(KV_GROUPS)]
    w_std = jnp.concatenate(cv + ksp + kwp + [q_r, k_r, v_r, g_r], axis=1).astype(BF16)
    gcols = []
    for g in range(KV_GROUPS):
        for br in range(3):
            for r in range(GROUP):
                c = (g * GROUP + r) * 3 + br
                gcols.append(g_a[:, c:c + 1])
        gcols.append(jnp.zeros((d, 4), w_in.dtype))
    w_tr = jnp.concatenate([q_a, v_s, v_w] + gcols, axis=1).T.astype(BF16)
    return w_std, w_tr


def _compress_weights(pe_k, pe_v, k_w1, k_w2, v_w1, v_w2):
    half = CMP_LEN // 2

    def w1_half(w1k, w1v, lo):
        a = w1k.reshape(CMP_LEN, HEAD_DIM, CMP_HIDDEN)[lo:lo + half]
        b = w1v.reshape(CMP_LEN, HEAD_DIM, CMP_HIDDEN)[lo:lo + half]
        za = jnp.zeros_like(a)
        top = jnp.concatenate([a, za], axis=2)
        bot = jnp.concatenate([za, b], axis=2)
        return jnp.concatenate([top, bot], axis=1).reshape(half * 2 * HEAD_DIM, 2 * CMP_HIDDEN).astype(BF16)

    def pe_half(lo):
        row = jnp.concatenate([pe_k[lo:lo + half], pe_v[lo:lo + half]], axis=1).reshape(1, -1)
        return jnp.broadcast_to(row, (8, row.shape[1])).astype(BF16)

    wa = w1_half(k_w1, v_w1, 0)
    wb = w1_half(k_w1, v_w1, half)
    w2k = jnp.zeros((2 * CMP_HIDDEN, LANES), F32).at[:CMP_HIDDEN, :HEAD_DIM].set(k_w2).astype(BF16)
    w2vt = jnp.zeros((HEAD_DIM, 2 * CMP_HIDDEN), F32).at[:, CMP_HIDDEN:].set(v_w2.T).astype(BF16)
    return wa, wb, pe_half(0), pe_half(half), w2k, w2vt


def _nsa_constants(seq):
    slopes = jnp.exp2(-8.0 * (jnp.arange(N_HEADS, dtype=F32) + 1.0) / N_HEADS) * LOG2E
    s_hi = slopes.astype(BF16)
    s_lo = (slopes - s_hi.astype(F32)).astype(BF16)
    rows = jnp.zeros((N_HEADS, HEAD_DIM), BF16)
    rows = rows.at[:, 0].set(s_hi).at[:, 1].set(s_hi).at[:, 2].set(s_lo).at[:, 3].set(s_lo)
    qaug = jnp.broadcast_to(rows[:, :, None], (N_HEADS, HEAD_DIM, LANES))
    n_rows = seq // CMP_STRIDE
    n_slc = seq // SLC_BLOCK
    start = np.arange(n_rows)[:, None] * CMP_STRIDE
    end = start + CMP_LEN - 1
    blk = np.arange(n_slc)[None, :] * SLC_BLOCK
    ovl = ((start < blk + SLC_BLOCK) & (end >= blk)).astype(np.float32)
    per_tile = TK_SEL // SLC_BLOCK
    grp = (np.arange(n_slc)[None, :] // per_tile == np.arange(n_slc // per_tile)[:, None])
    return qaug, jnp.asarray(ovl.T, BF16), jnp.asarray(grp.astype(np.float32), BF16)


def _retention_constants():
    h, c = RET_HEADS, RET_CHUNK
    lg = jnp.log1p(-jnp.exp2(-5.0 - jnp.arange(h, dtype=F32)))
    pos = jnp.arange(c, dtype=F32)
    diff = pos[:, None] - pos[None, :]
    scale = RET_DIM ** -0.5
    dm = jnp.where(diff >= 0, jnp.exp(lg[:, None, None] * jnp.maximum(diff, 0.0)), 0.0) * scale
    k_decay = jnp.exp(lg[:, None] * (c - 1.0 - pos)) * scale
    q_decay = jnp.exp(lg[:, None] * (pos + 1.0))
    chunk_decay = jnp.exp(lg * c)
    kd = jnp.broadcast_to(k_decay[:, :, None], (h, c, RET_DIM))
    qd = jnp.broadcast_to(q_decay[:, :, None], (h, c, RET_DIM))
    cd = jnp.broadcast_to(chunk_decay[:, None, None], (h, RET_DIM, RET_DIM))
    return dm, kd, qd, cd


def kernel(x, c, w_ada, b_ada, g_norm_mix, g_norm_ffn, g_norm_final, w_in, cmp_pe_k, cmp_pe_v,
           cmp_k_w1, cmp_k_w2, cmp_v_w1, cmp_v_w2, g_nsa_out, g_ret_out, w_out,
           peer_w_q, peer_sub_keys, peer_u, peer_v):
    batch, seq, d = x.shape
    depth = w_ada.shape[0]
    t_tokens = batch * seq
    xf = x.reshape(t_tokens, d)
    c_pad = jnp.zeros((8, d), F32).at[:batch].set(c)
    qaug, ovl_t, grp = _nsa_constants(seq)
    dm, kd, qd, cd = _retention_constants()

    for l in range(depth):
        mod = _adaln(c_pad, w_ada[l], b_ada[l][None, :])[:batch].reshape(batch, 6, 1, d)
        shift1, scale1, gate1, shift2, scale2, gate2 = (mod[:, k] for k in range(6))

        w_std, w_tr = _inproj_weights(w_in[l])
        (cv, ks, kw, q_r, k_r, v_r, g_r, qt, vst, vwt, gt) = _inproj(
            xf, scale1, shift1, g_norm_mix[l][None, :], w_std, w_tr, seq)

        cv4 = cv.reshape(KV_GROUPS, batch, seq // CMP_STRIDE, CMP_STRIDE * LANES)
        kcp, vct = _compress(cv4, *_compress_weights(cmp_pe_k[l], cmp_pe_v[l], cmp_k_w1[l], cmp_k_w2[l],
                                                     cmp_v_w1[l], cmp_v_w2[l]))
        gout_b = jnp.broadcast_to(g_nsa_out[l][:, :, None], (N_HEADS, HEAD_DIM, LANES))
        o_nsa = _nsa(qt, gt, kcp, vct, ks, kw, vst, vwt, qaug, gout_b, ovl_t, grp, batch, seq)

        go = jnp.broadcast_to(g_ret_out[l][:, None, :], (RET_HEADS, 8, RET_DIM))
        o_ret = _retention(q_r, k_r, v_r, g_r, dm, kd, qd, cd, go, batch, seq)

        sub_keys = peer_sub_keys[l].reshape(2 * PEER_HEADS, PEER_KEYS, -1).astype(BF16)
        x1, h2, st = _mid(o_nsa, o_ret, xf, gate1, scale2, shift2, g_norm_ffn[l][None, :],
                          w_out[l].astype(BF16), peer_w_q[l].T.astype(BF16), sub_keys, seq)

        lrow, w1, r2, w2 = _peer_select(st)
        peer_t = _peer_expert(h2, peer_u[l].astype(BF16), _transpose_bf16(peer_v[l]), lrow, w1, r2, w2)
        xf = _final(x1, peer_t, gate2, g_norm_final[None, :], seq, apply_norm=(l == depth - 1))
    return xf.reshape(batch, seq, d)
```

```python
import functools
import math

import numpy as np
import jax
import jax.numpy as jnp
from jax import lax
from jax.experimental import pallas as pl
from jax.experimental.pallas import tpu as pltpu

F32 = jnp.float32
BF16 = jnp.bfloat16

D_MODEL = 2048
N_HEADS = 16
HEAD_DIM = 64
KV_GROUPS = 4
GROUP = 4
CMP_LEN = 32
CMP_STRIDE = 16
CMP_HIDDEN = 128
SLC_BLOCK = 64
SLC_TOPK = 16
WINDOW = 512
FORCE_SCORE = 1e4
NEG = -1e30
RET_HEADS = 8
RET_DIM = 128
RET_CHUNK = 128
PEER_HEADS = 8
PEER_KEYS = 128
PEER_EXPERTS = PEER_KEYS * PEER_KEYS
PEER_TOPK = 16
EPS = 1e-6
LOG2E = 1.4426950408889634

LANES = 128
TQ = 128
TK_SEL = 256
TM_PROJ = 256
TM_PEER = 512
TE_PEER = 1024
TM_SEL = 256
RET_TILE = 512

_NT = (((1,), (1,)), ((), ()))
_TN = (((0,), (0,)), ((), ()))


def _params(sem, vmem_mb):
    return pltpu.CompilerParams(dimension_semantics=sem, vmem_limit_bytes=vmem_mb * 1024 * 1024)


def _resident(shape, index_map):
    return pl.BlockSpec(shape, index_map, pipeline_mode=pl.Buffered(1))


def _gelu(x):
    return jax.nn.gelu(x)


def _adaln_body(c_ref, w_ref, b_ref, o_ref):
    c = c_ref[...]
    act = (c * jax.nn.sigmoid(c)).astype(BF16)
    o_ref[...] = jnp.dot(act, w_ref[...].astype(BF16), preferred_element_type=F32) + b_ref[...]


def _adaln(c_pad, w, b):
    n = w.shape[1]
    tn = 1536
    return pl.pallas_call(
        _adaln_body,
        grid=(n // tn,),
        in_specs=[pl.BlockSpec((8, D_MODEL), lambda j: (0, 0)),
                  pl.BlockSpec((D_MODEL, tn), lambda j: (0, j)),
                  pl.BlockSpec((1, tn), lambda j: (0, j))],
        out_specs=pl.BlockSpec((8, tn), lambda j: (0, j)),
        out_shape=jax.ShapeDtypeStruct((8, n), F32),
        compiler_params=_params(("arbitrary",), 40),
        name="adaln",
    )(c_pad, w, b)


STD_COLS = 512 * 3 + 1024 * 4
TR_ROWS = 1024 + 256 + 256 + 64


def _inproj_body(x_ref, sc_ref, sh_ref, gn_ref, wstd_ref, wt_ref,
                 cv_ref, ks_ref, kw_ref, qr_ref, kr_ref, vr_ref, gr_ref,
                 qt_ref, vst_ref, vwt_ref, gt_ref, *, tiles_per_seq):
    tm = TM_PROJ
    i = pl.program_id(0)
    x = x_ref[...]
    ms = jnp.mean(x * x, axis=-1, keepdims=True)
    h = x * lax.rsqrt(ms + EPS) * gn_ref[...]
    h = h * (1.0 + sc_ref[0]) + sh_ref[0]
    hb = h.astype(BF16)

    def std(a, b):
        return jnp.dot(hb, wstd_ref[:, a:b], preferred_element_type=F32)

    y = std(0, 512)
    for g in range(KV_GROUPS):
        cv_ref[g] = y[:, g * LANES:(g + 1) * LANES].astype(BF16)

    t = (i % tiles_per_seq) * tm + lax.broadcasted_iota(jnp.int32, (tm, LANES), 0)
    lane = lax.broadcasted_iota(jnp.int32, (tm, LANES), 1)
    pos_hi = ((t >> 6) << 6).astype(F32)
    pos_lo = (t & 63).astype(F32)
    aug = jnp.where((lane == 64) | (lane == 66), pos_hi,
                    jnp.where((lane == 65) | (lane == 67), pos_lo, 0.0))
    for ref, off in ((ks_ref, 512), (kw_ref, 1024)):
        y = std(off, off + 512)
        for g in range(KV_GROUPS):
            ref[:, g * LANES:(g + 1) * LANES] = (y[:, g * LANES:(g + 1) * LANES] + aug).astype(BF16)

    for ref, off in ((qr_ref, 1536), (kr_ref, 2560), (vr_ref, 3584), (gr_ref, 4608)):
        ref[...] = std(off, off + 1024).astype(BF16)

    def tr(a, b):
        return lax.dot_general(wt_ref[a:b, :], hb, _NT, preferred_element_type=F32)

    qt = tr(0, 1024) * (HEAD_DIM ** -0.5 * LOG2E)
    vst = tr(1024, 1280)
    vwt = tr(1280, 1536)
    gt = tr(1536, 1600)
    for c in range(tm // LANES):
        sl = slice(c * LANES, (c + 1) * LANES)
        qt_ref[c] = qt[:, sl].astype(BF16)
        vst_ref[c] = vst[:, sl].astype(BF16)
        vwt_ref[c] = vwt[:, sl].astype(BF16)
        gt_ref[c] = gt[:, sl]


def _inproj(x2, scale1, shift1, g_mix, w_std, w_tr, seq):
    t_tokens = x2.shape[0]
    tm = TM_PROJ
    tps = seq // tm
    nt = t_tokens // tm
    c = tm // LANES
    row = lambda i: (i, 0)
    per_b = lambda i: (i // tps, 0, 0)
    out_shape = (
        jax.ShapeDtypeStruct((KV_GROUPS, t_tokens, LANES), BF16),
        jax.ShapeDtypeStruct((t_tokens, 512), BF16),
        jax.ShapeDtypeStruct((t_tokens, 512), BF16),
        jax.ShapeDtypeStruct((t_tokens, 1024), BF16),
        jax.ShapeDtypeStruct((t_tokens, 1024), BF16),
        jax.ShapeDtypeStruct((t_tokens, 1024), BF16),
        jax.ShapeDtypeStruct((t_tokens, 1024), BF16),
        jax.ShapeDtypeStruct((t_tokens // LANES, 1024, LANES), BF16),
        jax.ShapeDtypeStruct((t_tokens // LANES, 256, LANES), BF16),
        jax.ShapeDtypeStruct((t_tokens // LANES, 256, LANES), BF16),
        jax.ShapeDtypeStruct((t_tokens // LANES, 64, LANES), F32),
    )
    out_specs = (
        pl.BlockSpec((KV_GROUPS, tm, LANES), lambda i: (0, i, 0)),
        pl.BlockSpec((tm, 512), row),
        pl.BlockSpec((tm, 512), row),
        pl.BlockSpec((tm, 1024), row),
        pl.BlockSpec((tm, 1024), row),
        pl.BlockSpec((tm, 1024), row),
        pl.BlockSpec((tm, 1024), row),
        pl.BlockSpec((c, 1024, LANES), lambda i: (i, 0, 0)),
        pl.BlockSpec((c, 256, LANES), lambda i: (i, 0, 0)),
        pl.BlockSpec((c, 256, LANES), lambda i: (i, 0, 0)),
        pl.BlockSpec((c, 64, LANES), lambda i: (i, 0, 0)),
    )
    return pl.pallas_call(
        functools.partial(_inproj_body, tiles_per_seq=tps),
        grid=(nt,),
        in_specs=[pl.BlockSpec((tm, D_MODEL), row),
                  pl.BlockSpec((1, 1, D_MODEL), per_b),
                  pl.BlockSpec((1, 1, D_MODEL), per_b),
                  _resident((1, D_MODEL), lambda i: (0, 0)),
                  _resident((D_MODEL, STD_COLS), lambda i: (0, 0)),
                  _resident((TR_ROWS, D_MODEL), lambda i: (0, 0))],
        out_specs=out_specs,
        out_shape=out_shape,
        compiler_params=_params(("parallel",), 56),
        name="inproj",
    )(x2, scale1, shift1, g_mix, w_std, w_tr)


def _compress_body(x_ref, wa_ref, wb_ref, pea_ref, peb_ref, w2k_ref, w2vt_ref, kcp_ref, vct_ref):
    x = x_ref[0, 0]
    n_rows = x.shape[0]
    p = jnp.dot(x, wa_ref[...], preferred_element_type=F32)
    q = jnp.dot(x, wb_ref[...], preferred_element_type=F32)
    pe = (jnp.dot(pea_ref[...], wa_ref[...], preferred_element_type=F32)
          + jnp.dot(peb_ref[...], wb_ref[...], preferred_element_type=F32))[0:1, :]
    pre = p + pltpu.roll(q, n_rows - 1, 0) + pe
    hid = _gelu(pre).astype(BF16)
    kc = jnp.dot(hid, w2k_ref[...], preferred_element_type=F32)
    n = lax.broadcasted_iota(jnp.int32, (n_rows, LANES), 0)
    lane = lax.broadcasted_iota(jnp.int32, (n_rows, LANES), 1)
    ce = n * CMP_STRIDE + (CMP_LEN - 1)
    ce_hi = ((ce >> 6) << 6).astype(F32)
    ce_lo = (ce & 63).astype(F32)
    aug = jnp.where((lane == 64) | (lane == 66), ce_hi,
                    jnp.where((lane == 65) | (lane == 67), ce_lo, 0.0))
    kcp_ref[0] = (kc + aug).astype(BF16)
    vct_ref[0] = lax.dot_general(w2vt_ref[...], hid, _NT, preferred_element_type=F32).astype(BF16)


def _compress(cv4, wa, wb, pea, peb, w2k, w2vt):
    g_, b_, n_rows, _ = cv4.shape
    const2 = lambda n: (0, 0)
    return pl.pallas_call(
        _compress_body,
        grid=(b_ * g_,),
        in_specs=[pl.BlockSpec((1, 1, n_rows, 2048), lambda n: (n % KV_GROUPS, n // KV_GROUPS, 0, 0)),
                  pl.BlockSpec((2048, 256), const2),
                  pl.BlockSpec((2048, 256), const2),
                  pl.BlockSpec((8, 2048), const2),
                  pl.BlockSpec((8, 2048), const2),
                  pl.BlockSpec((256, LANES), const2),
                  pl.BlockSpec((64, 256), const2)],
        out_specs=(pl.BlockSpec((1, n_rows, LANES), lambda n: (n, 0, 0)),
                   pl.BlockSpec((1, 64, n_rows), lambda n: (n, 0, 0))),
        out_shape=(jax.ShapeDtypeStruct((b_ * g_, n_rows, LANES), BF16),
                   jax.ShapeDtypeStruct((b_ * g_, 64, n_rows), BF16)),
        compiler_params=_params(("parallel",), 32),
        name="nsa_compress",
    )(cv4, wa, wb, pea, peb, w2k, w2vt)


def _softmax_step(state, s, pv_prev):
    m_i, l_i, acc = state
    m_new = jnp.maximum(m_i, jnp.max(s, axis=0, keepdims=True))
    alpha = jnp.exp2(m_i - m_new)
    p = jnp.exp2(s - m_new)
    l_new = alpha * l_i + jnp.sum(p, axis=0, keepdims=True)
    return (m_new, l_new, (acc + pv_prev) * alpha), p.astype(BF16)


def _nsa_body(qt_ref, gt_ref, kcp_ref, vct_ref, ks_ref, kw_ref, vst_ref, vwt_ref,
              qaug_ref, gout_ref, ovl_ref, grp_ref, o_ref, selb_ref, sa_ref, sb_ref, pa_ref, pb_ref, *, n_cmp):
    qi = pl.program_id(2)
    t0 = qi * TQ
    wq = GROUP * TQ

    qp = jnp.concatenate(
        [jnp.concatenate([qt_ref[0, r * HEAD_DIM:(r + 1) * HEAD_DIM, :], qaug_ref[r]], axis=0)
         for r in range(GROUP)], axis=1)

    lane_q = lax.broadcasted_iota(jnp.int32, (1, wq), 1) & (TQ - 1)
    n_win = WINDOW // TQ + 1
    k_lo = pl.multiple_of(jnp.maximum(t0 - WINDOW, 0), TQ)

    s = jnp.dot(kcp_ref[0], qp, preferred_element_type=F32)
    s_w = jnp.dot(kw_ref[pl.ds(k_lo, n_win * TQ), :], qp, preferred_element_type=F32)
    n_io = lax.broadcasted_iota(jnp.int32, (n_cmp, wq), 0)
    tl = lax.broadcasted_iota(jnp.int32, (n_cmp, wq), 1) & (TQ - 1)
    valid = (n_io * CMP_STRIDE + (CMP_LEN - 1)) <= (t0 + tl)
    s = jnp.where(valid, s, NEG)
    m = jnp.max(s, axis=0, keepdims=True)
    p = jnp.where(valid, jnp.exp2(s - m), 0.0)
    l = jnp.sum(p, axis=0, keepdims=True)
    pn = p * (1.0 / jnp.maximum(l, 1e-30))
    o_c = jnp.dot(vct_ref[0], pn.astype(BF16), preferred_element_type=F32)

    ps = pn[:, 0:TQ] + pn[:, TQ:2 * TQ] + pn[:, 2 * TQ:3 * TQ] + pn[:, 3 * TQ:4 * TQ]
    hi = ps.astype(BF16)
    r1 = ps - hi.astype(F32)
    mid = r1.astype(BF16)
    lo = (r1 - mid.astype(F32)).astype(BF16)
    ovl = ovl_ref[...]
    imp = (jnp.dot(ovl, hi, preferred_element_type=F32)
           + jnp.dot(ovl, mid, preferred_element_type=F32)
           + jnp.dot(ovl, lo, preferred_element_type=F32))

    n_slc = imp.shape[0]
    m_io = lax.broadcasted_iota(jnp.int32, (n_slc, TQ), 0)
    q_io = lax.broadcasted_iota(jnp.int32, (n_slc, TQ), 1)
    back = ((t0 + q_io) >> 6) - m_io
    valid_s = back >= 0
    forced = valid_s & ((m_io == 0) | (back < 2))
    w = jnp.where(forced, -jnp.inf, jnp.where(valid_s, imp, -1.0))
    selb = jnp.where(forced, 0.0, NEG)

    def pick(carry, lanes=None):
        w, selb = carry
        mx = jnp.max(w, axis=0, keepdims=True)
        idx = jnp.min(jnp.where(w == mx, m_io, n_slc), axis=0, keepdims=True)
        hit = m_io == idx
        if lanes is not None:
            hit = hit & lanes
        return jnp.where(hit, -jnp.inf, w), jnp.where(hit, 0.0, selb)

    d = (t0 - k_lo) + lane_q - lax.broadcasted_iota(jnp.int32, (n_win * TQ, wq), 0)
    s_w = jnp.where((d >= 0) & (d < WINDOW), s_w, NEG)
    p = jnp.exp2(s_w - jnp.max(s_w, axis=0, keepdims=True))
    l_w = jnp.sum(p, axis=0, keepdims=True)
    vt_w = jnp.concatenate([vwt_ref[k_lo // TQ + u] for u in range(n_win)], axis=1)
    o_w = jnp.dot(vt_w, p.astype(BF16), preferred_element_type=F32) * (1.0 / l_w)

    carry = (w, selb)
    for _ in range(SLC_TOPK - 3):
        carry = pick(carry)
    carry = pick(carry, lanes=(t0 + q_io) < 2 * SLC_BLOCK)
    _, selb = pick(carry, lanes=(t0 + q_io) < SLC_BLOCK)
    for mblk in range(n_slc):
        selb_ref[mblk] = jnp.broadcast_to(selb[mblk:mblk + 1, :], (8, TQ))

    per_blk = TK_SEL // SLC_BLOCK

    def scores(j):
        k0 = pl.multiple_of(j * TK_SEL, TK_SEL)
        bias = jnp.concatenate(
            [jnp.tile(selb_ref[j * per_blk + u], (SLC_BLOCK // 8, 1)) for u in range(per_blk)], axis=0)
        bias = jnp.concatenate([bias] * GROUP, axis=1)
        return jnp.dot(ks_ref[pl.ds(k0, TK_SEL), :], qp, preferred_element_type=F32) + bias

    def vt_tile(j):
        return jnp.concatenate([vst_ref[2 * j], vst_ref[2 * j + 1]], axis=1)

    def causal(j):
        kr = lax.broadcasted_iota(jnp.int32, (TK_SEL, wq), 0)
        return kr <= (t0 - j * TK_SEL) + lane_q

    def pv_dot(j, p_ref):
        return jnp.dot(vt_tile(j), p_ref[...], preferred_element_type=F32)

    jd = qi // (TK_SEL // TQ)

    sel01 = jnp.where(selb == 0.0, 1.0, 0.0).astype(BF16)
    tile_any = jnp.max(jnp.dot(grp_ref[...], sel01, preferred_element_type=F32), axis=1, keepdims=True)
    j_io = lax.broadcasted_iota(jnp.int32, tile_any.shape, 0)
    q_lo = jnp.max(jnp.where((tile_any == 0.0) & (j_io <= jd), j_io, -1)) + 1
    n_pre = jnp.max(jnp.where((tile_any > 0.0) & (j_io < q_lo), j_io, -1)) + 1
    n_vis = n_pre + jd - q_lo + 1

    def tile_at(pos):
        return jnp.where(pos < n_pre, pos, pos - n_pre + q_lo)

    state = (jnp.full((1, wq), NEG, F32), jnp.zeros((1, wq), F32), jnp.zeros((HEAD_DIM, wq), F32))
    sa_ref[...] = scores(tile_at(0))
    pb_ref[...] = jnp.zeros_like(pb_ref)

    def pair(u, state):
        a = 2 * u
        sb_ref[...] = scores(tile_at(a + 1))
        state, p = _softmax_step(state, sa_ref[...], pv_dot(tile_at(jnp.maximum(a - 1, 0)), pb_ref))
        pa_ref[...] = p
        sa_ref[...] = scores(tile_at(a + 2))
        state, p = _softmax_step(state, sb_ref[...], pv_dot(tile_at(a), pa_ref))
        pb_ref[...] = p
        return state

    n_pair = (n_vis - 1) // 2
    state = lax.fori_loop(0, n_pair, pair, state)
    x = 2 * n_pair
    y = jnp.minimum(x + 1, n_vis - 1)
    tx, ty = tile_at(x), tile_at(y)
    sb_ref[...] = scores(ty)
    state, p = _softmax_step(state, jnp.where(causal(tx), sa_ref[...], NEG),
                             pv_dot(tile_at(jnp.maximum(x - 1, 0)), pb_ref))
    pa_ref[...] = p
    ty_mask = jnp.where(x + 1 < n_vis, ty, jd + 1)
    (_, l_s, acc_s), p = _softmax_step(state, jnp.where(causal(ty_mask), sb_ref[...], NEG), pv_dot(tx, pa_ref))
    acc_s = acc_s + jnp.dot(vt_tile(ty), p, preferred_element_type=F32)

    o_s = acc_s * (1.0 / l_s)
    gw = jax.nn.sigmoid(gt_ref[0])
    outs = []
    for r in range(GROUP):
        sl = slice(r * TQ, (r + 1) * TQ)
        o = (gw[r:r + 1, :] * o_c[:, sl] + gw[GROUP + r:GROUP + r + 1, :] * o_s[:, sl]
             + gw[2 * GROUP + r:2 * GROUP + r + 1, :] * o_w[:, sl])
        ms = jnp.mean(o * o, axis=0, keepdims=True)
        outs.append(o * lax.rsqrt(ms + EPS) * gout_ref[r])
    o_ref[...] = jnp.concatenate(outs, axis=0).T.astype(BF16)


def _nsa(qt, gt, kcp, vct, ks, kw, vst, vwt, qaug, gout_b, ovl_t, grp, batch, seq):
    nq = seq // TQ
    n_cmp = kcp.shape[1]
    n_slc = seq // SLC_BLOCK
    t_tokens = batch * seq
    per_b_chunks = seq // LANES
    return pl.pallas_call(
        functools.partial(_nsa_body, n_cmp=n_cmp),
        grid=(batch, KV_GROUPS, nq),
        in_specs=[
            pl.BlockSpec((1, 256, LANES), lambda b, g, q: (b * nq + q, g, 0)),
            pl.BlockSpec((1, 16, LANES), lambda b, g, q: (b * nq + q, g, 0)),
            pl.BlockSpec((1, n_cmp, LANES), lambda b, g, q: (b * KV_GROUPS + g, 0, 0)),
            pl.BlockSpec((1, 64, n_cmp), lambda b, g, q: (b * KV_GROUPS + g, 0, 0)),
            pl.BlockSpec((seq, LANES), lambda b, g, q: (b, g)),
            pl.BlockSpec((seq, LANES), lambda b, g, q: (b, g)),
            pl.BlockSpec((per_b_chunks, 64, LANES), lambda b, g, q: (b, g, 0)),
            pl.BlockSpec((per_b_chunks, 64, LANES), lambda b, g, q: (b, g, 0)),
            pl.BlockSpec((GROUP, 64, LANES), lambda b, g, q: (g, 0, 0)),
            pl.BlockSpec((GROUP, 64, LANES), lambda b, g, q: (g, 0, 0)),
            pl.BlockSpec((n_slc, n_cmp), lambda b, g, q: (0, 0)),
            pl.BlockSpec(grp.shape, lambda b, g, q: (0, 0)),
        ],
        out_specs=pl.BlockSpec((TQ, 256), lambda b, g, q: (b * nq + q, g)),
        out_shape=jax.ShapeDtypeStruct((t_tokens, 1024), BF16),
        scratch_shapes=[pltpu.VMEM((n_slc, 8, TQ), F32),
                        pltpu.VMEM((TK_SEL, GROUP * TQ), F32), pltpu.VMEM((TK_SEL, GROUP * TQ), F32),
                        pltpu.VMEM((TK_SEL, GROUP * TQ), BF16), pltpu.VMEM((TK_SEL, GROUP * TQ), BF16)],
        compiler_params=_params(("parallel", "parallel", "arbitrary"), 40),
        name="nsa_attention",
    )(qt, gt, kcp, vct, ks, kw, vst, vwt, qaug, gout_b, ovl_t, grp)


def _ret_body(q_ref, k_ref, v_ref, g_ref, dm_ref, kd_ref, qd_ref, cd_ref, go_ref, o_ref, st_ref):
    @pl.when(pl.program_id(2) == 0)
    def _():
        st_ref[...] = jnp.zeros_like(st_ref)

    c_ = RET_CHUNK
    for c in range(RET_TILE // c_):
        sl = slice(c * c_, (c + 1) * c_)
        q = q_ref[sl, :]
        k = k_ref[sl, :]
        v = v_ref[sl, :]
        att = lax.dot_general(q, k, _NT, preferred_element_type=F32) * dm_ref[0]
        state = st_ref[...]
        o = (jnp.dot(att.astype(BF16), v, preferred_element_type=F32)
             + qd_ref[0] * jnp.dot(q, state.astype(BF16), preferred_element_type=F32))
        kdec = (k.astype(F32) * kd_ref[0]).astype(BF16)
        kv = lax.dot_general(kdec, v, _TN, preferred_element_type=F32)
        st_ref[...] = state * cd_ref[0] + kv
        mu = jnp.mean(o, axis=-1, keepdims=True)
        oc = o - mu
        var = jnp.mean(oc * oc, axis=-1, keepdims=True)
        y = oc * lax.rsqrt(var + EPS) * go_ref[0, 0:1, :]
        gate = g_ref[sl, :].astype(F32)
        o_ref[sl, :] = (gate * jax.nn.sigmoid(gate) * y).astype(BF16)


def _retention(q_r, k_r, v_r, g_r, dm, kd, qd, cd, go, batch, seq):
    t_tokens = batch * seq
    nc = seq // RET_TILE
    tok = lambda b, h, c: (b * nc + c, h)
    per_h = lambda b, h, c: (h, 0, 0)
    sq = (1, RET_DIM, RET_DIM)
    return pl.pallas_call(
        _ret_body,
        grid=(batch, RET_HEADS, nc),
        in_specs=[pl.BlockSpec((RET_TILE, RET_DIM), tok)] * 4
        + [pl.BlockSpec(sq, per_h)] * 4 + [pl.BlockSpec((1, 8, RET_DIM), per_h)],
        out_specs=pl.BlockSpec((RET_TILE, RET_DIM), tok),
        out_shape=jax.ShapeDtypeStruct((t_tokens, RET_HEADS * RET_DIM), BF16),
        scratch_shapes=[pltpu.VMEM((RET_DIM, RET_DIM), F32)],
        compiler_params=_params(("parallel", "parallel", "arbitrary"), 32),
        name="retention",
    )(q_r, k_r, v_r, g_r, dm, kd, qd, cd, go)


def _mid_body(on_ref, or_ref, x_ref, g1_ref, sc_ref, sh_ref, gn_ref, wo_ref, wqt_ref, sk_ref,
              x1_ref, h2t_ref, st_ref):
    acc = (jnp.dot(on_ref[...], wo_ref[0:1024, :], preferred_element_type=F32)
           + jnp.dot(or_ref[...], wo_ref[1024:2048, :], preferred_element_type=F32))
    x1 = x_ref[...] + g1_ref[0] * acc
    x1_ref[...] = x1
    ms = jnp.mean(x1 * x1, axis=-1, keepdims=True)
    h2 = x1 * lax.rsqrt(ms + EPS) * gn_ref[...]
    h2 = h2 * (1.0 + sc_ref[0]) + sh_ref[0]
    h2t_ref[...] = h2.T.astype(BF16)
    h2 = h2.astype(BF16)
    qt = lax.dot_general(wqt_ref[...], h2, _NT, preferred_element_type=F32).astype(BF16)
    for hp in range(2 * PEER_HEADS):
        st_ref[hp] = jnp.dot(sk_ref[hp], qt[hp * 128:(hp + 1) * 128, :], preferred_element_type=F32)


def _mid(o_nsa, o_ret, x2, gate1, scale2, shift2, g_ffn, w_out, wq_t, sub_keys, seq):
    t_tokens = x2.shape[0]
    tm = TM_PROJ
    tps = seq // tm
    row = lambda i: (i, 0)
    per_b = lambda i: (i // tps, 0, 0)
    return pl.pallas_call(
        _mid_body,
        grid=(t_tokens // tm,),
        in_specs=[pl.BlockSpec((tm, 1024), row),
                  pl.BlockSpec((tm, 1024), row),
                  pl.BlockSpec((tm, D_MODEL), row),
                  pl.BlockSpec((1, 1, D_MODEL), per_b),
                  pl.BlockSpec((1, 1, D_MODEL), per_b),
                  pl.BlockSpec((1, 1, D_MODEL), per_b),
                  _resident((1, D_MODEL), lambda i: (0, 0)),
                  _resident((D_MODEL, D_MODEL), lambda i: (0, 0)),
                  _resident((D_MODEL, D_MODEL), lambda i: (0, 0)),
                  _resident((2 * PEER_HEADS, PEER_KEYS, 128), lambda i: (0, 0, 0))],
        out_specs=(pl.BlockSpec((tm, D_MODEL), row),
                   pl.BlockSpec((D_MODEL, tm), lambda i: (0, i)),
                   pl.BlockSpec((2 * PEER_HEADS, PEER_KEYS, tm), lambda i: (0, 0, i))),
        out_shape=(jax.ShapeDtypeStruct((t_tokens, D_MODEL), F32),
                   jax.ShapeDtypeStruct((D_MODEL, t_tokens), BF16),
                   jax.ShapeDtypeStruct((2 * PEER_HEADS, PEER_KEYS, t_tokens), F32)),
        compiler_params=_params(("parallel",), 48),
        name="outproj_peerq",
    )(o_nsa, o_ret, x2, gate1, scale2, shift2, g_ffn, w_out, wq_t, sub_keys)


def _top16(s, break_ties):
    n_rows, n = s.shape
    io = lax.broadcasted_iota(jnp.int32, (n_rows, n), 0)
    a_io = lax.broadcasted_iota(jnp.int32, (PEER_TOPK, n), 0)
    rank = jnp.full((n_rows, n), PEER_TOPK, jnp.int32)
    vals = jnp.zeros((PEER_TOPK, n), F32)
    for a in range(PEER_TOPK):
        mx = jnp.max(s, axis=0, keepdims=True)
        hit = s == mx
        if break_ties:
            hit = io == jnp.min(jnp.where(hit, io, n_rows), axis=0, keepdims=True)
        rank = jnp.where(hit, a, rank)
        s = jnp.where(hit, -jnp.inf, s)
        vals = jnp.where(a_io == a, mx, vals)
    return vals, rank


def _peer_select_body(s_ref, l_ref, w1_ref, r2_ref, w2_ref):
    s1 = s_ref[0]
    s2 = s_ref[1]
    n = s1.shape[1]

    def select(break_ties):
        v1, rank1 = _top16(s1, break_ties)
        v2, rank2 = _top16(s2, break_ties)
        a_io = lax.broadcasted_iota(jnp.int32, (PEER_TOPK, n), 0)
        cnt = jnp.zeros((PEER_TOPK, n), jnp.int32)
        cur = v1 + v2[0:1, :]
        top = v1[0:1, :] + v2[0:1, :]
        z = jnp.zeros((1, n), F32)
        for _ in range(PEER_TOPK):
            mx = jnp.max(cur, axis=0, keepdims=True)
            aidx = jnp.min(jnp.where(cur == mx, a_io, PEER_TOPK), axis=0, keepdims=True)
            hit = a_io == aidx
            cnt = cnt + hit.astype(jnp.int32)
            nxt = jnp.sum(jnp.where(hit, cnt, 0), axis=0, keepdims=True)
            nv = jnp.max(jnp.where(a_io == nxt, v2, -jnp.inf), axis=0, keepdims=True)
            cur = jnp.where(hit, v1 + nv, cur)
            z = z + jnp.exp(mx - top)
        cnt_f = cnt.astype(F32)
        lrow = jnp.zeros(s1.shape, F32)
        for a in range(PEER_TOPK):
            lrow = jnp.where(rank1 == a, cnt_f[a:a + 1, :], lrow)
        l_ref[0] = lrow
        w1_ref[0] = jnp.exp(s1 - v1[0:1, :])
        r2_ref[0] = rank2.astype(F32).astype(BF16)
        w2_ref[0] = (jnp.exp(s2 - v2[0:1, :]) * (1.0 / z)).astype(BF16)
        return rank1, rank2

    rank1, rank2 = select(False)
    marked = (jnp.sum((rank1 < PEER_TOPK).astype(jnp.int32), axis=0, keepdims=True)
              + jnp.sum((rank2 < PEER_TOPK).astype(jnp.int32), axis=0, keepdims=True))

    @pl.when(jnp.max(marked) != 2 * PEER_TOPK)
    def _():
        select(True)


def _peer_select(st):
    t_tokens = st.shape[2]
    tm = TM_SEL
    shp = jax.ShapeDtypeStruct((PEER_HEADS, PEER_KEYS, t_tokens), F32)
    shp_b = jax.ShapeDtypeStruct((PEER_HEADS, PEER_KEYS, t_tokens), BF16)
    spec = pl.BlockSpec((1, PEER_KEYS, tm), lambda i, h: (h, 0, i))
    return pl.pallas_call(
        _peer_select_body,
        grid=(t_tokens // tm, PEER_HEADS),
        in_specs=[pl.BlockSpec((2, PEER_KEYS, tm), lambda i, h: (h, 0, i))],
        out_specs=(spec, spec, spec, spec),
        out_shape=(shp, shp, shp_b, shp_b),
        compiler_params=_params(("parallel", "parallel"), 32),
        name="peer_select",
    )(st)


def _transpose_body(v_ref, o_ref):
    o_ref[...] = v_ref[...].T.astype(BF16)


def _transpose_bf16(v):
    n, d = v.shape
    tn = 512
    return pl.pallas_call(
        _transpose_body,
        grid=(n // tn,),
        in_specs=[pl.BlockSpec((tn, d), lambda i: (i, 0))],
        out_specs=pl.BlockSpec((d, tn), lambda i: (0, i)),
        out_shape=jax.ShapeDtypeStruct((d, n), BF16),
        compiler_params=_params(("parallel",), 32),
        name="transpose_v",
    )(v)


def _peer_expert_body(h2t_ref, u_ref, vt_ref, l_ref, w1_ref, r2_ref, w2_ref, o_ref, ce_ref, co_ref,
                      *, steps_per_tile):
    g = pl.program_id(0)
    sw = 256
    strips = [slice(c * sw, (c + 1) * sw) for c in range(TM_PEER // sw)]
    n_piece = TE_PEER // PEER_KEYS
    blk = D_MODEL // n_piece

    @pl.when(g == 0)
    def _():
        co_ref[...] = jnp.zeros_like(co_ref)

    @pl.when((g == 0) | ((g - 1) % steps_per_tile == 0))
    def _():
        o_ref[...] = jnp.zeros_like(o_ref)

    def run(c_new, c_old):
        def piece(j, carry):
            r0 = pl.multiple_of(j * PEER_KEYS, PEER_KEYS)
            d0 = pl.multiple_of(j * blk, blk)
            for ls in strips:
                coef = None
                for h in range(PEER_HEADS):
                    lrow = jnp.broadcast_to(l_ref[h, pl.ds(j, 1), ls], (16, sw)).astype(BF16)
                    w1row = jnp.broadcast_to(w1_ref[h, pl.ds(j, 1), ls], (16, sw)).astype(BF16)
                    lrow = jnp.tile(lrow, (PEER_KEYS // 16, 1))
                    w1row = jnp.tile(w1row, (PEER_KEYS // 16, 1))
                    term = jnp.where(r2_ref[h, :, ls] < lrow, w2_ref[h, :, ls] * w1row, jnp.zeros((), BF16))
                    coef = term if coef is None else coef + term
                a_t = jnp.dot(u_ref[pl.ds(r0, PEER_KEYS), :], h2t_ref[:, ls], preferred_element_type=F32)
                c_new[pl.ds(r0, PEER_KEYS), ls] = coef * _gelu(a_t).astype(BF16)
                o_ref[pl.ds(d0, blk), ls] += jnp.dot(vt_ref[pl.ds(d0, blk), :], c_old[:, ls],
                                                     preferred_element_type=F32)
            return carry

        lax.fori_loop(0, n_piece, piece, 0, unroll=4)

    @pl.when(g % 2 == 0)
    def _():
        run(ce_ref, co_ref)

    @pl.when(g % 2 == 1)
    def _():
        run(co_ref, ce_ref)


def _peer_expert(h2t, u_b, v_t, lrow, w1, r2, w2):
    t_tokens = h2t.shape[1]
    tm, te = TM_PEER, TE_PEER
    n_piece = te // PEER_KEYS
    ne = PEER_EXPERTS // te
    n_steps = (t_tokens // tm) * ne
    cur = lambda g: jnp.minimum(g, n_steps - 1)
    prev = lambda g: jnp.maximum(g - 1, 0)
    row_spec = pl.BlockSpec((PEER_HEADS, n_piece, tm), lambda g: (0, cur(g) % ne, cur(g) // ne))
    full_spec = pl.BlockSpec((PEER_HEADS, PEER_KEYS, tm), lambda g: (0, 0, cur(g) // ne))
    return pl.pallas_call(
        functools.partial(_peer_expert_body, steps_per_tile=ne),
        grid=(n_steps + 1,),
        in_specs=[pl.BlockSpec((D_MODEL, tm), lambda g: (0, cur(g) // ne)),
                  pl.BlockSpec((te, D_MODEL), lambda g: (cur(g) % ne, 0)),
                  pl.BlockSpec((D_MODEL, te), lambda g: (0, prev(g) % ne)),
                  row_spec, row_spec, full_spec, full_spec],
        out_specs=pl.BlockSpec((D_MODEL, tm), lambda g: (0, prev(g) // ne)),
        out_shape=jax.ShapeDtypeStruct((D_MODEL, t_tokens), F32),
        scratch_shapes=[pltpu.VMEM((te, tm), BF16), pltpu.VMEM((te, tm), BF16)],
        compiler_params=_params(("arbitrary",), 52),
        name="peer_experts",
    )(h2t, u_b, v_t, lrow, w1, r2, w2)


def _final_body(x1_ref, pt_ref, g2_ref, gn_ref, o_ref, *, apply_norm):
    y = x1_ref[...] + g2_ref[0] * pt_ref[...].T
    if apply_norm:
        ms = jnp.mean(y * y, axis=-1, keepdims=True)
        y = y * lax.rsqrt(ms + EPS) * gn_ref[...]
    o_ref[...] = y


def _final(x1, peer_t, gate2, g_final, seq, apply_norm):
    t_tokens = x1.shape[0]
    tm = TM_PROJ
    tps = seq // tm
    return pl.pallas_call(
        functools.partial(_final_body, apply_norm=apply_norm),
        grid=(t_tokens // tm,),
        in_specs=[pl.BlockSpec((tm, D_MODEL), lambda i: (i, 0)),
                  pl.BlockSpec((D_MODEL, tm), lambda i: (0, i)),
                  pl.BlockSpec((1, 1, D_MODEL), lambda i: (i // tps, 0, 0)),
                  pl.BlockSpec((1, D_MODEL), lambda i: (0, 0))],
        out_specs=pl.BlockSpec((tm, D_MODEL), lambda i: (i, 0)),
        out_shape=jax.ShapeDtypeStruct((t_tokens, D_MODEL), F32),
        compiler_params=_params(("parallel",), 32),
        name="final_norm",
    )(x1, peer_t, gate2, g_final)


def _split_cols(a, sizes):
    out, acc = [], 0
    for s in sizes:
        out.append(a[:, acc:acc + s])
        acc += s
    return out


def _inproj_weights(w_in):
    kvw = KV_GROUPS * HEAD_DIM
    sizes = (1024,) + (kvw,) * 6 + (3 * N_HEADS, 1024, 1024, 1024, 1024)
    q_a, k_c, v_c, k_s, v_s, k_w, v_w, g_a, q_r, k_r, v_r, g_r = _split_cols(w_in, sizes)
    d = w_in.shape[0]

    def grp(a, g):
        return a[:, g * HEAD_DIM:(g + 1) * HEAD_DIM]

    zeros = jnp.zeros((d, HEAD_DIM), w_in.dtype)
    cv = [jnp.concatenate([grp(k_c, g), grp(v_c, g)], axis=1) for g in range(KV_GROUPS)]
    ksp = [jnp.concatenate([grp(k_s, g), zeros], axis=1) for g in range(KV_GROUPS)]
    kwp = [jnp.concatenate([grp(k_w, g), zeros], axis=1) for g in range(KV_GROUPS)]
    w_std = jnp.concatenate(cv + ksp + kwp + [q_r, k_r, v_r, g_r], axis=1).astype(BF16)
    gcols = []
    for g in range(KV_GROUPS):
        for br in range(3):
            for r in range(GROUP):
                c = (g * GROUP + r) * 3 + br
                gcols.append(g_a[:, c:c + 1])
        gcols.append(jnp.zeros((d, 4), w_in.dtype))
    w_tr = jnp.concatenate([q_a, v_s, v_w] + gcols, axis=1).T.astype(BF16)
    return w_std, w_tr


def _compress_weights(pe_k, pe_v, k_w1, k_w2, v_w1, v_w2):
    half = CMP_LEN // 2

    def w1_half(w1k, w1v, lo):
        a = w1k.reshape(CMP_LEN, HEAD_DIM, CMP_HIDDEN)[lo:lo + half]
        b = w1v.reshape(CMP_LEN, HEAD_DIM, CMP_HIDDEN)[lo:lo + half]
        za = jnp.zeros_like(a)
        top = jnp.concatenate([a, za], axis=2)
        bot = jnp.concatenate([za, b], axis=2)
        return jnp.concatenate([top, bot], axis=1).reshape(half * 2 * HEAD_DIM, 2 * CMP_HIDDEN).astype(BF16)

    def pe_half(lo):
        row = jnp.concatenate([pe_k[lo:lo + half], pe_v[lo:lo + half]], axis=1).reshape(1, -1)
        return jnp.broadcast_to(row, (8, row.shape[1])).astype(BF16)

    wa = w1_half(k_w1, v_w1, 0)
    wb = w1_half(k_w1, v_w1, half)
    w2k = jnp.zeros((2 * CMP_HIDDEN, LANES), F32).at[:CMP_HIDDEN, :HEAD_DIM].set(k_w2).astype(BF16)
    w2vt = jnp.zeros((HEAD_DIM, 2 * CMP_HIDDEN), F32).at[:, CMP_HIDDEN:].set(v_w2.T).astype(BF16)
    return wa, wb, pe_half(0), pe_half(half), w2k, w2vt


def _nsa_constants(seq):
    slopes = jnp.exp2(-8.0 * (jnp.arange(N_HEADS, dtype=F32) + 1.0) / N_HEADS) * LOG2E
    s_hi = slopes.astype(BF16)
    s_lo = (slopes - s_hi.astype(F32)).astype(BF16)
    rows = jnp.zeros((N_HEADS, HEAD_DIM), BF16)
    rows = rows.at[:, 0].set(s_hi).at[:, 1].set(s_hi).at[:, 2].set(s_lo).at[:, 3].set(s_lo)
    qaug = jnp.broadcast_to(rows[:, :, None], (N_HEADS, HEAD_DIM, LANES))
    n_rows = seq // CMP_STRIDE
    n_slc = seq // SLC_BLOCK
    start = np.arange(n_rows)[:, None] * CMP_STRIDE
    end = start + CMP_LEN - 1
    blk = np.arange(n_slc)[None, :] * SLC_BLOCK
    ovl = ((start < blk + SLC_BLOCK) & (end >= blk)).astype(np.float32)
    per_tile = TK_SEL // SLC_BLOCK
    grp = (np.arange(n_slc)[None, :] // per_tile == np.arange(n_slc // per_tile)[:, None])
    return qaug, jnp.asarray(ovl.T, BF16), jnp.asarray(grp.astype(np.float32), BF16)


def _retention_constants():
    h, c = RET_HEADS, RET_CHUNK
    lg = jnp.log1p(-jnp.exp2(-5.0 - jnp.arange(h, dtype=F32)))
    pos = jnp.arange(c, dtype=F32)
    diff = pos[:, None] - pos[None, :]
    scale = RET_DIM ** -0.5
    dm = jnp.where(diff >= 0, jnp.exp(lg[:, None, None] * jnp.maximum(diff, 0.0)), 0.0) * scale
    k_decay = jnp.exp(lg[:, None] * (c - 1.0 - pos)) * scale
    q_decay = jnp.exp(lg[:, None] * (pos + 1.0))
    chunk_decay = jnp.exp(lg * c)
    kd = jnp.broadcast_to(k_decay[:, :, None], (h, c, RET_DIM))
    qd = jnp.broadcast_to(q_decay[:, :, None], (h, c, RET_DIM))
    cd = jnp.broadcast_to(chunk_decay[:, None, None], (h, RET_DIM, RET_DIM))
    return dm, kd, qd, cd


def kernel(x, c, w_ada, b_ada, g_norm_mix, g_norm_ffn, g_norm_final, w_in, cmp_pe_k, cmp_pe_v,
           cmp_k_w1, cmp_k_w2, cmp_v_w1, cmp_v_w2, g_nsa_out, g_ret_out, w_out,
           peer_w_q, peer_sub_keys, peer_u, peer_v):
    batch, seq, d = x.shape
    depth = w_ada.shape[0]
    t_tokens = batch * seq
    xf = x.reshape(t_tokens, d)
    c_pad = jnp.zeros((8, d), F32).at[:batch].set(c)
    qaug, ovl_t, grp = _nsa_constants(seq)
    dm, kd, qd, cd = _retention_constants()

    for l in range(depth):
        mod = _adaln(c_pad, w_ada[l], b_ada[l][None, :])[:batch].reshape(batch, 6, 1, d)
        shift1, scale1, gate1, shift2, scale2, gate2 = (mod[:, k] for k in range(6))

        w_std, w_tr = _inproj_weights(w_in[l])
        (cv, ks, kw, q_r, k_r, v_r, g_r, qt, vst, vwt, gt) = _inproj(
            xf, scale1, shift1, g_norm_mix[l][None, :], w_std, w_tr, seq)

        cv4 = cv.reshape(KV_GROUPS, batch, seq // CMP_STRIDE, CMP_STRIDE * LANES)
        kcp, vct = _compress(cv4, *_compress_weights(cmp_pe_k[l], cmp_pe_v[l], cmp_k_w1[l], cmp_k_w2[l],
                                                     cmp_v_w1[l], cmp_v_w2[l]))
        gout_b = jnp.broadcast_to(g_nsa_out[l][:, :, None], (N_HEADS, HEAD_DIM, LANES))
        o_nsa = _nsa(qt, gt, kcp, vct, ks, kw, vst, vwt, qaug, gout_b, ovl_t, grp, batch, seq)

        go = jnp.broadcast_to(g_ret_out[l][:, None, :], (RET_HEADS, 8, RET_DIM))
        o_ret = _retention(q_r, k_r, v_r, g_r, dm, kd, qd, cd, go, batch, seq)

        sub_keys = peer_sub_keys[l].reshape(2 * PEER_HEADS, PEER_KEYS, -1).astype(BF16)
        x1, h2, st = _mid(o_nsa, o_ret, xf, gate1, scale2, shift2, g_norm_ffn[l][None, :],
                          w_out[l].astype(BF16), peer_w_q[l].T.astype(BF16), sub_keys, seq)

        lrow, w1, r2, w2 = _peer_select(st)
        peer_t = _peer_expert(h2, peer_u[l].astype(BF16), _transpose_bf16(peer_v[l]), lrow, w1, r2, w2)
        xf = _final(x1, peer_t, gate2, g_norm_final[None, :], seq, apply_norm=(l == depth - 1))
    return xf.reshape(batch, seq, d)
```

```python
import functools
import math

import numpy as np
import jax
import jax.numpy as jnp
from jax import lax
from jax.experimental import pallas as pl
from jax.experimental.pallas import tpu as pltpu

F32 = jnp.float32
BF16 = jnp.bfloat16
FP8 = jnp.float8_e4m3fn
FP8_TARGET = 224.0

D_MODEL = 2048
N_HEADS = 16
HEAD_DIM = 64
KV_GROUPS = 4
GROUP = 4
CMP_LEN = 32
CMP_STRIDE = 16
CMP_HIDDEN = 128
SLC_BLOCK = 64
SLC_TOPK = 16
WINDOW = 512
FORCE_SCORE = 1e4
NEG = -1e30
RET_HEADS = 8
RET_DIM = 128
RET_CHUNK = 128
PEER_HEADS = 8
PEER_KEYS = 128
PEER_EXPERTS = PEER_KEYS * PEER_KEYS
PEER_TOPK = 16
COEF_BOUND_FACTOR = 1.25 * PEER_HEADS
EPS = 1e-6
LOG2E = 1.4426950408889634

LANES = 128
TQ = 128
TK_SEL = 256
TM_PROJ = 256
TM_PEER = 512
TE_PEER = 1024
TM_SEL = 256
RET_TILE = 512

_NT = (((1,), (1,)), ((), ()))
_TN = (((0,), (0,)), ((), ()))


def _params(sem, vmem_mb):
    return pltpu.CompilerParams(dimension_semantics=sem, vmem_limit_bytes=vmem_mb * 1024 * 1024)


def _resident(shape, index_map):
    return pl.BlockSpec(shape, index_map, pipeline_mode=pl.Buffered(1))


def _gelu(x):
    return jax.nn.gelu(x)


def _adaln_body(c_ref, w_ref, b_ref, o_ref):
    c = c_ref[...]
    act = (c * jax.nn.sigmoid(c)).astype(BF16)
    o_ref[...] = jnp.dot(act, w_ref[...].astype(BF16), preferred_element_type=F32) + b_ref[...]


def _adaln(c_pad, w, b):
    n = w.shape[1]
    tn = 1536
    return pl.pallas_call(
        _adaln_body,
        grid=(n // tn,),
        in_specs=[pl.BlockSpec((8, D_MODEL), lambda j: (0, 0)),
                  pl.BlockSpec((D_MODEL, tn), lambda j: (0, j)),
                  pl.BlockSpec((1, tn), lambda j: (0, j))],
        out_specs=pl.BlockSpec((8, tn), lambda j: (0, j)),
        out_shape=jax.ShapeDtypeStruct((8, n), F32),
        compiler_params=_params(("arbitrary",), 40),
        name="adaln",
    )(c_pad, w, b)


STD_COLS = 512 * 3 + 1024 * 4
TR_ROWS = 1024 + 256 + 256 + 64


def _inproj_body(x_ref, sc_ref, sh_ref, gn_ref, wstd_ref, wt_ref,
                 cv_ref, ks_ref, kw_ref, qr_ref, kr_ref, vr_ref, gr_ref,
                 qt_ref, vst_ref, vwt_ref, gt_ref, *, tiles_per_seq):
    tm = TM_PROJ
    i = pl.program_id(0)
    x = x_ref[...]
    ms = jnp.mean(x * x, axis=-1, keepdims=True)
    h = x * lax.rsqrt(ms + EPS) * gn_ref[...]
    h = h * (1.0 + sc_ref[0]) + sh_ref[0]
    hb = h.astype(BF16)

    def std(a, b):
        return jnp.dot(hb, wstd_ref[:, a:b], preferred_element_type=F32)

    y = std(0, 512)
    for g in range(KV_GROUPS):
        cv_ref[g] = y[:, g * LANES:(g + 1) * LANES].astype(BF16)

    t = (i % tiles_per_seq) * tm + lax.broadcasted_iota(jnp.int32, (tm, LANES), 0)
    lane = lax.broadcasted_iota(jnp.int32, (tm, LANES), 1)
    pos_hi = ((t >> 6) << 6).astype(F32)
    pos_lo = (t & 63).astype(F32)
    aug = jnp.where((lane == 64) | (lane == 66), pos_hi,
                    jnp.where((lane == 65) | (lane == 67), pos_lo, 0.0))
    for ref, off in ((ks_ref, 512), (kw_ref, 1024)):
        y = std(off, off + 512)
        for g in range(KV_GROUPS):
            ref[:, g * LANES:(g + 1) * LANES] = (y[:, g * LANES:(g + 1) * LANES] + aug).astype(BF16)

    for ref, off in ((qr_ref, 1536), (kr_ref, 2560), (vr_ref, 3584), (gr_ref, 4608)):
        ref[...] = std(off, off + 1024).astype(BF16)

    def tr(a, b):
        return lax.dot_general(wt_ref[a:b, :], hb, _NT, preferred_element_type=F32)

    qt = tr(0, 1024) * (HEAD_DIM ** -0.5 * LOG2E)
    vst = tr(1024, 1280)
    vwt = tr(1280, 1536)
    gt = tr(1536, 1600)
    for c in range(tm // LANES):
        sl = slice(c * LANES, (c + 1) * LANES)
        qt_ref[c] = qt[:, sl].astype(BF16)
        vst_ref[c] = vst[:, sl].astype(BF16)
        vwt_ref[c] = vwt[:, sl].astype(BF16)
        gt_ref[c] = gt[:, sl]


def _inproj(x2, scale1, shift1, g_mix, w_std, w_tr, seq):
    t_tokens = x2.shape[0]
    tm = TM_PROJ
    tps = seq // tm
    nt = t_tokens // tm
    c = tm // LANES
    row = lambda i: (i, 0)
    per_b = lambda i: (i // tps, 0, 0)
    out_shape = (
        jax.ShapeDtypeStruct((KV_GROUPS, t_tokens, LANES), BF16),
        jax.ShapeDtypeStruct((t_tokens, 512), BF16),
        jax.ShapeDtypeStruct((t_tokens, 512), BF16),
        jax.ShapeDtypeStruct((t_tokens, 1024), BF16),
        jax.ShapeDtypeStruct((t_tokens, 1024), BF16),
        jax.ShapeDtypeStruct((t_tokens, 1024), BF16),
        jax.ShapeDtypeStruct((t_tokens, 1024), BF16),
        jax.ShapeDtypeStruct((t_tokens // LANES, 1024, LANES), BF16),
        jax.ShapeDtypeStruct((t_tokens // LANES, 256, LANES), BF16),
        jax.ShapeDtypeStruct((t_tokens // LANES, 256, LANES), BF16),
        jax.ShapeDtypeStruct((t_tokens // LANES, 64, LANES), F32),
    )
    out_specs = (
        pl.BlockSpec((KV_GROUPS, tm, LANES), lambda i: (0, i, 0)),
        pl.BlockSpec((tm, 512), row),
        pl.BlockSpec((tm, 512), row),
        pl.BlockSpec((tm, 1024), row),
        pl.BlockSpec((tm, 1024), row),
        pl.BlockSpec((tm, 1024), row),
        pl.BlockSpec((tm, 1024), row),
        pl.BlockSpec((c, 1024, LANES), lambda i: (i, 0, 0)),
        pl.BlockSpec((c, 256, LANES), lambda i: (i, 0, 0)),
        pl.BlockSpec((c, 256, LANES), lambda i: (i, 0, 0)),
        pl.BlockSpec((c, 64, LANES), lambda i: (i, 0, 0)),
    )
    return pl.pallas_call(
        functools.partial(_inproj_body, tiles_per_seq=tps),
        grid=(nt,),
        in_specs=[pl.BlockSpec((tm, D_MODEL), row),
                  pl.BlockSpec((1, 1, D_MODEL), per_b),
                  pl.BlockSpec((1, 1, D_MODEL), per_b),
                  _resident((1, D_MODEL), lambda i: (0, 0)),
                  _resident((D_MODEL, STD_COLS), lambda i: (0, 0)),
                  _resident((TR_ROWS, D_MODEL), lambda i: (0, 0))],
        out_specs=out_specs,
        out_shape=out_shape,
        compiler_params=_params(("parallel",), 56),
        name="inproj",
    )(x2, scale1, shift1, g_mix, w_std, w_tr)


def _compress_body(x_ref, wa_ref, wb_ref, pea_ref, peb_ref, w2k_ref, w2vt_ref, kcp_ref, vct_ref):
    x = x_ref[0, 0]
    n_rows = x.shape[0]
    p = jnp.dot(x, wa_ref[...], preferred_element_type=F32)
    q = jnp.dot(x, wb_ref[...], preferred_element_type=F32)
    pe = (jnp.dot(pea_ref[...], wa_ref[...], preferred_element_type=F32)
          + jnp.dot(peb_ref[...], wb_ref[...], preferred_element_type=F32))[0:1, :]
    pre = p + pltpu.roll(q, n_rows - 1, 0) + pe
    hid = _gelu(pre).astype(BF16)
    kc = jnp.dot(hid, w2k_ref[...], preferred_element_type=F32)
    n = lax.broadcasted_iota(jnp.int32, (n_rows, LANES), 0)
    lane = lax.broadcasted_iota(jnp.int32, (n_rows, LANES), 1)
    ce = n * CMP_STRIDE + (CMP_LEN - 1)
    ce_hi = ((ce >> 6) << 6).astype(F32)
    ce_lo = (ce & 63).astype(F32)
    aug = jnp.where((lane == 64) | (lane == 66), ce_hi,
                    jnp.where((lane == 65) | (lane == 67), ce_lo, 0.0))
    kcp_ref[0] = (kc + aug).astype(BF16)
    vct_ref[0] = lax.dot_general(w2vt_ref[...], hid, _NT, preferred_element_type=F32).astype(BF16)


def _compress(cv4, wa, wb, pea, peb, w2k, w2vt):
    g_, b_, n_rows, _ = cv4.shape
    const2 = lambda n: (0, 0)
    return pl.pallas_call(
        _compress_body,
        grid=(b_ * g_,),
        in_specs=[pl.BlockSpec((1, 1, n_rows, 2048), lambda n: (n % KV_GROUPS, n // KV_GROUPS, 0, 0)),
                  pl.BlockSpec((2048, 256), const2),
                  pl.BlockSpec((2048, 256), const2),
                  pl.BlockSpec((8, 2048), const2),
                  pl.BlockSpec((8, 2048), const2),
                  pl.BlockSpec((256, LANES), const2),
                  pl.BlockSpec((64, 256), const2)],
        out_specs=(pl.BlockSpec((1, n_rows, LANES), lambda n: (n, 0, 0)),
                   pl.BlockSpec((1, 64, n_rows), lambda n: (n, 0, 0))),
        out_shape=(jax.ShapeDtypeStruct((b_ * g_, n_rows, LANES), BF16),
                   jax.ShapeDtypeStruct((b_ * g_, 64, n_rows), BF16)),
        compiler_params=_params(("parallel",), 32),
        name="nsa_compress",
    )(cv4, wa, wb, pea, peb, w2k, w2vt)


def _softmax_step(state, s, pv_prev):
    m_i, l_i, acc = state
    m_new = jnp.maximum(m_i, jnp.max(s, axis=0, keepdims=True))
    alpha = jnp.exp2(m_i - m_new)
    p = jnp.exp2(s - m_new)
    l_new = alpha * l_i + jnp.sum(p, axis=0, keepdims=True)
    return (m_new, l_new, (acc + pv_prev) * alpha), p.astype(BF16)


def _nsa_body(qt_ref, gt_ref, kcp_ref, vct_ref, ks_ref, kw_ref, vst_ref, vwt_ref,
              qaug_ref, gout_ref, ovl_ref, grp_ref, o_ref, selb_ref, sa_ref, sb_ref, pa_ref, pb_ref, *, n_cmp):
    qi = pl.program_id(2)
    t0 = qi * TQ
    wq = GROUP * TQ

    qp = jnp.concatenate(
        [jnp.concatenate([qt_ref[0, r * HEAD_DIM:(r + 1) * HEAD_DIM, :], qaug_ref[r]], axis=0)
         for r in range(GROUP)], axis=1)

    lane_q = lax.broadcasted_iota(jnp.int32, (1, wq), 1) & (TQ - 1)
    n_win = WINDOW // TQ + 1
    k_lo = pl.multiple_of(jnp.maximum(t0 - WINDOW, 0), TQ)

    s = jnp.dot(kcp_ref[0], qp, preferred_element_type=F32)
    s_w = jnp.dot(kw_ref[pl.ds(k_lo, n_win * TQ), :], qp, preferred_element_type=F32)
    n_io = lax.broadcasted_iota(jnp.int32, (n_cmp, wq), 0)
    tl = lax.broadcasted_iota(jnp.int32, (n_cmp, wq), 1) & (TQ - 1)
    valid = (n_io * CMP_STRIDE + (CMP_LEN - 1)) <= (t0 + tl)
    s = jnp.where(valid, s, NEG)
    m = jnp.max(s, axis=0, keepdims=True)
    p = jnp.where(valid, jnp.exp2(s - m), 0.0)
    l = jnp.sum(p, axis=0, keepdims=True)
    pn = p * (1.0 / jnp.maximum(l, 1e-30))
    o_c = jnp.dot(vct_ref[0], pn.astype(BF16), preferred_element_type=F32)

    ps = pn[:, 0:TQ] + pn[:, TQ:2 * TQ] + pn[:, 2 * TQ:3 * TQ] + pn[:, 3 * TQ:4 * TQ]
    hi = ps.astype(BF16)
    r1 = ps - hi.astype(F32)
    mid = r1.astype(BF16)
    lo = (r1 - mid.astype(F32)).astype(BF16)
    ovl = ovl_ref[...]
    imp = (jnp.dot(ovl, hi, preferred_element_type=F32)
           + jnp.dot(ovl, mid, preferred_element_type=F32)
           + jnp.dot(ovl, lo, preferred_element_type=F32))

    n_slc = imp.shape[0]
    m_io = lax.broadcasted_iota(jnp.int32, (n_slc, TQ), 0)
    q_io = lax.broadcasted_iota(jnp.int32, (n_slc, TQ), 1)
    back = ((t0 + q_io) >> 6) - m_io
    valid_s = back >= 0
    forced = valid_s & ((m_io == 0) | (back < 2))
    w = jnp.where(forced, -jnp.inf, jnp.where(valid_s, imp, -1.0))
    selb = jnp.where(forced, 0.0, NEG)

    def pick(carry, lanes=None):
        w, selb = carry
        mx = jnp.max(w, axis=0, keepdims=True)
        idx = jnp.min(jnp.where(w == mx, m_io, n_slc), axis=0, keepdims=True)
        hit = m_io == idx
        if lanes is not None:
            hit = hit & lanes
        return jnp.where(hit, -jnp.inf, w), jnp.where(hit, 0.0, selb)

    d = (t0 - k_lo) + lane_q - lax.broadcasted_iota(jnp.int32, (n_win * TQ, wq), 0)
    s_w = jnp.where((d >= 0) & (d < WINDOW), s_w, NEG)
    p = jnp.exp2(s_w - jnp.max(s_w, axis=0, keepdims=True))
    l_w = jnp.sum(p, axis=0, keepdims=True)
    vt_w = jnp.concatenate([vwt_ref[k_lo // TQ + u] for u in range(n_win)], axis=1)
    o_w = jnp.dot(vt_w, p.astype(BF16), preferred_element_type=F32) * (1.0 / l_w)

    carry = (w, selb)
    for _ in range(SLC_TOPK - 3):
        carry = pick(carry)
    carry = pick(carry, lanes=(t0 + q_io) < 2 * SLC_BLOCK)
    _, selb = pick(carry, lanes=(t0 + q_io) < SLC_BLOCK)
    for mblk in range(n_slc):
        selb_ref[mblk] = jnp.broadcast_to(selb[mblk:mblk + 1, :], (8, TQ))

    per_blk = TK_SEL // SLC_BLOCK

    def scores(j):
        k0 = pl.multiple_of(j * TK_SEL, TK_SEL)
        bias = jnp.concatenate(
            [jnp.tile(selb_ref[j * per_blk + u], (SLC_BLOCK // 8, 1)) for u in range(per_blk)], axis=0)
        bias = jnp.concatenate([bias] * GROUP, axis=1)
        return jnp.dot(ks_ref[pl.ds(k0, TK_SEL), :], qp, preferred_element_type=F32) + bias

    def vt_tile(j):
        return jnp.concatenate([vst_ref[2 * j], vst_ref[2 * j + 1]], axis=1)

    def causal(j):
        kr = lax.broadcasted_iota(jnp.int32, (TK_SEL, wq), 0)
        return kr <= (t0 - j * TK_SEL) + lane_q

    def pv_dot(j, p_ref):
        return jnp.dot(vt_tile(j), p_ref[...], preferred_element_type=F32)

    jd = qi // (TK_SEL // TQ)

    sel01 = jnp.where(selb == 0.0, 1.0, 0.0).astype(BF16)
    tile_any = jnp.max(jnp.dot(grp_ref[...], sel01, preferred_element_type=F32), axis=1, keepdims=True)
    j_io = lax.broadcasted_iota(jnp.int32, tile_any.shape, 0)
    q_lo = jnp.max(jnp.where((tile_any == 0.0) & (j_io <= jd), j_io, -1)) + 1
    n_pre = jnp.max(jnp.where((tile_any > 0.0) & (j_io < q_lo), j_io, -1)) + 1
    n_vis = n_pre + jd - q_lo + 1

    def tile_at(pos):
        return jnp.where(pos < n_pre, pos, pos - n_pre + q_lo)

    state = (jnp.full((1, wq), NEG, F32), jnp.zeros((1, wq), F32), jnp.zeros((HEAD_DIM, wq), F32))
    sa_ref[...] = scores(tile_at(0))
    pb_ref[...] = jnp.zeros_like(pb_ref)

    def pair(u, state):
        a = 2 * u
        sb_ref[...] = scores(tile_at(a + 1))
        state, p = _softmax_step(state, sa_ref[...], pv_dot(tile_at(jnp.maximum(a - 1, 0)), pb_ref))
        pa_ref[...] = p
        sa_ref[...] = scores(tile_at(a + 2))
        state, p = _softmax_step(state, sb_ref[...], pv_dot(tile_at(a), pa_ref))
        pb_ref[...] = p
        return state

    n_pair = (n_vis - 1) // 2
    state = lax.fori_loop(0, n_pair, pair, state)
    x = 2 * n_pair
    y = jnp.minimum(x + 1, n_vis - 1)
    tx, ty = tile_at(x), tile_at(y)
    sb_ref[...] = scores(ty)
    state, p = _softmax_step(state, jnp.where(causal(tx), sa_ref[...], NEG),
                             pv_dot(tile_at(jnp.maximum(x - 1, 0)), pb_ref))
    pa_ref[...] = p
    ty_mask = jnp.where(x + 1 < n_vis, ty, jd + 1)
    (_, l_s, acc_s), p = _softmax_step(state, jnp.where(causal(ty_mask), sb_ref[...], NEG), pv_dot(tx, pa_ref))
    acc_s = acc_s + jnp.dot(vt_tile(ty), p, preferred_element_type=F32)

    o_s = acc_s * (1.0 / l_s)
    gw = jax.nn.sigmoid(gt_ref[0])
    outs = []
    for r in range(GROUP):
        sl = slice(r * TQ, (r + 1) * TQ)
        o = (gw[r:r + 1, :] * o_c[:, sl] + gw[GROUP + r:GROUP + r + 1, :] * o_s[:, sl]
             + gw[2 * GROUP + r:2 * GROUP + r + 1, :] * o_w[:, sl])
        ms = jnp.mean(o * o, axis=0, keepdims=True)
        outs.append(o * lax.rsqrt(ms + EPS) * gout_ref[r])
    o_ref[...] = jnp.concatenate(outs, axis=0).T.astype(BF16)


def _nsa(qt, gt, kcp, vct, ks, kw, vst, vwt, qaug, gout_b, ovl_t, grp, batch, seq):
    nq = seq // TQ
    n_cmp = kcp.shape[1]
    n_slc = seq // SLC_BLOCK
    t_tokens = batch * seq
    per_b_chunks = seq // LANES
    return pl.pallas_call(
        functools.partial(_nsa_body, n_cmp=n_cmp),
        grid=(batch, KV_GROUPS, nq),
        in_specs=[
            pl.BlockSpec((1, 256, LANES), lambda b, g, q: (b * nq + q, g, 0)),
            pl.BlockSpec((1, 16, LANES), lambda b, g, q: (b * nq + q, g, 0)),
            pl.BlockSpec((1, n_cmp, LANES), lambda b, g, q: (b * KV_GROUPS + g, 0, 0)),
            pl.BlockSpec((1, 64, n_cmp), lambda b, g, q: (b * KV_GROUPS + g, 0, 0)),
            pl.BlockSpec((seq, LANES), lambda b, g, q: (b, g)),
            pl.BlockSpec((seq, LANES), lambda b, g, q: (b, g)),
            pl.BlockSpec((per_b_chunks, 64, LANES), lambda b, g, q: (b, g, 0)),
            pl.BlockSpec((per_b_chunks, 64, LANES), lambda b, g, q: (b, g, 0)),
            pl.BlockSpec((GROUP, 64, LANES), lambda b, g, q: (g, 0, 0)),
            pl.BlockSpec((GROUP, 64, LANES), lambda b, g, q: (g, 0, 0)),
            pl.BlockSpec((n_slc, n_cmp), lambda b, g, q: (0, 0)),
            pl.BlockSpec(grp.shape, lambda b, g, q: (0, 0)),
        ],
        out_specs=pl.BlockSpec((TQ, 256), lambda b, g, q: (b * nq + q, g)),
        out_shape=jax.ShapeDtypeStruct((t_tokens, 1024), BF16),
        scratch_shapes=[pltpu.VMEM((n_slc, 8, TQ), F32),
                        pltpu.VMEM((TK_SEL, GROUP * TQ), F32), pltpu.VMEM((TK_SEL, GROUP * TQ), F32),
                        pltpu.VMEM((TK_SEL, GROUP * TQ), BF16), pltpu.VMEM((TK_SEL, GROUP * TQ), BF16)],
        compiler_params=_params(("parallel", "parallel", "arbitrary"), 40),
        name="nsa_attention",
    )(qt, gt, kcp, vct, ks, kw, vst, vwt, qaug, gout_b, ovl_t, grp)


def _ret_body(q_ref, k_ref, v_ref, g_ref, dm_ref, kd_ref, qd_ref, cd_ref, go_ref, o_ref, st_ref):
    @pl.when(pl.program_id(2) == 0)
    def _():
        st_ref[...] = jnp.zeros_like(st_ref)

    c_ = RET_CHUNK
    for c in range(RET_TILE // c_):
        sl = slice(c * c_, (c + 1) * c_)
        q = q_ref[sl, :]
        k = k_ref[sl, :]
        v = v_ref[sl, :]
        att = lax.dot_general(q, k, _NT, preferred_element_type=F32) * dm_ref[0]
        state = st_ref[...]
        o = (jnp.dot(att.astype(BF16), v, preferred_element_type=F32)
             + qd_ref[0] * jnp.dot(q, state.astype(BF16), preferred_element_type=F32))
        kdec = (k.astype(F32) * kd_ref[0]).astype(BF16)
        kv = lax.dot_general(kdec, v, _TN, preferred_element_type=F32)
        st_ref[...] = state * cd_ref[0] + kv
        mu = jnp.mean(o, axis=-1, keepdims=True)
        oc = o - mu
        var = jnp.mean(oc * oc, axis=-1, keepdims=True)
        y = oc * lax.rsqrt(var + EPS) * go_ref[0, 0:1, :]
        gate = g_ref[sl, :].astype(F32)
        o_ref[sl, :] = (gate * jax.nn.sigmoid(gate) * y).astype(BF16)


def _retention(q_r, k_r, v_r, g_r, dm, kd, qd, cd, go, batch, seq):
    t_tokens = batch * seq
    nc = seq // RET_TILE
    tok = lambda b, h, c: (b * nc + c, h)
    per_h = lambda b, h, c: (h, 0, 0)
    sq = (1, RET_DIM, RET_DIM)
    return pl.pallas_call(
        _ret_body,
        grid=(batch, RET_HEADS, nc),
        in_specs=[pl.BlockSpec((RET_TILE, RET_DIM), tok)] * 4
        + [pl.BlockSpec(sq, per_h)] * 4 + [pl.BlockSpec((1, 8, RET_DIM), per_h)],
        out_specs=pl.BlockSpec((RET_TILE, RET_DIM), tok),
        out_shape=jax.ShapeDtypeStruct((t_tokens, RET_HEADS * RET_DIM), BF16),
        scratch_shapes=[pltpu.VMEM((RET_DIM, RET_DIM), F32)],
        compiler_params=_params(("parallel", "parallel", "arbitrary"), 32),
        name="retention",
    )(q_r, k_r, v_r, g_r, dm, kd, qd, cd, go)


def _pow2_scale(magnitude):
    return jnp.exp2(jnp.floor(jnp.log2(FP8_TARGET / jnp.maximum(magnitude, 1e-30))))


def _mid_body(on_ref, or_ref, x_ref, g1_ref, sc_ref, sh_ref, gn_ref, wo_ref, wqt_ref, sk_ref, ps_ref,
              x1_ref, h2t_ref, st_ref, scl_ref):
    acc = (jnp.dot(on_ref[...], wo_ref[0:1024, :], preferred_element_type=F32)
           + jnp.dot(or_ref[...], wo_ref[1024:2048, :], preferred_element_type=F32))
    x1 = x_ref[...] + g1_ref[0] * acc
    x1_ref[...] = x1
    ms = jnp.mean(x1 * x1, axis=-1, keepdims=True)
    h2 = x1 * lax.rsqrt(ms + EPS) * gn_ref[...]
    h2 = h2 * (1.0 + sc_ref[0]) + sh_ref[0]

    h2_t = h2.T
    amax = jnp.max(jnp.max(jnp.abs(h2_t), axis=0, keepdims=True), axis=1, keepdims=True)
    s_h = _pow2_scale(amax)
    h2t_ref[...] = (h2_t * s_h).astype(FP8)
    norm = jnp.sqrt(jnp.sum(h2_t * h2_t, axis=0, keepdims=True))
    s_c = _pow2_scale(COEF_BOUND_FACTOR * ps_ref[2:3, 0:1] * norm)
    scl_ref[...] = jnp.concatenate(
        [jnp.broadcast_to(ps_ref[0:1, 0:1] / s_h, s_c.shape), s_c, ps_ref[1:2, 0:1] / s_c,
         jnp.zeros((5, s_c.shape[1]), F32)], axis=0)
    h2 = h2.astype(BF16)
    qt = lax.dot_general(wqt_ref[...], h2, _NT, preferred_element_type=F32).astype(BF16)
    for hp in range(2 * PEER_HEADS):
        st_ref[hp] = jnp.dot(sk_ref[hp], qt[hp * 128:(hp + 1) * 128, :], preferred_element_type=F32)


def _mid(o_nsa, o_ret, x2, gate1, scale2, shift2, g_ffn, w_out, wq_t, sub_keys, peer_scales, seq):
    t_tokens = x2.shape[0]
    tm = TM_PROJ
    tps = seq // tm
    row = lambda i: (i, 0)
    per_b = lambda i: (i // tps, 0, 0)
    return pl.pallas_call(
        _mid_body,
        grid=(t_tokens // tm,),
        in_specs=[pl.BlockSpec((tm, 1024), row),
                  pl.BlockSpec((tm, 1024), row),
                  pl.BlockSpec((tm, D_MODEL), row),
                  pl.BlockSpec((1, 1, D_MODEL), per_b),
                  pl.BlockSpec((1, 1, D_MODEL), per_b),
                  pl.BlockSpec((1, 1, D_MODEL), per_b),
                  _resident((1, D_MODEL), lambda i: (0, 0)),
                  _resident((D_MODEL, D_MODEL), lambda i: (0, 0)),
                  _resident((D_MODEL, D_MODEL), lambda i: (0, 0)),
                  _resident((2 * PEER_HEADS, PEER_KEYS, 128), lambda i: (0, 0, 0)),
                  _resident((8, LANES), lambda i: (0, 0))],
        out_specs=(pl.BlockSpec((tm, D_MODEL), row),
                   pl.BlockSpec((D_MODEL, tm), lambda i: (0, i)),
                   pl.BlockSpec((2 * PEER_HEADS, PEER_KEYS, tm), lambda i: (0, 0, i)),
                   pl.BlockSpec((8, tm), lambda i: (0, i))),
        out_shape=(jax.ShapeDtypeStruct((t_tokens, D_MODEL), F32),
                   jax.ShapeDtypeStruct((D_MODEL, t_tokens), FP8),
                   jax.ShapeDtypeStruct((2 * PEER_HEADS, PEER_KEYS, t_tokens), F32),
                   jax.ShapeDtypeStruct((8, t_tokens), F32)),
        compiler_params=_params(("parallel",), 48),
        name="outproj_peerq",
    )(o_nsa, o_ret, x2, gate1, scale2, shift2, g_ffn, w_out, wq_t, sub_keys, peer_scales)


def _top16(s, break_ties):
    n_rows, n = s.shape
    io = lax.broadcasted_iota(jnp.int32, (n_rows, n), 0)
    a_io = lax.broadcasted_iota(jnp.int32, (PEER_TOPK, n), 0)
    rank = jnp.full((n_rows, n), PEER_TOPK, jnp.int32)
    vals = jnp.zeros((PEER_TOPK, n), F32)
    for a in range(PEER_TOPK):
        mx = jnp.max(s, axis=0, keepdims=True)
        hit = s == mx
        if break_ties:
            hit = io == jnp.min(jnp.where(hit, io, n_rows), axis=0, keepdims=True)
        rank = jnp.where(hit, a, rank)
        s = jnp.where(hit, -jnp.inf, s)
        vals = jnp.where(a_io == a, mx, vals)
    return vals, rank


def _peer_select_body(s_ref, l_ref, w1_ref, r2_ref, w2_ref):
    s1 = s_ref[0]
    s2 = s_ref[1]
    n = s1.shape[1]

    def select(break_ties):
        v1, rank1 = _top16(s1, break_ties)
        v2, rank2 = _top16(s2, break_ties)
        a_io = lax.broadcasted_iota(jnp.int32, (PEER_TOPK, n), 0)
        cnt = jnp.zeros((PEER_TOPK, n), jnp.int32)
        cur = v1 + v2[0:1, :]
        top = v1[0:1, :] + v2[0:1, :]
        z = jnp.zeros((1, n), F32)
        for _ in range(PEER_TOPK):
            mx = jnp.max(cur, axis=0, keepdims=True)
            aidx = jnp.min(jnp.where(cur == mx, a_io, PEER_TOPK), axis=0, keepdims=True)
            hit = a_io == aidx
            cnt = cnt + hit.astype(jnp.int32)
            nxt = jnp.sum(jnp.where(hit, cnt, 0), axis=0, keepdims=True)
            nv = jnp.max(jnp.where(a_io == nxt, v2, -jnp.inf), axis=0, keepdims=True)
            cur = jnp.where(hit, v1 + nv, cur)
            z = z + jnp.exp(mx - top)
        cnt_f = cnt.astype(F32)
        lrow = jnp.zeros(s1.shape, F32)
        for a in range(PEER_TOPK):
            lrow = jnp.where(rank1 == a, cnt_f[a:a + 1, :], lrow)
        l_ref[0] = lrow
        w1_ref[0] = jnp.exp(s1 - v1[0:1, :])
        r2_ref[0] = rank2.astype(F32).astype(BF16)
        w2_ref[0] = (jnp.exp(s2 - v2[0:1, :]) * (1.0 / z)).astype(BF16)
        return rank1, rank2

    rank1, rank2 = select(False)
    marked = (jnp.sum((rank1 < PEER_TOPK).astype(jnp.int32), axis=0, keepdims=True)
              + jnp.sum((rank2 < PEER_TOPK).astype(jnp.int32), axis=0, keepdims=True))

    @pl.when(jnp.max(marked) != 2 * PEER_TOPK)
    def _():
        select(True)


def _peer_select(st):
    t_tokens = st.shape[2]
    tm = TM_SEL
    shp = jax.ShapeDtypeStruct((PEER_HEADS, PEER_KEYS, t_tokens), F32)
    shp_b = jax.ShapeDtypeStruct((PEER_HEADS, PEER_KEYS, t_tokens), BF16)
    spec = pl.BlockSpec((1, PEER_KEYS, tm), lambda i, h: (h, 0, i))
    return pl.pallas_call(
        _peer_select_body,
        grid=(t_tokens // tm, PEER_HEADS),
        in_specs=[pl.BlockSpec((2, PEER_KEYS, tm), lambda i, h: (h, 0, i))],
        out_specs=(spec, spec, spec, spec),
        out_shape=(shp, shp, shp_b, shp_b),
        compiler_params=_params(("parallel", "parallel"), 32),
        name="peer_select",
    )(st)


def _transpose_body(v_ref, s_ref, o_ref):
    o_ref[...] = (v_ref[...].T * s_ref[0:1, 0:1]).astype(FP8)


def _transpose_fp8(v, scale_tile):
    n, d = v.shape
    tn = 512
    return pl.pallas_call(
        _transpose_body,
        grid=(n // tn,),
        in_specs=[pl.BlockSpec((tn, d), lambda i: (i, 0)),
                  pl.BlockSpec((8, LANES), lambda i: (0, 0))],
        out_specs=pl.BlockSpec((d, tn), lambda i: (0, i)),
        out_shape=jax.ShapeDtypeStruct((d, n), FP8),
        compiler_params=_params(("parallel",), 32),
        name="transpose_v",
    )(v, scale_tile)


def _peer_expert_body(h2t_ref, u_ref, vt_ref, l_ref, w1_ref, r2_ref, w2_ref, scl_ref, sclp_ref,
                      o_ref, ce_ref, co_ref, *, steps_per_tile):
    g = pl.program_id(0)
    sw = 256
    strips = [slice(c * sw, (c + 1) * sw) for c in range(TM_PEER // sw)]
    n_piece = TE_PEER // PEER_KEYS
    blk = D_MODEL // n_piece

    @pl.when(g == 0)
    def _():
        co_ref[...] = jnp.zeros_like(co_ref)

    @pl.when((g == 0) | ((g - 1) % steps_per_tile == 0))
    def _():
        o_ref[...] = jnp.zeros_like(o_ref)

    def run(c_new, c_old):
        def piece(j, carry):
            r0 = pl.multiple_of(j * PEER_KEYS, PEER_KEYS)
            d0 = pl.multiple_of(j * blk, blk)
            for ls in strips:
                coef = None
                for h in range(PEER_HEADS):
                    lrow = jnp.broadcast_to(l_ref[h, pl.ds(j, 1), ls], (16, sw)).astype(BF16)
                    w1row = jnp.broadcast_to(w1_ref[h, pl.ds(j, 1), ls], (16, sw)).astype(BF16)
                    lrow = jnp.tile(lrow, (PEER_KEYS // 16, 1))
                    w1row = jnp.tile(w1row, (PEER_KEYS // 16, 1))
                    term = jnp.where(r2_ref[h, :, ls] < lrow, w2_ref[h, :, ls] * w1row, jnp.zeros((), BF16))
                    coef = term if coef is None else coef + term
                a_t = jnp.dot(u_ref[pl.ds(r0, PEER_KEYS), :], h2t_ref[:, ls],
                              preferred_element_type=F32) * scl_ref[0:1, ls]
                c_scale = jnp.broadcast_to(scl_ref[1:2, ls], (16, sw)).astype(BF16)
                c_new[pl.ds(r0, PEER_KEYS), ls] = (
                    coef * _gelu(a_t).astype(BF16) * jnp.tile(c_scale, (PEER_KEYS // 16, 1))).astype(FP8)
                o_ref[pl.ds(d0, blk), ls] += jnp.dot(vt_ref[pl.ds(d0, blk), :], c_old[:, ls],
                                                     preferred_element_type=F32) * sclp_ref[2:3, ls]
            return carry

        lax.fori_loop(0, n_piece, piece, 0, unroll=4)

    @pl.when(g % 2 == 0)
    def _():
        run(ce_ref, co_ref)

    @pl.when(g % 2 == 1)
    def _():
        run(co_ref, ce_ref)


def _peer_expert(h2t, u_b, v_t, lrow, w1, r2, w2, scl):
    t_tokens = h2t.shape[1]
    tm, te = TM_PEER, TE_PEER
    n_piece = te // PEER_KEYS
    ne = PEER_EXPERTS // te
    n_steps = (t_tokens // tm) * ne
    cur = lambda g: jnp.minimum(g, n_steps - 1)
    prev = lambda g: jnp.maximum(g - 1, 0)
    row_spec = pl.BlockSpec((PEER_HEADS, n_piece, tm), lambda g: (0, cur(g) % ne, cur(g) // ne))
    full_spec = pl.BlockSpec((PEER_HEADS, PEER_KEYS, tm), lambda g: (0, 0, cur(g) // ne))
    return pl.pallas_call(
        functools.partial(_peer_expert_body, steps_per_tile=ne),
        grid=(n_steps + 1,),
        in_specs=[pl.BlockSpec((D_MODEL, tm), lambda g: (0, cur(g) // ne)),
                  pl.BlockSpec((te, D_MODEL), lambda g: (cur(g) % ne, 0)),
                  pl.BlockSpec((D_MODEL, te), lambda g: (0, prev(g) % ne)),
                  row_spec, row_spec, full_spec, full_spec,
                  pl.BlockSpec((8, tm), lambda g: (0, cur(g) // ne)),
                  pl.BlockSpec((8, tm), lambda g: (0, prev(g) // ne))],
        out_specs=pl.BlockSpec((D_MODEL, tm), lambda g: (0, prev(g) // ne)),
        out_shape=jax.ShapeDtypeStruct((D_MODEL, t_tokens), F32),
        scratch_shapes=[pltpu.VMEM((te, tm), FP8), pltpu.VMEM((te, tm), FP8)],
        compiler_params=_params(("arbitrary",), 52),
        name="peer_experts",
    )(h2t, u_b, v_t, lrow, w1, r2, w2, scl, scl)


def _final_body(x1_ref, pt_ref, g2_ref, gn_ref, o_ref, *, apply_norm):
    y = x1_ref[...] + g2_ref[0] * pt_ref[...].T
    if apply_norm:
        ms = jnp.mean(y * y, axis=-1, keepdims=True)
        y = y * lax.rsqrt(ms + EPS) * gn_ref[...]
    o_ref[...] = y


def _final(x1, peer_t, gate2, g_final, seq, apply_norm):
    t_tokens = x1.shape[0]
    tm = TM_PROJ
    tps = seq // tm
    return pl.pallas_call(
        functools.partial(_final_body, apply_norm=apply_norm),
        grid=(t_tokens // tm,),
        in_specs=[pl.BlockSpec((tm, D_MODEL), lambda i: (i, 0)),
                  pl.BlockSpec((D_MODEL, tm), lambda i: (0, i)),
                  pl.BlockSpec((1, 1, D_MODEL), lambda i: (i // tps, 0, 0)),
                  pl.BlockSpec((1, D_MODEL), lambda i: (0, 0))],
        out_specs=pl.BlockSpec((tm, D_MODEL), lambda i: (i, 0)),
        out_shape=jax.ShapeDtypeStruct((t_tokens, D_MODEL), F32),
        compiler_params=_params(("parallel",), 32),
        name="final_norm",
    )(x1, peer_t, gate2, g_final)


def _split_cols(a, sizes):
    out, acc = [], 0
    for s in sizes:
        out.append(a[:, acc:acc + s])
        acc += s
    return out


def _inproj_weights(w_in):
    kvw = KV_GROUPS * HEAD_DIM
    sizes = (1024,) + (kvw,) * 6 + (3 * N_HEADS, 1024, 1024, 1024, 1024)
    q_a, k_c, v_c, k_s, v_s, k_w, v_w, g_a, q_r, k_r, v_r, g_r = _split_cols(w_in, sizes)
    d = w_in.shape[0]

    def grp(a, g):
        return a[:, g * HEAD_DIM:(g + 1) * HEAD_DIM]

    zeros = jnp.zeros((d, HEAD_DIM), w_in.dtype)
    cv = [jnp.concatenate([grp(k_c, g), grp(v_c, g)], axis=1) for g in range(KV_GROUPS)]
    ksp = [jnp.concatenate([grp(k_s, g), zeros], axis=1) for g in range(KV_GROUPS)]
    kwp = [jnp.concatenate([grp(k_w, g), zeros], axis=1) for g in range(KV_GROUPS)]
    w_std = jnp.concatenate(cv + ksp + kwp + [q_r, k_r, v_r, g_r], axis=1).astype(BF16)
    gcols = []
    for g in range(KV_GROUPS):
        for br in range(3):
            for r in range(GROUP):
                c = (g * GROUP + r) * 3 + br
                gcols.append(g_a[:, c:c + 1])
        gcols.append(jnp.zeros((d, 4), w_in.dtype))
    w_tr = jnp.concatenate([q_a, v_s, v_w] + gcols, axis=1).T.astype(BF16)
    return w_std, w_tr


def _compress_weights(pe_k, pe_v, k_w1, k_w2, v_w1, v_w2):
    half = CMP_LEN // 2

    def w1_half(w1k, w1v, lo):
        a = w1k.reshape(CMP_LEN, HEAD_DIM, CMP_HIDDEN)[lo:lo + half]
        b = w1v.reshape(CMP_LEN, HEAD_DIM, CMP_HIDDEN)[lo:lo + half]
        za = jnp.zeros_like(a)
        top = jnp.concatenate([a, za], axis=2)
        bot = jnp.concatenate([za, b], axis=2)
        return jnp.concatenate([top, bot], axis=1).reshape(half * 2 * HEAD_DIM, 2 * CMP_HIDDEN).astype(BF16)

    def pe_half(lo):
        row = jnp.concatenate([pe_k[lo:lo + half], pe_v[lo:lo + half]], axis=1).reshape(1, -1)
        return jnp.broadcast_to(row, (8, row.shape[1])).astype(BF16)

    wa = w1_half(k_w1, v_w1, 0)
    wb = w1_half(k_w1, v_w1, half)
    w2k = jnp.zeros((2 * CMP_HIDDEN, LANES), F32).at[:CMP_HIDDEN, :HEAD_DIM].set(k_w2).astype(BF16)
    w2vt = jnp.zeros((HEAD_DIM, 2 * CMP_HIDDEN), F32).at[:, CMP_HIDDEN:].set(v_w2.T).astype(BF16)
    return wa, wb, pe_half(0), pe_half(half), w2k, w2vt


def _nsa_constants(seq):
    slopes = jnp.exp2(-8.0 * (jnp.arange(N_HEADS, dtype=F32) + 1.0) / N_HEADS) * LOG2E
    s_hi = slopes.astype(BF16)
    s_lo = (slopes - s_hi.astype(F32)).astype(BF16)
    rows = jnp.zeros((N_HEADS, HEAD_DIM), BF16)
    rows = rows.at[:, 0].set(s_hi).at[:, 1].set(s_hi).at[:, 2].set(s_lo).at[:, 3].set(s_lo)
    qaug = jnp.broadcast_to(rows[:, :, None], (N_HEADS, HEAD_DIM, LANES))
    n_rows = seq // CMP_STRIDE
    n_slc = seq // SLC_BLOCK
    start = np.arange(n_rows)[:, None] * CMP_STRIDE
    end = start + CMP_LEN - 1
    blk = np.arange(n_slc)[None, :] * SLC_BLOCK
    ovl = ((start < blk + SLC_BLOCK) & (end >= blk)).astype(np.float32)
    per_tile = TK_SEL // SLC_BLOCK
    grp = (np.arange(n_slc)[None, :] // per_tile == np.arange(n_slc // per_tile)[:, None])
    return qaug, jnp.asarray(ovl.T, BF16), jnp.asarray(grp.astype(np.float32), BF16)


def _retention_constants():
    h, c = RET_HEADS, RET_CHUNK
    lg = jnp.log1p(-jnp.exp2(-5.0 - jnp.arange(h, dtype=F32)))
    pos = jnp.arange(c, dtype=F32)
    diff = pos[:, None] - pos[None, :]
    scale = RET_DIM ** -0.5
    dm = jnp.where(diff >= 0, jnp.exp(lg[:, None, None] * jnp.maximum(diff, 0.0)), 0.0) * scale
    k_decay = jnp.exp(lg[:, None] * (c - 1.0 - pos)) * scale
    q_decay = jnp.exp(lg[:, None] * (pos + 1.0))
    chunk_decay = jnp.exp(lg * c)
    kd = jnp.broadcast_to(k_decay[:, :, None], (h, c, RET_DIM))
    qd = jnp.broadcast_to(q_decay[:, :, None], (h, c, RET_DIM))
    cd = jnp.broadcast_to(chunk_decay[:, None, None], (h, RET_DIM, RET_DIM))
    return dm, kd, qd, cd


def kernel(x, c, w_ada, b_ada, g_norm_mix, g_norm_ffn, g_norm_final, w_in, cmp_pe_k, cmp_pe_v,
           cmp_k_w1, cmp_k_w2, cmp_v_w1, cmp_v_w2, g_nsa_out, g_ret_out, w_out,
           peer_w_q, peer_sub_keys, peer_u, peer_v):
    batch, seq, d = x.shape
    depth = w_ada.shape[0]
    t_tokens = batch * seq
    xf = x.reshape(t_tokens, d)
    c_pad = jnp.zeros((8, d), F32).at[:batch].set(c)
    qaug, ovl_t, grp = _nsa_constants(seq)
    dm, kd, qd, cd = _retention_constants()

    for l in range(depth):
        mod = _adaln(c_pad, w_ada[l], b_ada[l][None, :])[:batch].reshape(batch, 6, 1, d)
        shift1, scale1, gate1, shift2, scale2, gate2 = (mod[:, k] for k in range(6))

        w_std, w_tr = _inproj_weights(w_in[l])
        (cv, ks, kw, q_r, k_r, v_r, g_r, qt, vst, vwt, gt) = _inproj(
            xf, scale1, shift1, g_norm_mix[l][None, :], w_std, w_tr, seq)

        cv4 = cv.reshape(KV_GROUPS, batch, seq // CMP_STRIDE, CMP_STRIDE * LANES)
        kcp, vct = _compress(cv4, *_compress_weights(cmp_pe_k[l], cmp_pe_v[l], cmp_k_w1[l], cmp_k_w2[l],
                                                     cmp_v_w1[l], cmp_v_w2[l]))
        gout_b = jnp.broadcast_to(g_nsa_out[l][:, :, None], (N_HEADS, HEAD_DIM, LANES))
        o_nsa = _nsa(qt, gt, kcp, vct, ks, kw, vst, vwt, qaug, gout_b, ovl_t, grp, batch, seq)

        go = jnp.broadcast_to(g_ret_out[l][:, None, :], (RET_HEADS, 8, RET_DIM))
        o_ret = _retention(q_r, k_r, v_r, g_r, dm, kd, qd, cd, go, batch, seq)

        sub_keys = peer_sub_keys[l].reshape(2 * PEER_HEADS, PEER_KEYS, -1).astype(BF16)
        s_u = _pow2_scale(jnp.max(jnp.abs(peer_u[l])))
        s_v = _pow2_scale(jnp.max(jnp.abs(peer_v[l])))
        u_norm = jnp.sqrt(jnp.max(jnp.sum(jnp.square(peer_u[l]), axis=1)))
        peer_scales = jnp.broadcast_to(
            jnp.stack([1.0 / s_u, 1.0 / s_v, u_norm] + [jnp.zeros((), F32)] * 5)[:, None], (8, LANES))
        x1, h2t, st, scl = _mid(o_nsa, o_ret, xf, gate1, scale2, shift2, g_norm_ffn[l][None, :],
                                w_out[l].astype(BF16), peer_w_q[l].T.astype(BF16), sub_keys, peer_scales, seq)

        lrow, w1, r2, w2 = _peer_select(st)
        peer_t = _peer_expert(h2t, (peer_u[l] * s_u).astype(FP8),
                              _transpose_fp8(peer_v[l], jnp.full((8, LANES), s_v, F32)),
                              lrow, w1, r2, w2, scl)
        xf = _final(x1, peer_t, gate2, g_norm_final[None, :], seq, apply_norm=(l == depth - 1))
    return xf.reshape(batch, seq, d)
```

```python
import functools
import math

import numpy as np
import jax
import jax.numpy as jnp
from jax import lax
from jax.experimental import pallas as pl
from jax.experimental.pallas import tpu as pltpu

F32 = jnp.float32
BF16 = jnp.bfloat16
FP8 = jnp.float8_e4m3fn
FP8_TARGET = 224.0

D_MODEL = 2048
N_HEADS = 16
HEAD_DIM = 64
KV_GROUPS = 4
GROUP = 4
CMP_LEN = 32
CMP_STRIDE = 16
CMP_HIDDEN = 128
SLC_BLOCK = 64
SLC_TOPK = 16
WINDOW = 512
FORCE_SCORE = 1e4
NEG = -1e30
RET_HEADS = 8
RET_DIM = 128
RET_CHUNK = 128
PEER_HEADS = 8
PEER_KEYS = 128
PEER_EXPERTS = PEER_KEYS * PEER_KEYS
PEER_TOPK = 16
COEF_BOUND_FACTOR = 1.25 * PEER_HEADS
EPS = 1e-6
LOG2E = 1.4426950408889634

LANES = 128
TQ = 128
TK_SEL = 256
TM_PROJ = 256
TM_PEER = 512
TE_PEER = 1024
TM_SEL = 256
RET_TILE = 512

_NT = (((1,), (1,)), ((), ()))
_TN = (((0,), (0,)), ((), ()))


def _params(sem, vmem_mb):
    return pltpu.CompilerParams(dimension_semantics=sem, vmem_limit_bytes=vmem_mb * 1024 * 1024)


def _resident(shape, index_map):
    return pl.BlockSpec(shape, index_map, pipeline_mode=pl.Buffered(1))


def _gelu(x):
    return jax.nn.gelu(x)


def _adaln_body(c_ref, w_ref, b_ref, o_ref):
    c = c_ref[...]
    act = (c * jax.nn.sigmoid(c)).astype(BF16)
    o_ref[...] = jnp.dot(act, w_ref[...].astype(BF16), preferred_element_type=F32) + b_ref[...]


def _adaln(c_pad, w, b):
    n = w.shape[1]
    tn = 1536
    return pl.pallas_call(
        _adaln_body,
        grid=(n // tn,),
        in_specs=[pl.BlockSpec((8, D_MODEL), lambda j: (0, 0)),
                  pl.BlockSpec((D_MODEL, tn), lambda j: (0, j)),
                  pl.BlockSpec((1, tn), lambda j: (0, j))],
        out_specs=pl.BlockSpec((8, tn), lambda j: (0, j)),
        out_shape=jax.ShapeDtypeStruct((8, n), F32),
        compiler_params=_params(("arbitrary",), 40),
        name="adaln",
    )(c_pad, w, b)


STD_COLS = 512 * 3 + 1024 * 4
TR_ROWS = 1024 + 256 + 256 + 64


def _inproj_body(x_ref, sc_ref, sh_ref, gn_ref, wstd_ref, wt_ref,
                 cv_ref, ks_ref, kw_ref, qr_ref, kr_ref, vr_ref, gr_ref,
                 qt_ref, vst_ref, vwt_ref, gt_ref, *, tiles_per_seq):
    tm = TM_PROJ
    i = pl.program_id(0)
    x = x_ref[...]
    ms = jnp.mean(x * x, axis=-1, keepdims=True)
    h = x * lax.rsqrt(ms + EPS) * gn_ref[...]
    h = h * (1.0 + sc_ref[0]) + sh_ref[0]
    hb = h.astype(BF16)

    def std(a, b):
        return jnp.dot(hb, wstd_ref[:, a:b], preferred_element_type=F32)

    y = std(0, 512)
    for g in range(KV_GROUPS):
        cv_ref[g] = y[:, g * LANES:(g + 1) * LANES].astype(BF16)

    t = (i % tiles_per_seq) * tm + lax.broadcasted_iota(jnp.int32, (tm, LANES), 0)
    lane = lax.broadcasted_iota(jnp.int32, (tm, LANES), 1)
    pos_hi = ((t >> 6) << 6).astype(F32)
    pos_lo = (t & 63).astype(F32)
    aug = jnp.where((lane == 64) | (lane == 66), pos_hi,
                    jnp.where((lane == 65) | (lane == 67), pos_lo, 0.0))
    for ref, off in ((ks_ref, 512), (kw_ref, 1024)):
        y = std(off, off + 512)
        for g in range(KV_GROUPS):
            ref[:, g * LANES:(g + 1) * LANES] = (y[:, g * LANES:(g + 1) * LANES] + aug).astype(BF16)

    for ref, off in ((qr_ref, 1536), (kr_ref, 2560), (vr_ref, 3584), (gr_ref, 4608)):
        ref[...] = std(off, off + 1024).astype(BF16)

    def tr(a, b):
        return lax.dot_general(wt_ref[a:b, :], hb, _NT, preferred_element_type=F32)

    qt = tr(0, 1024) * (HEAD_DIM ** -0.5 * LOG2E)
    vst = tr(1024, 1280)
    vwt = tr(1280, 1536)
    gt = tr(1536, 1600)
    for c in range(tm // LANES):
        sl = slice(c * LANES, (c + 1) * LANES)
        qt_ref[c] = qt[:, sl].astype(BF16)
        vst_ref[c] = vst[:, sl].astype(BF16)
        vwt_ref[c] = vwt[:, sl].astype(BF16)
        gt_ref[c] = gt[:, sl]


def _inproj(x2, scale1, shift1, g_mix, w_std, w_tr, seq):
    t_tokens = x2.shape[0]
    tm = TM_PROJ
    tps = seq // tm
    nt = t_tokens // tm
    c = tm // LANES
    row = lambda i: (i, 0)
    per_b = lambda i: (i // tps, 0, 0)
    out_shape = (
        jax.ShapeDtypeStruct((KV_GROUPS, t_tokens, LANES), BF16),
        jax.ShapeDtypeStruct((t_tokens, 512), BF16),
        jax.ShapeDtypeStruct((t_tokens, 512), BF16),
        jax.ShapeDtypeStruct((t_tokens, 1024), BF16),
        jax.ShapeDtypeStruct((t_tokens, 1024), BF16),
        jax.ShapeDtypeStruct((t_tokens, 1024), BF16),
        jax.ShapeDtypeStruct((t_tokens, 1024), BF16),
        jax.ShapeDtypeStruct((t_tokens // LANES, 1024, LANES), BF16),
        jax.ShapeDtypeStruct((t_tokens // LANES, 256, LANES), BF16),
        jax.ShapeDtypeStruct((t_tokens // LANES, 256, LANES), BF16),
        jax.ShapeDtypeStruct((t_tokens // LANES, 64, LANES), F32),
    )
    out_specs = (
        pl.BlockSpec((KV_GROUPS, tm, LANES), lambda i: (0, i, 0)),
        pl.BlockSpec((tm, 512), row),
        pl.BlockSpec((tm, 512), row),
        pl.BlockSpec((tm, 1024), row),
        pl.BlockSpec((tm, 1024), row),
        pl.BlockSpec((tm, 1024), row),
        pl.BlockSpec((tm, 1024), row),
        pl.BlockSpec((c, 1024, LANES), lambda i: (i, 0, 0)),
        pl.BlockSpec((c, 256, LANES), lambda i: (i, 0, 0)),
        pl.BlockSpec((c, 256, LANES), lambda i: (i, 0, 0)),
        pl.BlockSpec((c, 64, LANES), lambda i: (i, 0, 0)),
    )
    return pl.pallas_call(
        functools.partial(_inproj_body, tiles_per_seq=tps),
        grid=(nt,),
        in_specs=[pl.BlockSpec((tm, D_MODEL), row),
                  pl.BlockSpec((1, 1, D_MODEL), per_b),
                  pl.BlockSpec((1, 1, D_MODEL), per_b),
                  _resident((1, D_MODEL), lambda i: (0, 0)),
                  _resident((D_MODEL, STD_COLS), lambda i: (0, 0)),
                  _resident((TR_ROWS, D_MODEL), lambda i: (0, 0))],
        out_specs=out_specs,
        out_shape=out_shape,
        compiler_params=_params(("parallel",), 56),
        name="inproj",
    )(x2, scale1, shift1, g_mix, w_std, w_tr)


def _compress_body(x_ref, wa_ref, wb_ref, pea_ref, peb_ref, w2k_ref, w2vt_ref, kcp_ref, vct_ref):
    x = x_ref[0, 0]
    n_rows = x.shape[0]
    p = jnp.dot(x, wa_ref[...], preferred_element_type=F32)
    q = jnp.dot(x, wb_ref[...], preferred_element_type=F32)
    pe = (jnp.dot(pea_ref[...], wa_ref[...], preferred_element_type=F32)
          + jnp.dot(peb_ref[...], wb_ref[...], preferred_element_type=F32))[0:1, :]
    pre = p + pltpu.roll(q, n_rows - 1, 0) + pe
    hid = _gelu(pre).astype(BF16)
    kc = jnp.dot(hid, w2k_ref[...], preferred_element_type=F32)
    n = lax.broadcasted_iota(jnp.int32, (n_rows, LANES), 0)
    lane = lax.broadcasted_iota(jnp.int32, (n_rows, LANES), 1)
    ce = n * CMP_STRIDE + (CMP_LEN - 1)
    ce_hi = ((ce >> 6) << 6).astype(F32)
    ce_lo = (ce & 63).astype(F32)
    aug = jnp.where((lane == 64) | (lane == 66), ce_hi,
                    jnp.where((lane == 65) | (lane == 67), ce_lo, 0.0))
    kcp_ref[0] = (kc + aug).astype(BF16)
    vct_ref[0] = lax.dot_general(w2vt_ref[...], hid, _NT, preferred_element_type=F32).astype(BF16)


def _compress(cv4, wa, wb, pea, peb, w2k, w2vt):
    g_, b_, n_rows, _ = cv4.shape
    const2 = lambda n: (0, 0)
    return pl.pallas_call(
        _compress_body,
        grid=(b_ * g_,),
        in_specs=[pl.BlockSpec((1, 1, n_rows, 2048), lambda n: (n % KV_GROUPS, n // KV_GROUPS, 0, 0)),
                  pl.BlockSpec((2048, 256), const2),
                  pl.BlockSpec((2048, 256), const2),
                  pl.BlockSpec((8, 2048), const2),
                  pl.BlockSpec((8, 2048), const2),
                  pl.BlockSpec((256, LANES), const2),
                  pl.BlockSpec((64, 256), const2)],
        out_specs=(pl.BlockSpec((1, n_rows, LANES), lambda n: (n, 0, 0)),
                   pl.BlockSpec((1, 64, n_rows), lambda n: (n, 0, 0))),
        out_shape=(jax.ShapeDtypeStruct((b_ * g_, n_rows, LANES), BF16),
                   jax.ShapeDtypeStruct((b_ * g_, 64, n_rows), BF16)),
        compiler_params=_params(("parallel",), 32),
        name="nsa_compress",
    )(cv4, wa, wb, pea, peb, w2k, w2vt)


def _softmax_step(state, s, pv_prev):
    m_i, l_i, acc = state
    m_new = jnp.maximum(m_i, jnp.max(s, axis=0, keepdims=True))
    alpha = jnp.exp2(m_i - m_new)
    p = jnp.exp2(s - m_new)
    l_new = alpha * l_i + jnp.sum(p, axis=0, keepdims=True)
    return (m_new, l_new, (acc + pv_prev) * alpha), p.astype(BF16)


def _nsa_body(qt_ref, gt_ref, kcp_ref, vct_ref, ks_ref, kw_ref, vst_ref, vwt_ref,
              qaug_ref, gout_ref, ovl_ref, grp_ref, o_ref, selb_ref, sa_ref, sb_ref, pa_ref, pb_ref, *, n_cmp):
    qi = pl.program_id(2)
    t0 = qi * TQ
    wq = GROUP * TQ

    qp = jnp.concatenate(
        [jnp.concatenate([qt_ref[0, r * HEAD_DIM:(r + 1) * HEAD_DIM, :], qaug_ref[r]], axis=0)
         for r in range(GROUP)], axis=1)

    lane_q = lax.broadcasted_iota(jnp.int32, (1, wq), 1) & (TQ - 1)
    n_win = WINDOW // TQ + 1
    k_lo = pl.multiple_of(jnp.maximum(t0 - WINDOW, 0), TQ)

    s = jnp.dot(kcp_ref[0], qp, preferred_element_type=F32)
    s_w = jnp.dot(kw_ref[pl.ds(k_lo, n_win * TQ), :], qp, preferred_element_type=F32)
    n_io = lax.broadcasted_iota(jnp.int32, (n_cmp, wq), 0)
    tl = lax.broadcasted_iota(jnp.int32, (n_cmp, wq), 1) & (TQ - 1)
    valid = (n_io * CMP_STRIDE + (CMP_LEN - 1)) <= (t0 + tl)
    s = jnp.where(valid, s, NEG)
    m = jnp.maximum(jnp.max(s, axis=0, keepdims=True), 0.5 * NEG)
    p = jnp.exp2(s - m)
    l = jnp.sum(p, axis=0, keepdims=True)
    pn = p * (1.0 / jnp.maximum(l, 1e-30))
    o_c = jnp.dot(vct_ref[0], pn.astype(BF16), preferred_element_type=F32)

    ps = pn[:, 0:TQ] + pn[:, TQ:2 * TQ] + pn[:, 2 * TQ:3 * TQ] + pn[:, 3 * TQ:4 * TQ]
    hi = ps.astype(BF16)
    r1 = ps - hi.astype(F32)
    mid = r1.astype(BF16)
    lo = (r1 - mid.astype(F32)).astype(BF16)
    ovl = ovl_ref[...]
    imp = (jnp.dot(ovl, hi, preferred_element_type=F32)
           + jnp.dot(ovl, mid, preferred_element_type=F32)
           + jnp.dot(ovl, lo, preferred_element_type=F32))

    n_slc = imp.shape[0]
    m_io = lax.broadcasted_iota(jnp.int32, (n_slc, TQ), 0)
    q_io = lax.broadcasted_iota(jnp.int32, (n_slc, TQ), 1)
    back = ((t0 + q_io) >> 6) - m_io
    valid_s = back >= 0
    forced = valid_s & ((m_io == 0) | (back < 2))
    w = jnp.where(forced, -jnp.inf, jnp.where(valid_s, imp, -1.0))
    selb = jnp.where(forced, 0.0, NEG)

    def pick(carry, lanes=None):
        w, selb = carry
        mx = jnp.max(w, axis=0, keepdims=True)
        idx = jnp.min(jnp.where(w == mx, m_io, n_slc), axis=0, keepdims=True)
        hit = m_io == idx
        if lanes is not None:
            hit = hit & lanes
        return jnp.where(hit, -jnp.inf, w), jnp.where(hit, 0.0, selb)

    carry = (w, selb)
    for _ in range(SLC_TOPK - 3):
        carry = pick(carry)
    carry = pick(carry, lanes=(t0 + q_io) < 2 * SLC_BLOCK)
    _, selb = pick(carry, lanes=(t0 + q_io) < SLC_BLOCK)
    for mblk in range(n_slc):
        selb_ref[mblk] = jnp.broadcast_to(selb[mblk:mblk + 1, :], (8, TQ))

    per_blk = TK_SEL // SLC_BLOCK

    def scores(j):
        k0 = pl.multiple_of(j * TK_SEL, TK_SEL)
        bias = jnp.concatenate(
            [jnp.tile(selb_ref[j * per_blk + u], (SLC_BLOCK // 8, 1)) for u in range(per_blk)], axis=0)
        bias = jnp.concatenate([bias] * GROUP, axis=1)
        return jnp.dot(ks_ref[pl.ds(k0, TK_SEL), :], qp, preferred_element_type=F32) + bias

    def vt_tile(j):
        return jnp.concatenate([vst_ref[2 * j], vst_ref[2 * j + 1]], axis=1)

    def causal(j):
        kr = lax.broadcasted_iota(jnp.int32, (TK_SEL, wq), 0)
        return kr <= (t0 - j * TK_SEL) + lane_q

    def pv_dot(j, p_ref):
        return jnp.dot(vt_tile(j), p_ref[...], preferred_element_type=F32)

    jd = qi // (TK_SEL // TQ)

    sa_ref[...] = scores(0)
    pb_ref[...] = jnp.zeros_like(pb_ref)
    sel01 = jnp.where(selb == 0.0, 1.0, 0.0).astype(BF16)
    tile_any = jnp.max(jnp.dot(grp_ref[...], sel01, preferred_element_type=F32), axis=1, keepdims=True)
    j_io = lax.broadcasted_iota(jnp.int32, tile_any.shape, 0)
    q_lo_v = jnp.max(jnp.where((tile_any == 0.0) & (j_io <= jd), j_io, -1), axis=0, keepdims=True) + 1
    n_pre_v = jnp.max(jnp.where((tile_any > 0.0) & (j_io < q_lo_v), j_io, -1), axis=0, keepdims=True) + 1

    d = (t0 - k_lo) + lane_q - lax.broadcasted_iota(jnp.int32, (n_win * TQ, wq), 0)
    s_w = jnp.where(lax.bitcast_convert_type(d, jnp.uint32) < WINDOW, s_w, NEG)
    p = jnp.exp2(s_w - jnp.max(s_w, axis=0, keepdims=True))
    l_w = jnp.sum(p, axis=0, keepdims=True)
    vt_w = jnp.concatenate([vwt_ref[k_lo // TQ + u] for u in range(n_win)], axis=1)
    o_w = jnp.dot(vt_w, p.astype(BF16), preferred_element_type=F32) * (1.0 / l_w)

    q_lo = q_lo_v[0, 0]
    n_pre = n_pre_v[0, 0]
    n_vis = n_pre + jd - q_lo + 1

    def tile_at(pos):
        return jnp.where(pos < n_pre, pos, pos - n_pre + q_lo)

    state = (jnp.full((1, wq), NEG, F32), jnp.zeros((1, wq), F32), jnp.zeros((HEAD_DIM, wq), F32))

    def pair(u, state):
        a = 2 * u
        sb_ref[...] = scores(tile_at(a + 1))
        state, p = _softmax_step(state, sa_ref[...], pv_dot(tile_at(jnp.maximum(a - 1, 0)), pb_ref))
        pa_ref[...] = p
        sa_ref[...] = scores(tile_at(a + 2))
        state, p = _softmax_step(state, sb_ref[...], pv_dot(tile_at(a), pa_ref))
        pb_ref[...] = p
        return state

    n_pair = (n_vis - 1) // 2
    state = lax.fori_loop(0, n_pair, pair, state)
    x = 2 * n_pair
    y = jnp.minimum(x + 1, n_vis - 1)
    tx, ty = tile_at(x), tile_at(y)
    sb_ref[...] = scores(ty)
    state, p = _softmax_step(state, jnp.where(causal(tx), sa_ref[...], NEG),
                             pv_dot(tile_at(jnp.maximum(x - 1, 0)), pb_ref))
    pa_ref[...] = p
    ty_mask = jnp.where(x + 1 < n_vis, ty, jd + 1)
    (_, l_s, acc_s), p = _softmax_step(state, jnp.where(causal(ty_mask), sb_ref[...], NEG), pv_dot(tx, pa_ref))
    acc_s = acc_s + jnp.dot(vt_tile(ty), p, preferred_element_type=F32)

    o_s = acc_s * (1.0 / l_s)
    gw = jax.nn.sigmoid(gt_ref[0])
    outs = []
    for r in range(GROUP):
        sl = slice(r * TQ, (r + 1) * TQ)
        o = (gw[r:r + 1, :] * o_c[:, sl] + gw[GROUP + r:GROUP + r + 1, :] * o_s[:, sl]
             + gw[2 * GROUP + r:2 * GROUP + r + 1, :] * o_w[:, sl])
        ms = jnp.mean(o * o, axis=0, keepdims=True)
        outs.append(o * lax.rsqrt(ms + EPS) * gout_ref[r])
    o_ref[...] = jnp.concatenate(outs, axis=0).T.astype(BF16)


def _nsa(qt, gt, kcp, vct, ks, kw, vst, vwt, qaug, gout_b, ovl_t, grp, batch, seq):
    nq = seq // TQ
    n_cmp = kcp.shape[1]
    n_slc = seq // SLC_BLOCK
    t_tokens = batch * seq
    per_b_chunks = seq // LANES
    return pl.pallas_call(
        functools.partial(_nsa_body, n_cmp=n_cmp),
        grid=(batch, KV_GROUPS, nq),
        in_specs=[
            pl.BlockSpec((1, 256, LANES), lambda b, g, q: (b * nq + q, g, 0)),
            pl.BlockSpec((1, 16, LANES), lambda b, g, q: (b * nq + q, g, 0)),
            pl.BlockSpec((1, n_cmp, LANES), lambda b, g, q: (b * KV_GROUPS + g, 0, 0)),
            pl.BlockSpec((1, 64, n_cmp), lambda b, g, q: (b * KV_GROUPS + g, 0, 0)),
            pl.BlockSpec((seq, LANES), lambda b, g, q: (b, g)),
            pl.BlockSpec((seq, LANES), lambda b, g, q: (b, g)),
            pl.BlockSpec((per_b_chunks, 64, LANES), lambda b, g, q: (b, g, 0)),
            pl.BlockSpec((per_b_chunks, 64, LANES), lambda b, g, q: (b, g, 0)),
            pl.BlockSpec((GROUP, 64, LANES), lambda b, g, q: (g, 0, 0)),
            pl.BlockSpec((GROUP, 64, LANES), lambda b, g, q: (g, 0, 0)),
            pl.BlockSpec((n_slc, n_cmp), lambda b, g, q: (0, 0)),
            pl.BlockSpec(grp.shape, lambda b, g, q: (0, 0)),
        ],
        out_specs=pl.BlockSpec((TQ, 256), lambda b, g, q: (b * nq + q, g)),
        out_shape=jax.ShapeDtypeStruct((t_tokens, 1024), BF16),
        scratch_shapes=[pltpu.VMEM((n_slc, 8, TQ), F32),
                        pltpu.VMEM((TK_SEL, GROUP * TQ), F32), pltpu.VMEM((TK_SEL, GROUP * TQ), F32),
                        pltpu.VMEM((TK_SEL, GROUP * TQ), BF16), pltpu.VMEM((TK_SEL, GROUP * TQ), BF16)],
        compiler_params=_params(("parallel", "parallel", "arbitrary"), 40),
        name="nsa_attention",
    )(qt, gt, kcp, vct, ks, kw, vst, vwt, qaug, gout_b, ovl_t, grp)


def _ret_body(q_ref, k_ref, v_ref, g_ref, dm_ref, kd_ref, qd_ref, cd_ref, go_ref, o_ref, st_ref):
    @pl.when(pl.program_id(2) == 0)
    def _():
        st_ref[...] = jnp.zeros_like(st_ref)

    c_ = RET_CHUNK
    for c in range(RET_TILE // c_):
        sl = slice(c * c_, (c + 1) * c_)
        q = q_ref[sl, :]
        k = k_ref[sl, :]
        v = v_ref[sl, :]
        att = lax.dot_general(q, k, _NT, preferred_element_type=F32) * dm_ref[0]
        state = st_ref[...]
        o = (jnp.dot(att.astype(BF16), v, preferred_element_type=F32)
             + qd_ref[0] * jnp.dot(q, state.astype(BF16), preferred_element_type=F32))
        kdec = (k.astype(F32) * kd_ref[0]).astype(BF16)
        kv = lax.dot_general(kdec, v, _TN, preferred_element_type=F32)
        st_ref[...] = state * cd_ref[0] + kv
        mu = jnp.mean(o, axis=-1, keepdims=True)
        oc = o - mu
        var = jnp.mean(oc * oc, axis=-1, keepdims=True)
        y = oc * lax.rsqrt(var + EPS) * go_ref[0, 0:1, :]
        gate = g_ref[sl, :].astype(F32)
        o_ref[sl, :] = (gate * jax.nn.sigmoid(gate) * y).astype(BF16)


def _retention(q_r, k_r, v_r, g_r, dm, kd, qd, cd, go, batch, seq):
    t_tokens = batch * seq
    nc = seq // RET_TILE
    tok = lambda b, h, c: (b * nc + c, h)
    per_h = lambda b, h, c: (h, 0, 0)
    sq = (1, RET_DIM, RET_DIM)
    return pl.pallas_call(
        _ret_body,
        grid=(batch, RET_HEADS, nc),
        in_specs=[pl.BlockSpec((RET_TILE, RET_DIM), tok)] * 4
        + [pl.BlockSpec(sq, per_h)] * 4 + [pl.BlockSpec((1, 8, RET_DIM), per_h)],
        out_specs=pl.BlockSpec((RET_TILE, RET_DIM), tok),
        out_shape=jax.ShapeDtypeStruct((t_tokens, RET_HEADS * RET_DIM), BF16),
        scratch_shapes=[pltpu.VMEM((RET_DIM, RET_DIM), F32)],
        compiler_params=_params(("parallel", "parallel", "arbitrary"), 32),
        name="retention",
    )(q_r, k_r, v_r, g_r, dm, kd, qd, cd, go)


def _pow2_scale(magnitude):
    return jnp.exp2(jnp.floor(jnp.log2(FP8_TARGET / jnp.maximum(magnitude, 1e-30))))


def _mid_body(on_ref, or_ref, x_ref, g1_ref, sc_ref, sh_ref, gn_ref, wo_ref, wqt_ref, sk_ref, ps_ref,
              x1_ref, h2t_ref, st_ref, scl_ref):
    acc = (jnp.dot(on_ref[...], wo_ref[0:1024, :], preferred_element_type=F32)
           + jnp.dot(or_ref[...], wo_ref[1024:2048, :], preferred_element_type=F32))
    x1 = x_ref[...] + g1_ref[0] * acc
    x1_ref[...] = x1
    ms = jnp.mean(x1 * x1, axis=-1, keepdims=True)
    h2 = x1 * lax.rsqrt(ms + EPS) * gn_ref[...]
    h2 = h2 * (1.0 + sc_ref[0]) + sh_ref[0]

    h2_t = h2.T
    amax = jnp.max(jnp.max(jnp.abs(h2_t), axis=0, keepdims=True), axis=1, keepdims=True)
    s_h = _pow2_scale(amax)
    h2t_ref[...] = (h2_t * s_h).astype(FP8)
    norm = jnp.sqrt(jnp.sum(h2_t * h2_t, axis=0, keepdims=True))
    s_c = _pow2_scale(COEF_BOUND_FACTOR * ps_ref[2:3, 0:1] * norm)
    scl_ref[...] = jnp.concatenate(
        [jnp.broadcast_to(ps_ref[0:1, 0:1] / s_h, s_c.shape), s_c, ps_ref[1:2, 0:1] / s_c,
         jnp.zeros((5, s_c.shape[1]), F32)], axis=0)
    h2 = h2.astype(BF16)
    qt = lax.dot_general(wqt_ref[...], h2, _NT, preferred_element_type=F32).astype(BF16)
    for hp in range(2 * PEER_HEADS):
        st_ref[hp] = jnp.dot(sk_ref[hp], qt[hp * 128:(hp + 1) * 128, :], preferred_element_type=F32)


def _mid(o_nsa, o_ret, x2, gate1, scale2, shift2, g_ffn, w_out, wq_t, sub_keys, peer_scales, seq):
    t_tokens = x2.shape[0]
    tm = TM_PROJ
    tps = seq // tm
    row = lambda i: (i, 0)
    per_b = lambda i: (i // tps, 0, 0)
    return pl.pallas_call(
        _mid_body,
        grid=(t_tokens // tm,),
        in_specs=[pl.BlockSpec((tm, 1024), row),
                  pl.BlockSpec((tm, 1024), row),
                  pl.BlockSpec((tm, D_MODEL), row),
                  pl.BlockSpec((1, 1, D_MODEL), per_b),
                  pl.BlockSpec((1, 1, D_MODEL), per_b),
                  pl.BlockSpec((1, 1, D_MODEL), per_b),
                  _resident((1, D_MODEL), lambda i: (0, 0)),
                  _resident((D_MODEL, D_MODEL), lambda i: (0, 0)),
                  _resident((D_MODEL, D_MODEL), lambda i: (0, 0)),
                  _resident((2 * PEER_HEADS, PEER_KEYS, 128), lambda i: (0, 0, 0)),
                  _resident((8, LANES), lambda i: (0, 0))],
        out_specs=(pl.BlockSpec((tm, D_MODEL), row),
                   pl.BlockSpec((D_MODEL, tm), lambda i: (0, i)),
                   pl.BlockSpec((2 * PEER_HEADS, PEER_KEYS, tm), lambda i: (0, 0, i)),
                   pl.BlockSpec((8, tm), lambda i: (0, i))),
        out_shape=(jax.ShapeDtypeStruct((t_tokens, D_MODEL), F32),
                   jax.ShapeDtypeStruct((D_MODEL, t_tokens), FP8),
                   jax.ShapeDtypeStruct((2 * PEER_HEADS, PEER_KEYS, t_tokens), F32),
                   jax.ShapeDtypeStruct((8, t_tokens), F32)),
        compiler_params=_params(("parallel",), 48),
        name="outproj_peerq",
    )(o_nsa, o_ret, x2, gate1, scale2, shift2, g_ffn, w_out, wq_t, sub_keys, peer_scales)


def _top16(s, break_ties):
    n_rows, n = s.shape
    io = lax.broadcasted_iota(jnp.int32, (n_rows, n), 0)
    a_io = lax.broadcasted_iota(jnp.int32, (PEER_TOPK, n), 0)
    rank = jnp.full((n_rows, n), PEER_TOPK, jnp.int32)
    vals = jnp.zeros((PEER_TOPK, n), F32)
    for a in range(PEER_TOPK):
        mx = jnp.max(s, axis=0, keepdims=True)
        hit = s == mx
        if break_ties:
            hit = io == jnp.min(jnp.where(hit, io, n_rows), axis=0, keepdims=True)
        rank = jnp.where(hit, a, rank)
        s = jnp.where(hit, -jnp.inf, s)
        vals = jnp.where(a_io == a, mx, vals)
    return vals, rank


def _peer_select_body(s_ref, scl_ref, l_ref, w1_ref, r2_ref, w2_ref):
    s1 = s_ref[0]
    s2 = s_ref[1]
    n = s1.shape[1]

    def select(break_ties):
        v1, rank1 = _top16(s1, break_ties)
        v2, rank2 = _top16(s2, break_ties)
        a_io = lax.broadcasted_iota(jnp.int32, (PEER_TOPK, n), 0)
        cnt = jnp.zeros((PEER_TOPK, n), jnp.int32)
        cur = v1 + v2[0:1, :]
        top = v1[0:1, :] + v2[0:1, :]
        z = jnp.zeros((1, n), F32)
        for _ in range(PEER_TOPK):
            mx = jnp.max(cur, axis=0, keepdims=True)
            aidx = jnp.min(jnp.where(cur == mx, a_io, PEER_TOPK), axis=0, keepdims=True)
            hit = a_io == aidx
            cnt = cnt + hit.astype(jnp.int32)
            nxt = jnp.sum(jnp.where(hit, cnt, 0), axis=0, keepdims=True)
            nv = jnp.max(jnp.where(a_io == nxt, v2, -jnp.inf), axis=0, keepdims=True)
            cur = jnp.where(hit, v1 + nv, cur)
            z = z + jnp.exp(mx - top)
        cnt_f = cnt.astype(F32)
        lrow = jnp.zeros(s1.shape, F32)
        for a in range(PEER_TOPK):
            lrow = jnp.where(rank1 == a, cnt_f[a:a + 1, :], lrow)
        l_ref[0] = lrow
        w1_ref[0] = jnp.exp(s1 - v1[0:1, :])
        r2_ref[0] = rank2.astype(F32).astype(BF16)
        w2_ref[0] = (jnp.exp(s2 - v2[0:1, :]) * (scl_ref[1:2, :] / z)).astype(BF16)
        return rank1, rank2

    rank1, rank2 = select(False)
    marked = (jnp.sum((rank1 < PEER_TOPK).astype(jnp.int32), axis=0, keepdims=True)
              + jnp.sum((rank2 < PEER_TOPK).astype(jnp.int32), axis=0, keepdims=True))

    @pl.when(jnp.max(marked) != 2 * PEER_TOPK)
    def _():
        select(True)


def _peer_select(st, scl):
    t_tokens = st.shape[2]
    tm = TM_SEL
    shp = jax.ShapeDtypeStruct((PEER_HEADS, PEER_KEYS, t_tokens), F32)
    shp_b = jax.ShapeDtypeStruct((PEER_HEADS, PEER_KEYS, t_tokens), BF16)
    spec = pl.BlockSpec((1, PEER_KEYS, tm), lambda i, h: (h, 0, i))
    return pl.pallas_call(
        _peer_select_body,
        grid=(t_tokens // tm, PEER_HEADS),
        in_specs=[pl.BlockSpec((2, PEER_KEYS, tm), lambda i, h: (h, 0, i)),
                  pl.BlockSpec((8, tm), lambda i, h: (0, i))],
        out_specs=(spec, spec, spec, spec),
        out_shape=(shp, shp, shp_b, shp_b),
        compiler_params=_params(("parallel", "parallel"), 32),
        name="peer_select",
    )(st, scl)


def _transpose_body(v_ref, s_ref, o_ref):
    o_ref[...] = (v_ref[...].T * s_ref[0:1, 0:1]).astype(FP8)


def _transpose_fp8(v, scale_tile):
    n, d = v.shape
    tn = 512
    return pl.pallas_call(
        _transpose_body,
        grid=(n // tn,),
        in_specs=[pl.BlockSpec((tn, d), lambda i: (i, 0)),
                  pl.BlockSpec((8, LANES), lambda i: (0, 0))],
        out_specs=pl.BlockSpec((d, tn), lambda i: (0, i)),
        out_shape=jax.ShapeDtypeStruct((d, n), FP8),
        compiler_params=_params(("parallel",), 32),
        name="transpose_v",
    )(v, scale_tile)


def _peer_expert_body(h2t_ref, u_ref, vt_ref, l_ref, w1_ref, r2_ref, w2_ref, scl_ref,
                      o_ref, ce_ref, co_ref, *, steps_per_tile):
    g = pl.program_id(0)
    sw = 256
    strips = [slice(c * sw, (c + 1) * sw) for c in range(TM_PEER // sw)]
    n_piece = TE_PEER // PEER_KEYS
    blk = D_MODEL // n_piece

    @pl.when(g == 0)
    def _():
        co_ref[...] = jnp.zeros_like(co_ref)

    @pl.when((g == 0) | ((g - 1) % steps_per_tile == 0))
    def _():
        o_ref[...] = jnp.zeros_like(o_ref)

    def run(c_new, c_old):
        def piece(j, carry):
            r0 = pl.multiple_of(j * PEER_KEYS, PEER_KEYS)
            d0 = pl.multiple_of(j * blk, blk)
            for ls in strips:
                coef = None
                for h in range(PEER_HEADS):
                    lrow = jnp.broadcast_to(l_ref[h, pl.ds(j, 1), ls], (16, sw)).astype(BF16)
                    w1row = jnp.broadcast_to(w1_ref[h, pl.ds(j, 1), ls], (16, sw)).astype(BF16)
                    lrow = jnp.tile(lrow, (PEER_KEYS // 16, 1))
                    w1row = jnp.tile(w1row, (PEER_KEYS // 16, 1))
                    term = jnp.where(r2_ref[h, :, ls] < lrow, w2_ref[h, :, ls] * w1row, jnp.zeros((), BF16))
                    coef = term if coef is None else coef + term
                a_t = jnp.dot(u_ref[pl.ds(r0, PEER_KEYS), :], h2t_ref[:, ls], preferred_element_type=F32)
                a_scale = jnp.broadcast_to(scl_ref[0:1, ls], (16, sw)).astype(BF16)
                act = _gelu(a_t.astype(BF16) * jnp.tile(a_scale, (PEER_KEYS // 16, 1)))
                c_new[pl.ds(r0, PEER_KEYS), ls] = (coef * act).astype(FP8)
                o_ref[pl.ds(d0, blk), ls] += jnp.dot(vt_ref[pl.ds(d0, blk), :], c_old[:, ls],
                                                     preferred_element_type=F32)
            return carry

        lax.fori_loop(0, n_piece, piece, 0, unroll=4)

    @pl.when(g % 2 == 0)
    def _():
        run(ce_ref, co_ref)

    @pl.when(g % 2 == 1)
    def _():
        run(co_ref, ce_ref)


def _peer_expert(h2t, u_b, v_t, lrow, w1, r2, w2, scl):
    t_tokens = h2t.shape[1]
    tm, te = TM_PEER, TE_PEER
    n_piece = te // PEER_KEYS
    ne = PEER_EXPERTS // te
    n_steps = (t_tokens // tm) * ne
    cur = lambda g: jnp.minimum(g, n_steps - 1)
    prev = lambda g: jnp.maximum(g - 1, 0)
    row_spec = pl.BlockSpec((PEER_HEADS, n_piece, tm), lambda g: (0, cur(g) % ne, cur(g) // ne))
    full_spec = pl.BlockSpec((PEER_HEADS, PEER_KEYS, tm), lambda g: (0, 0, cur(g) // ne))
    return pl.pallas_call(
        functools.partial(_peer_expert_body, steps_per_tile=ne),
        grid=(n_steps + 1,),
        in_specs=[pl.BlockSpec((D_MODEL, tm), lambda g: (0, cur(g) // ne)),
                  pl.BlockSpec((te, D_MODEL), lambda g: (cur(g) % ne, 0)),
                  pl.BlockSpec((D_MODEL, te), lambda g: (0, prev(g) % ne)),
                  row_spec, row_spec, full_spec, full_spec,
                  pl.BlockSpec((8, tm), lambda g: (0, cur(g) // ne))],
        out_specs=pl.BlockSpec((D_MODEL, tm), lambda g: (0, prev(g) // ne)),
        out_shape=jax.ShapeDtypeStruct((D_MODEL, t_tokens), F32),
        scratch_shapes=[pltpu.VMEM((te, tm), FP8), pltpu.VMEM((te, tm), FP8)],
        compiler_params=_params(("arbitrary",), 52),
        name="peer_experts",
    )(h2t, u_b, v_t, lrow, w1, r2, w2, scl)


def _final_body(x1_ref, pt_ref, scl_ref, g2_ref, gn_ref, o_ref, *, apply_norm):
    peer = (pt_ref[...] * scl_ref[2:3, :]).T
    y = x1_ref[...] + g2_ref[0] * peer
    if apply_norm:
        ms = jnp.mean(y * y, axis=-1, keepdims=True)
        y = y * lax.rsqrt(ms + EPS) * gn_ref[...]
    o_ref[...] = y


def _final(x1, peer_t, scl, gate2, g_final, seq, apply_norm):
    t_tokens = x1.shape[0]
    tm = TM_PROJ
    tps = seq // tm
    return pl.pallas_call(
        functools.partial(_final_body, apply_norm=apply_norm),
        grid=(t_tokens // tm,),
        in_specs=[pl.BlockSpec((tm, D_MODEL), lambda i: (i, 0)),
                  pl.BlockSpec((D_MODEL, tm), lambda i: (0, i)),
                  pl.BlockSpec((8, tm), lambda i: (0, i)),
                  pl.BlockSpec((1, 1, D_MODEL), lambda i: (i // tps, 0, 0)),
                  pl.BlockSpec((1, D_MODEL), lambda i: (0, 0))],
        out_specs=pl.BlockSpec((tm, D_MODEL), lambda i: (i, 0)),
        out_shape=jax.ShapeDtypeStruct((t_tokens, D_MODEL), F32),
        compiler_params=_params(("parallel",), 32),
        name="final_norm",
    )(x1, peer_t, scl, gate2, g_final)


def _split_cols(a, sizes):
    out, acc = [], 0
    for s in sizes:
        out.append(a[:, acc:acc + s])
        acc += s
    return out


def _inproj_weights(w_in):
    kvw = KV_GROUPS * HEAD_DIM
    sizes = (1024,) + (kvw,) * 6 + (3 * N_HEADS, 1024, 1024, 1024, 1024)
    q_a, k_c, v_c, k_s, v_s, k_w, v_w, g_a, q_r, k_r, v_r, g_r = _split_cols(w_in, sizes)
    d = w_in.shape[0]

    def grp(a, g):
        return a[:, g * HEAD_DIM:(g + 1) * HEAD_DIM]

    zeros = jnp.zeros((d, HEAD_DIM), w_in.dtype)
    cv = [jnp.concatenate([grp(k_c, g), grp(v_c, g)], axis=1) for g in range(KV_GROUPS)]
    ksp = [jnp.concatenate([grp(k_s, g), zeros], axis=1) for g in range(KV_GROUPS)]
    kwp = [jnp.concatenate([grp(k_w, g), zeros], axis=1) for g in range(KV_GROUPS)]
    w_std = jnp.concatenate(cv + ksp + kwp + [q_r, k_r, v_r, g_r], axis=1).astype(BF16)
    gcols = []
    for g in range(KV_GROUPS):
        for br in range(3):
            for r in range(GROUP):
                c = (g * GROUP + r) * 3 + br
                gcols.append(g_a[:, c:c + 1])
        gcols.append(jnp.zeros((d, 4), w_in.dtype))
    w_tr = jnp.concatenate([q_a, v_s, v_w] + gcols, axis=1).T.astype(BF16)
    return w_std, w_tr


def _compress_weights(pe_k, pe_v, k_w1, k_w2, v_w1, v_w2):
    half = CMP_LEN // 2

    def w1_half(w1k, w1v, lo):
        a = w1k.reshape(CMP_LEN, HEAD_DIM, CMP_HIDDEN)[lo:lo + half]
        b = w1v.reshape(CMP_LEN, HEAD_DIM, CMP_HIDDEN)[lo:lo + half]
        za = jnp.zeros_like(a)
        top = jnp.concatenate([a, za], axis=2)
        bot = jnp.concatenate([za, b], axis=2)
        return jnp.concatenate([top, bot], axis=1).reshape(half * 2 * HEAD_DIM, 2 * CMP_HIDDEN).astype(BF16)

    def pe_half(lo):
        row = jnp.concatenate([pe_k[lo:lo + half], pe_v[lo:lo + half]], axis=1).reshape(1, -1)
        return jnp.broadcast_to(row, (8, row.shape[1])).astype(BF16)

    wa = w1_half(k_w1, v_w1, 0)
    wb = w1_half(k_w1, v_w1, half)
    w2k = jnp.zeros((2 * CMP_HIDDEN, LANES), F32).at[:CMP_HIDDEN, :HEAD_DIM].set(k_w2).astype(BF16)
    w2vt = jnp.zeros((HEAD_DIM, 2 * CMP_HIDDEN), F32).at[:, CMP_HIDDEN:].set(v_w2.T).astype(BF16)
    return wa, wb, pe_half(0), pe_half(half), w2k, w2vt


def _nsa_constants(seq):
    slopes = jnp.exp2(-8.0 * (jnp.arange(N_HEADS, dtype=F32) + 1.0) / N_HEADS) * LOG2E
    s_hi = slopes.astype(BF16)
    s_lo = (slopes - s_hi.astype(F32)).astype(BF16)
    rows = jnp.zeros((N_HEADS, HEAD_DIM), BF16)
    rows = rows.at[:, 0].set(s_hi).at[:, 1].set(s_hi).at[:, 2].set(s_lo).at[:, 3].set(s_lo)
    qaug = jnp.broadcast_to(rows[:, :, None], (N_HEADS, HEAD_DIM, LANES))
    n_rows = seq // CMP_STRIDE
    n_slc = seq // SLC_BLOCK
    start = np.arange(n_rows)[:, None] * CMP_STRIDE
    end = start + CMP_LEN - 1
    blk = np.arange(n_slc)[None, :] * SLC_BLOCK
    ovl = ((start < blk + SLC_BLOCK) & (end >= blk)).astype(np.float32)
    per_tile = TK_SEL // SLC_BLOCK
    grp = (np.arange(n_slc)[None, :] // per_tile == np.arange(n_slc // per_tile)[:, None])
    return qaug, jnp.asarray(ovl.T, BF16), jnp.asarray(grp.astype(np.float32), BF16)


def _retention_constants():
    h, c = RET_HEADS, RET_CHUNK
    lg = jnp.log1p(-jnp.exp2(-5.0 - jnp.arange(h, dtype=F32)))
    pos = jnp.arange(c, dtype=F32)
    diff = pos[:, None] - pos[None, :]
    scale = RET_DIM ** -0.5
    dm = jnp.where(diff >= 0, jnp.exp(lg[:, None, None] * jnp.maximum(diff, 0.0)), 0.0) * scale
    k_decay = jnp.exp(lg[:, None] * (c - 1.0 - pos)) * scale
    q_decay = jnp.exp(lg[:, None] * (pos + 1.0))
    chunk_decay = jnp.exp(lg * c)
    kd = jnp.broadcast_to(k_decay[:, :, None], (h, c, RET_DIM))
    qd = jnp.broadcast_to(q_decay[:, :, None], (h, c, RET_DIM))
    cd = jnp.broadcast_to(chunk_decay[:, None, None], (h, RET_DIM, RET_DIM))
    return dm, kd, qd, cd


def kernel(x, c, w_ada, b_ada, g_norm_mix, g_norm_ffn, g_norm_final, w_in, cmp_pe_k, cmp_pe_v,
           cmp_k_w1, cmp_k_w2, cmp_v_w1, cmp_v_w2, g_nsa_out, g_ret_out, w_out,
           peer_w_q, peer_sub_keys, peer_u, peer_v):
    batch, seq, d = x.shape
    depth = w_ada.shape[0]
    t_tokens = batch * seq
    xf = x.reshape(t_tokens, d)
    c_pad = jnp.zeros((8, d), F32).at[:batch].set(c)
    qaug, ovl_t, grp = _nsa_constants(seq)
    dm, kd, qd, cd = _retention_constants()

    for l in range(depth):
        mod = _adaln(c_pad, w_ada[l], b_ada[l][None, :])[:batch].reshape(batch, 6, 1, d)
        shift1, scale1, gate1, shift2, scale2, gate2 = (mod[:, k] for k in range(6))

        w_std, w_tr = _inproj_weights(w_in[l])
        (cv, ks, kw, q_r, k_r, v_r, g_r, qt, vst, vwt, gt) = _inproj(
            xf, scale1, shift1, g_norm_mix[l][None, :], w_std, w_tr, seq)

        cv4 = cv.reshape(KV_GROUPS, batch, seq // CMP_STRIDE, CMP_STRIDE * LANES)
        kcp, vct = _compress(cv4, *_compress_weights(cmp_pe_k[l], cmp_pe_v[l], cmp_k_w1[l], cmp_k_w2[l],
                                                     cmp_v_w1[l], cmp_v_w2[l]))
        gout_b = jnp.broadcast_to(g_nsa_out[l][:, :, None], (N_HEADS, HEAD_DIM, LANES))
        o_nsa = _nsa(qt, gt, kcp, vct, ks, kw, vst, vwt, qaug, gout_b, ovl_t, grp, batch, seq)

        go = jnp.broadcast_to(g_ret_out[l][:, None, :], (RET_HEADS, 8, RET_DIM))
        o_ret = _retention(q_r, k_r, v_r, g_r, dm, kd, qd, cd, go, batch, seq)

        sub_keys = peer_sub_keys[l].reshape(2 * PEER_HEADS, PEER_KEYS, -1).astype(BF16)
        s_u = _pow2_scale(jnp.max(jnp.abs(peer_u[l])))
        s_v = _pow2_scale(jnp.max(jnp.abs(peer_v[l])))
        u_norm = jnp.sqrt(jnp.max(jnp.sum(jnp.square(peer_u[l]), axis=1)))
        peer_scales = jnp.broadcast_to(
            jnp.stack([1.0 / s_u, 1.0 / s_v, u_norm] + [jnp.zeros((), F32)] * 5)[:, None], (8, LANES))
        x1, h2t, st, scl = _mid(o_nsa, o_ret, xf, gate1, scale2, shift2, g_norm_ffn[l][None, :],
                                w_out[l].astype(BF16), peer_w_q[l].T.astype(BF16), sub_keys, peer_scales, seq)

        lrow, w1, r2, w2 = _peer_select(st, scl)
        peer_t = _peer_expert(h2t, (peer_u[l] * s_u).astype(FP8),
                              _transpose_fp8(peer_v[l], jnp.full((8, LANES), s_v, F32)),
                              lrow, w1, r2, w2, scl)
        xf = _final(x1, peer_t, scl, gate2, g_norm_final[None, :], seq, apply_norm=(l == depth - 1))
    return xf.reshape(batch, seq, d)
```

```python
import functools
import math

import numpy as np
import jax
import jax.numpy as jnp
from jax import lax
from jax.experimental import pallas as pl
from jax.experimental.pallas import tpu as pltpu

F32 = jnp.float32
BF16 = jnp.bfloat16
FP8 = jnp.float8_e4m3fn
FP8_TARGET = 224.0

D_MODEL = 2048
N_HEADS = 16
HEAD_DIM = 64
KV_GROUPS = 4
GROUP = 4
CMP_LEN = 32
CMP_STRIDE = 16
CMP_HIDDEN = 128
SLC_BLOCK = 64
SLC_TOPK = 16
WINDOW = 512
FORCE_SCORE = 1e4
NEG = -1e30
RET_HEADS = 8
RET_DIM = 128
RET_CHUNK = 128
PEER_HEADS = 8
PEER_KEYS = 128
PEER_EXPERTS = PEER_KEYS * PEER_KEYS
PEER_TOPK = 16
COEF_BOUND_FACTOR = 1.25 * PEER_HEADS
EPS = 1e-6
LOG2E = 1.4426950408889634

LANES = 128
TQ = 128
TK_SEL = 256
TM_PROJ = 256
TM_PEER = 512
TE_PEER = 1024
TM_SEL = 256
RET_TILE = 512

_NT = (((1,), (1,)), ((), ()))
_TN = (((0,), (0,)), ((), ()))


def _params(sem, vmem_mb):
    return pltpu.CompilerParams(dimension_semantics=sem, vmem_limit_bytes=vmem_mb * 1024 * 1024)


def _resident(shape, index_map):
    return pl.BlockSpec(shape, index_map, pipeline_mode=pl.Buffered(1))


def _gelu(x):
    return jax.nn.gelu(x)


def _adaln_body(c_ref, w_ref, b_ref, o_ref):
    c = c_ref[...]
    act = (c * jax.nn.sigmoid(c)).astype(BF16)
    o_ref[...] = jnp.dot(act, w_ref[...].astype(BF16), preferred_element_type=F32) + b_ref[...]


def _adaln(c_pad, w, b):
    n = w.shape[1]
    tn = 1536
    return pl.pallas_call(
        _adaln_body,
        grid=(n // tn,),
        in_specs=[pl.BlockSpec((8, D_MODEL), lambda j: (0, 0)),
                  pl.BlockSpec((D_MODEL, tn), lambda j: (0, j)),
                  pl.BlockSpec((1, tn), lambda j: (0, j))],
        out_specs=pl.BlockSpec((8, tn), lambda j: (0, j)),
        out_shape=jax.ShapeDtypeStruct((8, n), F32),
        compiler_params=_params(("arbitrary",), 40),
        name="adaln",
    )(c_pad, w, b)


STD_COLS = 512 * 3 + 1024 * 4
TR_ROWS = 1024 + 256 + 256 + 64


def _inproj_body(x_ref, sc_ref, sh_ref, gn_ref, wstd_ref, wt_ref,
                 cv_ref, ks_ref, kw_ref, qr_ref, kr_ref, vr_ref, gr_ref,
                 qt_ref, vst_ref, vwt_ref, gt_ref, *, tiles_per_seq):
    tm = TM_PROJ
    i = pl.program_id(0)
    x = x_ref[...]
    ms = jnp.mean(x * x, axis=-1, keepdims=True)
    h = x * lax.rsqrt(ms + EPS) * gn_ref[...]
    h = h * (1.0 + sc_ref[0]) + sh_ref[0]
    hb = h.astype(BF16)

    def std(a, b):
        return jnp.dot(hb, wstd_ref[:, a:b], preferred_element_type=F32)

    y = std(0, 512)
    for g in range(KV_GROUPS):
        cv_ref[g] = y[:, g * LANES:(g + 1) * LANES].astype(BF16)

    t = (i % tiles_per_seq) * tm + lax.broadcasted_iota(jnp.int32, (tm, LANES), 0)
    lane = lax.broadcasted_iota(jnp.int32, (tm, LANES), 1)
    pos_hi = ((t >> 6) << 6).astype(F32)
    pos_lo = (t & 63).astype(F32)
    aug = jnp.where((lane == 64) | (lane == 66), pos_hi,
                    jnp.where((lane == 65) | (lane == 67), pos_lo, 0.0))
    for ref, off in ((ks_ref, 512), (kw_ref, 1024)):
        y = std(off, off + 512)
        for g in range(KV_GROUPS):
            ref[:, g * LANES:(g + 1) * LANES] = (y[:, g * LANES:(g + 1) * LANES] + aug).astype(BF16)

    for ref, off in ((qr_ref, 1536), (kr_ref, 2560), (vr_ref, 3584), (gr_ref, 4608)):
        ref[...] = std(off, off + 1024).astype(BF16)

    def tr(a, b):
        return lax.dot_general(wt_ref[a:b, :], hb, _NT, preferred_element_type=F32)

    qt = tr(0, 1024) * (HEAD_DIM ** -0.5 * LOG2E)
    vst = tr(1024, 1280)
    vwt = tr(1280, 1536)
    gt = tr(1536, 1600)
    for c in range(tm // LANES):
        sl = slice(c * LANES, (c + 1) * LANES)
        qt_ref[c] = qt[:, sl].astype(BF16)
        vst_ref[c] = vst[:, sl].astype(BF16)
        vwt_ref[c] = vwt[:, sl].astype(BF16)
        gt_ref[c] = gt[:, sl]


def _inproj(x2, scale1, shift1, g_mix, w_std, w_tr, seq):
    t_tokens = x2.shape[0]
    tm = TM_PROJ
    tps = seq // tm
    nt = t_tokens // tm
    c = tm // LANES
    row = lambda i: (i, 0)
    per_b = lambda i: (i // tps, 0, 0)
    out_shape = (
        jax.ShapeDtypeStruct((KV_GROUPS, t_tokens, LANES), BF16),
        jax.ShapeDtypeStruct((t_tokens, 512), BF16),
        jax.ShapeDtypeStruct((t_tokens, 512), BF16),
        jax.ShapeDtypeStruct((t_tokens, 1024), BF16),
        jax.ShapeDtypeStruct((t_tokens, 1024), BF16),
        jax.ShapeDtypeStruct((t_tokens, 1024), BF16),
        jax.ShapeDtypeStruct((t_tokens, 1024), BF16),
        jax.ShapeDtypeStruct((t_tokens // LANES, 1024, LANES), BF16),
        jax.ShapeDtypeStruct((t_tokens // LANES, 256, LANES), BF16),
        jax.ShapeDtypeStruct((t_tokens // LANES, 256, LANES), BF16),
        jax.ShapeDtypeStruct((t_tokens // LANES, 64, LANES), F32),
    )
    out_specs = (
        pl.BlockSpec((KV_GROUPS, tm, LANES), lambda i: (0, i, 0)),
        pl.BlockSpec((tm, 512), row),
        pl.BlockSpec((tm, 512), row),
        pl.BlockSpec((tm, 1024), row),
        pl.BlockSpec((tm, 1024), row),
        pl.BlockSpec((tm, 1024), row),
        pl.BlockSpec((tm, 1024), row),
        pl.BlockSpec((c, 1024, LANES), lambda i: (i, 0, 0)),
        pl.BlockSpec((c, 256, LANES), lambda i: (i, 0, 0)),
        pl.BlockSpec((c, 256, LANES), lambda i: (i, 0, 0)),
        pl.BlockSpec((c, 64, LANES), lambda i: (i, 0, 0)),
    )
    return pl.pallas_call(
        functools.partial(_inproj_body, tiles_per_seq=tps),
        grid=(nt,),
        in_specs=[pl.BlockSpec((tm, D_MODEL), row),
                  pl.BlockSpec((1, 1, D_MODEL), per_b),
                  pl.BlockSpec((1, 1, D_MODEL), per_b),
                  _resident((1, D_MODEL), lambda i: (0, 0)),
                  _resident((D_MODEL, STD_COLS), lambda i: (0, 0)),
                  _resident((TR_ROWS, D_MODEL), lambda i: (0, 0))],
        out_specs=out_specs,
        out_shape=out_shape,
        compiler_params=_params(("parallel",), 56),
        name="inproj",
    )(x2, scale1, shift1, g_mix, w_std, w_tr)


def _compress_body(x_ref, wa_ref, wb_ref, pea_ref, peb_ref, w2k_ref, w2vt_ref, kcp_ref, vct_ref):
    x = x_ref[0, 0]
    n_rows = x.shape[0]
    p = jnp.dot(x, wa_ref[...], preferred_element_type=F32)
    q = jnp.dot(x, wb_ref[...], preferred_element_type=F32)
    pe = (jnp.dot(pea_ref[...], wa_ref[...], preferred_element_type=F32)
          + jnp.dot(peb_ref[...], wb_ref[...], preferred_element_type=F32))[0:1, :]
    pre = p + pltpu.roll(q, n_rows - 1, 0) + pe
    hid = _gelu(pre).astype(BF16)
    kc = jnp.dot(hid, w2k_ref[...], preferred_element_type=F32)
    n = lax.broadcasted_iota(jnp.int32, (n_rows, LANES), 0)
    lane = lax.broadcasted_iota(jnp.int32, (n_rows, LANES), 1)
    ce = n * CMP_STRIDE + (CMP_LEN - 1)
    ce_hi = ((ce >> 6) << 6).astype(F32)
    ce_lo = (ce & 63).astype(F32)
    aug = jnp.where((lane == 64) | (lane == 66), ce_hi,
                    jnp.where((lane == 65) | (lane == 67), ce_lo, 0.0))
    kcp_ref[0] = (kc + aug).astype(BF16)
    vct_ref[0] = lax.dot_general(w2vt_ref[...], hid, _NT, preferred_element_type=F32).astype(BF16)


def _compress(cv4, wa, wb, pea, peb, w2k, w2vt):
    g_, b_, n_rows, _ = cv4.shape
    const2 = lambda n: (0, 0)
    return pl.pallas_call(
        _compress_body,
        grid=(b_ * g_,),
        in_specs=[pl.BlockSpec((1, 1, n_rows, 2048), lambda n: (n % KV_GROUPS, n // KV_GROUPS, 0, 0)),
                  pl.BlockSpec((2048, 256), const2),
                  pl.BlockSpec((2048, 256), const2),
                  pl.BlockSpec((8, 2048), const2),
                  pl.BlockSpec((8, 2048), const2),
                  pl.BlockSpec((256, LANES), const2),
                  pl.BlockSpec((64, 256), const2)],
        out_specs=(pl.BlockSpec((1, n_rows, LANES), lambda n: (n, 0, 0)),
                   pl.BlockSpec((1, 64, n_rows), lambda n: (n, 0, 0))),
        out_shape=(jax.ShapeDtypeStruct((b_ * g_, n_rows, LANES), BF16),
                   jax.ShapeDtypeStruct((b_ * g_, 64, n_rows), BF16)),
        compiler_params=_params(("parallel",), 32),
        name="nsa_compress",
    )(cv4, wa, wb, pea, peb, w2k, w2vt)


def _softmax_step(state, s, pv_prev):
    m_i, l_i, acc = state
    m_new = jnp.maximum(m_i, jnp.max(s, axis=0, keepdims=True))
    alpha = jnp.exp2(m_i - m_new)
    p = jnp.exp2(s - m_new)
    l_new = alpha * l_i + jnp.sum(p, axis=0, keepdims=True)
    return (m_new, l_new, (acc + pv_prev) * alpha), p.astype(BF16)


def _nsa_body(qt_ref, gt_ref, kcp_ref, vct_ref, ks_ref, kw_ref, vst_ref, vwt_ref,
              qaug_ref, gout_ref, ovl_ref, grp_ref, o_ref, selb_ref, sa_ref, sb_ref, pa_ref, pb_ref, *, n_cmp):
    qi = pl.program_id(2)
    t0 = qi * TQ
    wq = GROUP * TQ

    qp = jnp.concatenate(
        [jnp.concatenate([qt_ref[0, r * HEAD_DIM:(r + 1) * HEAD_DIM, :], qaug_ref[r]], axis=0)
         for r in range(GROUP)], axis=1)

    lane_q = lax.broadcasted_iota(jnp.int32, (1, wq), 1) & (TQ - 1)
    n_win = WINDOW // TQ + 1
    k_lo = pl.multiple_of(jnp.maximum(t0 - WINDOW, 0), TQ)

    s = jnp.dot(kcp_ref[0], qp, preferred_element_type=F32)
    s_w = jnp.dot(kw_ref[pl.ds(k_lo, n_win * TQ), :], qp, preferred_element_type=F32)
    n_io = lax.broadcasted_iota(jnp.int32, (n_cmp, wq), 0)
    tl = lax.broadcasted_iota(jnp.int32, (n_cmp, wq), 1) & (TQ - 1)
    valid = (n_io * CMP_STRIDE + (CMP_LEN - 1)) <= (t0 + tl)
    s = jnp.where(valid, s, NEG)
    m = jnp.maximum(jnp.max(s, axis=0, keepdims=True), 0.5 * NEG)
    p = jnp.exp2(s - m)
    l = jnp.sum(p, axis=0, keepdims=True)
    pn = p * (1.0 / jnp.maximum(l, 1e-30))
    o_c = jnp.dot(vct_ref[0], pn.astype(BF16), preferred_element_type=F32)

    ps = pn[:, 0:TQ] + pn[:, TQ:2 * TQ] + pn[:, 2 * TQ:3 * TQ] + pn[:, 3 * TQ:4 * TQ]
    hi = ps.astype(BF16)
    r1 = ps - hi.astype(F32)
    mid = r1.astype(BF16)
    lo = (r1 - mid.astype(F32)).astype(BF16)
    ovl = ovl_ref[...]
    imp = (jnp.dot(ovl, hi, preferred_element_type=F32)
           + jnp.dot(ovl, mid, preferred_element_type=F32)
           + jnp.dot(ovl, lo, preferred_element_type=F32))

    n_slc = imp.shape[0]
    m_io = lax.broadcasted_iota(jnp.int32, (n_slc, TQ), 0)
    q_io = lax.broadcasted_iota(jnp.int32, (n_slc, TQ), 1)
    back = ((t0 + q_io) >> 6) - m_io
    valid_s = back >= 0
    forced = valid_s & ((m_io == 0) | (back < 2))
    w = jnp.where(forced, -jnp.inf, jnp.where(valid_s, imp, -1.0))
    selb = jnp.where(forced, 0.0, NEG)

    def pick(carry, lanes=None):
        w, selb = carry
        mx = jnp.max(w, axis=0, keepdims=True)
        idx = jnp.min(jnp.where(w == mx, m_io, n_slc), axis=0, keepdims=True)
        hit = m_io == idx
        if lanes is not None:
            hit = hit & lanes
        return jnp.where(hit, -jnp.inf, w), jnp.where(hit, 0.0, selb)

    carry = (w, selb)
    for _ in range(SLC_TOPK - 3):
        carry = pick(carry)
    carry = pick(carry, lanes=(t0 + q_io) < 2 * SLC_BLOCK)
    _, selb = pick(carry, lanes=(t0 + q_io) < SLC_BLOCK)
    for mblk in range(n_slc):
        selb_ref[mblk] = jnp.broadcast_to(selb[mblk:mblk + 1, :], (8, TQ))

    per_blk = TK_SEL // SLC_BLOCK

    def scores(j):
        k0 = pl.multiple_of(j * TK_SEL, TK_SEL)
        bias = jnp.concatenate(
            [jnp.tile(selb_ref[j * per_blk + u], (SLC_BLOCK // 8, 1)) for u in range(per_blk)], axis=0)
        bias = jnp.concatenate([bias] * GROUP, axis=1)
        return jnp.dot(ks_ref[pl.ds(k0, TK_SEL), :], qp, preferred_element_type=F32) + bias

    def vt_tile(j):
        return jnp.concatenate([vst_ref[2 * j], vst_ref[2 * j + 1]], axis=1)

    def causal(j):
        kr = lax.broadcasted_iota(jnp.int32, (TK_SEL, wq), 0)
        return kr <= (t0 - j * TK_SEL) + lane_q

    def pv_dot(j, p_ref):
        return jnp.dot(vt_tile(j), p_ref[...], preferred_element_type=F32)

    jd = qi // (TK_SEL // TQ)

    sa_ref[...] = scores(0)
    pb_ref[...] = jnp.zeros_like(pb_ref)
    sel01 = jnp.where(selb == 0.0, 1.0, 0.0).astype(BF16)
    tile_any = jnp.max(jnp.dot(grp_ref[...], sel01, preferred_element_type=F32), axis=1, keepdims=True)
    j_io = lax.broadcasted_iota(jnp.int32, tile_any.shape, 0)
    q_lo_v = jnp.max(jnp.where((tile_any == 0.0) & (j_io <= jd), j_io, -1), axis=0, keepdims=True) + 1
    n_pre_v = jnp.max(jnp.where((tile_any > 0.0) & (j_io < q_lo_v), j_io, -1), axis=0, keepdims=True) + 1

    d = (t0 - k_lo) + lane_q - lax.broadcasted_iota(jnp.int32, (n_win * TQ, wq), 0)
    s_w = jnp.where(lax.bitcast_convert_type(d, jnp.uint32) < WINDOW, s_w, NEG)
    p = jnp.exp2(s_w - jnp.max(s_w, axis=0, keepdims=True))
    l_w = jnp.sum(p, axis=0, keepdims=True)
    vt_w = jnp.concatenate([vwt_ref[k_lo // TQ + u] for u in range(n_win)], axis=1)
    o_w = jnp.dot(vt_w, p.astype(BF16), preferred_element_type=F32) * (1.0 / l_w)

    q_lo = q_lo_v[0, 0]
    n_pre = n_pre_v[0, 0]
    n_vis = n_pre + jd - q_lo + 1

    def tile_at(pos):
        return jnp.where(pos < n_pre, pos, pos - n_pre + q_lo)

    state = (jnp.full((1, wq), NEG, F32), jnp.zeros((1, wq), F32), jnp.zeros((HEAD_DIM, wq), F32))

    def pair(u, state):
        a = 2 * u
        sb_ref[...] = scores(tile_at(a + 1))
        state, p = _softmax_step(state, sa_ref[...], pv_dot(tile_at(jnp.maximum(a - 1, 0)), pb_ref))
        pa_ref[...] = p
        sa_ref[...] = scores(tile_at(a + 2))
        state, p = _softmax_step(state, sb_ref[...], pv_dot(tile_at(a), pa_ref))
        pb_ref[...] = p
        return state

    n_pair = (n_vis - 1) // 2
    state = lax.fori_loop(0, n_pair, pair, state)
    x = 2 * n_pair
    y = jnp.minimum(x + 1, n_vis - 1)
    tx, ty = tile_at(x), tile_at(y)
    sb_ref[...] = scores(ty)
    state, p = _softmax_step(state, jnp.where(causal(tx), sa_ref[...], NEG),
                             pv_dot(tile_at(jnp.maximum(x - 1, 0)), pb_ref))
    pa_ref[...] = p
    ty_mask = jnp.where(x + 1 < n_vis, ty, jd + 1)
    (_, l_s, acc_s), p = _softmax_step(state, jnp.where(causal(ty_mask), sb_ref[...], NEG), pv_dot(tx, pa_ref))
    acc_s = acc_s + jnp.dot(vt_tile(ty), p, preferred_element_type=F32)

    o_s = acc_s * (1.0 / l_s)
    gw = jax.nn.sigmoid(gt_ref[0])
    outs = []
    for r in range(GROUP):
        sl = slice(r * TQ, (r + 1) * TQ)
        o = (gw[r:r + 1, :] * o_c[:, sl] + gw[GROUP + r:GROUP + r + 1, :] * o_s[:, sl]
             + gw[2 * GROUP + r:2 * GROUP + r + 1, :] * o_w[:, sl])
        ms = jnp.mean(o * o, axis=0, keepdims=True)
        outs.append(o * lax.rsqrt(ms + EPS) * gout_ref[r])
    o_ref[...] = jnp.concatenate(outs, axis=0).T.astype(BF16)


def _nsa(qt, gt, kcp, vct, ks, kw, vst, vwt, qaug, gout_b, ovl_t, grp, batch, seq):
    nq = seq // TQ
    n_cmp = kcp.shape[1]
    n_slc = seq // SLC_BLOCK
    t_tokens = batch * seq
    per_b_chunks = seq // LANES
    return pl.pallas_call(
        functools.partial(_nsa_body, n_cmp=n_cmp),
        grid=(batch, KV_GROUPS, nq),
        in_specs=[
            pl.BlockSpec((1, 256, LANES), lambda b, g, q: (b * nq + q, g, 0)),
            pl.BlockSpec((1, 16, LANES), lambda b, g, q: (b * nq + q, g, 0)),
            pl.BlockSpec((1, n_cmp, LANES), lambda b, g, q: (b * KV_GROUPS + g, 0, 0)),
            pl.BlockSpec((1, 64, n_cmp), lambda b, g, q: (b * KV_GROUPS + g, 0, 0)),
            pl.BlockSpec((seq, LANES), lambda b, g, q: (b, g)),
            pl.BlockSpec((seq, LANES), lambda b, g, q: (b, g)),
            pl.BlockSpec((per_b_chunks, 64, LANES), lambda b, g, q: (b, g, 0)),
            pl.BlockSpec((per_b_chunks, 64, LANES), lambda b, g, q: (b, g, 0)),
            pl.BlockSpec((GROUP, 64, LANES), lambda b, g, q: (g, 0, 0)),
            pl.BlockSpec((GROUP, 64, LANES), lambda b, g, q: (g, 0, 0)),
            pl.BlockSpec((n_slc, n_cmp), lambda b, g, q: (0, 0)),
            pl.BlockSpec(grp.shape, lambda b, g, q: (0, 0)),
        ],
        out_specs=pl.BlockSpec((TQ, 256), lambda b, g, q: (b * nq + q, g)),
        out_shape=jax.ShapeDtypeStruct((t_tokens, 1024), BF16),
        scratch_shapes=[pltpu.VMEM((n_slc, 8, TQ), F32),
                        pltpu.VMEM((TK_SEL, GROUP * TQ), F32), pltpu.VMEM((TK_SEL, GROUP * TQ), F32),
                        pltpu.VMEM((TK_SEL, GROUP * TQ), BF16), pltpu.VMEM((TK_SEL, GROUP * TQ), BF16)],
        compiler_params=_params(("parallel", "parallel", "arbitrary"), 40),
        name="nsa_attention",
    )(qt, gt, kcp, vct, ks, kw, vst, vwt, qaug, gout_b, ovl_t, grp)


def _ret_body(q_ref, k_ref, v_ref, g_ref, dm_ref, kd_ref, qd_ref, cd_ref, go_ref, o_ref, st_ref):
    @pl.when(pl.program_id(2) == 0)
    def _():
        st_ref[...] = jnp.zeros_like(st_ref)

    c_ = RET_CHUNK
    slices = [slice(c * c_, (c + 1) * c_) for c in range(RET_TILE // c_)]
    atts, kvs = [], []
    for sl in slices:
        k = k_ref[sl, :]
        atts.append((lax.dot_general(q_ref[sl, :], k, _NT, preferred_element_type=F32) * dm_ref[0]).astype(BF16))
        kdec = (k.astype(F32) * kd_ref[0]).astype(BF16)
        kvs.append(lax.dot_general(kdec, v_ref[sl, :], _TN, preferred_element_type=F32))
    states = [st_ref[...]]
    for kv in kvs:
        states.append(states[-1] * cd_ref[0] + kv)
    st_ref[...] = states[-1]
    for sl, att, state in zip(slices, atts, states):
        o = (jnp.dot(att, v_ref[sl, :], preferred_element_type=F32)
             + qd_ref[0] * jnp.dot(q_ref[sl, :], state.astype(BF16), preferred_element_type=F32))
        mu = jnp.mean(o, axis=-1, keepdims=True)
        oc = o - mu
        var = jnp.mean(oc * oc, axis=-1, keepdims=True)
        y = oc * lax.rsqrt(var + EPS) * go_ref[0, 0:1, :]
        gate = g_ref[sl, :].astype(F32)
        o_ref[sl, :] = (gate * jax.nn.sigmoid(gate) * y).astype(BF16)


def _retention(q_r, k_r, v_r, g_r, dm, kd, qd, cd, go, batch, seq):
    t_tokens = batch * seq
    nc = seq // RET_TILE
    tok = lambda b, h, c: (b * nc + c, h)
    per_h = lambda b, h, c: (h, 0, 0)
    sq = (1, RET_DIM, RET_DIM)
    return pl.pallas_call(
        _ret_body,
        grid=(batch, RET_HEADS, nc),
        in_specs=[pl.BlockSpec((RET_TILE, RET_DIM), tok)] * 4
        + [pl.BlockSpec(sq, per_h)] * 4 + [pl.BlockSpec((1, 8, RET_DIM), per_h)],
        out_specs=pl.BlockSpec((RET_TILE, RET_DIM), tok),
        out_shape=jax.ShapeDtypeStruct((t_tokens, RET_HEADS * RET_DIM), BF16),
        scratch_shapes=[pltpu.VMEM((RET_DIM, RET_DIM), F32)],
        compiler_params=_params(("parallel", "parallel", "arbitrary"), 32),
        name="retention",
    )(q_r, k_r, v_r, g_r, dm, kd, qd, cd, go)


def _pow2_scale(magnitude):
    return jnp.exp2(jnp.floor(jnp.log2(FP8_TARGET / jnp.maximum(magnitude, 1e-30))))


def _mid_body(on_ref, or_ref, x_ref, g1_ref, sc_ref, sh_ref, gn_ref, wo_ref, wqt_ref, sk_ref, ps_ref,
              x1_ref, h2t_ref, st_ref, scl_ref):
    acc = (jnp.dot(on_ref[...], wo_ref[0:1024, :], preferred_element_type=F32)
           + jnp.dot(or_ref[...], wo_ref[1024:2048, :], preferred_element_type=F32))
    x1 = x_ref[...] + g1_ref[0] * acc
    x1_ref[...] = x1
    ms = jnp.mean(x1 * x1, axis=-1, keepdims=True)
    h2 = x1 * lax.rsqrt(ms + EPS) * gn_ref[...]
    h2 = h2 * (1.0 + sc_ref[0]) + sh_ref[0]

    h2_t = h2.T
    amax = jnp.max(jnp.max(jnp.abs(h2_t), axis=0, keepdims=True), axis=1, keepdims=True)
    s_h = _pow2_scale(amax)
    h2t_ref[...] = (h2_t * s_h).astype(FP8)
    norm = jnp.sqrt(jnp.sum(h2_t * h2_t, axis=0, keepdims=True))
    s_c = _pow2_scale(COEF_BOUND_FACTOR * ps_ref[2:3, 0:1] * norm)
    scl_ref[...] = jnp.concatenate(
        [jnp.broadcast_to(ps_ref[0:1, 0:1] / s_h, s_c.shape), s_c, ps_ref[1:2, 0:1] / s_c,
         jnp.zeros((5, s_c.shape[1]), F32)], axis=0)
    h2 = h2.astype(BF16)
    qt = lax.dot_general(wqt_ref[...], h2, _NT, preferred_element_type=F32).astype(BF16)
    for hp in range(2 * PEER_HEADS):
        st_ref[hp] = jnp.dot(sk_ref[hp], qt[hp * 128:(hp + 1) * 128, :], preferred_element_type=F32)


def _mid(o_nsa, o_ret, x2, gate1, scale2, shift2, g_ffn, w_out, wq_t, sub_keys, peer_scales, seq):
    t_tokens = x2.shape[0]
    tm = TM_PROJ
    tps = seq // tm
    row = lambda i: (i, 0)
    per_b = lambda i: (i // tps, 0, 0)
    return pl.pallas_call(
        _mid_body,
        grid=(t_tokens // tm,),
        in_specs=[pl.BlockSpec((tm, 1024), row),
                  pl.BlockSpec((tm, 1024), row),
                  pl.BlockSpec((tm, D_MODEL), row),
                  pl.BlockSpec((1, 1, D_MODEL), per_b),
                  pl.BlockSpec((1, 1, D_MODEL), per_b),
                  pl.BlockSpec((1, 1, D_MODEL), per_b),
                  _resident((1, D_MODEL), lambda i: (0, 0)),
                  _resident((D_MODEL, D_MODEL), lambda i: (0, 0)),
                  _resident((D_MODEL, D_MODEL), lambda i: (0, 0)),
                  _resident((2 * PEER_HEADS, PEER_KEYS, 128), lambda i: (0, 0, 0)),
                  _resident((8, LANES), lambda i: (0, 0))],
        out_specs=(pl.BlockSpec((tm, D_MODEL), row),
                   pl.BlockSpec((D_MODEL, tm), lambda i: (0, i)),
                   pl.BlockSpec((2 * PEER_HEADS, PEER_KEYS, tm), lambda i: (0, 0, i)),
                   pl.BlockSpec((8, tm), lambda i: (0, i))),
        out_shape=(jax.ShapeDtypeStruct((t_tokens, D_MODEL), F32),
                   jax.ShapeDtypeStruct((D_MODEL, t_tokens), FP8),
                   jax.ShapeDtypeStruct((2 * PEER_HEADS, PEER_KEYS, t_tokens), F32),
                   jax.ShapeDtypeStruct((8, t_tokens), F32)),
        compiler_params=_params(("parallel",), 48),
        name="outproj_peerq",
    )(o_nsa, o_ret, x2, gate1, scale2, shift2, g_ffn, w_out, wq_t, sub_keys, peer_scales)


def _top16(s, break_ties):
    n_rows, n = s.shape
    io = lax.broadcasted_iota(jnp.int32, (n_rows, n), 0)
    a_io = lax.broadcasted_iota(jnp.int32, (PEER_TOPK, n), 0)
    rank = jnp.full((n_rows, n), PEER_TOPK, jnp.int32)
    vals = jnp.zeros((PEER_TOPK, n), F32)
    for a in range(PEER_TOPK):
        mx = jnp.max(s, axis=0, keepdims=True)
        hit = s == mx
        if break_ties:
            hit = io == jnp.min(jnp.where(hit, io, n_rows), axis=0, keepdims=True)
        rank = jnp.where(hit, a, rank)
        s = jnp.where(hit, -jnp.inf, s)
        vals = jnp.where(a_io == a, mx, vals)
    return vals, rank


def _peer_select_body(s_ref, scl_ref, l_ref, w1_ref, r2_ref, w2_ref):
    s1 = s_ref[0]
    s2 = s_ref[1]
    n = s1.shape[1]

    def select(break_ties):
        v1, rank1 = _top16(s1, break_ties)
        v2, rank2 = _top16(s2, break_ties)
        a_io = lax.broadcasted_iota(jnp.int32, (PEER_TOPK, n), 0)
        cnt = jnp.zeros((PEER_TOPK, n), jnp.int32)
        cur = v1 + v2[0:1, :]
        top = v1[0:1, :] + v2[0:1, :]
        z = jnp.zeros((1, n), F32)
        for _ in range(PEER_TOPK):
            mx = jnp.max(cur, axis=0, keepdims=True)
            aidx = jnp.min(jnp.where(cur == mx, a_io, PEER_TOPK), axis=0, keepdims=True)
            hit = a_io == aidx
            cnt = cnt + hit.astype(jnp.int32)
            nxt = jnp.sum(jnp.where(hit, cnt, 0), axis=0, keepdims=True)
            nv = jnp.max(jnp.where(a_io == nxt, v2, -jnp.inf), axis=0, keepdims=True)
            cur = jnp.where(hit, v1 + nv, cur)
            z = z + jnp.exp(mx - top)
        cnt_b = cnt.astype(F32).astype(BF16)
        rank_b = rank1.astype(F32).astype(BF16)
        lrow = jnp.zeros(s1.shape, BF16)
        for a in range(PEER_TOPK):
            lrow = jnp.where(rank_b == a, jnp.broadcast_to(cnt_b[a:a + 1, :], s1.shape), lrow)
        l_ref[0] = lrow.astype(F32)
        w1_ref[0] = jnp.exp(s1 - v1[0:1, :])
        r2_ref[0] = rank2.astype(F32).astype(BF16)
        w2_ref[0] = (jnp.exp(s2 - v2[0:1, :]) * (scl_ref[1:2, :] / z)).astype(BF16)
        return rank1, rank2

    rank1, rank2 = select(False)
    marked = (jnp.sum((rank1 < PEER_TOPK).astype(jnp.int32), axis=0, keepdims=True)
              + jnp.sum((rank2 < PEER_TOPK).astype(jnp.int32), axis=0, keepdims=True))

    @pl.when(jnp.max(marked) != 2 * PEER_TOPK)
    def _():
        select(True)


def _peer_select(st, scl):
    t_tokens = st.shape[2]
    tm = TM_SEL
    shp = jax.ShapeDtypeStruct((PEER_HEADS, PEER_KEYS, t_tokens), F32)
    shp_b = jax.ShapeDtypeStruct((PEER_HEADS, PEER_KEYS, t_tokens), BF16)
    spec = pl.BlockSpec((1, PEER_KEYS, tm), lambda i, h: (h, 0, i))
    return pl.pallas_call(
        _peer_select_body,
        grid=(t_tokens // tm, PEER_HEADS),
        in_specs=[pl.BlockSpec((2, PEER_KEYS, tm), lambda i, h: (h, 0, i)),
                  pl.BlockSpec((8, tm), lambda i, h: (0, i))],
        out_specs=(spec, spec, spec, spec),
        out_shape=(shp, shp, shp_b, shp_b),
        compiler_params=_params(("parallel", "parallel"), 32),
        name="peer_select",
    )(st, scl)


def _transpose_body(v_ref, s_ref, o_ref):
    o_ref[...] = (v_ref[...].T * s_ref[0:1, 0:1]).astype(FP8)


def _transpose_fp8(v, scale_tile):
    n, d = v.shape
    tn = 512
    return pl.pallas_call(
        _transpose_body,
        grid=(n // tn,),
        in_specs=[pl.BlockSpec((tn, d), lambda i: (i, 0)),
                  pl.BlockSpec((8, LANES), lambda i: (0, 0))],
        out_specs=pl.BlockSpec((d, tn), lambda i: (0, i)),
        out_shape=jax.ShapeDtypeStruct((d, n), FP8),
        compiler_params=_params(("parallel",), 32),
        name="transpose_v",
    )(v, scale_tile)


def _peer_expert_body(h2t_ref, u_ref, vt_ref, l_ref, w1_ref, r2_ref, w2_ref, scl_ref,
                      o_ref, ce_ref, co_ref, *, steps_per_tile):
    g = pl.program_id(0)
    sw = 256
    strips = [slice(c * sw, (c + 1) * sw) for c in range(TM_PEER // sw)]
    n_piece = TE_PEER // PEER_KEYS
    blk = D_MODEL // n_piece

    @pl.when(g == 0)
    def _():
        co_ref[...] = jnp.zeros_like(co_ref)

    @pl.when((g == 0) | ((g - 1) % steps_per_tile == 0))
    def _():
        o_ref[...] = jnp.zeros_like(o_ref)

    def run(c_new, c_old):
        def piece(j, carry):
            r0 = pl.multiple_of(j * PEER_KEYS, PEER_KEYS)
            d0 = pl.multiple_of(j * blk, blk)
            for ls in strips:
                coef = None
                for h in range(PEER_HEADS):
                    lrow = jnp.broadcast_to(l_ref[h, pl.ds(j, 1), ls], (16, sw)).astype(BF16)
                    w1row = jnp.broadcast_to(w1_ref[h, pl.ds(j, 1), ls], (16, sw)).astype(BF16)
                    lrow = jnp.tile(lrow, (PEER_KEYS // 16, 1))
                    w1row = jnp.tile(w1row, (PEER_KEYS // 16, 1))
                    term = jnp.where(r2_ref[h, :, ls] < lrow, w2_ref[h, :, ls] * w1row, jnp.zeros((), BF16))
                    coef = term if coef is None else coef + term
                a_t = jnp.dot(u_ref[pl.ds(r0, PEER_KEYS), :], h2t_ref[:, ls], preferred_element_type=F32)
                a_scale = jnp.broadcast_to(scl_ref[0:1, ls], (16, sw)).astype(BF16)
                act = _gelu(a_t.astype(BF16) * jnp.tile(a_scale, (PEER_KEYS // 16, 1)))
                c_new[pl.ds(r0, PEER_KEYS), ls] = (coef * act).astype(FP8)
                o_ref[pl.ds(d0, blk), ls] += jnp.dot(vt_ref[pl.ds(d0, blk), :], c_old[:, ls],
                                                     preferred_element_type=F32)
            return carry

        lax.fori_loop(0, n_piece, piece, 0, unroll=4)

    @pl.when(g % 2 == 0)
    def _():
        run(ce_ref, co_ref)

    @pl.when(g % 2 == 1)
    def _():
        run(co_ref, ce_ref)


def _peer_expert(h2t, u_b, v_t, lrow, w1, r2, w2, scl):
    t_tokens = h2t.shape[1]
    tm, te = TM_PEER, TE_PEER
    n_piece = te // PEER_KEYS
    ne = PEER_EXPERTS // te
    n_steps = (t_tokens // tm) * ne
    cur = lambda g: jnp.minimum(g, n_steps - 1)
    prev = lambda g: jnp.maximum(g - 1, 0)
    row_spec = pl.BlockSpec((PEER_HEADS, n_piece, tm), lambda g: (0, cur(g) % ne, cur(g) // ne))
    full_spec = pl.BlockSpec((PEER_HEADS, PEER_KEYS, tm), lambda g: (0, 0, cur(g) // ne))
    return pl.pallas_call(
        functools.partial(_peer_expert_body, steps_per_tile=ne),
        grid=(n_steps + 1,),
        in_specs=[pl.BlockSpec((D_MODEL, tm), lambda g: (0, cur(g) // ne)),
                  pl.BlockSpec((te, D_MODEL), lambda g: (cur(g) % ne, 0)),
                  pl.BlockSpec((D_MODEL, te), lambda g: (0, prev(g) % ne)),
                  row_spec, row_spec, full_spec, full_spec,
                  pl.BlockSpec((8, tm), lambda g: (0, cur(g) // ne))],
        out_specs=pl.BlockSpec((D_MODEL, tm), lambda g: (0, prev(g) // ne)),
        out_shape=jax.ShapeDtypeStruct((D_MODEL, t_tokens), F32),
        scratch_shapes=[pltpu.VMEM((te, tm), FP8), pltpu.VMEM((te, tm), FP8)],
        compiler_params=_params(("arbitrary",), 52),
        name="peer_experts",
    )(h2t, u_b, v_t, lrow, w1, r2, w2, scl)


def _final_body(x1_ref, pt_ref, scl_ref, g2_ref, gn_ref, o_ref, *, apply_norm):
    peer = (pt_ref[...] * scl_ref[2:3, :]).T
    y = x1_ref[...] + g2_ref[0] * peer
    if apply_norm:
        ms = jnp.mean(y * y, axis=-1, keepdims=True)
        y = y * lax.rsqrt(ms + EPS) * gn_ref[...]
    o_ref[...] = y


def _final(x1, peer_t, scl, gate2, g_final, seq, apply_norm):
    t_tokens = x1.shape[0]
    tm = TM_PROJ
    tps = seq // tm
    return pl.pallas_call(
        functools.partial(_final_body, apply_norm=apply_norm),
        grid=(t_tokens // tm,),
        in_specs=[pl.BlockSpec((tm, D_MODEL), lambda i: (i, 0)),
                  pl.BlockSpec((D_MODEL, tm), lambda i: (0, i)),
                  pl.BlockSpec((8, tm), lambda i: (0, i)),
                  pl.BlockSpec((1, 1, D_MODEL), lambda i: (i // tps, 0, 0)),
                  pl.BlockSpec((1, D_MODEL), lambda i: (0, 0))],
        out_specs=pl.BlockSpec((tm, D_MODEL), lambda i: (i, 0)),
        out_shape=jax.ShapeDtypeStruct((t_tokens, D_MODEL), F32),
        compiler_params=_params(("parallel",), 32),
        name="final_norm",
    )(x1, peer_t, scl, gate2, g_final)


def _split_cols(a, sizes):
    out, acc = [], 0
    for s in sizes:
        out.append(a[:, acc:acc + s])
        acc += s
    return out


def _inproj_weights(w_in):
    kvw = KV_GROUPS * HEAD_DIM
    sizes = (1024,) + (kvw,) * 6 + (3 * N_HEADS, 1024, 1024, 1024, 1024)
    q_a, k_c, v_c, k_s, v_s, k_w, v_w, g_a, q_r, k_r, v_r, g_r = _split_cols(w_in, sizes)
    d = w_in.shape[0]

    def grp(a, g):
        return a[:, g * HEAD_DIM:(g + 1) * HEAD_DIM]

    zeros = jnp.zeros((d, HEAD_DIM), w_in.dtype)
    cv = [jnp.concatenate([grp(k_c, g), grp(v_c, g)], axis=1) for g in range(KV_GROUPS)]
    ksp = [jnp.concatenate([grp(k_s, g), zeros], axis=1) for g in range(KV_GROUPS)]
    kwp = [jnp.concatenate([grp(k_w, g), zeros], axis=1) for g in range(KV_GROUPS)]
    w_std = jnp.concatenate(cv + ksp + kwp + [q_r, k_r, v_r, g_r], axis=1).astype(BF16)
    gcols = []
    for g in range(KV_GROUPS):
        for br in range(3):
            for r in range(GROUP):
                c = (g * GROUP + r) * 3 + br
                gcols.append(g_a[:, c:c + 1])
        gcols.append(jnp.zeros((d, 4), w_in.dtype))
    w_tr = jnp.concatenate([q_a, v_s, v_w] + gcols, axis=1).T.astype(BF16)
    return w_std, w_tr


def _compress_weights(pe_k, pe_v, k_w1, k_w2, v_w1, v_w2):
    half = CMP_LEN // 2

    def w1_half(w1k, w1v, lo):
        a = w1k.reshape(CMP_LEN, HEAD_DIM, CMP_HIDDEN)[lo:lo + half]
        b = w1v.reshape(CMP_LEN, HEAD_DIM, CMP_HIDDEN)[lo:lo + half]
        za = jnp.zeros_like(a)
        top = jnp.concatenate([a, za], axis=2)
        bot = jnp.concatenate([za, b], axis=2)
        return jnp.concatenate([top, bot], axis=1).reshape(half * 2 * HEAD_DIM, 2 * CMP_HIDDEN).astype(BF16)

    def pe_half(lo):
        row = jnp.concatenate([pe_k[lo:lo + half], pe_v[lo:lo + half]], axis=1).reshape(1, -1)
        return jnp.broadcast_to(row, (8, row.shape[1])).astype(BF16)

    wa = w1_half(k_w1, v_w1, 0)
    wb = w1_half(k_w1, v_w1, half)
    w2k = jnp.zeros((2 * CMP_HIDDEN, LANES), F32).at[:CMP_HIDDEN, :HEAD_DIM].set(k_w2).astype(BF16)
    w2vt = jnp.zeros((HEAD_DIM, 2 * CMP_HIDDEN), F32).at[:, CMP_HIDDEN:].set(v_w2.T).astype(BF16)
    return wa, wb, pe_half(0), pe_half(half), w2k, w2vt


def _nsa_constants(seq):
    slopes = jnp.exp2(-8.0 * (jnp.arange(N_HEADS, dtype=F32) + 1.0) / N_HEADS) * LOG2E
    s_hi = slopes.astype(BF16)
    s_lo = (slopes - s_hi.astype(F32)).astype(BF16)
    rows = jnp.zeros((N_HEADS, HEAD_DIM), BF16)
    rows = rows.at[:, 0].set(s_hi).at[:, 1].set(s_hi).at[:, 2].set(s_lo).at[:, 3].set(s_lo)
    qaug = jnp.broadcast_to(rows[:, :, None], (N_HEADS, HEAD_DIM, LANES))
    n_rows = seq // CMP_STRIDE
    n_slc = seq // SLC_BLOCK
    start = np.arange(n_rows)[:, None] * CMP_STRIDE
    end = start + CMP_LEN - 1
    blk = np.arange(n_slc)[None, :] * SLC_BLOCK
    ovl = ((start < blk + SLC_BLOCK) & (end >= blk)).astype(np.float32)
    per_tile = TK_SEL // SLC_BLOCK
    grp = (np.arange(n_slc)[None, :] // per_tile == np.arange(n_slc // per_tile)[:, None])
    return qaug, jnp.asarray(ovl.T, BF16), jnp.asarray(grp.astype(np.float32), BF16)


def _retention_constants():
    h, c = RET_HEADS, RET_CHUNK
    lg = jnp.log1p(-jnp.exp2(-5.0 - jnp.arange(h, dtype=F32)))
    pos = jnp.arange(c, dtype=F32)
    diff = pos[:, None] - pos[None, :]
    scale = RET_DIM ** -0.5
    dm = jnp.where(diff >= 0, jnp.exp(lg[:, None, None] * jnp.maximum(diff, 0.0)), 0.0) * scale
    k_decay = jnp.exp(lg[:, None] * (c - 1.0 - pos)) * scale
    q_decay = jnp.exp(lg[:, None] * (pos + 1.0))
    chunk_decay = jnp.exp(lg * c)
    kd = jnp.broadcast_to(k_decay[:, :, None], (h, c, RET_DIM))
    qd = jnp.broadcast_to(q_decay[:, :, None], (h, c, RET_DIM))
    cd = jnp.broadcast_to(chunk_decay[:, None, None], (h, RET_DIM, RET_DIM))
    return dm, kd, qd, cd


def kernel(x, c, w_ada, b_ada, g_norm_mix, g_norm_ffn, g_norm_final, w_in, cmp_pe_k, cmp_pe_v,
           cmp_k_w1, cmp_k_w2, cmp_v_w1, cmp_v_w2, g_nsa_out, g_ret_out, w_out,
           peer_w_q, peer_sub_keys, peer_u, peer_v):
    batch, seq, d = x.shape
    depth = w_ada.shape[0]
    t_tokens = batch * seq
    xf = x.reshape(t_tokens, d)
    c_pad = jnp.zeros((8, d), F32).at[:batch].set(c)
    qaug, ovl_t, grp = _nsa_constants(seq)
    dm, kd, qd, cd = _retention_constants()

    for l in range(depth):
        mod = _adaln(c_pad, w_ada[l], b_ada[l][None, :])[:batch].reshape(batch, 6, 1, d)
        shift1, scale1, gate1, shift2, scale2, gate2 = (mod[:, k] for k in range(6))

        w_std, w_tr = _inproj_weights(w_in[l])
        (cv, ks, kw, q_r, k_r, v_r, g_r, qt, vst, vwt, gt) = _inproj(
            xf, scale1, shift1, g_norm_mix[l][None, :], w_std, w_tr, seq)

        cv4 = cv.reshape(KV_GROUPS, batch, seq // CMP_STRIDE, CMP_STRIDE * LANES)
        kcp, vct = _compress(cv4, *_compress_weights(cmp_pe_k[l], cmp_pe_v[l], cmp_k_w1[l], cmp_k_w2[l],
                                                     cmp_v_w1[l], cmp_v_w2[l]))
        gout_b = jnp.broadcast_to(g_nsa_out[l][:, :, None], (N_HEADS, HEAD_DIM, LANES))
        o_nsa = _nsa(qt, gt, kcp, vct, ks, kw, vst, vwt, qaug, gout_b, ovl_t, grp, batch, seq)

        go = jnp.broadcast_to(g_ret_out[l][:, None, :], (RET_HEADS, 8, RET_DIM))
        o_ret = _retention(q_r, k_r, v_r, g_r, dm, kd, qd, cd, go, batch, seq)

        sub_keys = peer_sub_keys[l].reshape(2 * PEER_HEADS, PEER_KEYS, -1).astype(BF16)
        u_norm = jnp.sqrt(jnp.max(jnp.sum(jnp.square(peer_u[l]), axis=1)))
        s_u = _pow2_scale(u_norm)
        s_v = _pow2_scale(jnp.max(jnp.abs(peer_v[l])))
        peer_scales = jnp.broadcast_to(
            jnp.stack([1.0 / s_u, 1.0 / s_v, u_norm] + [jnp.zeros((), F32)] * 5)[:, None], (8, LANES))
        x1, h2t, st, scl = _mid(o_nsa, o_ret, xf, gate1, scale2, shift2, g_norm_ffn[l][None, :],
                                w_out[l].astype(BF16), peer_w_q[l].T.astype(BF16), sub_keys, peer_scales, seq)

        lrow, w1, r2, w2 = _peer_select(st, scl)
        peer_t = _peer_expert(h2t, (peer_u[l] * s_u).astype(FP8),
                              _transpose_fp8(peer_v[l], jnp.full((8, LANES), s_v, F32)),
                              lrow, w1, r2, w2, scl)
        xf = _final(x1, peer_t, scl, gate2, g_norm_final[None, :], seq, apply_norm=(l == depth - 1))
    return xf.reshape(batch, seq, d)
```

```python
import functools
import math

import numpy as np
import jax
import jax.numpy as jnp
from jax import lax
from jax.experimental import pallas as pl
from jax.experimental.pallas import tpu as pltpu

F32 = jnp.float32
BF16 = jnp.bfloat16
FP8 = jnp.float8_e4m3fn
FP8_TARGET = 224.0

D_MODEL = 2048
N_HEADS = 16
HEAD_DIM = 64
KV_GROUPS = 4
GROUP = 4
CMP_LEN = 32
CMP_STRIDE = 16
CMP_HIDDEN = 128
SLC_BLOCK = 64
SLC_TOPK = 16
WINDOW = 512
FORCE_SCORE = 1e4
NEG = -1e30
RET_HEADS = 8
RET_DIM = 128
RET_CHUNK = 128
PEER_HEADS = 8
PEER_KEYS = 128
PEER_EXPERTS = PEER_KEYS * PEER_KEYS
PEER_TOPK = 16
COEF_BOUND_FACTOR = 1.25 * PEER_HEADS
EPS = 1e-6
LOG2E = 1.4426950408889634

LANES = 128
TQ = 128
TK_SEL = 256
TM_PROJ = 256
TM_PEER = 512
TE_PEER = 1024
TM_SEL = 256
RET_TILE = 512

_NT = (((1,), (1,)), ((), ()))
_TN = (((0,), (0,)), ((), ()))


def _params(sem, vmem_mb):
    return pltpu.CompilerParams(dimension_semantics=sem, vmem_limit_bytes=vmem_mb * 1024 * 1024)


def _resident(shape, index_map):
    return pl.BlockSpec(shape, index_map, pipeline_mode=pl.Buffered(1))


def _gelu(x):
    return jax.nn.gelu(x)


def _adaln_body(c_ref, w_ref, b_ref, o_ref):
    c = c_ref[...]
    act = (c * jax.nn.sigmoid(c)).astype(BF16)
    o_ref[...] = jnp.dot(act, w_ref[...].astype(BF16), preferred_element_type=F32) + b_ref[...]


def _adaln(c_pad, w, b):
    n = w.shape[1]
    tn = 1536
    return pl.pallas_call(
        _adaln_body,
        grid=(n // tn,),
        in_specs=[pl.BlockSpec((8, D_MODEL), lambda j: (0, 0)),
                  pl.BlockSpec((D_MODEL, tn), lambda j: (0, j)),
                  pl.BlockSpec((1, tn), lambda j: (0, j))],
        out_specs=pl.BlockSpec((8, tn), lambda j: (0, j)),
        out_shape=jax.ShapeDtypeStruct((8, n), F32),
        compiler_params=_params(("arbitrary",), 40),
        name="adaln",
    )(c_pad, w, b)


STD_COLS = 512 * 3 + 1024 * 4
TR_ROWS = 1024 + 256 + 256 + 64


def _inproj_body(x_ref, sc_ref, sh_ref, gn_ref, wstd_ref, wt_ref,
                 cv_ref, ks_ref, kw_ref, qr_ref, kr_ref, vr_ref, gr_ref,
                 qt_ref, vst_ref, vwt_ref, gt_ref, *, tiles_per_seq):
    tm = TM_PROJ
    i = pl.program_id(0)
    x = x_ref[...]
    ms = jnp.mean(x * x, axis=-1, keepdims=True)
    h = x * lax.rsqrt(ms + EPS) * gn_ref[...]
    h = h * (1.0 + sc_ref[0]) + sh_ref[0]
    hb = h.astype(BF16)

    def std(a, b):
        return jnp.dot(hb, wstd_ref[:, a:b], preferred_element_type=F32)

    y = std(0, 512)
    for g in range(KV_GROUPS):
        cv_ref[g] = y[:, g * LANES:(g + 1) * LANES].astype(BF16)

    t = (i % tiles_per_seq) * tm + lax.broadcasted_iota(jnp.int32, (tm, LANES), 0)
    lane = lax.broadcasted_iota(jnp.int32, (tm, LANES), 1)
    pos_hi = ((t >> 6) << 6).astype(F32)
    pos_lo = (t & 63).astype(F32)
    aug = jnp.where((lane == 64) | (lane == 66), pos_hi,
                    jnp.where((lane == 65) | (lane == 67), pos_lo, 0.0))
    for ref, off in ((ks_ref, 512), (kw_ref, 1024)):
        y = std(off, off + 512)
        for g in range(KV_GROUPS):
            ref[:, g * LANES:(g + 1) * LANES] = (y[:, g * LANES:(g + 1) * LANES] + aug).astype(BF16)

    for ref, off in ((qr_ref, 1536), (kr_ref, 2560), (vr_ref, 3584), (gr_ref, 4608)):
        ref[...] = std(off, off + 1024).astype(BF16)

    def tr(a, b):
        return lax.dot_general(wt_ref[a:b, :], hb, _NT, preferred_element_type=F32)

    qt = tr(0, 1024) * (HEAD_DIM ** -0.5 * LOG2E)
    vst = tr(1024, 1280)
    vwt = tr(1280, 1536)
    gt = tr(1536, 1600)
    for c in range(tm // LANES):
        sl = slice(c * LANES, (c + 1) * LANES)
        qt_ref[c] = qt[:, sl].astype(BF16)
        vst_ref[c] = vst[:, sl].astype(BF16)
        vwt_ref[c] = vwt[:, sl].astype(BF16)
        gt_ref[c] = gt[:, sl]


def _inproj(x2, scale1, shift1, g_mix, w_std, w_tr, seq):
    t_tokens = x2.shape[0]
    tm = TM_PROJ
    tps = seq // tm
    nt = t_tokens // tm
    c = tm // LANES
    row = lambda i: (i, 0)
    per_b = lambda i: (i // tps, 0, 0)
    out_shape = (
        jax.ShapeDtypeStruct((KV_GROUPS, t_tokens, LANES), BF16),
        jax.ShapeDtypeStruct((t_tokens, 512), BF16),
        jax.ShapeDtypeStruct((t_tokens, 512), BF16),
        jax.ShapeDtypeStruct((t_tokens, 1024), BF16),
        jax.ShapeDtypeStruct((t_tokens, 1024), BF16),
        jax.ShapeDtypeStruct((t_tokens, 1024), BF16),
        jax.ShapeDtypeStruct((t_tokens, 1024), BF16),
        jax.ShapeDtypeStruct((t_tokens // LANES, 1024, LANES), BF16),
        jax.ShapeDtypeStruct((t_tokens // LANES, 256, LANES), BF16),
        jax.ShapeDtypeStruct((t_tokens // LANES, 256, LANES), BF16),
        jax.ShapeDtypeStruct((t_tokens // LANES, 64, LANES), F32),
    )
    out_specs = (
        pl.BlockSpec((KV_GROUPS, tm, LANES), lambda i: (0, i, 0)),
        pl.BlockSpec((tm, 512), row),
        pl.BlockSpec((tm, 512), row),
        pl.BlockSpec((tm, 1024), row),
        pl.BlockSpec((tm, 1024), row),
        pl.BlockSpec((tm, 1024), row),
        pl.BlockSpec((tm, 1024), row),
        pl.BlockSpec((c, 1024, LANES), lambda i: (i, 0, 0)),
        pl.BlockSpec((c, 256, LANES), lambda i: (i, 0, 0)),
        pl.BlockSpec((c, 256, LANES), lambda i: (i, 0, 0)),
        pl.BlockSpec((c, 64, LANES), lambda i: (i, 0, 0)),
    )
    return pl.pallas_call(
        functools.partial(_inproj_body, tiles_per_seq=tps),
        grid=(nt,),
        in_specs=[pl.BlockSpec((tm, D_MODEL), row),
                  pl.BlockSpec((1, 1, D_MODEL), per_b),
                  pl.BlockSpec((1, 1, D_MODEL), per_b),
                  _resident((1, D_MODEL), lambda i: (0, 0)),
                  _resident((D_MODEL, STD_COLS), lambda i: (0, 0)),
                  _resident((TR_ROWS, D_MODEL), lambda i: (0, 0))],
        out_specs=out_specs,
        out_shape=out_shape,
        compiler_params=_params(("parallel",), 56),
        name="inproj",
    )(x2, scale1, shift1, g_mix, w_std, w_tr)


def _compress_body(x_ref, wa_ref, wb_ref, pea_ref, peb_ref, w2k_ref, w2vt_ref, kcp_ref, vct_ref):
    x = x_ref[0, 0]
    n_rows = x.shape[0]
    p = jnp.dot(x, wa_ref[...], preferred_element_type=F32)
    q = jnp.dot(x, wb_ref[...], preferred_element_type=F32)
    pe = (jnp.dot(pea_ref[...], wa_ref[...], preferred_element_type=F32)
          + jnp.dot(peb_ref[...], wb_ref[...], preferred_element_type=F32))[0:1, :]
    pre = p + pltpu.roll(q, n_rows - 1, 0) + pe
    hid = _gelu(pre).astype(BF16)
    kc = jnp.dot(hid, w2k_ref[...], preferred_element_type=F32)
    n = lax.broadcasted_iota(jnp.int32, (n_rows, LANES), 0)
    lane = lax.broadcasted_iota(jnp.int32, (n_rows, LANES), 1)
    ce = n * CMP_STRIDE + (CMP_LEN - 1)
    ce_hi = ((ce >> 6) << 6).astype(F32)
    ce_lo = (ce & 63).astype(F32)
    aug = jnp.where((lane == 64) | (lane == 66), ce_hi,
                    jnp.where((lane == 65) | (lane == 67), ce_lo, 0.0))
    kcp_ref[0] = (kc + aug).astype(BF16)
    vct_ref[0] = lax.dot_general(w2vt_ref[...], hid, _NT, preferred_element_type=F32).astype(BF16)


def _compress(cv4, wa, wb, pea, peb, w2k, w2vt):
    g_, b_, n_rows, _ = cv4.shape
    const2 = lambda n: (0, 0)
    return pl.pallas_call(
        _compress_body,
        grid=(b_ * g_,),
        in_specs=[pl.BlockSpec((1, 1, n_rows, 2048), lambda n: (n % KV_GROUPS, n // KV_GROUPS, 0, 0)),
                  pl.BlockSpec((2048, 256), const2),
                  pl.BlockSpec((2048, 256), const2),
                  pl.BlockSpec((8, 2048), const2),
                  pl.BlockSpec((8, 2048), const2),
                  pl.BlockSpec((256, LANES), const2),
                  pl.BlockSpec((64, 256), const2)],
        out_specs=(pl.BlockSpec((1, n_rows, LANES), lambda n: (n, 0, 0)),
                   pl.BlockSpec((1, 64, n_rows), lambda n: (n, 0, 0))),
        out_shape=(jax.ShapeDtypeStruct((b_ * g_, n_rows, LANES), BF16),
                   jax.ShapeDtypeStruct((b_ * g_, 64, n_rows), BF16)),
        compiler_params=_params(("parallel",), 32),
        name="nsa_compress",
    )(cv4, wa, wb, pea, peb, w2k, w2vt)


def _softmax_step(state, s, pv_prev):
    m_i, l_i, acc = state
    m_new = jnp.maximum(m_i, jnp.max(s, axis=0, keepdims=True))
    alpha = jnp.exp2(m_i - m_new)
    p = jnp.exp2(s - m_new)
    l_new = alpha * l_i + jnp.sum(p, axis=0, keepdims=True)
    return (m_new, l_new, (acc + pv_prev) * alpha), p.astype(BF16)


def _nsa_body(qt_ref, gt_ref, kcp_ref, vct_ref, ks_ref, kw_ref, vst_ref, vwt_ref,
              qaug_ref, gout_ref, ovl_ref, grp_ref, o_ref, selb_ref, sa_ref, sb_ref, pa_ref, pb_ref, *, n_cmp):
    qi = pl.program_id(2)
    t0 = qi * TQ
    wq = GROUP * TQ

    qp = jnp.concatenate(
        [jnp.concatenate([qt_ref[0, r * HEAD_DIM:(r + 1) * HEAD_DIM, :], qaug_ref[r]], axis=0)
         for r in range(GROUP)], axis=1)

    lane_q = lax.broadcasted_iota(jnp.int32, (1, wq), 1) & (TQ - 1)
    n_win = WINDOW // TQ + 1
    k_lo = pl.multiple_of(jnp.maximum(t0 - WINDOW, 0), TQ)

    s = jnp.dot(kcp_ref[0], qp, preferred_element_type=F32)
    s_w = jnp.dot(kw_ref[pl.ds(k_lo, n_win * TQ), :], qp, preferred_element_type=F32)
    n_io = lax.broadcasted_iota(jnp.int32, (n_cmp, wq), 0)
    tl = lax.broadcasted_iota(jnp.int32, (n_cmp, wq), 1) & (TQ - 1)
    valid = (n_io * CMP_STRIDE + (CMP_LEN - 1)) <= (t0 + tl)
    s = jnp.where(valid, s, NEG)
    m = jnp.maximum(jnp.max(s, axis=0, keepdims=True), 0.5 * NEG)
    p = jnp.exp2(s - m)
    l = jnp.sum(p, axis=0, keepdims=True)
    pn = p * (1.0 / jnp.maximum(l, 1e-30))
    o_c = jnp.dot(vct_ref[0], pn.astype(BF16), preferred_element_type=F32)

    ps = pn[:, 0:TQ] + pn[:, TQ:2 * TQ] + pn[:, 2 * TQ:3 * TQ] + pn[:, 3 * TQ:4 * TQ]
    hi = ps.astype(BF16)
    r1 = ps - hi.astype(F32)
    mid = r1.astype(BF16)
    lo = (r1 - mid.astype(F32)).astype(BF16)
    ovl = ovl_ref[...]
    imp = (jnp.dot(ovl, hi, preferred_element_type=F32)
           + jnp.dot(ovl, mid, preferred_element_type=F32)
           + jnp.dot(ovl, lo, preferred_element_type=F32))

    n_slc = imp.shape[0]
    m_io = lax.broadcasted_iota(jnp.int32, (n_slc, TQ), 0)
    q_io = lax.broadcasted_iota(jnp.int32, (n_slc, TQ), 1)
    back = ((t0 + q_io) >> 6) - m_io
    valid_s = back >= 0
    forced = valid_s & ((m_io == 0) | (back < 2))
    w = jnp.where(forced, -jnp.inf, jnp.where(valid_s, imp, -1.0))
    selb = jnp.where(forced, 0.0, NEG)

    def pick(carry, lanes=None):
        w, selb = carry
        mx = jnp.max(w, axis=0, keepdims=True)
        idx = jnp.min(jnp.where(w == mx, m_io, n_slc), axis=0, keepdims=True)
        hit = m_io == idx
        if lanes is not None:
            hit = hit & lanes
        return jnp.where(hit, -jnp.inf, w), jnp.where(hit, 0.0, selb)

    carry = (w, selb)
    for _ in range(SLC_TOPK - 3):
        carry = pick(carry)
    carry = pick(carry, lanes=(t0 + q_io) < 2 * SLC_BLOCK)
    _, selb = pick(carry, lanes=(t0 + q_io) < SLC_BLOCK)
    for mblk in range(n_slc):
        selb_ref[mblk] = jnp.broadcast_to(selb[mblk:mblk + 1, :], (8, TQ))

    per_blk = TK_SEL // SLC_BLOCK

    def scores(j):
        k0 = pl.multiple_of(j * TK_SEL, TK_SEL)
        bias = jnp.concatenate(
            [jnp.tile(selb_ref[j * per_blk + u], (SLC_BLOCK // 8, 1)) for u in range(per_blk)], axis=0)
        bias = jnp.concatenate([bias] * GROUP, axis=1)
        return jnp.dot(ks_ref[pl.ds(k0, TK_SEL), :], qp, preferred_element_type=F32) + bias

    def vt_tile(j):
        return jnp.concatenate([vst_ref[2 * j], vst_ref[2 * j + 1]], axis=1)

    def causal(j):
        kr = lax.broadcasted_iota(jnp.int32, (TK_SEL, wq), 0)
        return kr <= (t0 - j * TK_SEL) + lane_q

    def pv_dot(j, p_ref):
        return jnp.dot(vt_tile(j), p_ref[...], preferred_element_type=F32)

    jd = qi // (TK_SEL // TQ)

    sa_ref[...] = scores(0)
    pb_ref[...] = jnp.zeros_like(pb_ref)
    sel01 = jnp.where(selb == 0.0, 1.0, 0.0).astype(BF16)
    tile_any = jnp.max(jnp.dot(grp_ref[...], sel01, preferred_element_type=F32), axis=1, keepdims=True)
    j_io = lax.broadcasted_iota(jnp.int32, tile_any.shape, 0)
    q_lo_v = jnp.max(jnp.where((tile_any == 0.0) & (j_io <= jd), j_io, -1), axis=0, keepdims=True) + 1
    n_pre_v = jnp.max(jnp.where((tile_any > 0.0) & (j_io < q_lo_v), j_io, -1), axis=0, keepdims=True) + 1

    d = (t0 - k_lo) + lane_q - lax.broadcasted_iota(jnp.int32, (n_win * TQ, wq), 0)
    s_w = jnp.where(lax.bitcast_convert_type(d, jnp.uint32) < WINDOW, s_w, NEG)
    p = jnp.exp2(s_w - jnp.max(s_w, axis=0, keepdims=True))
    l_w = jnp.sum(p, axis=0, keepdims=True)
    vt_w = jnp.concatenate([vwt_ref[k_lo // TQ + u] for u in range(n_win)], axis=1)
    o_w = jnp.dot(vt_w, p.astype(BF16), preferred_element_type=F32) * (1.0 / l_w)

    q_lo = q_lo_v[0, 0]
    n_pre = n_pre_v[0, 0]
    n_vis = n_pre + jd - q_lo + 1

    def tile_at(pos):
        return jnp.where(pos < n_pre, pos, pos - n_pre + q_lo)

    state = (jnp.full((1, wq), NEG, F32), jnp.zeros((1, wq), F32), jnp.zeros((HEAD_DIM, wq), F32))

    def pair(u, state):
        a = 2 * u
        sb_ref[...] = scores(tile_at(a + 1))
        state, p = _softmax_step(state, sa_ref[...], pv_dot(tile_at(jnp.maximum(a - 1, 0)), pb_ref))
        pa_ref[...] = p
        sa_ref[...] = scores(tile_at(a + 2))
        state, p = _softmax_step(state, sb_ref[...], pv_dot(tile_at(a), pa_ref))
        pb_ref[...] = p
        return state

    n_pair = (n_vis - 1) // 2
    state = lax.fori_loop(0, n_pair, pair, state)
    x = 2 * n_pair
    y = jnp.minimum(x + 1, n_vis - 1)
    tx, ty = tile_at(x), tile_at(y)
    sb_ref[...] = scores(ty)
    state, p = _softmax_step(state, jnp.where(causal(tx), sa_ref[...], NEG),
                             pv_dot(tile_at(jnp.maximum(x - 1, 0)), pb_ref))
    pa_ref[...] = p
    ty_mask = jnp.where(x + 1 < n_vis, ty, jd + 1)
    (_, l_s, acc_s), p = _softmax_step(state, jnp.where(causal(ty_mask), sb_ref[...], NEG), pv_dot(tx, pa_ref))
    acc_s = acc_s + jnp.dot(vt_tile(ty), p, preferred_element_type=F32)

    o_s = acc_s * (1.0 / l_s)
    gw = jax.nn.sigmoid(gt_ref[0])
    outs = []
    for r in range(GROUP):
        sl = slice(r * TQ, (r + 1) * TQ)
        o = (gw[r:r + 1, :] * o_c[:, sl] + gw[GROUP + r:GROUP + r + 1, :] * o_s[:, sl]
             + gw[2 * GROUP + r:2 * GROUP + r + 1, :] * o_w[:, sl])
        ms = jnp.mean(o * o, axis=0, keepdims=True)
        outs.append(o * lax.rsqrt(ms + EPS) * gout_ref[r])
    o_ref[...] = jnp.concatenate(outs, axis=0).T.astype(BF16)


def _nsa(qt, gt, kcp, vct, ks, kw, vst, vwt, qaug, gout_b, ovl_t, grp, batch, seq):
    nq = seq // TQ
    n_cmp = kcp.shape[1]
    n_slc = seq // SLC_BLOCK
    t_tokens = batch * seq
    per_b_chunks = seq // LANES
    return pl.pallas_call(
        functools.partial(_nsa_body, n_cmp=n_cmp),
        grid=(batch, KV_GROUPS, nq),
        in_specs=[
            pl.BlockSpec((1, 256, LANES), lambda b, g, q: (b * nq + q, g, 0)),
            pl.BlockSpec((1, 16, LANES), lambda b, g, q: (b * nq + q, g, 0)),
            pl.BlockSpec((1, n_cmp, LANES), lambda b, g, q: (b * KV_GROUPS + g, 0, 0)),
            pl.BlockSpec((1, 64, n_cmp), lambda b, g, q: (b * KV_GROUPS + g, 0, 0)),
            pl.BlockSpec((seq, LANES), lambda b, g, q: (b, g)),
            pl.BlockSpec((seq, LANES), lambda b, g, q: (b, g)),
            pl.BlockSpec((per_b_chunks, 64, LANES), lambda b, g, q: (b, g, 0)),
            pl.BlockSpec((per_b_chunks, 64, LANES), lambda b, g, q: (b, g, 0)),
            pl.BlockSpec((GROUP, 64, LANES), lambda b, g, q: (g, 0, 0)),
            pl.BlockSpec((GROUP, 64, LANES), lambda b, g, q: (g, 0, 0)),
            pl.BlockSpec((n_slc, n_cmp), lambda b, g, q: (0, 0)),
            pl.BlockSpec(grp.shape, lambda b, g, q: (0, 0)),
        ],
        out_specs=pl.BlockSpec((TQ, 256), lambda b, g, q: (b * nq + q, g)),
        out_shape=jax.ShapeDtypeStruct((t_tokens, 1024), BF16),
        scratch_shapes=[pltpu.VMEM((n_slc, 8, TQ), F32),
                        pltpu.VMEM((TK_SEL, GROUP * TQ), F32), pltpu.VMEM((TK_SEL, GROUP * TQ), F32),
                        pltpu.VMEM((TK_SEL, GROUP * TQ), BF16), pltpu.VMEM((TK_SEL, GROUP * TQ), BF16)],
        compiler_params=_params(("parallel", "parallel", "arbitrary"), 40),
        name="nsa_attention",
    )(qt, gt, kcp, vct, ks, kw, vst, vwt, qaug, gout_b, ovl_t, grp)


def _ret_body(q_ref, k_ref, v_ref, g_ref, dm_ref, kd_ref, qd_ref, cd_ref, go_ref, o_ref, st_ref):
    @pl.when(pl.program_id(2) == 0)
    def _():
        st_ref[...] = jnp.zeros_like(st_ref)

    c_ = RET_CHUNK
    slices = [slice(c * c_, (c + 1) * c_) for c in range(RET_TILE // c_)]
    atts, kvs = [], []
    for sl in slices:
        k = k_ref[sl, :]
        atts.append((lax.dot_general(q_ref[sl, :], k, _NT, preferred_element_type=F32) * dm_ref[0]).astype(BF16))
        kdec = (k.astype(F32) * kd_ref[0]).astype(BF16)
        kvs.append(lax.dot_general(kdec, v_ref[sl, :], _TN, preferred_element_type=F32))
    states = [st_ref[...]]
    for kv in kvs:
        states.append(states[-1] * cd_ref[0] + kv)
    st_ref[...] = states[-1]
    for sl, att, state in zip(slices, atts, states):
        o = (jnp.dot(att, v_ref[sl, :], preferred_element_type=F32)
             + qd_ref[0] * jnp.dot(q_ref[sl, :], state.astype(BF16), preferred_element_type=F32))
        mu = jnp.mean(o, axis=-1, keepdims=True)
        oc = o - mu
        var = jnp.mean(oc * oc, axis=-1, keepdims=True)
        y = oc * lax.rsqrt(var + EPS) * go_ref[0, 0:1, :]
        gate = g_ref[sl, :].astype(F32)
        o_ref[sl, :] = (gate * jax.nn.sigmoid(gate) * y).astype(BF16)


def _retention(q_r, k_r, v_r, g_r, dm, kd, qd, cd, go, batch, seq):
    t_tokens = batch * seq
    nc = seq // RET_TILE
    tok = lambda b, h, c: (b * nc + c, h)
    per_h = lambda b, h, c: (h, 0, 0)
    sq = (1, RET_DIM, RET_DIM)
    return pl.pallas_call(
        _ret_body,
        grid=(batch, RET_HEADS, nc),
        in_specs=[pl.BlockSpec((RET_TILE, RET_DIM), tok)] * 4
        + [pl.BlockSpec(sq, per_h)] * 4 + [pl.BlockSpec((1, 8, RET_DIM), per_h)],
        out_specs=pl.BlockSpec((RET_TILE, RET_DIM), tok),
        out_shape=jax.ShapeDtypeStruct((t_tokens, RET_HEADS * RET_DIM), BF16),
        scratch_shapes=[pltpu.VMEM((RET_DIM, RET_DIM), F32)],
        compiler_params=_params(("parallel", "parallel", "arbitrary"), 32),
        name="retention",
    )(q_r, k_r, v_r, g_r, dm, kd, qd, cd, go)


def _pow2_scale(magnitude):
    return jnp.exp2(jnp.floor(jnp.log2(FP8_TARGET / jnp.maximum(magnitude, 1e-30))))


def _mid_body(on_ref, or_ref, x_ref, g1_ref, sc_ref, sh_ref, gn_ref, wo_ref, wqt_ref, sk_ref, ps_ref,
              x1_ref, h2t_ref, st_ref, scl_ref):
    acc = (jnp.dot(on_ref[...], wo_ref[0:1024, :], preferred_element_type=F32)
           + jnp.dot(or_ref[...], wo_ref[1024:2048, :], preferred_element_type=F32))
    x1 = x_ref[...] + g1_ref[0] * acc
    x1_ref[...] = x1
    ms = jnp.mean(x1 * x1, axis=-1, keepdims=True)
    h2 = x1 * lax.rsqrt(ms + EPS) * gn_ref[...]
    h2 = h2 * (1.0 + sc_ref[0]) + sh_ref[0]

    h2_t = h2.T
    amax = jnp.max(jnp.max(jnp.abs(h2_t), axis=0, keepdims=True), axis=1, keepdims=True)
    s_h = _pow2_scale(amax)
    h2t_ref[...] = (h2_t * s_h).astype(FP8)
    norm = jnp.sqrt(jnp.sum(h2_t * h2_t, axis=0, keepdims=True))
    s_c = _pow2_scale(COEF_BOUND_FACTOR * ps_ref[2:3, 0:1] * norm)
    scl_ref[...] = jnp.concatenate(
        [jnp.broadcast_to(ps_ref[0:1, 0:1] / s_h, s_c.shape), s_c, ps_ref[1:2, 0:1] / s_c,
         jnp.zeros((5, s_c.shape[1]), F32)], axis=0)
    h2 = h2.astype(BF16)
    qt = lax.dot_general(wqt_ref[...], h2, _NT, preferred_element_type=F32).astype(BF16)
    for hp in range(2 * PEER_HEADS):
        st_ref[hp] = jnp.dot(sk_ref[hp], qt[hp * 128:(hp + 1) * 128, :], preferred_element_type=F32)


def _mid(o_nsa, o_ret, x2, gate1, scale2, shift2, g_ffn, w_out, wq_t, sub_keys, peer_scales, seq):
    t_tokens = x2.shape[0]
    tm = TM_PROJ
    tps = seq // tm
    row = lambda i: (i, 0)
    per_b = lambda i: (i // tps, 0, 0)
    return pl.pallas_call(
        _mid_body,
        grid=(t_tokens // tm,),
        in_specs=[pl.BlockSpec((tm, 1024), row),
                  pl.BlockSpec((tm, 1024), row),
                  pl.BlockSpec((tm, D_MODEL), row),
                  pl.BlockSpec((1, 1, D_MODEL), per_b),
                  pl.BlockSpec((1, 1, D_MODEL), per_b),
                  pl.BlockSpec((1, 1, D_MODEL), per_b),
                  _resident((1, D_MODEL), lambda i: (0, 0)),
                  _resident((D_MODEL, D_MODEL), lambda i: (0, 0)),
                  _resident((D_MODEL, D_MODEL), lambda i: (0, 0)),
                  _resident((2 * PEER_HEADS, PEER_KEYS, 128), lambda i: (0, 0, 0)),
                  _resident((8, LANES), lambda i: (0, 0))],
        out_specs=(pl.BlockSpec((tm, D_MODEL), row),
                   pl.BlockSpec((D_MODEL, tm), lambda i: (0, i)),
                   pl.BlockSpec((2 * PEER_HEADS, PEER_KEYS, tm), lambda i: (0, 0, i)),
                   pl.BlockSpec((8, tm), lambda i: (0, i))),
        out_shape=(jax.ShapeDtypeStruct((t_tokens, D_MODEL), F32),
                   jax.ShapeDtypeStruct((D_MODEL, t_tokens), FP8),
                   jax.ShapeDtypeStruct((2 * PEER_HEADS, PEER_KEYS, t_tokens), F32),
                   jax.ShapeDtypeStruct((8, t_tokens), F32)),
        compiler_params=_params(("parallel",), 48),
        name="outproj_peerq",
    )(o_nsa, o_ret, x2, gate1, scale2, shift2, g_ffn, w_out, wq_t, sub_keys, peer_scales)


def _top16(s, break_ties):
    n_rows, n = s.shape
    io = lax.broadcasted_iota(jnp.int32, (n_rows, n), 0)
    a_io = lax.broadcasted_iota(jnp.int32, (PEER_TOPK, n), 0)
    rank = jnp.full((n_rows, n), PEER_TOPK, jnp.int32)
    vals = jnp.zeros((PEER_TOPK, n), F32)
    for a in range(PEER_TOPK):
        mx = jnp.max(s, axis=0, keepdims=True)
        hit = s == mx
        if break_ties:
            hit = io == jnp.min(jnp.where(hit, io, n_rows), axis=0, keepdims=True)
        rank = jnp.where(hit, a, rank)
        s = jnp.where(hit, -jnp.inf, s)
        vals = jnp.where(a_io == a, mx, vals)
    return vals, rank


def _peer_select_body(s_ref, scl_ref, l_ref, w1_ref, r2_ref, w2_ref):
    s1 = s_ref[0]
    s2 = s_ref[1]
    n = s1.shape[1]

    def select(break_ties):
        v1, rank1 = _top16(s1, break_ties)
        v2, rank2 = _top16(s2, break_ties)
        a_io = lax.broadcasted_iota(jnp.int32, (PEER_TOPK, n), 0)
        cnt = jnp.zeros((PEER_TOPK, n), jnp.int32)
        cur = v1 + v2[0:1, :]
        top = v1[0:1, :] + v2[0:1, :]
        z = jnp.zeros((1, n), F32)
        for _ in range(PEER_TOPK):
            mx = jnp.max(cur, axis=0, keepdims=True)
            aidx = jnp.min(jnp.where(cur == mx, a_io, PEER_TOPK), axis=0, keepdims=True)
            hit = a_io == aidx
            cnt = cnt + hit.astype(jnp.int32)
            nxt = jnp.sum(jnp.where(hit, cnt, 0), axis=0, keepdims=True)
            nv = jnp.max(jnp.where(a_io == nxt, v2, -jnp.inf), axis=0, keepdims=True)
            cur = jnp.where(hit, v1 + nv, cur)
            z = z + jnp.exp(mx - top)
        cnt_b = cnt.astype(F32).astype(BF16)
        rank_b = rank1.astype(F32).astype(BF16)
        lrow = jnp.zeros(s1.shape, BF16)
        for a in range(PEER_TOPK):
            lrow = jnp.where(rank_b == a, jnp.broadcast_to(cnt_b[a:a + 1, :], s1.shape), lrow)
        l_ref[0] = lrow.astype(F32)
        w1_ref[0] = jnp.exp(s1 - v1[0:1, :])
        r2_ref[0] = rank2.astype(F32).astype(BF16)
        w2_ref[0] = (jnp.exp(s2 - v2[0:1, :]) * (scl_ref[1:2, :] / z)).astype(BF16)
        return rank1, rank2

    rank1, rank2 = select(False)
    marked = (jnp.sum((rank1 < PEER_TOPK).astype(jnp.int32), axis=0, keepdims=True)
              + jnp.sum((rank2 < PEER_TOPK).astype(jnp.int32), axis=0, keepdims=True))

    @pl.when(jnp.max(marked) != 2 * PEER_TOPK)
    def _():
        select(True)


def _peer_select(st, scl):
    t_tokens = st.shape[2]
    tm = TM_SEL
    shp = jax.ShapeDtypeStruct((PEER_HEADS, PEER_KEYS, t_tokens), F32)
    shp_b = jax.ShapeDtypeStruct((PEER_HEADS, PEER_KEYS, t_tokens), BF16)
    spec = pl.BlockSpec((1, PEER_KEYS, tm), lambda i, h: (h, 0, i))
    return pl.pallas_call(
        _peer_select_body,
        grid=(t_tokens // tm, PEER_HEADS),
        in_specs=[pl.BlockSpec((2, PEER_KEYS, tm), lambda i, h: (h, 0, i)),
                  pl.BlockSpec((8, tm), lambda i, h: (0, i))],
        out_specs=(spec, spec, spec, spec),
        out_shape=(shp, shp, shp_b, shp_b),
        compiler_params=_params(("parallel", "parallel"), 32),
        name="peer_select",
    )(st, scl)


def _transpose_body(v_ref, s_ref, o_ref):
    o_ref[...] = (v_ref[...].T * s_ref[0:1, 0:1]).astype(FP8)


def _transpose_fp8(v, scale_tile):
    n, d = v.shape
    tn = 512
    return pl.pallas_call(
        _transpose_body,
        grid=(n // tn,),
        in_specs=[pl.BlockSpec((tn, d), lambda i: (i, 0)),
                  pl.BlockSpec((8, LANES), lambda i: (0, 0))],
        out_specs=pl.BlockSpec((d, tn), lambda i: (0, i)),
        out_shape=jax.ShapeDtypeStruct((d, n), FP8),
        compiler_params=_params(("parallel",), 32),
        name="transpose_v",
    )(v, scale_tile)


def _peer_expert_body(h2t_ref, u_ref, vt_ref, l_ref, w1_ref, r2_ref, w2_ref, scl_ref,
                      o_ref, ce_ref, co_ref, *, steps_per_tile):
    g = pl.program_id(0)
    sw = 256
    strips = [slice(c * sw, (c + 1) * sw) for c in range(TM_PEER // sw)]
    n_piece = TE_PEER // PEER_KEYS
    blk = D_MODEL // n_piece

    @pl.when(g == 0)
    def _():
        co_ref[...] = jnp.zeros_like(co_ref)

    @pl.when((g == 0) | ((g - 1) % steps_per_tile == 0))
    def _():
        o_ref[...] = jnp.zeros_like(o_ref)

    def run(c_new, c_old):
        def piece(j, carry):
            r0 = pl.multiple_of(j * PEER_KEYS, PEER_KEYS)
            d0 = pl.multiple_of(j * blk, blk)
            for ls in strips:
                coef = None
                for h in range(PEER_HEADS):
                    lrow = jnp.broadcast_to(l_ref[h, pl.ds(j, 1), ls], (16, sw)).astype(BF16)
                    w1row = jnp.broadcast_to(w1_ref[h, pl.ds(j, 1), ls], (16, sw)).astype(BF16)
                    lrow = jnp.tile(lrow, (PEER_KEYS // 16, 1))
                    w1row = jnp.tile(w1row, (PEER_KEYS // 16, 1))
                    term = jnp.where(r2_ref[h, :, ls] < lrow, w2_ref[h, :, ls] * w1row, jnp.zeros((), BF16))
                    coef = term if coef is None else coef + term
                a_t = jnp.dot(u_ref[pl.ds(r0, PEER_KEYS), :], h2t_ref[:, ls], preferred_element_type=F32)
                a_scale = jnp.broadcast_to(scl_ref[0:1, ls], (16, sw)).astype(BF16)
                act = _gelu(a_t.astype(BF16) * jnp.tile(a_scale, (PEER_KEYS // 16, 1)))
                c_new[pl.ds(r0, PEER_KEYS), ls] = (coef * act).astype(FP8)
                o_ref[pl.ds(d0, blk), ls] += jnp.dot(vt_ref[pl.ds(d0, blk), :], c_old[:, ls],
                                                     preferred_element_type=F32)
            return carry

        lax.fori_loop(0, n_piece, piece, 0, unroll=4)

    @pl.when(g % 2 == 0)
    def _():
        run(ce_ref, co_ref)

    @pl.when(g % 2 == 1)
    def _():
        run(co_ref, ce_ref)


def _peer_expert(h2t, u_b, v_t, lrow, w1, r2, w2, scl):
    t_tokens = h2t.shape[1]
    tm, te = TM_PEER, TE_PEER
    n_piece = te // PEER_KEYS
    ne = PEER_EXPERTS // te
    n_steps = (t_tokens // tm) * ne
    cur = lambda g: jnp.minimum(g, n_steps - 1)
    prev = lambda g: jnp.maximum(g - 1, 0)
    row_spec = pl.BlockSpec((PEER_HEADS, n_piece, tm), lambda g: (0, cur(g) % ne, cur(g) // ne))
    full_spec = pl.BlockSpec((PEER_HEADS, PEER_KEYS, tm), lambda g: (0, 0, cur(g) // ne))
    return pl.pallas_call(
        functools.partial(_peer_expert_body, steps_per_tile=ne),
        grid=(n_steps + 1,),
        in_specs=[pl.BlockSpec((D_MODEL, tm), lambda g: (0, cur(g) // ne)),
                  pl.BlockSpec((te, D_MODEL), lambda g: (cur(g) % ne, 0)),
                  pl.BlockSpec((D_MODEL, te), lambda g: (0, prev(g) % ne)),
                  row_spec, row_spec, full_spec, full_spec,
                  pl.BlockSpec((8, tm), lambda g: (0, cur(g) // ne))],
        out_specs=pl.BlockSpec((D_MODEL, tm), lambda g: (0, prev(g) // ne)),
        out_shape=jax.ShapeDtypeStruct((D_MODEL, t_tokens), F32),
        scratch_shapes=[pltpu.VMEM((te, tm), FP8), pltpu.VMEM((te, tm), FP8)],
        compiler_params=_params(("arbitrary",), 52),
        name="peer_experts",
    )(h2t, u_b, v_t, lrow, w1, r2, w2, scl)


def _final_body(x1_ref, pt_ref, scl_ref, g2_ref, gn_ref, o_ref, *, apply_norm):
    peer = (pt_ref[...] * scl_ref[2:3, :]).T
    y = x1_ref[...] + g2_ref[0] * peer
    if apply_norm:
        ms = jnp.mean(y * y, axis=-1, keepdims=True)
        y = y * lax.rsqrt(ms + EPS) * gn_ref[...]
    o_ref[...] = y


def _final(x1, peer_t, scl, gate2, g_final, seq, apply_norm):
    t_tokens = x1.shape[0]
    tm = TM_PROJ
    tps = seq // tm
    return pl.pallas_call(
        functools.partial(_final_body, apply_norm=apply_norm),
        grid=(t_tokens // tm,),
        in_specs=[pl.BlockSpec((tm, D_MODEL), lambda i: (i, 0)),
                  pl.BlockSpec((D_MODEL, tm), lambda i: (0, i)),
                  pl.BlockSpec((8, tm), lambda i: (0, i)),
                  pl.BlockSpec((1, 1, D_MODEL), lambda i: (i // tps, 0, 0)),
                  pl.BlockSpec((1, D_MODEL), lambda i: (0, 0))],
        out_specs=pl.BlockSpec((tm, D_MODEL), lambda i: (i, 0)),
        out_shape=jax.ShapeDtypeStruct((t_tokens, D_MODEL), F32),
        compiler_params=_params(("parallel",), 32),
        name="final_norm",
    )(x1, peer_t, scl, gate2, g_final)


def _split_cols(a, sizes):
    out, acc = [], 0
    for s in sizes:
        out.append(a[:, acc:acc + s])
        acc += s
    return out


def _inproj_weights(w_in):
    kvw = KV_GROUPS * HEAD_DIM
    sizes = (1024,) + (kvw,) * 6 + (3 * N_HEADS, 1024, 1024, 1024, 1024)
    q_a, k_c, v_c, k_s, v_s, k_w, v_w, g_a, q_r, k_r, v_r, g_r = _split_cols(w_in, sizes)
    d = w_in.shape[0]

    def grp(a, g):
        return a[:, g * HEAD_DIM:(g + 1) * HEAD_DIM]

    zeros = jnp.zeros((d, HEAD_DIM), w_in.dtype)
    cv = [jnp.concatenate([grp(k_c, g), grp(v_c, g)], axis=1) for g in range(KV_GROUPS)]
    ksp = [jnp.concatenate([grp(k_s, g), zeros], axis=1) for g in range(KV_GROUPS)]
    kwp = [jnp.concatenate([grp(k_w, g), zeros], axis=1) for g in range(KV_GROUPS)]
    w_std = jnp.concatenate(cv + ksp + kwp + [q_r, k_r, v_r, g_r], axis=1).astype(BF16)
    gcols = []
    for g in range(KV_GROUPS):
        for br in range(3):
            for r in range(GROUP):
                c = (g * GROUP + r) * 3 + br
                gcols.append(g_a[:, c:c + 1])
        gcols.append(jnp.zeros((d, 4), w_in.dtype))
    w_tr = jnp.concatenate([q_a, v_s, v_w] + gcols, axis=1).T.astype(BF16)
    return w_std, w_tr


def _compress_weights(pe_k, pe_v, k_w1, k_w2, v_w1, v_w2):
    half = CMP_LEN // 2

    def w1_half(w1k, w1v, lo):
        a = w1k.reshape(CMP_LEN, HEAD_DIM, CMP_HIDDEN)[lo:lo + half]
        b = w1v.reshape(CMP_LEN, HEAD_DIM, CMP_HIDDEN)[lo:lo + half]
        za = jnp.zeros_like(a)
        top = jnp.concatenate([a, za], axis=2)
        bot = jnp.concatenate([za, b], axis=2)
        return jnp.concatenate([top, bot], axis=1).reshape(half * 2 * HEAD_DIM, 2 * CMP_HIDDEN).astype(BF16)

    def pe_half(lo):
        row = jnp.concatenate([pe_k[lo:lo + half], pe_v[lo:lo + half]], axis=1).reshape(1, -1)
        return jnp.broadcast_to(row, (8, row.shape[1])).astype(BF16)

    wa = w1_half(k_w1, v_w1, 0)
    wb = w1_half(k_w1, v_w1, half)
    w2k = jnp.zeros((2 * CMP_HIDDEN, LANES), F32).at[:CMP_HIDDEN, :HEAD_DIM].set(k_w2).astype(BF16)
    w2vt = jnp.zeros((HEAD_DIM, 2 * CMP_HIDDEN), F32).at[:, CMP_HIDDEN:].set(v_w2.T).astype(BF16)
    return wa, wb, pe_half(0), pe_half(half), w2k, w2vt


def _nsa_constants(seq):
    slopes = jnp.exp2(-8.0 * (jnp.arange(N_HEADS, dtype=F32) + 1.0) / N_HEADS) * LOG2E
    s_hi = slopes.astype(BF16)
    s_lo = (slopes - s_hi.astype(F32)).astype(BF16)
    rows = jnp.zeros((N_HEADS, HEAD_DIM), BF16)
    rows = rows.at[:, 0].set(s_hi).at[:, 1].set(s_hi).at[:, 2].set(s_lo).at[:, 3].set(s_lo)
    qaug = jnp.broadcast_to(rows[:, :, None], (N_HEADS, HEAD_DIM, LANES))
    n_rows = seq // CMP_STRIDE
    n_slc = seq // SLC_BLOCK
    start = np.arange(n_rows)[:, None] * CMP_STRIDE
    end = start + CMP_LEN - 1
    blk = np.arange(n_slc)[None, :] * SLC_BLOCK
    ovl = ((start < blk + SLC_BLOCK) & (end >= blk)).astype(np.float32)
    per_tile = TK_SEL // SLC_BLOCK
    grp = (np.arange(n_slc)[None, :] // per_tile == np.arange(n_slc // per_tile)[:, None])
    return qaug, jnp.asarray(ovl.T, BF16), jnp.asarray(grp.astype(np.float32), BF16)


def _retention_constants():
    h, c = RET_HEADS, RET_CHUNK
    lg = jnp.log1p(-jnp.exp2(-5.0 - jnp.arange(h, dtype=F32)))
    pos = jnp.arange(c, dtype=F32)
    diff = pos[:, None] - pos[None, :]
    scale = RET_DIM ** -0.5
    dm = jnp.where(diff >= 0, jnp.exp(lg[:, None, None] * jnp.maximum(diff, 0.0)), 0.0) * scale
    k_decay = jnp.exp(lg[:, None] * (c - 1.0 - pos)) * scale
    q_decay = jnp.exp(lg[:, None] * (pos + 1.0))
    chunk_decay = jnp.exp(lg * c)
    kd = jnp.broadcast_to(k_decay[:, :, None], (h, c, RET_DIM))
    qd = jnp.broadcast_to(q_decay[:, :, None], (h, c, RET_DIM))
    cd = jnp.broadcast_to(chunk_decay[:, None, None], (h, RET_DIM, RET_DIM))
    return dm, kd, qd, cd


def kernel(x, c, w_ada, b_ada, g_norm_mix, g_norm_ffn, g_norm_final, w_in, cmp_pe_k, cmp_pe_v,
           cmp_k_w1, cmp_k_w2, cmp_v_w1, cmp_v_w2, g_nsa_out, g_ret_out, w_out,
           peer_w_q, peer_sub_keys, peer_u, peer_v):
    batch, seq, d = x.shape
    depth = w_ada.shape[0]
    t_tokens = batch * seq
    xf = x.reshape(t_tokens, d)
    c_pad = jnp.zeros((8, d), F32).at[:batch].set(c)
    qaug, ovl_t, grp = _nsa_constants(seq)
    dm, kd, qd, cd = _retention_constants()

    for l in range(depth):
        mod = _adaln(c_pad, w_ada[l], b_ada[l][None, :])[:batch].reshape(batch, 6, 1, d)
        shift1, scale1, gate1, shift2, scale2, gate2 = (mod[:, k] for k in range(6))

        w_std, w_tr = _inproj_weights(w_in[l])
        (cv, ks, kw, q_r, k_r, v_r, g_r, qt, vst, vwt, gt) = _inproj(
            xf, scale1, shift1, g_norm_mix[l][None, :], w_std, w_tr, seq)

        cv4 = cv.reshape(KV_GROUPS, batch, seq // CMP_STRIDE, CMP_STRIDE * LANES)
        kcp, vct = _compress(cv4, *_compress_weights(cmp_pe_k[l], cmp_pe_v[l], cmp_k_w1[l], cmp_k_w2[l],
                                                     cmp_v_w1[l], cmp_v_w2[l]))
        gout_b = jnp.broadcast_to(g_nsa_out[l][:, :, None], (N_HEADS, HEAD_DIM, LANES))
        o_nsa = _nsa(qt, gt, kcp, vct, ks, kw, vst, vwt, qaug, gout_b, ovl_t, grp, batch, seq)

        go = jnp.broadcast_to(g_ret_out[l][:, None, :], (RET_HEADS, 8, RET_DIM))
        o_ret = _retention(q_r, k_r, v_r, g_r, dm, kd, qd, cd, go, batch, seq)

        sub_keys = peer_sub_keys[l].reshape(2 * PEER_HEADS, PEER_KEYS, -1).astype(BF16)
        s_u = _pow2_scale(jnp.max(jnp.abs(peer_u[l])))
        s_v = _pow2_scale(jnp.max(jnp.abs(peer_v[l])))
        u_norm = jnp.sqrt(jnp.max(jnp.sum(jnp.square(peer_u[l]), axis=1)))
        peer_scales = jnp.broadcast_to(
            jnp.stack([1.0 / s_u, 1.0 / s_v, u_norm] + [jnp.zeros((), F32)] * 5)[:, None], (8, LANES))
        x1, h2t, st, scl = _mid(o_nsa, o_ret, xf, gate1, scale2, shift2, g_norm_ffn[l][None, :],
                                w_out[l].astype(BF16), peer_w_q[l].T.astype(BF16), sub_keys, peer_scales, seq)

        lrow, w1, r2, w2 = _peer_select(st, scl)
        peer_t = _peer_expert(h2t, (peer_u[l] * s_u).astype(FP8),
                              _transpose_fp8(peer_v[l], jnp.full((8, LANES), s_v, F32)),
                              lrow, w1, r2, w2, scl)
        xf = _final(x1, peer_t, scl, gate2, g_norm_final[None, :], seq, apply_norm=(l == depth - 1))
    return xf.reshape(batch, seq, d)
```

```python
import functools
import math

import numpy as np
import jax
import jax.numpy as jnp
from jax import lax
from jax.experimental import pallas as pl
from jax.experimental.pallas import tpu as pltpu

F32 = jnp.float32
BF16 = jnp.bfloat16
FP8 = jnp.float8_e4m3fn
FP8_TARGET = 448.0

D_MODEL = 2048
N_HEADS = 16
HEAD_DIM = 64
KV_GROUPS = 4
GROUP = 4
CMP_LEN = 32
CMP_STRIDE = 16
CMP_HIDDEN = 128
SLC_BLOCK = 64
SLC_TOPK = 16
WINDOW = 512
FORCE_SCORE = 1e4
NEG = -1e30
RET_HEADS = 8
RET_DIM = 128
RET_CHUNK = 128
PEER_HEADS = 8
PEER_KEYS = 128
PEER_EXPERTS = PEER_KEYS * PEER_KEYS
PEER_TOPK = 16
COEF_BOUND_FACTOR = 1.25 * PEER_HEADS
EPS = 1e-6
LOG2E = 1.4426950408889634

LANES = 128
TQ = 128
TK_SEL = 256
TM_PROJ = 256
TM_PEER = 512
TE_PEER = 1024
TM_SEL = 256
RET_TILE = 512

_NT = (((1,), (1,)), ((), ()))
_TN = (((0,), (0,)), ((), ()))


def _params(sem, vmem_mb):
    return pltpu.CompilerParams(dimension_semantics=sem, vmem_limit_bytes=vmem_mb * 1024 * 1024)


def _resident(shape, index_map):
    return pl.BlockSpec(shape, index_map, pipeline_mode=pl.Buffered(1))


def _gelu(x):
    return jax.nn.gelu(x)


def _adaln_body(c_ref, w_ref, b_ref, o_ref):
    c = c_ref[...]
    act = (c * jax.nn.sigmoid(c)).astype(BF16)
    o_ref[...] = jnp.dot(act, w_ref[...].astype(BF16), preferred_element_type=F32) + b_ref[...]


def _adaln(c_pad, w, b):
    n = w.shape[1]
    tn = 1536
    return pl.pallas_call(
        _adaln_body,
        grid=(n // tn,),
        in_specs=[pl.BlockSpec((8, D_MODEL), lambda j: (0, 0)),
                  pl.BlockSpec((D_MODEL, tn), lambda j: (0, j)),
                  pl.BlockSpec((1, tn), lambda j: (0, j))],
        out_specs=pl.BlockSpec((8, tn), lambda j: (0, j)),
        out_shape=jax.ShapeDtypeStruct((8, n), F32),
        compiler_params=_params(("arbitrary",), 40),
        name="adaln",
    )(c_pad, w, b)


STD_COLS = 512 * 3 + 1024 * 4
TR_ROWS = 1024 + 256 + 256 + 64


def _inproj_body(x_ref, sc_ref, sh_ref, gn_ref, wstd_ref, wt_ref,
                 cv_ref, ks_ref, kw_ref, qr_ref, kr_ref, vr_ref, gr_ref,
                 qt_ref, vst_ref, vwt_ref, gt_ref, *, tiles_per_seq):
    tm = TM_PROJ
    i = pl.program_id(0)
    x = x_ref[...]
    ms = jnp.mean(x * x, axis=-1, keepdims=True)
    h = x * lax.rsqrt(ms + EPS) * gn_ref[...]
    h = h * (1.0 + sc_ref[0]) + sh_ref[0]
    hb = h.astype(BF16)

    def std(a, b):
        return jnp.dot(hb, wstd_ref[:, a:b], preferred_element_type=F32)

    y = std(0, 512)
    for g in range(KV_GROUPS):
        cv_ref[g] = y[:, g * LANES:(g + 1) * LANES].astype(BF16)

    t = (i % tiles_per_seq) * tm + lax.broadcasted_iota(jnp.int32, (tm, LANES), 0)
    lane = lax.broadcasted_iota(jnp.int32, (tm, LANES), 1)
    pos_hi = ((t >> 6) << 6).astype(F32)
    pos_lo = (t & 63).astype(F32)
    aug = jnp.where((lane == 64) | (lane == 66), pos_hi,
                    jnp.where((lane == 65) | (lane == 67), pos_lo, 0.0))
    for ref, off in ((ks_ref, 512), (kw_ref, 1024)):
        y = std(off, off + 512)
        for g in range(KV_GROUPS):
            ref[:, g * LANES:(g + 1) * LANES] = (y[:, g * LANES:(g + 1) * LANES] + aug).astype(BF16)

    for ref, off in ((qr_ref, 1536), (kr_ref, 2560), (vr_ref, 3584), (gr_ref, 4608)):
        ref[...] = std(off, off + 1024).astype(BF16)

    def tr(a, b):
        return lax.dot_general(wt_ref[a:b, :], hb, _NT, preferred_element_type=F32)

    qt = tr(0, 1024) * (HEAD_DIM ** -0.5 * LOG2E)
    vst = tr(1024, 1280)
    vwt = tr(1280, 1536)
    gt = tr(1536, 1600)
    for c in range(tm // LANES):
        sl = slice(c * LANES, (c + 1) * LANES)
        qt_ref[c] = qt[:, sl].astype(BF16)
        vst_ref[c] = vst[:, sl].astype(BF16)
        vwt_ref[c] = vwt[:, sl].astype(BF16)
        gt_ref[c] = gt[:, sl]


def _inproj(x2, scale1, shift1, g_mix, w_std, w_tr, seq):
    t_tokens = x2.shape[0]
    tm = TM_PROJ
    tps = seq // tm
    nt = t_tokens // tm
    c = tm // LANES
    row = lambda i: (i, 0)
    per_b = lambda i: (i // tps, 0, 0)
    out_shape = (
        jax.ShapeDtypeStruct((KV_GROUPS, t_tokens, LANES), BF16),
        jax.ShapeDtypeStruct((t_tokens, 512), BF16),
        jax.ShapeDtypeStruct((t_tokens, 512), BF16),
        jax.ShapeDtypeStruct((t_tokens, 1024), BF16),
        jax.ShapeDtypeStruct((t_tokens, 1024), BF16),
        jax.ShapeDtypeStruct((t_tokens, 1024), BF16),
        jax.ShapeDtypeStruct((t_tokens, 1024), BF16),
        jax.ShapeDtypeStruct((t_tokens // LANES, 1024, LANES), BF16),
        jax.ShapeDtypeStruct((t_tokens // LANES, 256, LANES), BF16),
        jax.ShapeDtypeStruct((t_tokens // LANES, 256, LANES), BF16),
        jax.ShapeDtypeStruct((t_tokens // LANES, 64, LANES), F32),
    )
    out_specs = (
        pl.BlockSpec((KV_GROUPS, tm, LANES), lambda i: (0, i, 0)),
        pl.BlockSpec((tm, 512), row),
        pl.BlockSpec((tm, 512), row),
        pl.BlockSpec((tm, 1024), row),
        pl.BlockSpec((tm, 1024), row),
        pl.BlockSpec((tm, 1024), row),
        pl.BlockSpec((tm, 1024), row),
        pl.BlockSpec((c, 1024, LANES), lambda i: (i, 0, 0)),
        pl.BlockSpec((c, 256, LANES), lambda i: (i, 0, 0)),
        pl.BlockSpec((c, 256, LANES), lambda i: (i, 0, 0)),
        pl.BlockSpec((c, 64, LANES), lambda i: (i, 0, 0)),
    )
    return pl.pallas_call(
        functools.partial(_inproj_body, tiles_per_seq=tps),
        grid=(nt,),
        in_specs=[pl.BlockSpec((tm, D_MODEL), row),
                  pl.BlockSpec((1, 1, D_MODEL), per_b),
                  pl.BlockSpec((1, 1, D_MODEL), per_b),
                  _resident((1, D_MODEL), lambda i: (0, 0)),
                  _resident((D_MODEL, STD_COLS), lambda i: (0, 0)),
                  _resident((TR_ROWS, D_MODEL), lambda i: (0, 0))],
        out_specs=out_specs,
        out_shape=out_shape,
        compiler_params=_params(("parallel",), 56),
        name="inproj",
    )(x2, scale1, shift1, g_mix, w_std, w_tr)


def _compress_body(x_ref, wa_ref, wb_ref, pea_ref, peb_ref, w2k_ref, w2vt_ref, kcp_ref, vct_ref):
    x = x_ref[0, 0]
    n_rows = x.shape[0]
    p = jnp.dot(x, wa_ref[...], preferred_element_type=F32)
    q = jnp.dot(x, wb_ref[...], preferred_element_type=F32)
    pe = (jnp.dot(pea_ref[...], wa_ref[...], preferred_element_type=F32)
          + jnp.dot(peb_ref[...], wb_ref[...], preferred_element_type=F32))[0:1, :]
    pre = p + pltpu.roll(q, n_rows - 1, 0) + pe
    hid = _gelu(pre).astype(BF16)
    kc = jnp.dot(hid, w2k_ref[...], preferred_element_type=F32)
    n = lax.broadcasted_iota(jnp.int32, (n_rows, LANES), 0)
    lane = lax.broadcasted_iota(jnp.int32, (n_rows, LANES), 1)
    ce = n * CMP_STRIDE + (CMP_LEN - 1)
    ce_hi = ((ce >> 6) << 6).astype(F32)
    ce_lo = (ce & 63).astype(F32)
    aug = jnp.where((lane == 64) | (lane == 66), ce_hi,
                    jnp.where((lane == 65) | (lane == 67), ce_lo, 0.0))
    kcp_ref[0] = (kc + aug).astype(BF16)
    vct_ref[0] = lax.dot_general(w2vt_ref[...], hid, _NT, preferred_element_type=F32).astype(BF16)


def _compress(cv4, wa, wb, pea, peb, w2k, w2vt):
    g_, b_, n_rows, _ = cv4.shape
    const2 = lambda n: (0, 0)
    return pl.pallas_call(
        _compress_body,
        grid=(b_ * g_,),
        in_specs=[pl.BlockSpec((1, 1, n_rows, 2048), lambda n: (n % KV_GROUPS, n // KV_GROUPS, 0, 0)),
                  pl.BlockSpec((2048, 256), const2),
                  pl.BlockSpec((2048, 256), const2),
                  pl.BlockSpec((8, 2048), const2),
                  pl.BlockSpec((8, 2048), const2),
                  pl.BlockSpec((256, LANES), const2),
                  pl.BlockSpec((64, 256), const2)],
        out_specs=(pl.BlockSpec((1, n_rows, LANES), lambda n: (n, 0, 0)),
                   pl.BlockSpec((1, 64, n_rows), lambda n: (n, 0, 0))),
        out_shape=(jax.ShapeDtypeStruct((b_ * g_, n_rows, LANES), BF16),
                   jax.ShapeDtypeStruct((b_ * g_, 64, n_rows), BF16)),
        compiler_params=_params(("parallel",), 32),
        name="nsa_compress",
    )(cv4, wa, wb, pea, peb, w2k, w2vt)


def _softmax_step(state, s, pv_prev):
    m_i, l_i, acc = state
    m_new = jnp.maximum(m_i, jnp.max(s, axis=0, keepdims=True))
    alpha = jnp.exp2(m_i - m_new)
    p = jnp.exp2(s - m_new)
    l_new = alpha * l_i + jnp.sum(p, axis=0, keepdims=True)
    return (m_new, l_new, (acc + pv_prev) * alpha), p.astype(BF16)


def _nsa_body(qt_ref, gt_ref, kcp_ref, vct_ref, ks_ref, kw_ref, vst_ref, vwt_ref,
              qaug_ref, gout_ref, ovl_ref, grp_ref, o_ref, selb_ref, sa_ref, sb_ref, pa_ref, pb_ref, *, n_cmp):
    qi = pl.program_id(2)
    t0 = qi * TQ
    wq = GROUP * TQ

    qp = jnp.concatenate(
        [jnp.concatenate([qt_ref[0, r * HEAD_DIM:(r + 1) * HEAD_DIM, :], qaug_ref[r]], axis=0)
         for r in range(GROUP)], axis=1)

    lane_q = lax.broadcasted_iota(jnp.int32, (1, wq), 1) & (TQ - 1)
    n_win = WINDOW // TQ + 1
    k_lo = pl.multiple_of(jnp.maximum(t0 - WINDOW, 0), TQ)

    s = jnp.dot(kcp_ref[0], qp, preferred_element_type=F32)
    s_w = jnp.dot(kw_ref[pl.ds(k_lo, n_win * TQ), :], qp, preferred_element_type=F32)
    n_io = lax.broadcasted_iota(jnp.int32, (n_cmp, wq), 0)
    tl = lax.broadcasted_iota(jnp.int32, (n_cmp, wq), 1) & (TQ - 1)
    valid = (n_io * CMP_STRIDE + (CMP_LEN - 1)) <= (t0 + tl)
    s = jnp.where(valid, s, NEG)
    m = jnp.maximum(jnp.max(s, axis=0, keepdims=True), 0.5 * NEG)
    p = jnp.exp2(s - m)
    l = jnp.sum(p, axis=0, keepdims=True)
    pn = p * (1.0 / jnp.maximum(l, 1e-30))
    o_c = jnp.dot(vct_ref[0], pn.astype(BF16), preferred_element_type=F32)

    ps = pn[:, 0:TQ] + pn[:, TQ:2 * TQ] + pn[:, 2 * TQ:3 * TQ] + pn[:, 3 * TQ:4 * TQ]
    hi = ps.astype(BF16)
    r1 = ps - hi.astype(F32)
    mid = r1.astype(BF16)
    lo = (r1 - mid.astype(F32)).astype(BF16)
    ovl = ovl_ref[...]
    imp = (jnp.dot(ovl, hi, preferred_element_type=F32)
           + jnp.dot(ovl, mid, preferred_element_type=F32)
           + jnp.dot(ovl, lo, preferred_element_type=F32))

    n_slc = imp.shape[0]
    m_io = lax.broadcasted_iota(jnp.int32, (n_slc, TQ), 0)
    q_io = lax.broadcasted_iota(jnp.int32, (n_slc, TQ), 1)
    back = ((t0 + q_io) >> 6) - m_io
    valid_s = back >= 0
    forced = valid_s & ((m_io == 0) | (back < 2))
    w = jnp.where(forced, -jnp.inf, jnp.where(valid_s, imp, -1.0))
    selb = jnp.where(forced, 0.0, NEG)

    def pick(carry, lanes=None):
        w, selb = carry
        mx = jnp.max(w, axis=0, keepdims=True)
        idx = jnp.min(jnp.where(w == mx, m_io, n_slc), axis=0, keepdims=True)
        hit = m_io == idx
        if lanes is not None:
            hit = hit & lanes
        return jnp.where(hit, -jnp.inf, w), jnp.where(hit, 0.0, selb)

    carry = (w, selb)
    for _ in range(SLC_TOPK - 3):
        carry = pick(carry)
    carry = pick(carry, lanes=(t0 + q_io) < 2 * SLC_BLOCK)
    _, selb = pick(carry, lanes=(t0 + q_io) < SLC_BLOCK)
    for mblk in range(n_slc):
        selb_ref[mblk] = jnp.broadcast_to(selb[mblk:mblk + 1, :], (8, TQ))

    per_blk = TK_SEL // SLC_BLOCK

    def scores(j):
        k0 = pl.multiple_of(j * TK_SEL, TK_SEL)
        bias = jnp.concatenate(
            [jnp.tile(selb_ref[j * per_blk + u], (SLC_BLOCK // 8, 1)) for u in range(per_blk)], axis=0)
        bias = jnp.concatenate([bias] * GROUP, axis=1)
        return jnp.dot(ks_ref[pl.ds(k0, TK_SEL), :], qp, preferred_element_type=F32) + bias

    def vt_tile(j):
        return jnp.concatenate([vst_ref[2 * j], vst_ref[2 * j + 1]], axis=1)

    def causal(j):
        kr = lax.broadcasted_iota(jnp.int32, (TK_SEL, wq), 0)
        return kr <= (t0 - j * TK_SEL) + lane_q

    def pv_dot(j, p_ref):
        return jnp.dot(vt_tile(j), p_ref[...], preferred_element_type=F32)

    jd = qi // (TK_SEL // TQ)

    sa_ref[...] = scores(0)
    pb_ref[...] = jnp.zeros_like(pb_ref)
    sel01 = jnp.where(selb == 0.0, 1.0, 0.0).astype(BF16)
    tile_any = jnp.max(jnp.dot(grp_ref[...], sel01, preferred_element_type=F32), axis=1, keepdims=True)
    j_io = lax.broadcasted_iota(jnp.int32, tile_any.shape, 0)
    q_lo_v = jnp.max(jnp.where((tile_any == 0.0) & (j_io <= jd), j_io, -1), axis=0, keepdims=True) + 1
    n_pre_v = jnp.max(jnp.where((tile_any > 0.0) & (j_io < q_lo_v), j_io, -1), axis=0, keepdims=True) + 1

    d = (t0 - k_lo) + lane_q - lax.broadcasted_iota(jnp.int32, (n_win * TQ, wq), 0)
    s_w = jnp.where(lax.bitcast_convert_type(d, jnp.uint32) < WINDOW, s_w, NEG)
    p = jnp.exp2(s_w - jnp.max(s_w, axis=0, keepdims=True))
    l_w = jnp.sum(p, axis=0, keepdims=True)
    vt_w = jnp.concatenate([vwt_ref[k_lo // TQ + u] for u in range(n_win)], axis=1)
    o_w = jnp.dot(vt_w, p.astype(BF16), preferred_element_type=F32) * (1.0 / l_w)

    q_lo = q_lo_v[0, 0]
    n_pre = n_pre_v[0, 0]
    n_vis = n_pre + jd - q_lo + 1

    def tile_at(pos):
        return jnp.where(pos < n_pre, pos, pos - n_pre + q_lo)

    state = (jnp.full((1, wq), NEG, F32), jnp.zeros((1, wq), F32), jnp.zeros((HEAD_DIM, wq), F32))

    def pair(u, state):
        a = 2 * u
        sb_ref[...] = scores(tile_at(a + 1))
        state, p = _softmax_step(state, sa_ref[...], pv_dot(tile_at(jnp.maximum(a - 1, 0)), pb_ref))
        pa_ref[...] = p
        sa_ref[...] = scores(tile_at(a + 2))
        state, p = _softmax_step(state, sb_ref[...], pv_dot(tile_at(a), pa_ref))
        pb_ref[...] = p
        return state

    n_pair = (n_vis - 1) // 2
    state = lax.fori_loop(0, n_pair, pair, state)
    x = 2 * n_pair
    y = jnp.minimum(x + 1, n_vis - 1)
    tx, ty = tile_at(x), tile_at(y)
    sb_ref[...] = scores(ty)
    state, p = _softmax_step(state, jnp.where(causal(tx), sa_ref[...], NEG),
                             pv_dot(tile_at(jnp.maximum(x - 1, 0)), pb_ref))
    pa_ref[...] = p
    ty_mask = jnp.where(x + 1 < n_vis, ty, jd + 1)
    (_, l_s, acc_s), p = _softmax_step(state, jnp.where(causal(ty_mask), sb_ref[...], NEG), pv_dot(tx, pa_ref))
    acc_s = acc_s + jnp.dot(vt_tile(ty), p, preferred_element_type=F32)

    o_s = acc_s * (1.0 / l_s)
    gw = jax.nn.sigmoid(gt_ref[0])
    outs = []
    for r in range(GROUP):
        sl = slice(r * TQ, (r + 1) * TQ)
        o = (gw[r:r + 1, :] * o_c[:, sl] + gw[GROUP + r:GROUP + r + 1, :] * o_s[:, sl]
             + gw[2 * GROUP + r:2 * GROUP + r + 1, :] * o_w[:, sl])
        ms = jnp.mean(o * o, axis=0, keepdims=True)
        outs.append(o * lax.rsqrt(ms + EPS) * gout_ref[r])
    o_ref[...] = jnp.concatenate(outs, axis=0).T.astype(BF16)


def _nsa(qt, gt, kcp, vct, ks, kw, vst, vwt, qaug, gout_b, ovl_t, grp, batch, seq):
    nq = seq // TQ
    n_cmp = kcp.shape[1]
    n_slc = seq // SLC_BLOCK
    t_tokens = batch * seq
    per_b_chunks = seq // LANES
    return pl.pallas_call(
        functools.partial(_nsa_body, n_cmp=n_cmp),
        grid=(batch, KV_GROUPS, nq),
        in_specs=[
            pl.BlockSpec((1, 256, LANES), lambda b, g, q: (b * nq + q, g, 0)),
            pl.BlockSpec((1, 16, LANES), lambda b, g, q: (b * nq + q, g, 0)),
            pl.BlockSpec((1, n_cmp, LANES), lambda b, g, q: (b * KV_GROUPS + g, 0, 0)),
            pl.BlockSpec((1, 64, n_cmp), lambda b, g, q: (b * KV_GROUPS + g, 0, 0)),
            pl.BlockSpec((seq, LANES), lambda b, g, q: (b, g)),
            pl.BlockSpec((seq, LANES), lambda b, g, q: (b, g)),
            pl.BlockSpec((per_b_chunks, 64, LANES), lambda b, g, q: (b, g, 0)),
            pl.BlockSpec((per_b_chunks, 64, LANES), lambda b, g, q: (b, g, 0)),
            pl.BlockSpec((GROUP, 64, LANES), lambda b, g, q: (g, 0, 0)),
            pl.BlockSpec((GROUP, 64, LANES), lambda b, g, q: (g, 0, 0)),
            pl.BlockSpec((n_slc, n_cmp), lambda b, g, q: (0, 0)),
            pl.BlockSpec(grp.shape, lambda b, g, q: (0, 0)),
        ],
        out_specs=pl.BlockSpec((TQ, 256), lambda b, g, q: (b * nq + q, g)),
        out_shape=jax.ShapeDtypeStruct((t_tokens, 1024), BF16),
        scratch_shapes=[pltpu.VMEM((n_slc, 8, TQ), F32),
                        pltpu.VMEM((TK_SEL, GROUP * TQ), F32), pltpu.VMEM((TK_SEL, GROUP * TQ), F32),
                        pltpu.VMEM((TK_SEL, GROUP * TQ), BF16), pltpu.VMEM((TK_SEL, GROUP * TQ), BF16)],
        compiler_params=_params(("parallel", "parallel", "arbitrary"), 40),
        name="nsa_attention",
    )(qt, gt, kcp, vct, ks, kw, vst, vwt, qaug, gout_b, ovl_t, grp)


def _ret_body(q_ref, k_ref, v_ref, g_ref, dm_ref, kd_ref, qd_ref, cd_ref, go_ref, o_ref, st_ref):
    @pl.when(pl.program_id(2) == 0)
    def _():
        st_ref[...] = jnp.zeros_like(st_ref)

    c_ = RET_CHUNK
    slices = [slice(c * c_, (c + 1) * c_) for c in range(RET_TILE // c_)]
    atts, kvs = [], []
    for sl in slices:
        k = k_ref[sl, :]
        atts.append((lax.dot_general(q_ref[sl, :], k, _NT, preferred_element_type=F32) * dm_ref[0]).astype(BF16))
        kdec = (k.astype(F32) * kd_ref[0]).astype(BF16)
        kvs.append(lax.dot_general(kdec, v_ref[sl, :], _TN, preferred_element_type=F32))
    states = [st_ref[...]]
    for kv in kvs:
        states.append(states[-1] * cd_ref[0] + kv)
    st_ref[...] = states[-1]
    for sl, att, state in zip(slices, atts, states):
        o = (jnp.dot(att, v_ref[sl, :], preferred_element_type=F32)
             + qd_ref[0] * jnp.dot(q_ref[sl, :], state.astype(BF16), preferred_element_type=F32))
        mu = jnp.mean(o, axis=-1, keepdims=True)
        oc = o - mu
        var = jnp.mean(oc * oc, axis=-1, keepdims=True)
        y = oc * lax.rsqrt(var + EPS) * go_ref[0, 0:1, :]
        gate = g_ref[sl, :].astype(F32)
        o_ref[sl, :] = (gate * jax.nn.sigmoid(gate) * y).astype(BF16)


def _retention(q_r, k_r, v_r, g_r, dm, kd, qd, cd, go, batch, seq):
    t_tokens = batch * seq
    nc = seq // RET_TILE
    tok = lambda b, h, c: (b * nc + c, h)
    per_h = lambda b, h, c: (h, 0, 0)
    sq = (1, RET_DIM, RET_DIM)
    return pl.pallas_call(
        _ret_body,
        grid=(batch, RET_HEADS, nc),
        in_specs=[pl.BlockSpec((RET_TILE, RET_DIM), tok)] * 4
        + [pl.BlockSpec(sq, per_h)] * 4 + [pl.BlockSpec((1, 8, RET_DIM), per_h)],
        out_specs=pl.BlockSpec((RET_TILE, RET_DIM), tok),
        out_shape=jax.ShapeDtypeStruct((t_tokens, RET_HEADS * RET_DIM), BF16),
        scratch_shapes=[pltpu.VMEM((RET_DIM, RET_DIM), F32)],
        compiler_params=_params(("parallel", "parallel", "arbitrary"), 32),
        name="retention",
    )(q_r, k_r, v_r, g_r, dm, kd, qd, cd, go)


def _pow2_scale(magnitude):
    return jnp.exp2(jnp.floor(jnp.log2(FP8_TARGET / jnp.maximum(magnitude, 1e-30))))


def _mid_body(on_ref, or_ref, x_ref, g1_ref, sc_ref, sh_ref, gn_ref, wo_ref, wqt_ref, sk_ref, ps_ref,
              x1_ref, h2t_ref, st_ref, scl_ref):
    acc = (jnp.dot(on_ref[...], wo_ref[0:1024, :], preferred_element_type=F32)
           + jnp.dot(or_ref[...], wo_ref[1024:2048, :], preferred_element_type=F32))
    x1 = x_ref[...] + g1_ref[0] * acc
    x1_ref[...] = x1
    ms = jnp.mean(x1 * x1, axis=-1, keepdims=True)
    h2 = x1 * lax.rsqrt(ms + EPS) * gn_ref[...]
    h2 = h2 * (1.0 + sc_ref[0]) + sh_ref[0]

    h2_t = h2.T
    amax = jnp.max(jnp.max(jnp.abs(h2_t), axis=0, keepdims=True), axis=1, keepdims=True)
    s_h = _pow2_scale(amax)
    h2t_ref[...] = (h2_t * s_h).astype(FP8)
    norm = jnp.sqrt(jnp.sum(h2_t * h2_t, axis=0, keepdims=True))
    s_c = _pow2_scale(COEF_BOUND_FACTOR * ps_ref[2:3, 0:1] * norm)
    scl_ref[...] = jnp.concatenate(
        [jnp.broadcast_to(ps_ref[0:1, 0:1] / s_h, s_c.shape), s_c, ps_ref[1:2, 0:1] / s_c,
         jnp.zeros((5, s_c.shape[1]), F32)], axis=0)
    h2 = h2.astype(BF16)
    qt = lax.dot_general(wqt_ref[...], h2, _NT, preferred_element_type=F32).astype(BF16)
    for hp in range(2 * PEER_HEADS):
        st_ref[hp] = jnp.dot(sk_ref[hp], qt[hp * 128:(hp + 1) * 128, :], preferred_element_type=F32)


def _mid(o_nsa, o_ret, x2, gate1, scale2, shift2, g_ffn, w_out, wq_t, sub_keys, peer_scales, seq):
    t_tokens = x2.shape[0]
    tm = TM_PROJ
    tps = seq // tm
    row = lambda i: (i, 0)
    per_b = lambda i: (i // tps, 0, 0)
    return pl.pallas_call(
        _mid_body,
        grid=(t_tokens // tm,),
        in_specs=[pl.BlockSpec((tm, 1024), row),
                  pl.BlockSpec((tm, 1024), row),
                  pl.BlockSpec((tm, D_MODEL), row),
                  pl.BlockSpec((1, 1, D_MODEL), per_b),
                  pl.BlockSpec((1, 1, D_MODEL), per_b),
                  pl.BlockSpec((1, 1, D_MODEL), per_b),
                  _resident((1, D_MODEL), lambda i: (0, 0)),
                  _resident((D_MODEL, D_MODEL), lambda i: (0, 0)),
                  _resident((D_MODEL, D_MODEL), lambda i: (0, 0)),
                  _resident((2 * PEER_HEADS, PEER_KEYS, 128), lambda i: (0, 0, 0)),
                  _resident((8, LANES), lambda i: (0, 0))],
        out_specs=(pl.BlockSpec((tm, D_MODEL), row),
                   pl.BlockSpec((D_MODEL, tm), lambda i: (0, i)),
                   pl.BlockSpec((2 * PEER_HEADS, PEER_KEYS, tm), lambda i: (0, 0, i)),
                   pl.BlockSpec((8, tm), lambda i: (0, i))),
        out_shape=(jax.ShapeDtypeStruct((t_tokens, D_MODEL), F32),
                   jax.ShapeDtypeStruct((D_MODEL, t_tokens), FP8),
                   jax.ShapeDtypeStruct((2 * PEER_HEADS, PEER_KEYS, t_tokens), F32),
                   jax.ShapeDtypeStruct((8, t_tokens), F32)),
        compiler_params=_params(("parallel",), 48),
        name="outproj_peerq",
    )(o_nsa, o_ret, x2, gate1, scale2, shift2, g_ffn, w_out, wq_t, sub_keys, peer_scales)


def _top16(s, break_ties):
    n_rows, n = s.shape
    io = lax.broadcasted_iota(jnp.int32, (n_rows, n), 0)
    a_io = lax.broadcasted_iota(jnp.int32, (PEER_TOPK, n), 0)
    rank = jnp.full((n_rows, n), PEER_TOPK, jnp.int32)
    vals = jnp.zeros((PEER_TOPK, n), F32)
    for a in range(PEER_TOPK):
        mx = jnp.max(s, axis=0, keepdims=True)
        hit = s == mx
        if break_ties:
            hit = io == jnp.min(jnp.where(hit, io, n_rows), axis=0, keepdims=True)
        rank = jnp.where(hit, a, rank)
        s = jnp.where(hit, -jnp.inf, s)
        vals = jnp.where(a_io == a, mx, vals)
    return vals, rank


def _peer_select_body(s_ref, scl_ref, l_ref, w1_ref, r2_ref, w2_ref):
    s1 = s_ref[0]
    s2 = s_ref[1]
    n = s1.shape[1]

    def select(break_ties):
        v1, rank1 = _top16(s1, break_ties)
        v2, rank2 = _top16(s2, break_ties)
        a_io = lax.broadcasted_iota(jnp.int32, (PEER_TOPK, n), 0)
        cnt = jnp.zeros((PEER_TOPK, n), jnp.int32)
        cur = v1 + v2[0:1, :]
        top = v1[0:1, :] + v2[0:1, :]
        z = jnp.zeros((1, n), F32)
        for _ in range(PEER_TOPK):
            mx = jnp.max(cur, axis=0, keepdims=True)
            aidx = jnp.min(jnp.where(cur == mx, a_io, PEER_TOPK), axis=0, keepdims=True)
            hit = a_io == aidx
            cnt = cnt + hit.astype(jnp.int32)
            nxt = jnp.sum(jnp.where(hit, cnt, 0), axis=0, keepdims=True)
            nv = jnp.max(jnp.where(a_io == nxt, v2, -jnp.inf), axis=0, keepdims=True)
            cur = jnp.where(hit, v1 + nv, cur)
            z = z + jnp.exp(mx - top)
        cnt_b = cnt.astype(F32).astype(BF16)
        rank_b = rank1.astype(F32).astype(BF16)
        lrow = jnp.zeros(s1.shape, BF16)
        for a in range(PEER_TOPK):
            lrow = jnp.where(rank_b == a, jnp.broadcast_to(cnt_b[a:a + 1, :], s1.shape), lrow)
        l_ref[0] = lrow.astype(F32)
        w1_ref[0] = jnp.exp(s1 - v1[0:1, :])
        r2_ref[0] = rank2.astype(F32).astype(BF16)
        w2_ref[0] = (jnp.exp(s2 - v2[0:1, :]) * (scl_ref[1:2, :] / z)).astype(BF16)
        return rank1, rank2

    rank1, rank2 = select(False)
    marked = (jnp.sum((rank1 < PEER_TOPK).astype(jnp.int32), axis=0, keepdims=True)
              + jnp.sum((rank2 < PEER_TOPK).astype(jnp.int32), axis=0, keepdims=True))

    @pl.when(jnp.max(marked) != 2 * PEER_TOPK)
    def _():
        select(True)


def _peer_select(st, scl):
    t_tokens = st.shape[2]
    tm = TM_SEL
    shp = jax.ShapeDtypeStruct((PEER_HEADS, PEER_KEYS, t_tokens), F32)
    shp_b = jax.ShapeDtypeStruct((PEER_HEADS, PEER_KEYS, t_tokens), BF16)
    spec = pl.BlockSpec((1, PEER_KEYS, tm), lambda i, h: (h, 0, i))
    return pl.pallas_call(
        _peer_select_body,
        grid=(t_tokens // tm, PEER_HEADS),
        in_specs=[pl.BlockSpec((2, PEER_KEYS, tm), lambda i, h: (h, 0, i)),
                  pl.BlockSpec((8, tm), lambda i, h: (0, i))],
        out_specs=(spec, spec, spec, spec),
        out_shape=(shp, shp, shp_b, shp_b),
        compiler_params=_params(("parallel", "parallel"), 32),
        name="peer_select",
    )(st, scl)


def _transpose_body(v_ref, s_ref, o_ref):
    o_ref[...] = (v_ref[...].T * s_ref[0:1, 0:1]).astype(FP8)


def _transpose_fp8(v, scale_tile):
    n, d = v.shape
    tn = 512
    return pl.pallas_call(
        _transpose_body,
        grid=(n // tn,),
        in_specs=[pl.BlockSpec((tn, d), lambda i: (i, 0)),
                  pl.BlockSpec((8, LANES), lambda i: (0, 0))],
        out_specs=pl.BlockSpec((d, tn), lambda i: (0, i)),
        out_shape=jax.ShapeDtypeStruct((d, n), FP8),
        compiler_params=_params(("parallel",), 32),
        name="transpose_v",
    )(v, scale_tile)


def _peer_expert_body(h2t_ref, u_ref, vt_ref, l_ref, w1_ref, r2_ref, w2_ref, scl_ref,
                      o_ref, ce_ref, co_ref, *, steps_per_tile):
    g = pl.program_id(0)
    sw = 256
    strips = [slice(c * sw, (c + 1) * sw) for c in range(TM_PEER // sw)]
    n_piece = TE_PEER // PEER_KEYS
    blk = D_MODEL // n_piece

    @pl.when(g == 0)
    def _():
        co_ref[...] = jnp.zeros_like(co_ref)

    @pl.when((g == 0) | ((g - 1) % steps_per_tile == 0))
    def _():
        o_ref[...] = jnp.zeros_like(o_ref)

    def run(c_new, c_old):
        def piece(j, carry):
            r0 = pl.multiple_of(j * PEER_KEYS, PEER_KEYS)
            d0 = pl.multiple_of(j * blk, blk)
            for ls in strips:
                coef = None
                for h in range(PEER_HEADS):
                    lrow = jnp.broadcast_to(l_ref[h, pl.ds(j, 1), ls], (16, sw)).astype(BF16)
                    w1row = jnp.broadcast_to(w1_ref[h, pl.ds(j, 1), ls], (16, sw)).astype(BF16)
                    lrow = jnp.tile(lrow, (PEER_KEYS // 16, 1))
                    w1row = jnp.tile(w1row, (PEER_KEYS // 16, 1))
                    term = jnp.where(r2_ref[h, :, ls] < lrow, w2_ref[h, :, ls] * w1row, jnp.zeros((), BF16))
                    coef = term if coef is None else coef + term
                a_t = jnp.dot(u_ref[pl.ds(r0, PEER_KEYS), :], h2t_ref[:, ls], preferred_element_type=F32)
                a_scale = jnp.broadcast_to(scl_ref[0:1, ls], (16, sw)).astype(BF16)
                act = _gelu(a_t.astype(BF16) * jnp.tile(a_scale, (PEER_KEYS // 16, 1)))
                c_new[pl.ds(r0, PEER_KEYS), ls] = (coef * act).astype(FP8)
                o_ref[pl.ds(d0, blk), ls] += jnp.dot(vt_ref[pl.ds(d0, blk), :], c_old[:, ls],
                                                     preferred_element_type=F32)
            return carry

        lax.fori_loop(0, n_piece, piece, 0, unroll=4)

    @pl.when(g % 2 == 0)
    def _():
        run(ce_ref, co_ref)

    @pl.when(g % 2 == 1)
    def _():
        run(co_ref, ce_ref)


def _peer_expert(h2t, u_b, v_t, lrow, w1, r2, w2, scl):
    t_tokens = h2t.shape[1]
    tm, te = TM_PEER, TE_PEER
    n_piece = te // PEER_KEYS
    ne = PEER_EXPERTS // te
    n_steps = (t_tokens // tm) * ne
    cur = lambda g: jnp.minimum(g, n_steps - 1)
    prev = lambda g: jnp.maximum(g - 1, 0)
    row_spec = pl.BlockSpec((PEER_HEADS, n_piece, tm), lambda g: (0, cur(g) % ne, cur(g) // ne))
    full_spec = pl.BlockSpec((PEER_HEADS, PEER_KEYS, tm), lambda g: (0, 0, cur(g) // ne))
    return pl.pallas_call(
        functools.partial(_peer_expert_body, steps_per_tile=ne),
        grid=(n_steps + 1,),
        in_specs=[pl.BlockSpec((D_MODEL, tm), lambda g: (0, cur(g) // ne)),
                  pl.BlockSpec((te, D_MODEL), lambda g: (cur(g) % ne, 0)),
                  pl.BlockSpec((D_MODEL, te), lambda g: (0, prev(g) % ne)),
                  row_spec, row_spec, full_spec, full_spec,
                  pl.BlockSpec((8, tm), lambda g: (0, cur(g) // ne))],
        out_specs=pl.BlockSpec((D_MODEL, tm), lambda g: (0, prev(g) // ne)),
        out_shape=jax.ShapeDtypeStruct((D_MODEL, t_tokens), F32),
        scratch_shapes=[pltpu.VMEM((te, tm), FP8), pltpu.VMEM((te, tm), FP8)],
        compiler_params=_params(("arbitrary",), 52),
        name="peer_experts",
    )(h2t, u_b, v_t, lrow, w1, r2, w2, scl)


def _final_body(x1_ref, pt_ref, scl_ref, g2_ref, gn_ref, o_ref, *, apply_norm):
    peer = (pt_ref[...] * scl_ref[2:3, :]).T
    y = x1_ref[...] + g2_ref[0] * peer
    if apply_norm:
        ms = jnp.mean(y * y, axis=-1, keepdims=True)
        y = y * lax.rsqrt(ms + EPS) * gn_ref[...]
    o_ref[...] = y


def _final(x1, peer_t, scl, gate2, g_final, seq, apply_norm):
    t_tokens = x1.shape[0]
    tm = TM_PROJ
    tps = seq // tm
    return pl.pallas_call(
        functools.partial(_final_body, apply_norm=apply_norm),
        grid=(t_tokens // tm,),
        in_specs=[pl.BlockSpec((tm, D_MODEL), lambda i: (i, 0)),
                  pl.BlockSpec((D_MODEL, tm), lambda i: (0, i)),
                  pl.BlockSpec((8, tm), lambda i: (0, i)),
                  pl.BlockSpec((1, 1, D_MODEL), lambda i: (i // tps, 0, 0)),
                  pl.BlockSpec((1, D_MODEL), lambda i: (0, 0))],
        out_specs=pl.BlockSpec((tm, D_MODEL), lambda i: (i, 0)),
        out_shape=jax.ShapeDtypeStruct((t_tokens, D_MODEL), F32),
        compiler_params=_params(("parallel",), 32),
        name="final_norm",
    )(x1, peer_t, scl, gate2, g_final)


def _split_cols(a, sizes):
    out, acc = [], 0
    for s in sizes:
        out.append(a[:, acc:acc + s])
        acc += s
    return out


def _inproj_weights(w_in):
    kvw = KV_GROUPS * HEAD_DIM
    sizes = (1024,) + (kvw,) * 6 + (3 * N_HEADS, 1024, 1024, 1024, 1024)
    q_a, k_c, v_c, k_s, v_s, k_w, v_w, g_a, q_r, k_r, v_r, g_r = _split_cols(w_in, sizes)
    d = w_in.shape[0]

    def grp(a, g):
        return a[:, g * HEAD_DIM:(g + 1) * HEAD_DIM]

    zeros = jnp.zeros((d, HEAD_DIM), w_in.dtype)
    cv = [jnp.concatenate([grp(k_c, g), grp(v_c, g)], axis=1) for g in range(KV_GROUPS)]
    ksp = [jnp.concatenate([grp(k_s, g), zeros], axis=1) for g in range(KV_GROUPS)]
    kwp = [jnp.concatenate([grp(k_w, g), zeros], axis=1) for g in range(KV_GROUPS)]
    w_std = jnp.concatenate(cv + ksp + kwp + [q_r, k_r, v_r, g_r], axis=1).astype(BF16)
    gcols = []
    for g in range(KV_GROUPS):
        for br in range(3):
            for r in range(GROUP):
                c = (g * GROUP + r) * 3 + br
                gcols.append(g_a[:, c:c + 1])
        gcols.append(jnp.zeros((d, 4), w_in.dtype))
    w_tr = jnp.concatenate([q_a, v_s, v_w] + gcols, axis=1).T.astype(BF16)
    return w_std, w_tr


def _compress_weights(pe_k, pe_v, k_w1, k_w2, v_w1, v_w2):
    half = CMP_LEN // 2

    def w1_half(w1k, w1v, lo):
        a = w1k.reshape(CMP_LEN, HEAD_DIM, CMP_HIDDEN)[lo:lo + half]
        b = w1v.reshape(CMP_LEN, HEAD_DIM, CMP_HIDDEN)[lo:lo + half]
        za = jnp.zeros_like(a)
        top = jnp.concatenate([a, za], axis=2)
        bot = jnp.concatenate([za, b], axis=2)
        return jnp.concatenate([top, bot], axis=1).reshape(half * 2 * HEAD_DIM, 2 * CMP_HIDDEN).astype(BF16)

    def pe_half(lo):
        row = jnp.concatenate([pe_k[lo:lo + half], pe_v[lo:lo + half]], axis=1).reshape(1, -1)
        return jnp.broadcast_to(row, (8, row.shape[1])).astype(BF16)

    wa = w1_half(k_w1, v_w1, 0)
    wb = w1_half(k_w1, v_w1, half)
    w2k = jnp.zeros((2 * CMP_HIDDEN, LANES), F32).at[:CMP_HIDDEN, :HEAD_DIM].set(k_w2).astype(BF16)
    w2vt = jnp.zeros((HEAD_DIM, 2 * CMP_HIDDEN), F32).at[:, CMP_HIDDEN:].set(v_w2.T).astype(BF16)
    return wa, wb, pe_half(0), pe_half(half), w2k, w2vt


def _nsa_constants(seq):
    slopes = jnp.exp2(-8.0 * (jnp.arange(N_HEADS, dtype=F32) + 1.0) / N_HEADS) * LOG2E
    s_hi = slopes.astype(BF16)
    s_lo = (slopes - s_hi.astype(F32)).astype(BF16)
    rows = jnp.zeros((N_HEADS, HEAD_DIM), BF16)
    rows = rows.at[:, 0].set(s_hi).at[:, 1].set(s_hi).at[:, 2].set(s_lo).at[:, 3].set(s_lo)
    qaug = jnp.broadcast_to(rows[:, :, None], (N_HEADS, HEAD_DIM, LANES))
    n_rows = seq // CMP_STRIDE
    n_slc = seq // SLC_BLOCK
    start = np.arange(n_rows)[:, None] * CMP_STRIDE
    end = start + CMP_LEN - 1
    blk = np.arange(n_slc)[None, :] * SLC_BLOCK
    ovl = ((start < blk + SLC_BLOCK) & (end >= blk)).astype(np.float32)
    per_tile = TK_SEL // SLC_BLOCK
    grp = (np.arange(n_slc)[None, :] // per_tile == np.arange(n_slc // per_tile)[:, None])
    return qaug, jnp.asarray(ovl.T, BF16), jnp.asarray(grp.astype(np.float32), BF16)


def _retention_constants():
    h, c = RET_HEADS, RET_CHUNK
    lg = jnp.log1p(-jnp.exp2(-5.0 - jnp.arange(h, dtype=F32)))
    pos = jnp.arange(c, dtype=F32)
    diff = pos[:, None] - pos[None, :]
    scale = RET_DIM ** -0.5
    dm = jnp.where(diff >= 0, jnp.exp(lg[:, None, None] * jnp.maximum(diff, 0.0)), 0.0) * scale
    k_decay = jnp.exp(lg[:, None] * (c - 1.0 - pos)) * scale
    q_decay = jnp.exp(lg[:, None] * (pos + 1.0))
    chunk_decay = jnp.exp(lg * c)
    kd = jnp.broadcast_to(k_decay[:, :, None], (h, c, RET_DIM))
    qd = jnp.broadcast_to(q_decay[:, :, None], (h, c, RET_DIM))
    cd = jnp.broadcast_to(chunk_decay[:, None, None], (h, RET_DIM, RET_DIM))
    return dm, kd, qd, cd


def kernel(x, c, w_ada, b_ada, g_norm_mix, g_norm_ffn, g_norm_final, w_in, cmp_pe_k, cmp_pe_v,
           cmp_k_w1, cmp_k_w2, cmp_v_w1, cmp_v_w2, g_nsa_out, g_ret_out, w_out,
           peer_w_q, peer_sub_keys, peer_u, peer_v):
    batch, seq, d = x.shape
    depth = w_ada.shape[0]
    t_tokens = batch * seq
    xf = x.reshape(t_tokens, d)
    c_pad = jnp.zeros((8, d), F32).at[:batch].set(c)
    qaug, ovl_t, grp = _nsa_constants(seq)
    dm, kd, qd, cd = _retention_constants()

    for l in range(depth):
        mod = _adaln(c_pad, w_ada[l], b_ada[l][None, :])[:batch].reshape(batch, 6, 1, d)
        shift1, scale1, gate1, shift2, scale2, gate2 = (mod[:, k] for k in range(6))

        w_std, w_tr = _inproj_weights(w_in[l])
        (cv, ks, kw, q_r, k_r, v_r, g_r, qt, vst, vwt, gt) = _inproj(
            xf, scale1, shift1, g_norm_mix[l][None, :], w_std, w_tr, seq)

        cv4 = cv.reshape(KV_GROUPS, batch, seq // CMP_STRIDE, CMP_STRIDE * LANES)
        kcp, vct = _compress(cv4, *_compress_weights(cmp_pe_k[l], cmp_pe_v[l], cmp_k_w1[l], cmp_k_w2[l],
                                                     cmp_v_w1[l], cmp_v_w2[l]))
        gout_b = jnp.broadcast_to(g_nsa_out[l][:, :, None], (N_HEADS, HEAD_DIM, LANES))
        o_nsa = _nsa(qt, gt, kcp, vct, ks, kw, vst, vwt, qaug, gout_b, ovl_t, grp, batch, seq)

        go = jnp.broadcast_to(g_ret_out[l][:, None, :], (RET_HEADS, 8, RET_DIM))
        o_ret = _retention(q_r, k_r, v_r, g_r, dm, kd, qd, cd, go, batch, seq)

        sub_keys = peer_sub_keys[l].reshape(2 * PEER_HEADS, PEER_KEYS, -1).astype(BF16)
        s_u = _pow2_scale(jnp.max(jnp.abs(peer_u[l])))
        s_v = _pow2_scale(jnp.max(jnp.abs(peer_v[l])))
        u_norm = jnp.sqrt(jnp.max(jnp.sum(jnp.square(peer_u[l]), axis=1)))
        peer_scales = jnp.broadcast_to(
            jnp.stack([1.0 / s_u, 1.0 / s_v, u_norm] + [jnp.zeros((), F32)] * 5)[:, None], (8, LANES))
        x1, h2t, st, scl = _mid(o_nsa, o_ret, xf, gate1, scale2, shift2, g_norm_ffn[l][None, :],
                                w_out[l].astype(BF16), peer_w_q[l].T.astype(BF16), sub_keys, peer_scales, seq)

        lrow, w1, r2, w2 = _peer_select(st, scl)
        peer_t = _peer_expert(h2t, (peer_u[l] * s_u).astype(FP8),
                              _transpose_fp8(peer_v[l], jnp.full((8, LANES), s_v, F32)),
                              lrow, w1, r2, w2, scl)
        xf = _final(x1, peer_t, scl, gate2, g_norm_final[None, :], seq, apply_norm=(l == depth - 1))
    return xf.reshape(batch, seq, d)
```

```python
import functools
import math

import numpy as np
import jax
import jax.numpy as jnp
from jax import lax
from jax.experimental import pallas as pl
from jax.experimental.pallas import tpu as pltpu

F32 = jnp.float32
BF16 = jnp.bfloat16
FP8 = jnp.float8_e4m3fn
FP8_TARGET = 224.0

D_MODEL = 2048
N_HEADS = 16
HEAD_DIM = 64
KV_GROUPS = 4
GROUP = 4
CMP_LEN = 32
CMP_STRIDE = 16
CMP_HIDDEN = 128
SLC_BLOCK = 64
SLC_TOPK = 16
WINDOW = 512
FORCE_SCORE = 1e4
NEG = -1e30
RET_HEADS = 8
RET_DIM = 128
RET_CHUNK = 128
PEER_HEADS = 8
PEER_KEYS = 128
PEER_EXPERTS = PEER_KEYS * PEER_KEYS
PEER_TOPK = 16
COEF_BOUND_FACTOR = 1.25 * PEER_HEADS
EPS = 1e-6
LOG2E = 1.4426950408889634

LANES = 128
TQ = 128
TK_SEL = 256
TM_PROJ = 256
TM_PEER = 512
TE_PEER = 1024
TM_SEL = 256
SEL_HEADS_PER_STEP = 4
RET_TILE = 512

_NT = (((1,), (1,)), ((), ()))
_TN = (((0,), (0,)), ((), ()))


def _params(sem, vmem_mb):
    return pltpu.CompilerParams(dimension_semantics=sem, vmem_limit_bytes=vmem_mb * 1024 * 1024)


def _resident(shape, index_map):
    return pl.BlockSpec(shape, index_map, pipeline_mode=pl.Buffered(1))


def _gelu(x):
    return jax.nn.gelu(x)


def _adaln_body(c_ref, w_ref, b_ref, o_ref):
    c = c_ref[...]
    act = (c * jax.nn.sigmoid(c)).astype(BF16)
    o_ref[...] = jnp.dot(act, w_ref[...].astype(BF16), preferred_element_type=F32) + b_ref[...]


def _adaln(c_pad, w, b):
    n = w.shape[1]
    tn = 1536
    return pl.pallas_call(
        _adaln_body,
        grid=(n // tn,),
        in_specs=[pl.BlockSpec((8, D_MODEL), lambda j: (0, 0)),
                  pl.BlockSpec((D_MODEL, tn), lambda j: (0, j)),
                  pl.BlockSpec((1, tn), lambda j: (0, j))],
        out_specs=pl.BlockSpec((8, tn), lambda j: (0, j)),
        out_shape=jax.ShapeDtypeStruct((8, n), F32),
        compiler_params=_params(("arbitrary",), 40),
        name="adaln",
    )(c_pad, w, b)


STD_COLS = 512 * 3 + 1024 * 4
TR_ROWS = 1024 + 256 + 256 + 64


def _inproj_body(x_ref, sc_ref, sh_ref, gn_ref, wstd_ref, wt_ref,
                 cv_ref, ks_ref, kw_ref, qr_ref, kr_ref, vr_ref, gr_ref,
                 qt_ref, vst_ref, vwt_ref, gt_ref, *, tiles_per_seq):
    tm = TM_PROJ
    i = pl.program_id(0)
    x = x_ref[...]
    ms = jnp.mean(x * x, axis=-1, keepdims=True)
    h = x * lax.rsqrt(ms + EPS) * gn_ref[...]
    h = h * (1.0 + sc_ref[0]) + sh_ref[0]
    hb = h.astype(BF16)

    def std(a, b):
        return jnp.dot(hb, wstd_ref[:, a:b], preferred_element_type=F32)

    y = std(0, 512)
    for g in range(KV_GROUPS):
        cv_ref[g] = y[:, g * LANES:(g + 1) * LANES].astype(BF16)

    t = (i % tiles_per_seq) * tm + lax.broadcasted_iota(jnp.int32, (tm, LANES), 0)
    lane = lax.broadcasted_iota(jnp.int32, (tm, LANES), 1)
    pos_hi = ((t >> 6) << 6).astype(F32)
    pos_lo = (t & 63).astype(F32)
    aug = jnp.where((lane == 64) | (lane == 66), pos_hi,
                    jnp.where((lane == 65) | (lane == 67), pos_lo, 0.0))
    for ref, off in ((ks_ref, 512), (kw_ref, 1024)):
        y = std(off, off + 512)
        for g in range(KV_GROUPS):
            ref[:, g * LANES:(g + 1) * LANES] = (y[:, g * LANES:(g + 1) * LANES] + aug).astype(BF16)

    for ref, off in ((qr_ref, 1536), (kr_ref, 2560), (vr_ref, 3584), (gr_ref, 4608)):
        ref[...] = std(off, off + 1024).astype(BF16)

    def tr(a, b):
        return lax.dot_general(wt_ref[a:b, :], hb, _NT, preferred_element_type=F32)

    qt = tr(0, 1024) * (HEAD_DIM ** -0.5 * LOG2E)
    vst = tr(1024, 1280)
    vwt = tr(1280, 1536)
    gt = tr(1536, 1600)
    for c in range(tm // LANES):
        sl = slice(c * LANES, (c + 1) * LANES)
        qt_ref[c] = qt[:, sl].astype(BF16)
        vst_ref[c] = vst[:, sl].astype(BF16)
        vwt_ref[c] = vwt[:, sl].astype(BF16)
        gt_ref[c] = gt[:, sl]


def _inproj(x2, scale1, shift1, g_mix, w_std, w_tr, seq):
    t_tokens = x2.shape[0]
    tm = TM_PROJ
    tps = seq // tm
    nt = t_tokens // tm
    c = tm // LANES
    row = lambda i: (i, 0)
    per_b = lambda i: (i // tps, 0, 0)
    out_shape = (
        jax.ShapeDtypeStruct((KV_GROUPS, t_tokens, LANES), BF16),
        jax.ShapeDtypeStruct((t_tokens, 512), BF16),
        jax.ShapeDtypeStruct((t_tokens, 512), BF16),
        jax.ShapeDtypeStruct((t_tokens, 1024), BF16),
        jax.ShapeDtypeStruct((t_tokens, 1024), BF16),
        jax.ShapeDtypeStruct((t_tokens, 1024), BF16),
        jax.ShapeDtypeStruct((t_tokens, 1024), BF16),
        jax.ShapeDtypeStruct((t_tokens // LANES, 1024, LANES), BF16),
        jax.ShapeDtypeStruct((t_tokens // LANES, 256, LANES), BF16),
        jax.ShapeDtypeStruct((t_tokens // LANES, 256, LANES), BF16),
        jax.ShapeDtypeStruct((t_tokens // LANES, 64, LANES), F32),
    )
    out_specs = (
        pl.BlockSpec((KV_GROUPS, tm, LANES), lambda i: (0, i, 0)),
        pl.BlockSpec((tm, 512), row),
        pl.BlockSpec((tm, 512), row),
        pl.BlockSpec((tm, 1024), row),
        pl.BlockSpec((tm, 1024), row),
        pl.BlockSpec((tm, 1024), row),
        pl.BlockSpec((tm, 1024), row),
        pl.BlockSpec((c, 1024, LANES), lambda i: (i, 0, 0)),
        pl.BlockSpec((c, 256, LANES), lambda i: (i, 0, 0)),
        pl.BlockSpec((c, 256, LANES), lambda i: (i, 0, 0)),
        pl.BlockSpec((c, 64, LANES), lambda i: (i, 0, 0)),
    )
    return pl.pallas_call(
        functools.partial(_inproj_body, tiles_per_seq=tps),
        grid=(nt,),
        in_specs=[pl.BlockSpec((tm, D_MODEL), row),
                  pl.BlockSpec((1, 1, D_MODEL), per_b),
                  pl.BlockSpec((1, 1, D_MODEL), per_b),
                  _resident((1, D_MODEL), lambda i: (0, 0)),
                  _resident((D_MODEL, STD_COLS), lambda i: (0, 0)),
                  _resident((TR_ROWS, D_MODEL), lambda i: (0, 0))],
        out_specs=out_specs,
        out_shape=out_shape,
        compiler_params=_params(("parallel",), 56),
        name="inproj",
    )(x2, scale1, shift1, g_mix, w_std, w_tr)


def _compress_body(x_ref, wa_ref, wb_ref, pea_ref, peb_ref, w2k_ref, w2vt_ref, kcp_ref, vct_ref):
    x = x_ref[0, 0]
    n_rows = x.shape[0]
    p = jnp.dot(x, wa_ref[...], preferred_element_type=F32)
    q = jnp.dot(x, wb_ref[...], preferred_element_type=F32)
    pe = (jnp.dot(pea_ref[...], wa_ref[...], preferred_element_type=F32)
          + jnp.dot(peb_ref[...], wb_ref[...], preferred_element_type=F32))[0:1, :]
    pre = p + pltpu.roll(q, n_rows - 1, 0) + pe
    hid = _gelu(pre).astype(BF16)
    kc = jnp.dot(hid, w2k_ref[...], preferred_element_type=F32)
    n = lax.broadcasted_iota(jnp.int32, (n_rows, LANES), 0)
    lane = lax.broadcasted_iota(jnp.int32, (n_rows, LANES), 1)
    ce = n * CMP_STRIDE + (CMP_LEN - 1)
    ce_hi = ((ce >> 6) << 6).astype(F32)
    ce_lo = (ce & 63).astype(F32)
    aug = jnp.where((lane == 64) | (lane == 66), ce_hi,
                    jnp.where((lane == 65) | (lane == 67), ce_lo, 0.0))
    kcp_ref[0] = (kc + aug).astype(BF16)
    vct_ref[0] = lax.dot_general(w2vt_ref[...], hid, _NT, preferred_element_type=F32).astype(BF16)


def _compress(cv4, wa, wb, pea, peb, w2k, w2vt):
    g_, b_, n_rows, _ = cv4.shape
    const2 = lambda n: (0, 0)
    return pl.pallas_call(
        _compress_body,
        grid=(b_ * g_,),
        in_specs=[pl.BlockSpec((1, 1, n_rows, 2048), lambda n: (n % KV_GROUPS, n // KV_GROUPS, 0, 0)),
                  pl.BlockSpec((2048, 256), const2),
                  pl.BlockSpec((2048, 256), const2),
                  pl.BlockSpec((8, 2048), const2),
                  pl.BlockSpec((8, 2048), const2),
                  pl.BlockSpec((256, LANES), const2),
                  pl.BlockSpec((64, 256), const2)],
        out_specs=(pl.BlockSpec((1, n_rows, LANES), lambda n: (n, 0, 0)),
                   pl.BlockSpec((1, 64, n_rows), lambda n: (n, 0, 0))),
        out_shape=(jax.ShapeDtypeStruct((b_ * g_, n_rows, LANES), BF16),
                   jax.ShapeDtypeStruct((b_ * g_, 64, n_rows), BF16)),
        compiler_params=_params(("parallel",), 32),
        name="nsa_compress",
    )(cv4, wa, wb, pea, peb, w2k, w2vt)


def _softmax_step(state, s, pv_prev):
    m_i, l_i, acc = state
    m_new = jnp.maximum(m_i, jnp.max(s, axis=0, keepdims=True))
    alpha = jnp.exp2(m_i - m_new)
    p = jnp.exp2(s - m_new)
    l_new = alpha * l_i + jnp.sum(p, axis=0, keepdims=True)
    return (m_new, l_new, (acc + pv_prev) * alpha), p.astype(BF16)


def _nsa_body(qt_ref, gt_ref, kcp_ref, vct_ref, ks_ref, kw_ref, vst_ref, vwt_ref,
              qaug_ref, gout_ref, ovl_ref, grp_ref, o_ref, selb_ref, sa_ref, sb_ref, pa_ref, pb_ref, *, n_cmp):
    qi = pl.program_id(2)
    t0 = qi * TQ
    wq = GROUP * TQ

    qp = jnp.concatenate(
        [jnp.concatenate([qt_ref[0, r * HEAD_DIM:(r + 1) * HEAD_DIM, :], qaug_ref[r]], axis=0)
         for r in range(GROUP)], axis=1)

    lane_q = lax.broadcasted_iota(jnp.int32, (1, wq), 1) & (TQ - 1)
    n_win = WINDOW // TQ + 1
    k_lo = pl.multiple_of(jnp.maximum(t0 - WINDOW, 0), TQ)

    s = jnp.dot(kcp_ref[0], qp, preferred_element_type=F32)
    s_w = jnp.dot(kw_ref[pl.ds(k_lo, n_win * TQ), :], qp, preferred_element_type=F32)
    n_io = lax.broadcasted_iota(jnp.int32, (n_cmp, wq), 0)
    tl = lax.broadcasted_iota(jnp.int32, (n_cmp, wq), 1) & (TQ - 1)
    valid = (n_io * CMP_STRIDE + (CMP_LEN - 1)) <= (t0 + tl)
    s = jnp.where(valid, s, NEG)
    m = jnp.maximum(jnp.max(s, axis=0, keepdims=True), 0.5 * NEG)
    p = jnp.exp2(s - m)
    l = jnp.sum(p, axis=0, keepdims=True)
    pn = p * (1.0 / jnp.maximum(l, 1e-30))
    o_c = jnp.dot(vct_ref[0], pn.astype(BF16), preferred_element_type=F32)

    ps = pn[:, 0:TQ] + pn[:, TQ:2 * TQ] + pn[:, 2 * TQ:3 * TQ] + pn[:, 3 * TQ:4 * TQ]
    hi = ps.astype(BF16)
    r1 = ps - hi.astype(F32)
    mid = r1.astype(BF16)
    lo = (r1 - mid.astype(F32)).astype(BF16)
    ovl = ovl_ref[...]
    imp = (jnp.dot(ovl, hi, preferred_element_type=F32)
           + jnp.dot(ovl, mid, preferred_element_type=F32)
           + jnp.dot(ovl, lo, preferred_element_type=F32))

    n_slc = imp.shape[0]
    m_io = lax.broadcasted_iota(jnp.int32, (n_slc, TQ), 0)
    q_io = lax.broadcasted_iota(jnp.int32, (n_slc, TQ), 1)
    back = ((t0 + q_io) >> 6) - m_io
    valid_s = back >= 0
    forced = valid_s & ((m_io == 0) | (back < 2))
    w = jnp.where(forced, -jnp.inf, jnp.where(valid_s, imp, -1.0))
    selb = jnp.where(forced, 0.0, NEG)

    def pick(carry, lanes=None):
        w, selb = carry
        mx = jnp.max(w, axis=0, keepdims=True)
        idx = jnp.min(jnp.where(w == mx, m_io, n_slc), axis=0, keepdims=True)
        hit = m_io == idx
        if lanes is not None:
            hit = hit & lanes
        return jnp.where(hit, -jnp.inf, w), jnp.where(hit, 0.0, selb)

    carry = (w, selb)
    for _ in range(SLC_TOPK - 3):
        carry = pick(carry)
    carry = pick(carry, lanes=(t0 + q_io) < 2 * SLC_BLOCK)
    _, selb = pick(carry, lanes=(t0 + q_io) < SLC_BLOCK)
    for mblk in range(n_slc):
        selb_ref[mblk] = jnp.broadcast_to(selb[mblk:mblk + 1, :], (8, TQ))

    per_blk = TK_SEL // SLC_BLOCK

    def scores(j):
        k0 = pl.multiple_of(j * TK_SEL, TK_SEL)
        bias = jnp.concatenate(
            [jnp.tile(selb_ref[j * per_blk + u], (SLC_BLOCK // 8, 1)) for u in range(per_blk)], axis=0)
        bias = jnp.concatenate([bias] * GROUP, axis=1)
        return jnp.dot(ks_ref[pl.ds(k0, TK_SEL), :], qp, preferred_element_type=F32) + bias

    def vt_tile(j):
        return jnp.concatenate([vst_ref[2 * j], vst_ref[2 * j + 1]], axis=1)

    def causal(j):
        kr = lax.broadcasted_iota(jnp.int32, (TK_SEL, wq), 0)
        return kr <= (t0 - j * TK_SEL) + lane_q

    def pv_dot(j, p_ref):
        return jnp.dot(vt_tile(j), p_ref[...], preferred_element_type=F32)

    jd = qi // (TK_SEL // TQ)

    sa_ref[...] = scores(0)
    pb_ref[...] = jnp.zeros_like(pb_ref)
    sel01 = jnp.where(selb == 0.0, 1.0, 0.0).astype(BF16)
    tile_any = jnp.max(jnp.dot(grp_ref[...], sel01, preferred_element_type=F32), axis=1, keepdims=True)
    j_io = lax.broadcasted_iota(jnp.int32, tile_any.shape, 0)
    q_lo_v = jnp.max(jnp.where((tile_any == 0.0) & (j_io <= jd), j_io, -1), axis=0, keepdims=True) + 1
    n_pre_v = jnp.max(jnp.where((tile_any > 0.0) & (j_io < q_lo_v), j_io, -1), axis=0, keepdims=True) + 1

    d = (t0 - k_lo) + lane_q - lax.broadcasted_iota(jnp.int32, (n_win * TQ, wq), 0)
    s_w = jnp.where(lax.bitcast_convert_type(d, jnp.uint32) < WINDOW, s_w, NEG)
    p = jnp.exp2(s_w - jnp.max(s_w, axis=0, keepdims=True))
    l_w = jnp.sum(p, axis=0, keepdims=True)
    vt_w = jnp.concatenate([vwt_ref[k_lo // TQ + u] for u in range(n_win)], axis=1)
    o_w = jnp.dot(vt_w, p.astype(BF16), preferred_element_type=F32) * (1.0 / l_w)

    q_lo = q_lo_v[0, 0]
    n_pre = n_pre_v[0, 0]
    n_vis = n_pre + jd - q_lo + 1

    def tile_at(pos):
        return jnp.where(pos < n_pre, pos, pos - n_pre + q_lo)

    state = (jnp.full((1, wq), NEG, F32), jnp.zeros((1, wq), F32), jnp.zeros((HEAD_DIM, wq), F32))

    def pair(u, state):
        a = 2 * u
        sb_ref[...] = scores(tile_at(a + 1))
        state, p = _softmax_step(state, sa_ref[...], pv_dot(tile_at(jnp.maximum(a - 1, 0)), pb_ref))
        pa_ref[...] = p
        sa_ref[...] = scores(tile_at(a + 2))
        state, p = _softmax_step(state, sb_ref[...], pv_dot(tile_at(a), pa_ref))
        pb_ref[...] = p
        return state

    n_pair = (n_vis - 1) // 2
    state = lax.fori_loop(0, n_pair, pair, state)
    x = 2 * n_pair
    y = jnp.minimum(x + 1, n_vis - 1)
    tx, ty = tile_at(x), tile_at(y)
    sb_ref[...] = scores(ty)
    state, p = _softmax_step(state, jnp.where(causal(tx), sa_ref[...], NEG),
                             pv_dot(tile_at(jnp.maximum(x - 1, 0)), pb_ref))
    pa_ref[...] = p
    ty_mask = jnp.where(x + 1 < n_vis, ty, jd + 1)
    (_, l_s, acc_s), p = _softmax_step(state, jnp.where(causal(ty_mask), sb_ref[...], NEG), pv_dot(tx, pa_ref))
    acc_s = acc_s + jnp.dot(vt_tile(ty), p, preferred_element_type=F32)

    o_s = acc_s * (1.0 / l_s)
    gw = jax.nn.sigmoid(gt_ref[0])
    outs = []
    for r in range(GROUP):
        sl = slice(r * TQ, (r + 1) * TQ)
        o = (gw[r:r + 1, :] * o_c[:, sl] + gw[GROUP + r:GROUP + r + 1, :] * o_s[:, sl]
             + gw[2 * GROUP + r:2 * GROUP + r + 1, :] * o_w[:, sl])
        ms = jnp.mean(o * o, axis=0, keepdims=True)
        outs.append(o * lax.rsqrt(ms + EPS) * gout_ref[r])
    o_ref[...] = jnp.concatenate(outs, axis=0).T.astype(BF16)


def _nsa(qt, gt, kcp, vct, ks, kw, vst, vwt, qaug, gout_b, ovl_t, grp, batch, seq):
    nq = seq // TQ
    n_cmp = kcp.shape[1]
    n_slc = seq // SLC_BLOCK
    t_tokens = batch * seq
    per_b_chunks = seq // LANES
    return pl.pallas_call(
        functools.partial(_nsa_body, n_cmp=n_cmp),
        grid=(batch, KV_GROUPS, nq),
        in_specs=[
            pl.BlockSpec((1, 256, LANES), lambda b, g, q: (b * nq + q, g, 0)),
            pl.BlockSpec((1, 16, LANES), lambda b, g, q: (b * nq + q, g, 0)),
            pl.BlockSpec((1, n_cmp, LANES), lambda b, g, q: (b * KV_GROUPS + g, 0, 0)),
            pl.BlockSpec((1, 64, n_cmp), lambda b, g, q: (b * KV_GROUPS + g, 0, 0)),
            pl.BlockSpec((seq, LANES), lambda b, g, q: (b, g)),
            pl.BlockSpec((seq, LANES), lambda b, g, q: (b, g)),
            pl.BlockSpec((per_b_chunks, 64, LANES), lambda b, g, q: (b, g, 0)),
            pl.BlockSpec((per_b_chunks, 64, LANES), lambda b, g, q: (b, g, 0)),
            pl.BlockSpec((GROUP, 64, LANES), lambda b, g, q: (g, 0, 0)),
            pl.BlockSpec((GROUP, 64, LANES), lambda b, g, q: (g, 0, 0)),
            pl.BlockSpec((n_slc, n_cmp), lambda b, g, q: (0, 0)),
            pl.BlockSpec(grp.shape, lambda b, g, q: (0, 0)),
        ],
        out_specs=pl.BlockSpec((TQ, 256), lambda b, g, q: (b * nq + q, g)),
        out_shape=jax.ShapeDtypeStruct((t_tokens, 1024), BF16),
        scratch_shapes=[pltpu.VMEM((n_slc, 8, TQ), F32),
                        pltpu.VMEM((TK_SEL, GROUP * TQ), F32), pltpu.VMEM((TK_SEL, GROUP * TQ), F32),
                        pltpu.VMEM((TK_SEL, GROUP * TQ), BF16), pltpu.VMEM((TK_SEL, GROUP * TQ), BF16)],
        compiler_params=_params(("parallel", "parallel", "arbitrary"), 40),
        name="nsa_attention",
    )(qt, gt, kcp, vct, ks, kw, vst, vwt, qaug, gout_b, ovl_t, grp)


def _ret_body(q_ref, k_ref, v_ref, g_ref, dm_ref, kd_ref, qd_ref, cd_ref, go_ref, o_ref, st_ref):
    @pl.when(pl.program_id(2) == 0)
    def _():
        st_ref[...] = jnp.zeros_like(st_ref)

    c_ = RET_CHUNK
    slices = [slice(c * c_, (c + 1) * c_) for c in range(RET_TILE // c_)]
    atts, kvs = [], []
    for sl in slices:
        k = k_ref[sl, :]
        atts.append((lax.dot_general(q_ref[sl, :], k, _NT, preferred_element_type=F32) * dm_ref[0]).astype(BF16))
        kdec = (k.astype(F32) * kd_ref[0]).astype(BF16)
        kvs.append(lax.dot_general(kdec, v_ref[sl, :], _TN, preferred_element_type=F32))
    states = [st_ref[...]]
    for kv in kvs:
        states.append(states[-1] * cd_ref[0] + kv)
    st_ref[...] = states[-1]
    for sl, att, state in zip(slices, atts, states):
        o = (jnp.dot(att, v_ref[sl, :], preferred_element_type=F32)
             + qd_ref[0] * jnp.dot(q_ref[sl, :], state.astype(BF16), preferred_element_type=F32))
        mu = jnp.mean(o, axis=-1, keepdims=True)
        oc = o - mu
        var = jnp.mean(oc * oc, axis=-1, keepdims=True)
        y = oc * lax.rsqrt(var + EPS) * go_ref[0, 0:1, :]
        gate = g_ref[sl, :].astype(F32)
        o_ref[sl, :] = (gate * jax.nn.sigmoid(gate) * y).astype(BF16)


def _retention(q_r, k_r, v_r, g_r, dm, kd, qd, cd, go, batch, seq):
    t_tokens = batch * seq
    nc = seq // RET_TILE
    tok = lambda b, h, c: (b * nc + c, h)
    per_h = lambda b, h, c: (h, 0, 0)
    sq = (1, RET_DIM, RET_DIM)
    return pl.pallas_call(
        _ret_body,
        grid=(batch, RET_HEADS, nc),
        in_specs=[pl.BlockSpec((RET_TILE, RET_DIM), tok)] * 4
        + [pl.BlockSpec(sq, per_h)] * 4 + [pl.BlockSpec((1, 8, RET_DIM), per_h)],
        out_specs=pl.BlockSpec((RET_TILE, RET_DIM), tok),
        out_shape=jax.ShapeDtypeStruct((t_tokens, RET_HEADS * RET_DIM), BF16),
        scratch_shapes=[pltpu.VMEM((RET_DIM, RET_DIM), F32)],
        compiler_params=_params(("parallel", "parallel", "arbitrary"), 32),
        name="retention",
    )(q_r, k_r, v_r, g_r, dm, kd, qd, cd, go)


def _pow2_scale(magnitude):
    return jnp.exp2(jnp.floor(jnp.log2(FP8_TARGET / jnp.maximum(magnitude, 1e-30))))


def _mid_body(on_ref, or_ref, x_ref, g1_ref, sc_ref, sh_ref, gn_ref, wo_ref, wqt_ref, sk_ref, ps_ref,
              x1_ref, h2t_ref, st_ref, scl_ref):
    acc = (jnp.dot(on_ref[...], wo_ref[0:1024, :], preferred_element_type=F32)
           + jnp.dot(or_ref[...], wo_ref[1024:2048, :], preferred_element_type=F32))
    x1 = x_ref[...] + g1_ref[0] * acc
    x1_ref[...] = x1
    ms = jnp.mean(x1 * x1, axis=-1, keepdims=True)
    h2 = x1 * lax.rsqrt(ms + EPS) * gn_ref[...]
    h2 = h2 * (1.0 + sc_ref[0]) + sh_ref[0]

    h2_t = h2.T
    amax = jnp.max(jnp.max(jnp.abs(h2_t), axis=0, keepdims=True), axis=1, keepdims=True)
    s_h = _pow2_scale(amax)
    h2t_ref[...] = (h2_t * s_h).astype(FP8)
    norm = jnp.sqrt(jnp.sum(h2_t * h2_t, axis=0, keepdims=True))
    s_c = _pow2_scale(COEF_BOUND_FACTOR * ps_ref[2:3, 0:1] * norm)
    scl_ref[...] = jnp.concatenate(
        [jnp.broadcast_to(ps_ref[0:1, 0:1] / s_h, s_c.shape), s_c, ps_ref[1:2, 0:1] / s_c,
         jnp.zeros((5, s_c.shape[1]), F32)], axis=0)
    h2 = h2.astype(BF16)
    qt = lax.dot_general(wqt_ref[...], h2, _NT, preferred_element_type=F32).astype(BF16)
    for hp in range(2 * PEER_HEADS):
        st_ref[hp] = jnp.dot(sk_ref[hp], qt[hp * 128:(hp + 1) * 128, :], preferred_element_type=F32)


def _mid(o_nsa, o_ret, x2, gate1, scale2, shift2, g_ffn, w_out, wq_t, sub_keys, peer_scales, seq):
    t_tokens = x2.shape[0]
    tm = TM_PROJ
    tps = seq // tm
    row = lambda i: (i, 0)
    per_b = lambda i: (i // tps, 0, 0)
    return pl.pallas_call(
        _mid_body,
        grid=(t_tokens // tm,),
        in_specs=[pl.BlockSpec((tm, 1024), row),
                  pl.BlockSpec((tm, 1024), row),
                  pl.BlockSpec((tm, D_MODEL), row),
                  pl.BlockSpec((1, 1, D_MODEL), per_b),
                  pl.BlockSpec((1, 1, D_MODEL), per_b),
                  pl.BlockSpec((1, 1, D_MODEL), per_b),
                  _resident((1, D_MODEL), lambda i: (0, 0)),
                  _resident((D_MODEL, D_MODEL), lambda i: (0, 0)),
                  _resident((D_MODEL, D_MODEL), lambda i: (0, 0)),
                  _resident((2 * PEER_HEADS, PEER_KEYS, 128), lambda i: (0, 0, 0)),
                  _resident((8, LANES), lambda i: (0, 0))],
        out_specs=(pl.BlockSpec((tm, D_MODEL), row),
                   pl.BlockSpec((D_MODEL, tm), lambda i: (0, i)),
                   pl.BlockSpec((2 * PEER_HEADS, PEER_KEYS, tm), lambda i: (0, 0, i)),
                   pl.BlockSpec((8, tm), lambda i: (0, i))),
        out_shape=(jax.ShapeDtypeStruct((t_tokens, D_MODEL), F32),
                   jax.ShapeDtypeStruct((D_MODEL, t_tokens), FP8),
                   jax.ShapeDtypeStruct((2 * PEER_HEADS, PEER_KEYS, t_tokens), F32),
                   jax.ShapeDtypeStruct((8, t_tokens), F32)),
        compiler_params=_params(("parallel",), 48),
        name="outproj_peerq",
    )(o_nsa, o_ret, x2, gate1, scale2, shift2, g_ffn, w_out, wq_t, sub_keys, peer_scales)


def _top16(s, break_ties):
    n_rows, n = s.shape
    io = lax.broadcasted_iota(jnp.int32, (n_rows, n), 0)
    a_io = lax.broadcasted_iota(jnp.int32, (PEER_TOPK, n), 0)
    rank = jnp.full((n_rows, n), PEER_TOPK, jnp.int32)
    vals = jnp.zeros((PEER_TOPK, n), F32)
    for a in range(PEER_TOPK):
        mx = jnp.max(s, axis=0, keepdims=True)
        hit = s == mx
        if break_ties:
            hit = io == jnp.min(jnp.where(hit, io, n_rows), axis=0, keepdims=True)
        rank = jnp.where(hit, a, rank)
        s = jnp.where(hit, -jnp.inf, s)
        vals = jnp.where(a_io == a, mx, vals)
    return vals, rank


def _peer_select_body(s_ref, scl_ref, l_ref, w1_ref, r2_ref, w2_ref):
    n = s_ref.shape[2]

    def select(hh, break_ties):
        s1 = s_ref[2 * hh]
        s2 = s_ref[2 * hh + 1]
        v1, rank1 = _top16(s1, break_ties)
        v2, rank2 = _top16(s2, break_ties)
        a_io = lax.broadcasted_iota(jnp.int32, (PEER_TOPK, n), 0)
        cnt = jnp.zeros((PEER_TOPK, n), jnp.int32)
        cur = v1 + v2[0:1, :]
        top = v1[0:1, :] + v2[0:1, :]
        z = jnp.zeros((1, n), F32)
        for _ in range(PEER_TOPK):
            mx = jnp.max(cur, axis=0, keepdims=True)
            aidx = jnp.min(jnp.where(cur == mx, a_io, PEER_TOPK), axis=0, keepdims=True)
            hit = a_io == aidx
            cnt = cnt + hit.astype(jnp.int32)
            nxt = jnp.sum(jnp.where(hit, cnt, 0), axis=0, keepdims=True)
            nv = jnp.max(jnp.where(a_io == nxt, v2, -jnp.inf), axis=0, keepdims=True)
            cur = jnp.where(hit, v1 + nv, cur)
            z = z + jnp.exp(mx - top)
        cnt_b = cnt.astype(F32).astype(BF16)
        rank_b = rank1.astype(F32).astype(BF16)
        lrow = jnp.zeros(s1.shape, BF16)
        for a in range(PEER_TOPK):
            lrow = jnp.where(rank_b == a, jnp.broadcast_to(cnt_b[a:a + 1, :], s1.shape), lrow)
        l_ref[hh] = lrow.astype(F32)
        w1_ref[hh] = jnp.exp(s1 - v1[0:1, :])
        r2_ref[hh] = rank2.astype(F32).astype(BF16)
        w2_ref[hh] = (jnp.exp(s2 - v2[0:1, :]) * (scl_ref[1:2, :] / z)).astype(BF16)
        return (jnp.sum((rank1 < PEER_TOPK).astype(jnp.int32), axis=0, keepdims=True)
                + jnp.sum((rank2 < PEER_TOPK).astype(jnp.int32), axis=0, keepdims=True))

    marked = functools.reduce(jnp.maximum, [select(hh, False) for hh in range(SEL_HEADS_PER_STEP)])

    @pl.when(jnp.max(marked) != 2 * PEER_TOPK)
    def _():
        for hh in range(SEL_HEADS_PER_STEP):
            select(hh, True)


def _peer_select(st, scl):
    t_tokens = st.shape[2]
    tm = TM_SEL
    shp = jax.ShapeDtypeStruct((PEER_HEADS, PEER_KEYS, t_tokens), F32)
    shp_b = jax.ShapeDtypeStruct((PEER_HEADS, PEER_KEYS, t_tokens), BF16)
    hps = SEL_HEADS_PER_STEP
    spec = pl.BlockSpec((hps, PEER_KEYS, tm), lambda i, h: (h, 0, i))
    return pl.pallas_call(
        _peer_select_body,
        grid=(t_tokens // tm, PEER_HEADS // hps),
        in_specs=[pl.BlockSpec((2 * hps, PEER_KEYS, tm), lambda i, h: (h, 0, i)),
                  pl.BlockSpec((8, tm), lambda i, h: (0, i))],
        out_specs=(spec, spec, spec, spec),
        out_shape=(shp, shp, shp_b, shp_b),
        compiler_params=_params(("parallel", "parallel"), 32),
        name="peer_select",
    )(st, scl)


def _transpose_body(v_ref, s_ref, o_ref):
    o_ref[...] = (v_ref[...].T * s_ref[0:1, 0:1]).astype(FP8)


def _transpose_fp8(v, scale_tile):
    n, d = v.shape
    tn = 512
    return pl.pallas_call(
        _transpose_body,
        grid=(n // tn,),
        in_specs=[pl.BlockSpec((tn, d), lambda i: (i, 0)),
                  pl.BlockSpec((8, LANES), lambda i: (0, 0))],
        out_specs=pl.BlockSpec((d, tn), lambda i: (0, i)),
        out_shape=jax.ShapeDtypeStruct((d, n), FP8),
        compiler_params=_params(("parallel",), 32),
        name="transpose_v",
    )(v, scale_tile)


def _peer_expert_body(h2t_ref, u_ref, vt_ref, l_ref, w1_ref, r2_ref, w2_ref, scl_ref,
                      o_ref, ce_ref, co_ref, *, steps_per_tile):
    g = pl.program_id(0)
    sw = 256
    strips = [slice(c * sw, (c + 1) * sw) for c in range(TM_PEER // sw)]
    n_piece = TE_PEER // PEER_KEYS
    blk = D_MODEL // n_piece

    @pl.when(g == 0)
    def _():
        co_ref[...] = jnp.zeros_like(co_ref)

    @pl.when((g == 0) | ((g - 1) % steps_per_tile == 0))
    def _():
        o_ref[...] = jnp.zeros_like(o_ref)

    def run(c_new, c_old):
        def piece(j, carry):
            r0 = pl.multiple_of(j * PEER_KEYS, PEER_KEYS)
            d0 = pl.multiple_of(j * blk, blk)
            for ls in strips:
                coef = None
                for h in range(PEER_HEADS):
                    lrow = jnp.broadcast_to(l_ref[h, pl.ds(j, 1), ls], (16, sw)).astype(BF16)
                    w1row = jnp.broadcast_to(w1_ref[h, pl.ds(j, 1), ls], (16, sw)).astype(BF16)
                    lrow = jnp.tile(lrow, (PEER_KEYS // 16, 1))
                    w1row = jnp.tile(w1row, (PEER_KEYS // 16, 1))
                    term = jnp.where(r2_ref[h, :, ls] < lrow, w2_ref[h, :, ls] * w1row, jnp.zeros((), BF16))
                    coef = term if coef is None else coef + term
                a_t = jnp.dot(u_ref[pl.ds(r0, PEER_KEYS), :], h2t_ref[:, ls], preferred_element_type=F32)
                a_scale = jnp.broadcast_to(scl_ref[0:1, ls], (16, sw)).astype(BF16)
                act = _gelu(a_t.astype(BF16) * jnp.tile(a_scale, (PEER_KEYS // 16, 1)))
                c_new[pl.ds(r0, PEER_KEYS), ls] = (coef * act).astype(FP8)
                o_ref[pl.ds(d0, blk), ls] += jnp.dot(vt_ref[pl.ds(d0, blk), :], c_old[:, ls],
                                                     preferred_element_type=F32)
            return carry

        lax.fori_loop(0, n_piece, piece, 0, unroll=8)

    @pl.when(g % 2 == 0)
    def _():
        run(ce_ref, co_ref)

    @pl.when(g % 2 == 1)
    def _():
        run(co_ref, ce_ref)


def _peer_expert(h2t, u_b, v_t, lrow, w1, r2, w2, scl):
    t_tokens = h2t.shape[1]
    tm, te = TM_PEER, TE_PEER
    n_piece = te // PEER_KEYS
    ne = PEER_EXPERTS // te
    n_steps = (t_tokens // tm) * ne
    cur = lambda g: jnp.minimum(g, n_steps - 1)
    prev = lambda g: jnp.maximum(g - 1, 0)
    row_spec = pl.BlockSpec((PEER_HEADS, n_piece, tm), lambda g: (0, cur(g) % ne, cur(g) // ne))
    full_spec = pl.BlockSpec((PEER_HEADS, PEER_KEYS, tm), lambda g: (0, 0, cur(g) // ne))
    return pl.pallas_call(
        functools.partial(_peer_expert_body, steps_per_tile=ne),
        grid=(n_steps + 1,),
        in_specs=[pl.BlockSpec((D_MODEL, tm), lambda g: (0, cur(g) // ne)),
                  pl.BlockSpec((te, D_MODEL), lambda g: (cur(g) % ne, 0)),
                  pl.BlockSpec((D_MODEL, te), lambda g: (0, prev(g) % ne)),
                  row_spec, row_spec, full_spec, full_spec,
                  pl.BlockSpec((8, tm), lambda g: (0, cur(g) // ne))],
        out_specs=pl.BlockSpec((D_MODEL, tm), lambda g: (0, prev(g) // ne)),
        out_shape=jax.ShapeDtypeStruct((D_MODEL, t_tokens), F32),
        scratch_shapes=[pltpu.VMEM((te, tm), FP8), pltpu.VMEM((te, tm), FP8)],
        compiler_params=_params(("arbitrary",), 52),
        name="peer_experts",
    )(h2t, u_b, v_t, lrow, w1, r2, w2, scl)


def _final_body(x1_ref, pt_ref, scl_ref, g2_ref, gn_ref, o_ref, *, apply_norm):
    peer = (pt_ref[...] * scl_ref[2:3, :]).T
    y = x1_ref[...] + g2_ref[0] * peer
    if apply_norm:
        ms = jnp.mean(y * y, axis=-1, keepdims=True)
        y = y * lax.rsqrt(ms + EPS) * gn_ref[...]
    o_ref[...] = y


def _final(x1, peer_t, scl, gate2, g_final, seq, apply_norm):
    t_tokens = x1.shape[0]
    tm = TM_PROJ
    tps = seq // tm
    return pl.pallas_call(
        functools.partial(_final_body, apply_norm=apply_norm),
        grid=(t_tokens // tm,),
        in_specs=[pl.BlockSpec((tm, D_MODEL), lambda i: (i, 0)),
                  pl.BlockSpec((D_MODEL, tm), lambda i: (0, i)),
                  pl.BlockSpec((8, tm), lambda i: (0, i)),
                  pl.BlockSpec((1, 1, D_MODEL), lambda i: (i // tps, 0, 0)),
                  pl.BlockSpec((1, D_MODEL), lambda i: (0, 0))],
        out_specs=pl.BlockSpec((tm, D_MODEL), lambda i: (i, 0)),
        out_shape=jax.ShapeDtypeStruct((t_tokens, D_MODEL), F32),
        compiler_params=_params(("parallel",), 32),
        name="final_norm",
    )(x1, peer_t, scl, gate2, g_final)


def _split_cols(a, sizes):
    out, acc = [], 0
    for s in sizes:
        out.append(a[:, acc:acc + s])
        acc += s
    return out


def _inproj_weights(w_in):
    kvw = KV_GROUPS * HEAD_DIM
    sizes = (1024,) + (kvw,) * 6 + (3 * N_HEADS, 1024, 1024, 1024, 1024)
    q_a, k_c, v_c, k_s, v_s, k_w, v_w, g_a, q_r, k_r, v_r, g_r = _split_cols(w_in, sizes)
    d = w_in.shape[0]

    def grp(a, g):
        return a[:, g * HEAD_DIM:(g + 1) * HEAD_DIM]

    zeros = jnp.zeros((d, HEAD_DIM), w_in.dtype)
    cv = [jnp.concatenate([grp(k_c, g), grp(v_c, g)], axis=1) for g in range(KV_GROUPS)]
    ksp = [jnp.concatenate([grp(k_s, g), zeros], axis=1) for g in range(KV_GROUPS)]
    kwp = [jnp.concatenate([grp(k_w, g), zeros], axis=1) for g in range(KV_GROUPS)]
    w_std = jnp.concatenate(cv + ksp + kwp + [q_r, k_r, v_r, g_r], axis=1).astype(BF16)
    gcols = []
    for g in range(KV_GROUPS):
        for br in range(3):
            for r in range(GROUP):
                c = (g * GROUP + r) * 3 + br
                gcols.append(g_a[:, c:c + 1])
        gcols.append(jnp.zeros((d, 4), w_in.dtype))
    w_tr = jnp.concatenate([q_a, v_s, v_w] + gcols, axis=1).T.astype(BF16)
    return w_std, w_tr


def _compress_weights(pe_k, pe_v, k_w1, k_w2, v_w1, v_w2):
    half = CMP_LEN // 2

    def w1_half(w1k, w1v, lo):
        a = w1k.reshape(CMP_LEN, HEAD_DIM, CMP_HIDDEN)[lo:lo + half]
        b = w1v.reshape(CMP_LEN, HEAD_DIM, CMP_HIDDEN)[lo:lo + half]
        za = jnp.zeros_like(a)
        top = jnp.concatenate([a, za], axis=2)
        bot = jnp.concatenate([za, b], axis=2)
        return jnp.concatenate([top, bot], axis=1).reshape(half * 2 * HEAD_DIM, 2 * CMP_HIDDEN).astype(BF16)

    def pe_half(lo):
        row = jnp.concatenate([pe_k[lo:lo + half], pe_v[lo:lo + half]], axis=1).reshape(1, -1)
        return jnp.broadcast_to(row, (8, row.shape[1])).astype(BF16)

    wa = w1_half(k_w1, v_w1, 0)
    wb = w1_half(k_w1, v_w1, half)
    w2k = jnp.zeros((2 * CMP_HIDDEN, LANES), F32).at[:CMP_HIDDEN, :HEAD_DIM].set(k_w2).astype(BF16)
    w2vt = jnp.zeros((HEAD_DIM, 2 * CMP_HIDDEN), F32).at[:, CMP_HIDDEN:].set(v_w2.T).astype(BF16)
    return wa, wb, pe_half(0), pe_half(half), w2k, w2vt


def _nsa_constants(seq):
    slopes = jnp.exp2(-8.0 * (jnp.arange(N_HEADS, dtype=F32) + 1.0) / N_HEADS) * LOG2E
    s_hi = slopes.astype(BF16)
    s_lo = (slopes - s_hi.astype(F32)).astype(BF16)
    rows = jnp.zeros((N_HEADS, HEAD_DIM), BF16)
    rows = rows.at[:, 0].set(s_hi).at[:, 1].set(s_hi).at[:, 2].set(s_lo).at[:, 3].set(s_lo)
    qaug = jnp.broadcast_to(rows[:, :, None], (N_HEADS, HEAD_DIM, LANES))
    n_rows = seq // CMP_STRIDE
    n_slc = seq // SLC_BLOCK
    start = np.arange(n_rows)[:, None] * CMP_STRIDE
    end = start + CMP_LEN - 1
    blk = np.arange(n_slc)[None, :] * SLC_BLOCK
    ovl = ((start < blk + SLC_BLOCK) & (end >= blk)).astype(np.float32)
    per_tile = TK_SEL // SLC_BLOCK
    grp = (np.arange(n_slc)[None, :] // per_tile == np.arange(n_slc // per_tile)[:, None])
    return qaug, jnp.asarray(ovl.T, BF16), jnp.asarray(grp.astype(np.float32), BF16)


def _retention_constants():
    h, c = RET_HEADS, RET_CHUNK
    lg = jnp.log1p(-jnp.exp2(-5.0 - jnp.arange(h, dtype=F32)))
    pos = jnp.arange(c, dtype=F32)
    diff = pos[:, None] - pos[None, :]
    scale = RET_DIM ** -0.5
    dm = jnp.where(diff >= 0, jnp.exp(lg[:, None, None] * jnp.maximum(diff, 0.0)), 0.0) * scale
    k_decay = jnp.exp(lg[:, None] * (c - 1.0 - pos)) * scale
    q_decay = jnp.exp(lg[:, None] * (pos + 1.0))
    chunk_decay = jnp.exp(lg * c)
    kd = jnp.broadcast_to(k_decay[:, :, None], (h, c, RET_DIM))
    qd = jnp.broadcast_to(q_decay[:, :, None], (h, c, RET_DIM))
    cd = jnp.broadcast_to(chunk_decay[:, None, None], (h, RET_DIM, RET_DIM))
    return dm, kd, qd, cd


def kernel(x, c, w_ada, b_ada, g_norm_mix, g_norm_ffn, g_norm_final, w_in, cmp_pe_k, cmp_pe_v,
           cmp_k_w1, cmp_k_w2, cmp_v_w1, cmp_v_w2, g_nsa_out, g_ret_out, w_out,
           peer_w_q, peer_sub_keys, peer_u, peer_v):
    batch, seq, d = x.shape
    depth = w_ada.shape[0]
    t_tokens = batch * seq
    xf = x.reshape(t_tokens, d)
    c_pad = jnp.zeros((8, d), F32).at[:batch].set(c)
    qaug, ovl_t, grp = _nsa_constants(seq)
    dm, kd, qd, cd = _retention_constants()

    for l in range(depth):
        mod = _adaln(c_pad, w_ada[l], b_ada[l][None, :])[:batch].reshape(batch, 6, 1, d)
        shift1, scale1, gate1, shift2, scale2, gate2 = (mod[:, k] for k in range(6))

        w_std, w_tr = _inproj_weights(w_in[l])
        (cv, ks, kw, q_r, k_r, v_r, g_r, qt, vst, vwt, gt) = _inproj(
            xf, scale1, shift1, g_norm_mix[l][None, :], w_std, w_tr, seq)

        cv4 = cv.reshape(KV_GROUPS, batch, seq // CMP_STRIDE, CMP_STRIDE * LANES)
        kcp, vct = _compress(cv4, *_compress_weights(cmp_pe_k[l], cmp_pe_v[l], cmp_k_w1[l], cmp_k_w2[l],
                                                     cmp_v_w1[l], cmp_v_w2[l]))
        gout_b = jnp.broadcast_to(g_nsa_out[l][:, :, None], (N_HEADS, HEAD_DIM, LANES))
        o_nsa = _nsa(qt, gt, kcp, vct, ks, kw, vst, vwt, qaug, gout_b, ovl_t, grp, batch, seq)

        go = jnp.broadcast_to(g_ret_out[l][:, None, :], (RET_HEADS, 8, RET_DIM))
        o_ret = _retention(q_r, k_r, v_r, g_r, dm, kd, qd, cd, go, batch, seq)

        sub_keys = peer_sub_keys[l].reshape(2 * PEER_HEADS, PEER_KEYS, -1).astype(BF16)
        s_u = _pow2_scale(jnp.max(jnp.abs(peer_u[l])))
        s_v = _pow2_scale(jnp.max(jnp.abs(peer_v[l])))
        u_norm = jnp.sqrt(jnp.max(jnp.sum(jnp.square(peer_u[l]), axis=1)))
        peer_scales = jnp.broadcast_to(
            jnp.stack([1.0 / s_u, 1.0 / s_v, u_norm] + [jnp.zeros((), F32)] * 5)[:, None], (8, LANES))
        x1, h2t, st, scl = _mid(o_nsa, o_ret, xf, gate1, scale2, shift2, g_norm_ffn[l][None, :],
                                w_out[l].astype(BF16), peer_w_q[l].T.astype(BF16), sub_keys, peer_scales, seq)

        lrow, w1, r2, w2 = _peer_select(st, scl)
        peer_t = _peer_expert(h2t, (peer_u[l] * s_u).astype(FP8),
                              _transpose_fp8(peer_v[l], jnp.full((8, LANES), s_v, F32)),
                              lrow, w1, r2, w2, scl)
        xf = _final(x1, peer_t, scl, gate2, g_norm_final[None, :], seq, apply_norm=(l == depth - 1))
    return xf.reshape(batch, seq, d)
```

```python
import functools
import math

import numpy as np
import jax
import jax.numpy as jnp
from jax import lax
from jax.experimental import pallas as pl
from jax.experimental.pallas import tpu as pltpu

F32 = jnp.float32
BF16 = jnp.bfloat16
FP8 = jnp.float8_e4m3fn
FP8_TARGET = 224.0

D_MODEL = 2048
N_HEADS = 16
HEAD_DIM = 64
KV_GROUPS = 4
GROUP = 4
CMP_LEN = 32
CMP_STRIDE = 16
CMP_HIDDEN = 128
SLC_BLOCK = 64
SLC_TOPK = 16
WINDOW = 512
NEG = -1e30
RET_HEADS = 8
RET_DIM = 128
RET_CHUNK = 128
PEER_HEADS = 8
PEER_KEYS = 128
PEER_EXPERTS = PEER_KEYS * PEER_KEYS
PEER_TOPK = 16
COEF_BOUND_FACTOR = 1.25 * PEER_HEADS
EPS = 1e-6
LOG2E = 1.4426950408889634

LANES = 128
MXU_COLS = 256
TQ = 128
NSA_TILES_PER_STEP = 2
TK_SEL = 256
TM_PROJ = 256
TM_PEER = 512
TE_PEER = 1024
TM_SEL = 256
SEL_HEADS_PER_STEP = 4
RET_TILE = 512

_NT = (((1,), (1,)), ((), ()))
_TN = (((0,), (0,)), ((), ()))


def _params(sem, vmem_mb):
    return pltpu.CompilerParams(dimension_semantics=sem, vmem_limit_bytes=vmem_mb * 1024 * 1024)


def _resident(shape, index_map):
    return pl.BlockSpec(shape, index_map, pipeline_mode=pl.Buffered(1))


def _gelu(x):
    return jax.nn.gelu(x)


def _adaln_body(c_ref, w_ref, b_ref, o_ref):
    c = c_ref[...]
    act = (c * jax.nn.sigmoid(c)).astype(BF16)
    o_ref[...] = jnp.dot(act, w_ref[...].astype(BF16), preferred_element_type=F32) + b_ref[...]


def _adaln(c_pad, w, b):
    n = w.shape[1]
    tn = 1536
    return pl.pallas_call(
        _adaln_body,
        grid=(n // tn,),
        in_specs=[pl.BlockSpec((8, D_MODEL), lambda j: (0, 0)),
                  pl.BlockSpec((D_MODEL, tn), lambda j: (0, j)),
                  pl.BlockSpec((1, tn), lambda j: (0, j))],
        out_specs=pl.BlockSpec((8, tn), lambda j: (0, j)),
        out_shape=jax.ShapeDtypeStruct((8, n), F32),
        compiler_params=_params(("arbitrary",), 40),
        name="adaln",
    )(c_pad, w, b)


STD_COLS = 512 + 256 * 2 + 1024 * 4
TR_ROWS = 1024 + 256 + 256 + 64


def _inproj_body(x_ref, sc_ref, sh_ref, gn_ref, wstd_ref, wt_ref,
                 cv_ref, ks_ref, kw_ref, qr_ref, kr_ref, vr_ref, gr_ref,
                 qt_ref, vst_ref, vwt_ref, gt_ref, *, tiles_per_seq):
    tm = TM_PROJ
    i = pl.program_id(0)
    x = x_ref[...]
    ms = jnp.mean(x * x, axis=-1, keepdims=True)
    h = x * lax.rsqrt(ms + EPS) * gn_ref[...]
    h = h * (1.0 + sc_ref[0]) + sh_ref[0]
    hb = h.astype(BF16)

    def std(a, b):
        return jnp.dot(hb, wstd_ref[:, a:b], preferred_element_type=F32)

    y = std(0, 512)
    for g in range(KV_GROUPS):
        cv_ref[g] = y[:, g * LANES:(g + 1) * LANES].astype(BF16)

    t = (i % tiles_per_seq) * tm + lax.broadcasted_iota(jnp.int32, (tm, LANES), 0)
    lane = lax.broadcasted_iota(jnp.int32, (tm, LANES), 1)
    pos_hi = ((t >> 6) << 6).astype(F32)
    pos_lo = (t & 63).astype(F32)
    aug = jnp.where((lane == 64) | (lane == 66), pos_hi,
                    jnp.where((lane == 65) | (lane == 67), pos_lo, 0.0))
    for ref, off in ((ks_ref, 512), (kw_ref, 768)):
        y = std(off, off + 256)
        for g in range(KV_GROUPS):
            pair = y[:, (g // 2) * LANES:(g // 2 + 1) * LANES]
            keys = pair if g % 2 == 0 else pltpu.roll(pair, HEAD_DIM, 1)
            ref[:, g * LANES:(g + 1) * LANES] = jnp.where(lane < HEAD_DIM, keys, aug).astype(BF16)

    for ref, off in ((qr_ref, 1024), (kr_ref, 2048), (vr_ref, 3072), (gr_ref, 4096)):
        ref[...] = std(off, off + 1024).astype(BF16)

    def tr(a, b):
        return lax.dot_general(wt_ref[a:b, :], hb, _NT, preferred_element_type=F32)

    qt = tr(0, 1024) * (HEAD_DIM ** -0.5 * LOG2E)
    vst = tr(1024, 1280)
    vwt = tr(1280, 1536)
    gt = tr(1536, 1600)
    for c in range(tm // LANES):
        sl = slice(c * LANES, (c + 1) * LANES)
        qt_ref[c] = qt[:, sl].astype(BF16)
        vst_ref[c] = vst[:, sl].astype(BF16)
        vwt_ref[c] = vwt[:, sl].astype(BF16)
        gt_ref[c] = gt[:, sl]


def _inproj(x2, scale1, shift1, g_mix, w_std, w_tr, seq):
    t_tokens = x2.shape[0]
    tm = TM_PROJ
    tps = seq // tm
    nt = t_tokens // tm
    c = tm // LANES
    row = lambda i: (i, 0)
    per_b = lambda i: (i // tps, 0, 0)
    out_shape = (
        jax.ShapeDtypeStruct((KV_GROUPS, t_tokens, LANES), BF16),
        jax.ShapeDtypeStruct((t_tokens, 512), BF16),
        jax.ShapeDtypeStruct((t_tokens, 512), BF16),
        jax.ShapeDtypeStruct((t_tokens, 1024), BF16),
        jax.ShapeDtypeStruct((t_tokens, 1024), BF16),
        jax.ShapeDtypeStruct((t_tokens, 1024), BF16),
        jax.ShapeDtypeStruct((t_tokens, 1024), BF16),
        jax.ShapeDtypeStruct((t_tokens // LANES, 1024, LANES), BF16),
        jax.ShapeDtypeStruct((t_tokens // LANES, 256, LANES), BF16),
        jax.ShapeDtypeStruct((t_tokens // LANES, 256, LANES), BF16),
        jax.ShapeDtypeStruct((t_tokens // LANES, 64, LANES), F32),
    )
    out_specs = (
        pl.BlockSpec((KV_GROUPS, tm, LANES), lambda i: (0, i, 0)),
        pl.BlockSpec((tm, 512), row),
        pl.BlockSpec((tm, 512), row),
        pl.BlockSpec((tm, 1024), row),
        pl.BlockSpec((tm, 1024), row),
        pl.BlockSpec((tm, 1024), row),
        pl.BlockSpec((tm, 1024), row),
        pl.BlockSpec((c, 1024, LANES), lambda i: (i, 0, 0)),
        pl.BlockSpec((c, 256, LANES), lambda i: (i, 0, 0)),
        pl.BlockSpec((c, 256, LANES), lambda i: (i, 0, 0)),
        pl.BlockSpec((c, 64, LANES), lambda i: (i, 0, 0)),
    )
    return pl.pallas_call(
        functools.partial(_inproj_body, tiles_per_seq=tps),
        grid=(nt,),
        in_specs=[pl.BlockSpec((tm, D_MODEL), row),
                  pl.BlockSpec((1, 1, D_MODEL), per_b),
                  pl.BlockSpec((1, 1, D_MODEL), per_b),
                  _resident((1, D_MODEL), lambda i: (0, 0)),
                  _resident((D_MODEL, STD_COLS), lambda i: (0, 0)),
                  _resident((TR_ROWS, D_MODEL), lambda i: (0, 0))],
        out_specs=out_specs,
        out_shape=out_shape,
        compiler_params=_params(("parallel",), 56),
        name="inproj",
    )(x2, scale1, shift1, g_mix, w_std, w_tr)


def _compress_body(x_ref, wa_ref, wb_ref, pea_ref, peb_ref, w2k_ref, w2vt_ref, kcp_ref, vct_ref):
    x = x_ref[0, 0]
    n_rows = x.shape[0]
    p = jnp.dot(x, wa_ref[...], preferred_element_type=F32)
    q = jnp.dot(x, wb_ref[...], preferred_element_type=F32)
    pe = (jnp.dot(pea_ref[...], wa_ref[...], preferred_element_type=F32)
          + jnp.dot(peb_ref[...], wb_ref[...], preferred_element_type=F32))[0:1, :]
    pre = p + pltpu.roll(q, n_rows - 1, 0) + pe
    hid = _gelu(pre).astype(BF16)
    kc = jnp.dot(hid, w2k_ref[...], preferred_element_type=F32)
    n = lax.broadcasted_iota(jnp.int32, (n_rows, LANES), 0)
    lane = lax.broadcasted_iota(jnp.int32, (n_rows, LANES), 1)
    ce = n * CMP_STRIDE + (CMP_LEN - 1)
    ce_hi = ((ce >> 6) << 6).astype(F32)
    ce_lo = (ce & 63).astype(F32)
    aug = jnp.where((lane == 64) | (lane == 66), ce_hi,
                    jnp.where((lane == 65) | (lane == 67), ce_lo, 0.0))
    kcp_ref[0] = (kc + aug).astype(BF16)
    vct_ref[0] = lax.dot_general(w2vt_ref[...], hid, _NT, preferred_element_type=F32).astype(BF16)


def _compress(cv4, wa, wb, pea, peb, w2k, w2vt):
    g_, b_, n_rows, _ = cv4.shape
    const2 = lambda n: (0, 0)
    return pl.pallas_call(
        _compress_body,
        grid=(b_ * g_,),
        in_specs=[pl.BlockSpec((1, 1, n_rows, 2048), lambda n: (n % KV_GROUPS, n // KV_GROUPS, 0, 0)),
                  pl.BlockSpec((2048, 256), const2),
                  pl.BlockSpec((2048, 256), const2),
                  pl.BlockSpec((8, 2048), const2),
                  pl.BlockSpec((8, 2048), const2),
                  pl.BlockSpec((256, LANES), const2),
                  pl.BlockSpec((64, 256), const2)],
        out_specs=(pl.BlockSpec((1, n_rows, LANES), lambda n: (n, 0, 0)),
                   pl.BlockSpec((1, 64, n_rows), lambda n: (n, 0, 0))),
        out_shape=(jax.ShapeDtypeStruct((b_ * g_, n_rows, LANES), BF16),
                   jax.ShapeDtypeStruct((b_ * g_, 64, n_rows), BF16)),
        compiler_params=_params(("parallel",), 32),
        name="nsa_compress",
    )(cv4, wa, wb, pea, peb, w2k, w2vt)


def _softmax_step(state, s, pv_prev):
    m_i, l_i, acc = state
    m_new = jnp.maximum(m_i, jnp.max(s, axis=0, keepdims=True))
    alpha = jnp.exp2(m_i - m_new)
    p = jnp.exp2(s - m_new)
    l_new = alpha * l_i + jnp.sum(p, axis=0, keepdims=True)
    return (m_new, l_new, (acc + pv_prev) * alpha), p.astype(BF16)


def _nsa_body(qt_ref, gt_ref, kcp_ref, vct_ref, ks_ref, kw_ref, vst_ref, vwt_ref,
              qaug_ref, gout_ref, ovl_ref, grp_ref, o_ref, *scratch, n_cmp):
    for sub in range(NSA_TILES_PER_STEP):
        _nsa_tile(pl.program_id(2) * NSA_TILES_PER_STEP + sub,
                  qt_ref.at[pl.ds(sub, 1)], gt_ref.at[pl.ds(sub, 1)], kcp_ref, vct_ref, ks_ref, kw_ref,
                  vst_ref, vwt_ref, qaug_ref, gout_ref, ovl_ref, grp_ref,
                  o_ref.at[pl.ds(sub * TQ, TQ)], *scratch, n_cmp=n_cmp)


def _nsa_tile(qi, qt_ref, gt_ref, kcp_ref, vct_ref, ks_ref, kw_ref, vst_ref, vwt_ref,
              qaug_ref, gout_ref, ovl_ref, grp_ref, o_ref, selb_ref, sa_ref, sb_ref, pa_ref, pb_ref, *, n_cmp):
    t0 = qi * TQ
    wq = GROUP * TQ

    qp = jnp.concatenate(
        [jnp.concatenate([qt_ref[0, r * HEAD_DIM:(r + 1) * HEAD_DIM, :], qaug_ref[r]], axis=0)
         for r in range(GROUP)], axis=1)

    lane_q = lax.broadcasted_iota(jnp.int32, (1, wq), 1) & (TQ - 1)
    n_win = WINDOW // TQ + 1
    k_lo = pl.multiple_of(jnp.maximum(t0 - WINDOW, 0), TQ)

    s = jnp.dot(kcp_ref[0], qp, preferred_element_type=F32)
    s_w = jnp.dot(kw_ref[pl.ds(k_lo, n_win * TQ), :], qp, preferred_element_type=F32)
    n_io = lax.broadcasted_iota(jnp.int32, (n_cmp, wq), 0)
    tl = lax.broadcasted_iota(jnp.int32, (n_cmp, wq), 1) & (TQ - 1)
    valid = (n_io * CMP_STRIDE + (CMP_LEN - 1)) <= (t0 + tl)
    s = jnp.where(valid, s, NEG)
    m = jnp.maximum(jnp.max(s, axis=0, keepdims=True), 0.5 * NEG)
    p = jnp.exp2(s - m)
    l = jnp.sum(p, axis=0, keepdims=True)
    pn = p * (1.0 / jnp.maximum(l, 1e-30))
    o_c = jnp.dot(vct_ref[0], pn.astype(BF16), preferred_element_type=F32)

    ps = pn[:, 0:TQ] + pn[:, TQ:2 * TQ] + pn[:, 2 * TQ:3 * TQ] + pn[:, 3 * TQ:4 * TQ]
    hi = ps.astype(BF16)
    r1 = ps - hi.astype(F32)
    mid = r1.astype(BF16)
    lo = (r1 - mid.astype(F32)).astype(BF16)
    ovl = ovl_ref[...]
    imp = (jnp.dot(ovl, hi, preferred_element_type=F32)
           + jnp.dot(ovl, mid, preferred_element_type=F32)
           + jnp.dot(ovl, lo, preferred_element_type=F32))

    n_slc = imp.shape[0]
    m_io = lax.broadcasted_iota(jnp.int32, (n_slc, TQ), 0)
    q_io = lax.broadcasted_iota(jnp.int32, (n_slc, TQ), 1)
    back = ((t0 + q_io) >> 6) - m_io
    valid_s = back >= 0
    forced = valid_s & ((m_io == 0) | (back < 2))
    w = jnp.where(forced, -jnp.inf, jnp.where(valid_s, imp, -1.0))
    selb = jnp.where(forced, 0.0, NEG)

    def pick(carry, lanes=None):
        w, selb = carry
        mx = jnp.max(w, axis=0, keepdims=True)
        idx = jnp.min(jnp.where(w == mx, m_io, n_slc), axis=0, keepdims=True)
        hit = m_io == idx
        if lanes is not None:
            hit = hit & lanes
        return jnp.where(hit, -jnp.inf, w), jnp.where(hit, 0.0, selb)

    carry = (w, selb)
    for _ in range(SLC_TOPK - 3):
        carry = pick(carry)
    carry = pick(carry, lanes=(t0 + q_io) < 2 * SLC_BLOCK)
    _, selb = pick(carry, lanes=(t0 + q_io) < SLC_BLOCK)
    selb_ref[...] = selb

    per_blk = TK_SEL // SLC_BLOCK

    def scores(j):
        k0 = pl.multiple_of(j * TK_SEL, TK_SEL)
        bias = jnp.concatenate(
            [jnp.broadcast_to(selb_ref[pl.ds(j * per_blk + u, 1), :], (SLC_BLOCK, TQ)) for u in range(per_blk)],
            axis=0)
        bias = jnp.concatenate([bias] * GROUP, axis=1)
        return jnp.dot(ks_ref[pl.ds(k0, TK_SEL), :], qp, preferred_element_type=F32) + bias

    def vt_tile(j):
        return jnp.concatenate([vst_ref[2 * j], vst_ref[2 * j + 1]], axis=1)

    def causal(j):
        kr = lax.broadcasted_iota(jnp.int32, (TK_SEL, wq), 0)
        return kr <= (t0 - j * TK_SEL) + lane_q

    def pv_dot(j, p_ref):
        return jnp.dot(vt_tile(j), p_ref[...], preferred_element_type=F32)

    jd = qi // (TK_SEL // TQ)

    sa_ref[...] = scores(0)
    pb_ref[...] = jnp.zeros_like(pb_ref)
    sel01 = jnp.where(selb == 0.0, 1.0, 0.0).astype(BF16)
    tile_any = jnp.max(jnp.dot(grp_ref[...], sel01, preferred_element_type=F32), axis=1, keepdims=True)
    j_io = lax.broadcasted_iota(jnp.int32, tile_any.shape, 0)
    q_lo_v = jnp.max(jnp.where((tile_any == 0.0) & (j_io <= jd), j_io, -1), axis=0, keepdims=True) + 1
    n_pre_v = jnp.max(jnp.where((tile_any > 0.0) & (j_io < q_lo_v), j_io, -1), axis=0, keepdims=True) + 1

    d = (t0 - k_lo) + lane_q - lax.broadcasted_iota(jnp.int32, (n_win * TQ, wq), 0)
    s_w = jnp.where(lax.bitcast_convert_type(d, jnp.uint32) < WINDOW, s_w, NEG)
    p = jnp.exp2(s_w - jnp.max(s_w, axis=0, keepdims=True))
    l_w = jnp.sum(p, axis=0, keepdims=True)
    vt_w = jnp.concatenate([vwt_ref[k_lo // TQ + u] for u in range(n_win)], axis=1)
    o_w = jnp.dot(vt_w, p.astype(BF16), preferred_element_type=F32) * (1.0 / l_w)

    q_lo = q_lo_v[0, 0]
    n_pre = n_pre_v[0, 0]
    n_vis = n_pre + jd - q_lo + 1

    def tile_at(pos):
        return jnp.where(pos < n_pre, pos, pos - n_pre + q_lo)

    state = (jnp.full((1, wq), NEG, F32), jnp.zeros((1, wq), F32), jnp.zeros((HEAD_DIM, wq), F32))

    def pair(u, state):
        a = 2 * u
        sb_ref[...] = scores(tile_at(a + 1))
        state, p = _softmax_step(state, sa_ref[...], pv_dot(tile_at(jnp.maximum(a - 1, 0)), pb_ref))
        pa_ref[...] = p
        sa_ref[...] = scores(tile_at(a + 2))
        state, p = _softmax_step(state, sb_ref[...], pv_dot(tile_at(a), pa_ref))
        pb_ref[...] = p
        return state

    n_pair = (n_vis - 1) // 2
    state = lax.fori_loop(0, n_pair, pair, state)
    x = 2 * n_pair
    y = jnp.minimum(x + 1, n_vis - 1)
    tx, ty = tile_at(x), tile_at(y)
    sb_ref[...] = scores(ty)
    state, p = _softmax_step(state, jnp.where(causal(tx), sa_ref[...], NEG),
                             pv_dot(tile_at(jnp.maximum(x - 1, 0)), pb_ref))
    pa_ref[...] = p
    ty_mask = jnp.where(x + 1 < n_vis, ty, jd + 1)
    (_, l_s, acc_s), p = _softmax_step(state, jnp.where(causal(ty_mask), sb_ref[...], NEG), pv_dot(tx, pa_ref))
    acc_s = acc_s + jnp.dot(vt_tile(ty), p, preferred_element_type=F32)

    o_s = acc_s * (1.0 / l_s)
    gw = jax.nn.sigmoid(gt_ref[0])
    outs = []
    for r in range(GROUP):
        sl = slice(r * TQ, (r + 1) * TQ)
        o = (gw[r:r + 1, :] * o_c[:, sl] + gw[GROUP + r:GROUP + r + 1, :] * o_s[:, sl]
             + gw[2 * GROUP + r:2 * GROUP + r + 1, :] * o_w[:, sl])
        ms = jnp.mean(o * o, axis=0, keepdims=True)
        outs.append(o * lax.rsqrt(ms + EPS) * gout_ref[r])
    o_ref[...] = jnp.concatenate(outs, axis=0).T.astype(BF16)


def _nsa(qt, gt, kcp, vct, ks, kw, vst, vwt, qaug, gout_b, ovl_t, grp, batch, seq):
    tps = NSA_TILES_PER_STEP
    nq = seq // (TQ * tps)
    n_cmp = kcp.shape[1]
    n_slc = seq // SLC_BLOCK
    t_tokens = batch * seq
    per_b_chunks = seq // LANES
    return pl.pallas_call(
        functools.partial(_nsa_body, n_cmp=n_cmp),
        grid=(batch, KV_GROUPS, nq),
        in_specs=[
            pl.BlockSpec((tps, 256, LANES), lambda b, g, q: (b * nq + q, g, 0)),
            pl.BlockSpec((tps, 16, LANES), lambda b, g, q: (b * nq + q, g, 0)),
            pl.BlockSpec((1, n_cmp, LANES), lambda b, g, q: (b * KV_GROUPS + g, 0, 0)),
            pl.BlockSpec((1, 64, n_cmp), lambda b, g, q: (b * KV_GROUPS + g, 0, 0)),
            pl.BlockSpec((seq, LANES), lambda b, g, q: (b, g)),
            pl.BlockSpec((seq, LANES), lambda b, g, q: (b, g)),
            pl.BlockSpec((per_b_chunks, 64, LANES), lambda b, g, q: (b, g, 0)),
            pl.BlockSpec((per_b_chunks, 64, LANES), lambda b, g, q: (b, g, 0)),
            pl.BlockSpec((GROUP, 64, LANES), lambda b, g, q: (g, 0, 0)),
            pl.BlockSpec((GROUP, 64, LANES), lambda b, g, q: (g, 0, 0)),
            pl.BlockSpec((n_slc, n_cmp), lambda b, g, q: (0, 0)),
            pl.BlockSpec(grp.shape, lambda b, g, q: (0, 0)),
        ],
        out_specs=pl.BlockSpec((tps * TQ, 256), lambda b, g, q: (b * nq + q, g)),
        out_shape=jax.ShapeDtypeStruct((t_tokens, 1024), BF16),
        scratch_shapes=[pltpu.VMEM((n_slc, TQ), F32),
                        pltpu.VMEM((TK_SEL, GROUP * TQ), F32), pltpu.VMEM((TK_SEL, GROUP * TQ), F32),
                        pltpu.VMEM((TK_SEL, GROUP * TQ), BF16), pltpu.VMEM((TK_SEL, GROUP * TQ), BF16)],
        compiler_params=_params(("parallel", "parallel", "arbitrary"), 40),
        name="nsa_attention",
    )(qt, gt, kcp, vct, ks, kw, vst, vwt, qaug, gout_b, ovl_t, grp)


def _ret_body(q_ref, k_ref, v_ref, g_ref, dm_ref, kd_ref, qd_ref, cd_ref, go_ref, o_ref, st_ref):
    @pl.when(pl.program_id(2) == 0)
    def _():
        st_ref[...] = jnp.zeros_like(st_ref)

    c_ = RET_CHUNK
    slices = [slice(c * c_, (c + 1) * c_) for c in range(RET_TILE // c_)]
    atts, kvs = [], []
    for sl in slices:
        k = k_ref[sl, :]
        atts.append((lax.dot_general(q_ref[sl, :], k, _NT, preferred_element_type=F32) * dm_ref[0]).astype(BF16))
        kdec = (k.astype(F32) * kd_ref[0]).astype(BF16)
        kvs.append(lax.dot_general(kdec, v_ref[sl, :], _TN, preferred_element_type=F32))
    states = [st_ref[...]]
    for kv in kvs:
        states.append(states[-1] * cd_ref[0] + kv)
    st_ref[...] = states[-1]
    for sl, att, state in zip(slices, atts, states):
        o = (jnp.dot(att, v_ref[sl, :], preferred_element_type=F32)
             + qd_ref[0] * jnp.dot(q_ref[sl, :], state.astype(BF16), preferred_element_type=F32))
        mu = jnp.mean(o, axis=-1, keepdims=True)
        oc = o - mu
        var = jnp.mean(oc * oc, axis=-1, keepdims=True)
        y = oc * lax.rsqrt(var + EPS) * go_ref[0, 0:1, :]
        gate = g_ref[sl, :].astype(F32)
        o_ref[sl, :] = (gate * jax.nn.sigmoid(gate) * y).astype(BF16)


def _retention(q_r, k_r, v_r, g_r, dm, kd, qd, cd, go, batch, seq):
    t_tokens = batch * seq
    nc = seq // RET_TILE
    tok = lambda b, h, c: (b * nc + c, h)
    per_h = lambda b, h, c: (h, 0, 0)
    sq = (1, RET_DIM, RET_DIM)
    return pl.pallas_call(
        _ret_body,
        grid=(batch, RET_HEADS, nc),
        in_specs=[pl.BlockSpec((RET_TILE, RET_DIM), tok)] * 4
        + [pl.BlockSpec(sq, per_h)] * 4 + [pl.BlockSpec((1, 8, RET_DIM), per_h)],
        out_specs=pl.BlockSpec((RET_TILE, RET_DIM), tok),
        out_shape=jax.ShapeDtypeStruct((t_tokens, RET_HEADS * RET_DIM), BF16),
        scratch_shapes=[pltpu.VMEM((RET_DIM, RET_DIM), F32)],
        compiler_params=_params(("parallel", "parallel", "arbitrary"), 32),
        name="retention",
    )(q_r, k_r, v_r, g_r, dm, kd, qd, cd, go)


def _pow2_scale(magnitude):
    return jnp.exp2(jnp.floor(jnp.log2(FP8_TARGET / jnp.maximum(magnitude, 1e-30))))


def _mid_body(on_ref, or_ref, x_ref, g1_ref, sc_ref, sh_ref, gn_ref, wo_ref, wqt_ref, sk_ref, ps_ref,
              x1_ref, h2t_ref, st_ref, scl_ref):
    acc = (jnp.dot(on_ref[...], wo_ref[0:1024, :], preferred_element_type=F32)
           + jnp.dot(or_ref[...], wo_ref[1024:2048, :], preferred_element_type=F32))
    x1 = x_ref[...] + g1_ref[0] * acc
    x1_ref[...] = x1
    ms = jnp.mean(x1 * x1, axis=-1, keepdims=True)
    h2 = x1 * lax.rsqrt(ms + EPS) * gn_ref[...]
    h2 = h2 * (1.0 + sc_ref[0]) + sh_ref[0]

    h2_t = h2.T
    amax = jnp.max(jnp.max(jnp.abs(h2_t), axis=0, keepdims=True), axis=1, keepdims=True)
    s_h = _pow2_scale(amax)
    h2t_ref[...] = (h2_t * s_h).astype(FP8)
    norm = jnp.sqrt(jnp.sum(h2_t * h2_t, axis=0, keepdims=True))
    s_c = _pow2_scale(COEF_BOUND_FACTOR * ps_ref[2:3, 0:1] * norm)
    scl_ref[...] = jnp.concatenate(
        [jnp.broadcast_to(ps_ref[0:1, 0:1] / s_h, s_c.shape), s_c, ps_ref[1:2, 0:1] / s_c,
         jnp.zeros((5, s_c.shape[1]), F32)], axis=0)
    h2 = h2.astype(BF16)
    qt = lax.dot_general(wqt_ref[...], h2, _NT, preferred_element_type=F32).astype(BF16)
    for hp in range(2 * PEER_HEADS):
        st_ref[hp] = jnp.dot(sk_ref[hp], qt[hp * 128:(hp + 1) * 128, :], preferred_element_type=F32)


def _mid(o_nsa, o_ret, x2, gate1, scale2, shift2, g_ffn, w_out, wq_t, sub_keys, peer_scales, seq):
    t_tokens = x2.shape[0]
    tm = TM_PROJ
    tps = seq // tm
    row = lambda i: (i, 0)
    per_b = lambda i: (i // tps, 0, 0)
    return pl.pallas_call(
        _mid_body,
        grid=(t_tokens // tm,),
        in_specs=[pl.BlockSpec((tm, 1024), row),
                  pl.BlockSpec((tm, 1024), row),
                  pl.BlockSpec((tm, D_MODEL), row),
                  pl.BlockSpec((1, 1, D_MODEL), per_b),
                  pl.BlockSpec((1, 1, D_MODEL), per_b),
                  pl.BlockSpec((1, 1, D_MODEL), per_b),
                  _resident((1, D_MODEL), lambda i: (0, 0)),
                  _resident((D_MODEL, D_MODEL), lambda i: (0, 0)),
                  _resident((D_MODEL, D_MODEL), lambda i: (0, 0)),
                  _resident((2 * PEER_HEADS, PEER_KEYS, 128), lambda i: (0, 0, 0)),
                  _resident((8, LANES), lambda i: (0, 0))],
        out_specs=(pl.BlockSpec((tm, D_MODEL), row),
                   pl.BlockSpec((D_MODEL, tm), lambda i: (0, i)),
                   pl.BlockSpec((2 * PEER_HEADS, PEER_KEYS, tm), lambda i: (0, 0, i)),
                   pl.BlockSpec((8, tm), lambda i: (0, i))),
        out_shape=(jax.ShapeDtypeStruct((t_tokens, D_MODEL), F32),
                   jax.ShapeDtypeStruct((D_MODEL, t_tokens), FP8),
                   jax.ShapeDtypeStruct((2 * PEER_HEADS, PEER_KEYS, t_tokens), F32),
                   jax.ShapeDtypeStruct((8, t_tokens), F32)),
        compiler_params=_params(("parallel",), 48),
        name="outproj_peerq",
    )(o_nsa, o_ret, x2, gate1, scale2, shift2, g_ffn, w_out, wq_t, sub_keys, peer_scales)


def _top16(s, break_ties):
    n_rows, n = s.shape
    io = lax.broadcasted_iota(jnp.int32, (n_rows, n), 0)
    a_io = lax.broadcasted_iota(jnp.int32, (PEER_TOPK, n), 0)
    rank = jnp.full((n_rows, n), PEER_TOPK, jnp.int32)
    vals = jnp.zeros((PEER_TOPK, n), F32)
    for a in range(PEER_TOPK):
        mx = jnp.max(s, axis=0, keepdims=True)
        hit = s == mx
        if break_ties:
            hit = io == jnp.min(jnp.where(hit, io, n_rows), axis=0, keepdims=True)
        rank = jnp.where(hit, a, rank)
        s = jnp.where(hit, -jnp.inf, s)
        vals = jnp.where(a_io == a, mx, vals)
    return vals, rank


def _peer_select_body(s_ref, scl_ref, l_ref, w1_ref, r2_ref, w2_ref):
    n = s_ref.shape[2]

    def select(hh, break_ties):
        s1 = s_ref[2 * hh]
        s2 = s_ref[2 * hh + 1]
        v1, rank1 = _top16(s1, break_ties)
        v2, rank2 = _top16(s2, break_ties)
        a_io = lax.broadcasted_iota(jnp.int32, (PEER_TOPK, n), 0)
        cnt = jnp.zeros((PEER_TOPK, n), jnp.int32)
        cur = v1 + v2[0:1, :]
        top = v1[0:1, :] + v2[0:1, :]
        z = jnp.zeros((1, n), F32)
        for _ in range(PEER_TOPK):
            mx = jnp.max(cur, axis=0, keepdims=True)
            aidx = jnp.min(jnp.where(cur == mx, a_io, PEER_TOPK), axis=0, keepdims=True)
            hit = a_io == aidx
            cnt = cnt + hit.astype(jnp.int32)
            nxt = jnp.sum(jnp.where(hit, cnt, 0), axis=0, keepdims=True)
            nv = jnp.max(jnp.where(a_io == nxt, v2, -jnp.inf), axis=0, keepdims=True)
            cur = jnp.where(hit, v1 + nv, cur)
            z = z + jnp.exp(mx - top)
        cnt_b = cnt.astype(F32).astype(BF16)
        rank_b = rank1.astype(F32).astype(BF16)
        lrow = jnp.zeros(s1.shape, BF16)
        for a in range(PEER_TOPK):
            lrow = jnp.where(rank_b == a, jnp.broadcast_to(cnt_b[a:a + 1, :], s1.shape), lrow)
        l_ref[hh] = lrow.astype(F32)
        w1_ref[hh] = jnp.exp(s1 - v1[0:1, :])
        r2_ref[hh] = rank2.astype(F32).astype(BF16)
        w2_ref[hh] = (jnp.exp(s2 - v2[0:1, :]) * (scl_ref[1:2, :] / z)).astype(BF16)
        return (jnp.sum((rank1 < PEER_TOPK).astype(jnp.int32), axis=0, keepdims=True)
                + jnp.sum((rank2 < PEER_TOPK).astype(jnp.int32), axis=0, keepdims=True))

    marked = functools.reduce(jnp.maximum, [select(hh, False) for hh in range(SEL_HEADS_PER_STEP)])

    @pl.when(jnp.max(marked) != 2 * PEER_TOPK)
    def _():
        for hh in range(SEL_HEADS_PER_STEP):
            select(hh, True)


def _peer_select(st, scl):
    t_tokens = st.shape[2]
    tm = TM_SEL
    shp = jax.ShapeDtypeStruct((PEER_HEADS, PEER_KEYS, t_tokens), F32)
    shp_b = jax.ShapeDtypeStruct((PEER_HEADS, PEER_KEYS, t_tokens), BF16)
    hps = SEL_HEADS_PER_STEP
    spec = pl.BlockSpec((hps, PEER_KEYS, tm), lambda i, h: (h, 0, i))
    return pl.pallas_call(
        _peer_select_body,
        grid=(t_tokens // tm, PEER_HEADS // hps),
        in_specs=[pl.BlockSpec((2 * hps, PEER_KEYS, tm), lambda i, h: (h, 0, i)),
                  pl.BlockSpec((8, tm), lambda i, h: (0, i))],
        out_specs=(spec, spec, spec, spec),
        out_shape=(shp, shp, shp_b, shp_b),
        compiler_params=_params(("parallel", "parallel"), 32),
        name="peer_select",
    )(st, scl)


def _transpose_body(v_ref, s_ref, o_ref):
    o_ref[...] = (v_ref[...].T * s_ref[0:1, 0:1]).astype(FP8)


def _transpose_fp8(v, scale_tile):
    n, d = v.shape
    tn = 512
    return pl.pallas_call(
        _transpose_body,
        grid=(n // tn,),
        in_specs=[pl.BlockSpec((tn, d), lambda i: (i, 0)),
                  pl.BlockSpec((8, LANES), lambda i: (0, 0))],
        out_specs=pl.BlockSpec((d, tn), lambda i: (0, i)),
        out_shape=jax.ShapeDtypeStruct((d, n), FP8),
        compiler_params=_params(("parallel",), 32),
        name="transpose_v",
    )(v, scale_tile)


def _peer_expert_body(h2t_ref, u_ref, vt_ref, l_ref, w1_ref, r2_ref, w2_ref, scl_ref,
                      o_ref, ce_ref, co_ref, *, steps_per_tile):
    g = pl.program_id(0)
    sw = MXU_COLS
    strips = [slice(c * sw, (c + 1) * sw) for c in range(TM_PEER // sw)]
    n_piece = TE_PEER // PEER_KEYS
    blk = D_MODEL // n_piece

    @pl.when(g == 0)
    def _():
        co_ref[...] = jnp.zeros_like(co_ref)

    @pl.when((g == 0) | ((g - 1) % steps_per_tile == 0))
    def _():
        o_ref[...] = jnp.zeros_like(o_ref)

    def run(c_new, c_old):
        def piece(j, carry):
            r0 = pl.multiple_of(j * PEER_KEYS, PEER_KEYS)
            d0 = pl.multiple_of(j * blk, blk)
            for ls in strips:
                coef = None
                for h in range(PEER_HEADS):
                    lrow = jnp.broadcast_to(l_ref[h, pl.ds(j, 1), ls], (16, sw)).astype(BF16)
                    w1row = jnp.broadcast_to(w1_ref[h, pl.ds(j, 1), ls], (16, sw)).astype(BF16)
                    lrow = jnp.tile(lrow, (PEER_KEYS // 16, 1))
                    w1row = jnp.tile(w1row, (PEER_KEYS // 16, 1))
                    term = jnp.where(r2_ref[h, :, ls] < lrow, w2_ref[h, :, ls] * w1row, jnp.zeros((), BF16))
                    coef = term if coef is None else coef + term
                a_t = jnp.dot(u_ref[pl.ds(r0, PEER_KEYS), :], h2t_ref[:, ls], preferred_element_type=F32)
                a_scale = jnp.broadcast_to(scl_ref[0:1, ls], (16, sw)).astype(BF16)
                act = _gelu(a_t.astype(BF16) * jnp.tile(a_scale, (PEER_KEYS // 16, 1)))
                c_new[pl.ds(r0, PEER_KEYS), ls] = (coef * act).astype(FP8)
                o_ref[pl.ds(d0, blk), ls] += jnp.dot(vt_ref[pl.ds(d0, blk), :], c_old[:, ls],
                                                     preferred_element_type=F32)
            return carry

        lax.fori_loop(0, n_piece, piece, 0, unroll=8)

    @pl.when(g % 2 == 0)
    def _():
        run(ce_ref, co_ref)

    @pl.when(g % 2 == 1)
    def _():
        run(co_ref, ce_ref)


def _peer_expert(h2t, u_b, v_t, lrow, w1, r2, w2, scl):
    t_tokens = h2t.shape[1]
    tm, te = TM_PEER, TE_PEER
    n_piece = te // PEER_KEYS
    ne = PEER_EXPERTS // te
    n_steps = (t_tokens // tm) * ne
    cur = lambda g: jnp.minimum(g, n_steps - 1)
    prev = lambda g: jnp.maximum(g - 1, 0)
    row_spec = pl.BlockSpec((PEER_HEADS, n_piece, tm), lambda g: (0, cur(g) % ne, cur(g) // ne))
    full_spec = pl.BlockSpec((PEER_HEADS, PEER_KEYS, tm), lambda g: (0, 0, cur(g) // ne))
    return pl.pallas_call(
        functools.partial(_peer_expert_body, steps_per_tile=ne),
        grid=(n_steps + 1,),
        in_specs=[pl.BlockSpec((D_MODEL, tm), lambda g: (0, cur(g) // ne)),
                  pl.BlockSpec((te, D_MODEL), lambda g: (cur(g) % ne, 0)),
                  pl.BlockSpec((D_MODEL, te), lambda g: (0, prev(g) % ne)),
                  row_spec, row_spec, full_spec, full_spec,
                  pl.BlockSpec((8, tm), lambda g: (0, cur(g) // ne))],
        out_specs=pl.BlockSpec((D_MODEL, tm), lambda g: (0, prev(g) // ne)),
        out_shape=jax.ShapeDtypeStruct((D_MODEL, t_tokens), F32),
        scratch_shapes=[pltpu.VMEM((te, tm), FP8), pltpu.VMEM((te, tm), FP8)],
        compiler_params=_params(("arbitrary",), 52),
        name="peer_experts",
    )(h2t, u_b, v_t, lrow, w1, r2, w2, scl)


def _final_body(x1_ref, pt_ref, scl_ref, g2_ref, gn_ref, o_ref, *, apply_norm):
    peer = (pt_ref[...] * scl_ref[2:3, :]).T
    y = x1_ref[...] + g2_ref[0] * peer
    if apply_norm:
        ms = jnp.mean(y * y, axis=-1, keepdims=True)
        y = y * lax.rsqrt(ms + EPS) * gn_ref[...]
    o_ref[...] = y


def _final(x1, peer_t, scl, gate2, g_final, seq, apply_norm):
    t_tokens = x1.shape[0]
    tm = TM_PROJ
    tps = seq // tm
    return pl.pallas_call(
        functools.partial(_final_body, apply_norm=apply_norm),
        grid=(t_tokens // tm,),
        in_specs=[pl.BlockSpec((tm, D_MODEL), lambda i: (i, 0)),
                  pl.BlockSpec((D_MODEL, tm), lambda i: (0, i)),
                  pl.BlockSpec((8, tm), lambda i: (0, i)),
                  pl.BlockSpec((1, 1, D_MODEL), lambda i: (i // tps, 0, 0)),
                  pl.BlockSpec((1, D_MODEL), lambda i: (0, 0))],
        out_specs=pl.BlockSpec((tm, D_MODEL), lambda i: (i, 0)),
        out_shape=jax.ShapeDtypeStruct((t_tokens, D_MODEL), F32),
        compiler_params=_params(("parallel",), 32),
        name="final_norm",
    )(x1, peer_t, scl, gate2, g_final)


def _split_cols(a, sizes):
    out, acc = [], 0
    for s in sizes:
        out.append(a[:, acc:acc + s])
        acc += s
    return out


def _inproj_weights(w_in):
    kvw = KV_GROUPS * HEAD_DIM
    sizes = (1024,) + (kvw,) * 6 + (3 * N_HEADS, 1024, 1024, 1024, 1024)
    q_a, k_c, v_c, k_s, v_s, k_w, v_w, g_a, q_r, k_r, v_r, g_r = _split_cols(w_in, sizes)
    d = w_in.shape[0]

    def grp(a, g):
        return a[:, g * HEAD_DIM:(g + 1) * HEAD_DIM]

    cv = [jnp.concatenate([grp(k_c, g), grp(v_c, g)], axis=1) for g in range(KV_GROUPS)]
    w_std = jnp.concatenate(cv + [k_s, k_w, q_r, k_r, v_r, g_r], axis=1).astype(BF16)
    gcols = []
    for g in range(KV_GROUPS):
        for br in range(3):
            for r in range(GROUP):
                c = (g * GROUP + r) * 3 + br
                gcols.append(g_a[:, c:c + 1])
        gcols.append(jnp.zeros((d, 4), w_in.dtype))
    w_tr = jnp.concatenate([q_a, v_s, v_w] + gcols, axis=1).T.astype(BF16)
    return w_std, w_tr


def _compress_weights(pe_k, pe_v, k_w1, k_w2, v_w1, v_w2):
    half = CMP_LEN // 2

    def w1_half(w1k, w1v, lo):
        a = w1k.reshape(CMP_LEN, HEAD_DIM, CMP_HIDDEN)[lo:lo + half]
        b = w1v.reshape(CMP_LEN, HEAD_DIM, CMP_HIDDEN)[lo:lo + half]
        za = jnp.zeros_like(a)
        top = jnp.concatenate([a, za], axis=2)
        bot = jnp.concatenate([za, b], axis=2)
        return jnp.concatenate([top, bot], axis=1).reshape(half * 2 * HEAD_DIM, 2 * CMP_HIDDEN).astype(BF16)

    def pe_half(lo):
        row = jnp.concatenate([pe_k[lo:lo + half], pe_v[lo:lo + half]], axis=1).reshape(1, -1)
        return jnp.broadcast_to(row, (8, row.shape[1])).astype(BF16)

    wa = w1_half(k_w1, v_w1, 0)
    wb = w1_half(k_w1, v_w1, half)
    w2k = jnp.zeros((2 * CMP_HIDDEN, LANES), F32).at[:CMP_HIDDEN, :HEAD_DIM].set(k_w2).astype(BF16)
    w2vt = jnp.zeros((HEAD_DIM, 2 * CMP_HIDDEN), F32).at[:, CMP_HIDDEN:].set(v_w2.T).astype(BF16)
    return wa, wb, pe_half(0), pe_half(half), w2k, w2vt


def _nsa_constants(seq):
    slopes = jnp.exp2(-8.0 * (jnp.arange(N_HEADS, dtype=F32) + 1.0) / N_HEADS) * LOG2E
    s_hi = slopes.astype(BF16)
    s_lo = (slopes - s_hi.astype(F32)).astype(BF16)
    rows = jnp.zeros((N_HEADS, HEAD_DIM), BF16)
    rows = rows.at[:, 0].set(s_hi).at[:, 1].set(s_hi).at[:, 2].set(s_lo).at[:, 3].set(s_lo)
    qaug = jnp.broadcast_to(rows[:, :, None], (N_HEADS, HEAD_DIM, LANES))
    n_rows = seq // CMP_STRIDE
    n_slc = seq // SLC_BLOCK
    start = np.arange(n_rows)[:, None] * CMP_STRIDE
    end = start + CMP_LEN - 1
    blk = np.arange(n_slc)[None, :] * SLC_BLOCK
    ovl = ((start < blk + SLC_BLOCK) & (end >= blk)).astype(np.float32)
    per_tile = TK_SEL // SLC_BLOCK
    grp = (np.arange(n_slc)[None, :] // per_tile == np.arange(n_slc // per_tile)[:, None])
    return qaug, jnp.asarray(ovl.T, BF16), jnp.asarray(grp.astype(np.float32), BF16)


def _retention_constants():
    h, c = RET_HEADS, RET_CHUNK
    lg = jnp.log1p(-jnp.exp2(-5.0 - jnp.arange(h, dtype=F32)))
    pos = jnp.arange(c, dtype=F32)
    diff = pos[:, None] - pos[None, :]
    scale = RET_DIM ** -0.5
    dm = jnp.where(diff >= 0, jnp.exp(lg[:, None, None] * jnp.maximum(diff, 0.0)), 0.0) * scale
    k_decay = jnp.exp(lg[:, None] * (c - 1.0 - pos)) * scale
    q_decay = jnp.exp(lg[:, None] * (pos + 1.0))
    chunk_decay = jnp.exp(lg * c)
    kd = jnp.broadcast_to(k_decay[:, :, None], (h, c, RET_DIM))
    qd = jnp.broadcast_to(q_decay[:, :, None], (h, c, RET_DIM))
    cd = jnp.broadcast_to(chunk_decay[:, None, None], (h, RET_DIM, RET_DIM))
    return dm, kd, qd, cd


def kernel(x, c, w_ada, b_ada, g_norm_mix, g_norm_ffn, g_norm_final, w_in, cmp_pe_k, cmp_pe_v,
           cmp_k_w1, cmp_k_w2, cmp_v_w1, cmp_v_w2, g_nsa_out, g_ret_out, w_out,
           peer_w_q, peer_sub_keys, peer_u, peer_v):
    batch, seq, d = x.shape
    depth = w_ada.shape[0]
    t_tokens = batch * seq
    xf = x.reshape(t_tokens, d)
    c_pad = jnp.zeros((8, d), F32).at[:batch].set(c)
    qaug, ovl_t, grp = _nsa_constants(seq)
    dm, kd, qd, cd = _retention_constants()

    for l in range(depth):
        mod = _adaln(c_pad, w_ada[l], b_ada[l][None, :])[:batch].reshape(batch, 6, 1, d)
        shift1, scale1, gate1, shift2, scale2, gate2 = (mod[:, k] for k in range(6))

        w_std, w_tr = _inproj_weights(w_in[l])
        (cv, ks, kw, q_r, k_r, v_r, g_r, qt, vst, vwt, gt) = _inproj(
            xf, scale1, shift1, g_norm_mix[l][None, :], w_std, w_tr, seq)

        cv4 = cv.reshape(KV_GROUPS, batch, seq // CMP_STRIDE, CMP_STRIDE * LANES)
        kcp, vct = _compress(cv4, *_compress_weights(cmp_pe_k[l], cmp_pe_v[l], cmp_k_w1[l], cmp_k_w2[l],
                                                     cmp_v_w1[l], cmp_v_w2[l]))
        gout_b = jnp.broadcast_to(g_nsa_out[l][:, :, None], (N_HEADS, HEAD_DIM, LANES))
        o_nsa = _nsa(qt, gt, kcp, vct, ks, kw, vst, vwt, qaug, gout_b, ovl_t, grp, batch, seq)

        go = jnp.broadcast_to(g_ret_out[l][:, None, :], (RET_HEADS, 8, RET_DIM))
        o_ret = _retention(q_r, k_r, v_r, g_r, dm, kd, qd, cd, go, batch, seq)

        sub_keys = peer_sub_keys[l].reshape(2 * PEER_HEADS, PEER_KEYS, -1).astype(BF16)
        s_u = _pow2_scale(jnp.max(jnp.abs(peer_u[l])))
        s_v = _pow2_scale(jnp.max(jnp.abs(peer_v[l])))
        u_norm = jnp.sqrt(jnp.max(jnp.sum(jnp.square(peer_u[l]), axis=1)))
        peer_scales = jnp.broadcast_to(
            jnp.stack([1.0 / s_u, 1.0 / s_v, u_norm] + [jnp.zeros((), F32)] * 5)[:, None], (8, LANES))
        x1, h2t, st, scl = _mid(o_nsa, o_ret, xf, gate1, scale2, shift2, g_norm_ffn[l][None, :],
                                w_out[l].astype(BF16), peer_w_q[l].T.astype(BF16), sub_keys, peer_scales, seq)

        lrow, w1, r2, w2 = _peer_select(st, scl)
        peer_t = _peer_expert(h2t, (peer_u[l] * s_u).astype(FP8),
                              _transpose_fp8(peer_v[l], jnp.full((8, LANES), s_v, F32)),
                              lrow, w1, r2, w2, scl)
        xf = _final(x1, peer_t, scl, gate2, g_norm_final[None, :], seq, apply_norm=(l == depth - 1))
    return xf.reshape(batch, seq, d)
```

```python
import functools
import math

import numpy as np
import jax
import jax.numpy as jnp
from jax import lax
from jax.experimental import pallas as pl
from jax.experimental.pallas import tpu as pltpu

F32 = jnp.float32
BF16 = jnp.bfloat16
FP8 = jnp.float8_e4m3fn
FP8_TARGET = 224.0

D_MODEL = 2048
N_HEADS = 16
HEAD_DIM = 64
KV_GROUPS = 4
GROUP = 4
CMP_LEN = 32
CMP_STRIDE = 16
CMP_HIDDEN = 128
SLC_BLOCK = 64
SLC_TOPK = 16
WINDOW = 512
NEG = -1e30
RET_HEADS = 8
RET_DIM = 128
RET_CHUNK = 128
PEER_HEADS = 8
PEER_KEYS = 128
PEER_EXPERTS = PEER_KEYS * PEER_KEYS
PEER_TOPK = 16
COEF_BOUND_FACTOR = 1.25 * PEER_HEADS
EPS = 1e-6
LOG2E = 1.4426950408889634

LANES = 128
MXU_COLS = 256
TQ = 128
NSA_TILES_PER_STEP = 2
TK_SEL = 256
TM_PROJ = 256
TM_PEER = 512
TE_PEER = 1024
TM_SEL = 256
SEL_HEADS_PER_STEP = 4
RET_TILE = 512

_NT = (((1,), (1,)), ((), ()))
_TN = (((0,), (0,)), ((), ()))


def _params(sem, vmem_mb):
    return pltpu.CompilerParams(dimension_semantics=sem, vmem_limit_bytes=vmem_mb * 1024 * 1024)


def _resident(shape, index_map):
    return pl.BlockSpec(shape, index_map, pipeline_mode=pl.Buffered(1))


def _gelu(x):
    return jax.nn.gelu(x)


def _adaln_body(c_ref, w_ref, b_ref, o_ref):
    c = c_ref[...]
    act = (c * jax.nn.sigmoid(c)).astype(BF16)
    o_ref[...] = jnp.dot(act, w_ref[...].astype(BF16), preferred_element_type=F32) + b_ref[...]


def _adaln(c_pad, w, b):
    n = w.shape[1]
    tn = 1536
    return pl.pallas_call(
        _adaln_body,
        grid=(n // tn,),
        in_specs=[pl.BlockSpec((8, D_MODEL), lambda j: (0, 0)),
                  pl.BlockSpec((D_MODEL, tn), lambda j: (0, j)),
                  pl.BlockSpec((1, tn), lambda j: (0, j))],
        out_specs=pl.BlockSpec((8, tn), lambda j: (0, j)),
        out_shape=jax.ShapeDtypeStruct((8, n), F32),
        compiler_params=_params(("arbitrary",), 40),
        name="adaln",
    )(c_pad, w, b)


STD_COLS = 512 + 256 * 2 + 1024 * 4
TR_ROWS = 1024 + 256 + 256 + 64


def _inproj_body(x_ref, sc_ref, sh_ref, gn_ref, wstd_ref, wt_ref,
                 cv_ref, ks_ref, kw_ref, qr_ref, kr_ref, vr_ref, gr_ref,
                 qt_ref, vst_ref, vwt_ref, gt_ref, *, tiles_per_seq):
    tm = TM_PROJ
    i = pl.program_id(0)
    x = x_ref[...]
    ms = jnp.mean(x * x, axis=-1, keepdims=True)
    h = x * lax.rsqrt(ms + EPS) * gn_ref[...]
    h = h * (1.0 + sc_ref[0]) + sh_ref[0]
    hb = h.astype(BF16)

    def std(a, b):
        return jnp.dot(hb, wstd_ref[:, a:b], preferred_element_type=F32)

    y = std(0, 512)
    for g in range(KV_GROUPS):
        cv_ref[g] = y[:, g * LANES:(g + 1) * LANES].astype(BF16)

    t = (i % tiles_per_seq) * tm + lax.broadcasted_iota(jnp.int32, (tm, LANES), 0)
    lane = lax.broadcasted_iota(jnp.int32, (tm, LANES), 1)
    pos_hi = ((t >> 6) << 6).astype(F32)
    pos_lo = (t & 63).astype(F32)
    aug = jnp.where((lane == 64) | (lane == 66), pos_hi,
                    jnp.where((lane == 65) | (lane == 67), pos_lo, 0.0))
    for ref, off in ((ks_ref, 512), (kw_ref, 768)):
        y = std(off, off + 256)
        for g in range(KV_GROUPS):
            pair = y[:, (g // 2) * LANES:(g // 2 + 1) * LANES]
            keys = pair if g % 2 == 0 else pltpu.roll(pair, HEAD_DIM, 1)
            ref[:, g * LANES:(g + 1) * LANES] = jnp.where(lane < HEAD_DIM, keys, aug).astype(BF16)

    for ref, off in ((qr_ref, 1024), (kr_ref, 2048), (vr_ref, 3072), (gr_ref, 4096)):
        ref[...] = std(off, off + 1024).astype(BF16)

    def tr(a, b):
        return lax.dot_general(wt_ref[a:b, :], hb, _NT, preferred_element_type=F32)

    qt = tr(0, 1024) * (HEAD_DIM ** -0.5 * LOG2E)
    vst = tr(1024, 1280)
    vwt = tr(1280, 1536)
    gt = tr(1536, 1600)
    for c in range(tm // LANES):
        sl = slice(c * LANES, (c + 1) * LANES)
        qt_ref[c] = qt[:, sl].astype(BF16)
        vst_ref[c] = vst[:, sl].astype(BF16)
        vwt_ref[c] = vwt[:, sl].astype(BF16)
        gt_ref[c] = gt[:, sl]


def _inproj(x2, scale1, shift1, g_mix, w_std, w_tr, seq):
    t_tokens = x2.shape[0]
    tm = TM_PROJ
    tps = seq // tm
    nt = t_tokens // tm
    c = tm // LANES
    row = lambda i: (i, 0)
    per_b = lambda i: (i // tps, 0, 0)
    out_shape = (
        jax.ShapeDtypeStruct((KV_GROUPS, t_tokens, LANES), BF16),
        jax.ShapeDtypeStruct((t_tokens, 512), BF16),
        jax.ShapeDtypeStruct((t_tokens, 512), BF16),
        jax.ShapeDtypeStruct((t_tokens, 1024), BF16),
        jax.ShapeDtypeStruct((t_tokens, 1024), BF16),
        jax.ShapeDtypeStruct((t_tokens, 1024), BF16),
        jax.ShapeDtypeStruct((t_tokens, 1024), BF16),
        jax.ShapeDtypeStruct((t_tokens // LANES, 1024, LANES), BF16),
        jax.ShapeDtypeStruct((t_tokens // LANES, 256, LANES), BF16),
        jax.ShapeDtypeStruct((t_tokens // LANES, 256, LANES), BF16),
        jax.ShapeDtypeStruct((t_tokens // LANES, 64, LANES), F32),
    )
    out_specs = (
        pl.BlockSpec((KV_GROUPS, tm, LANES), lambda i: (0, i, 0)),
        pl.BlockSpec((tm, 512), row),
        pl.BlockSpec((tm, 512), row),
        pl.BlockSpec((tm, 1024), row),
        pl.BlockSpec((tm, 1024), row),
        pl.BlockSpec((tm, 1024), row),
        pl.BlockSpec((tm, 1024), row),
        pl.BlockSpec((c, 1024, LANES), lambda i: (i, 0, 0)),
        pl.BlockSpec((c, 256, LANES), lambda i: (i, 0, 0)),
        pl.BlockSpec((c, 256, LANES), lambda i: (i, 0, 0)),
        pl.BlockSpec((c, 64, LANES), lambda i: (i, 0, 0)),
    )
    return pl.pallas_call(
        functools.partial(_inproj_body, tiles_per_seq=tps),
        grid=(nt,),
        in_specs=[pl.BlockSpec((tm, D_MODEL), row),
                  pl.BlockSpec((1, 1, D_MODEL), per_b),
                  pl.BlockSpec((1, 1, D_MODEL), per_b),
                  _resident((1, D_MODEL), lambda i: (0, 0)),
                  _resident((D_MODEL, STD_COLS), lambda i: (0, 0)),
                  _resident((TR_ROWS, D_MODEL), lambda i: (0, 0))],
        out_specs=out_specs,
        out_shape=out_shape,
        compiler_params=_params(("parallel",), 56),
        name="inproj",
    )(x2, scale1, shift1, g_mix, w_std, w_tr)


def _compress_body(x_ref, wa_ref, wb_ref, pea_ref, peb_ref, w2k_ref, w2vt_ref, kcp_ref, vct_ref):
    x = x_ref[0, 0]
    n_rows = x.shape[0]
    p = jnp.dot(x, wa_ref[...], preferred_element_type=F32)
    q = jnp.dot(x, wb_ref[...], preferred_element_type=F32)
    pe = (jnp.dot(pea_ref[...], wa_ref[...], preferred_element_type=F32)
          + jnp.dot(peb_ref[...], wb_ref[...], preferred_element_type=F32))[0:1, :]
    pre = p + pltpu.roll(q, n_rows - 1, 0) + pe
    hid = _gelu(pre).astype(BF16)
    kc = jnp.dot(hid, w2k_ref[...], preferred_element_type=F32)
    n = lax.broadcasted_iota(jnp.int32, (n_rows, LANES), 0)
    lane = lax.broadcasted_iota(jnp.int32, (n_rows, LANES), 1)
    ce = n * CMP_STRIDE + (CMP_LEN - 1)
    ce_hi = ((ce >> 6) << 6).astype(F32)
    ce_lo = (ce & 63).astype(F32)
    aug = jnp.where((lane == 64) | (lane == 66), ce_hi,
                    jnp.where((lane == 65) | (lane == 67), ce_lo, 0.0))
    kcp_ref[0] = (kc + aug).astype(BF16)
    vct_ref[0] = lax.dot_general(w2vt_ref[...], hid, _NT, preferred_element_type=F32).astype(BF16)


def _compress(cv4, wa, wb, pea, peb, w2k, w2vt):
    g_, b_, n_rows, _ = cv4.shape
    const2 = lambda n: (0, 0)
    return pl.pallas_call(
        _compress_body,
        grid=(b_ * g_,),
        in_specs=[pl.BlockSpec((1, 1, n_rows, 2048), lambda n: (n % KV_GROUPS, n // KV_GROUPS, 0, 0)),
                  pl.BlockSpec((2048, 256), const2),
                  pl.BlockSpec((2048, 256), const2),
                  pl.BlockSpec((8, 2048), const2),
                  pl.BlockSpec((8, 2048), const2),
                  pl.BlockSpec((256, LANES), const2),
                  pl.BlockSpec((64, 256), const2)],
        out_specs=(pl.BlockSpec((1, n_rows, LANES), lambda n: (n, 0, 0)),
                   pl.BlockSpec((1, 64, n_rows), lambda n: (n, 0, 0))),
        out_shape=(jax.ShapeDtypeStruct((b_ * g_, n_rows, LANES), BF16),
                   jax.ShapeDtypeStruct((b_ * g_, 64, n_rows), BF16)),
        compiler_params=_params(("parallel",), 32),
        name="nsa_compress",
    )(cv4, wa, wb, pea, peb, w2k, w2vt)


def _softmax_step(state, s, pv_prev):
    m_i, l_i, acc = state
    m_new = jnp.maximum(m_i, jnp.max(s, axis=0, keepdims=True))
    alpha = jnp.exp2(m_i - m_new)
    p = jnp.exp2(s - m_new)
    l_new = alpha * l_i + jnp.sum(p, axis=0, keepdims=True)
    return (m_new, l_new, (acc + pv_prev) * alpha), p.astype(BF16)


def _nsa_body(qt_ref, gt_ref, kcp_ref, vct_ref, ks_ref, kw_ref, vst_ref, vwt_ref,
              qaug_ref, gout_ref, ovl_ref, grp_ref, o_ref, *scratch, n_cmp):
    per_tile = len(scratch) // NSA_TILES_PER_STEP
    for sub in range(NSA_TILES_PER_STEP):
        _nsa_tile(pl.program_id(2) * NSA_TILES_PER_STEP + sub,
                  qt_ref.at[pl.ds(sub, 1)], gt_ref.at[pl.ds(sub, 1)], kcp_ref, vct_ref, ks_ref, kw_ref,
                  vst_ref, vwt_ref, qaug_ref, gout_ref, ovl_ref, grp_ref,
                  o_ref.at[pl.ds(sub * TQ, TQ)], *scratch[sub * per_tile:(sub + 1) * per_tile], n_cmp=n_cmp)


def _nsa_tile(qi, qt_ref, gt_ref, kcp_ref, vct_ref, ks_ref, kw_ref, vst_ref, vwt_ref,
              qaug_ref, gout_ref, ovl_ref, grp_ref, o_ref, selb_ref, sa_ref, sb_ref, pa_ref, pb_ref, *, n_cmp):
    t0 = qi * TQ
    wq = GROUP * TQ

    qp = jnp.concatenate(
        [jnp.concatenate([qt_ref[0, r * HEAD_DIM:(r + 1) * HEAD_DIM, :], qaug_ref[r]], axis=0)
         for r in range(GROUP)], axis=1)

    lane_q = lax.broadcasted_iota(jnp.int32, (1, wq), 1) & (TQ - 1)
    n_win = WINDOW // TQ + 1
    k_lo = pl.multiple_of(jnp.maximum(t0 - WINDOW, 0), TQ)

    s = jnp.dot(kcp_ref[0], qp, preferred_element_type=F32)
    s_w = jnp.dot(kw_ref[pl.ds(k_lo, n_win * TQ), :], qp, preferred_element_type=F32)
    n_io = lax.broadcasted_iota(jnp.int32, (n_cmp, wq), 0)
    tl = lax.broadcasted_iota(jnp.int32, (n_cmp, wq), 1) & (TQ - 1)
    valid = (n_io * CMP_STRIDE + (CMP_LEN - 1)) <= (t0 + tl)
    s = jnp.where(valid, s, NEG)
    m = jnp.maximum(jnp.max(s, axis=0, keepdims=True), 0.5 * NEG)
    p = jnp.exp2(s - m)
    l = jnp.sum(p, axis=0, keepdims=True)
    pn = p * (1.0 / jnp.maximum(l, 1e-30))
    o_c = jnp.dot(vct_ref[0], pn.astype(BF16), preferred_element_type=F32)

    ps = pn[:, 0:TQ] + pn[:, TQ:2 * TQ] + pn[:, 2 * TQ:3 * TQ] + pn[:, 3 * TQ:4 * TQ]
    hi = ps.astype(BF16)
    r1 = ps - hi.astype(F32)
    mid = r1.astype(BF16)
    lo = (r1 - mid.astype(F32)).astype(BF16)
    ovl = ovl_ref[...]
    imp = (jnp.dot(ovl, hi, preferred_element_type=F32)
           + jnp.dot(ovl, mid, preferred_element_type=F32)
           + jnp.dot(ovl, lo, preferred_element_type=F32))

    n_slc = imp.shape[0]
    m_io = lax.broadcasted_iota(jnp.int32, (n_slc, TQ), 0)
    q_io = lax.broadcasted_iota(jnp.int32, (n_slc, TQ), 1)
    back = ((t0 + q_io) >> 6) - m_io
    valid_s = back >= 0
    forced = valid_s & ((m_io == 0) | (back < 2))
    w = jnp.where(forced, -jnp.inf, jnp.where(valid_s, imp, -1.0))
    selb = jnp.where(forced, 0.0, NEG)

    def pick(carry, lanes=None):
        w, selb = carry
        mx = jnp.max(w, axis=0, keepdims=True)
        idx = jnp.min(jnp.where(w == mx, m_io, n_slc), axis=0, keepdims=True)
        hit = m_io == idx
        if lanes is not None:
            hit = hit & lanes
        return jnp.where(hit, -jnp.inf, w), jnp.where(hit, 0.0, selb)

    carry = (w, selb)
    for _ in range(SLC_TOPK - 3):
        carry = pick(carry)
    carry = pick(carry, lanes=(t0 + q_io) < 2 * SLC_BLOCK)
    _, selb = pick(carry, lanes=(t0 + q_io) < SLC_BLOCK)
    selb_ref[...] = selb

    per_blk = TK_SEL // SLC_BLOCK

    def scores(j):
        k0 = pl.multiple_of(j * TK_SEL, TK_SEL)
        bias = jnp.concatenate(
            [jnp.broadcast_to(selb_ref[pl.ds(j * per_blk + u, 1), :], (SLC_BLOCK, TQ)) for u in range(per_blk)],
            axis=0)
        bias = jnp.concatenate([bias] * GROUP, axis=1)
        return jnp.dot(ks_ref[pl.ds(k0, TK_SEL), :], qp, preferred_element_type=F32) + bias

    def vt_tile(j):
        return jnp.concatenate([vst_ref[2 * j], vst_ref[2 * j + 1]], axis=1)

    def causal(j):
        kr = lax.broadcasted_iota(jnp.int32, (TK_SEL, wq), 0)
        return kr <= (t0 - j * TK_SEL) + lane_q

    def pv_dot(j, p_ref):
        return jnp.dot(vt_tile(j), p_ref[...], preferred_element_type=F32)

    jd = qi // (TK_SEL // TQ)

    sa_ref[...] = scores(0)
    pb_ref[...] = jnp.zeros_like(pb_ref)
    sel01 = jnp.where(selb == 0.0, 1.0, 0.0).astype(BF16)
    tile_any = jnp.max(jnp.dot(grp_ref[...], sel01, preferred_element_type=F32), axis=1, keepdims=True)
    j_io = lax.broadcasted_iota(jnp.int32, tile_any.shape, 0)
    q_lo_v = jnp.max(jnp.where((tile_any == 0.0) & (j_io <= jd), j_io, -1), axis=0, keepdims=True) + 1
    n_pre_v = jnp.max(jnp.where((tile_any > 0.0) & (j_io < q_lo_v), j_io, -1), axis=0, keepdims=True) + 1

    d = (t0 - k_lo) + lane_q - lax.broadcasted_iota(jnp.int32, (n_win * TQ, wq), 0)
    s_w = jnp.where(lax.bitcast_convert_type(d, jnp.uint32) < WINDOW, s_w, NEG)
    p = jnp.exp2(s_w - jnp.max(s_w, axis=0, keepdims=True))
    l_w = jnp.sum(p, axis=0, keepdims=True)
    vt_w = jnp.concatenate([vwt_ref[k_lo // TQ + u] for u in range(n_win)], axis=1)
    o_w = jnp.dot(vt_w, p.astype(BF16), preferred_element_type=F32) * (1.0 / l_w)

    q_lo = q_lo_v[0, 0]
    n_pre = n_pre_v[0, 0]
    n_vis = n_pre + jd - q_lo + 1

    def tile_at(pos):
        return jnp.where(pos < n_pre, pos, pos - n_pre + q_lo)

    state = (jnp.full((1, wq), NEG, F32), jnp.zeros((1, wq), F32), jnp.zeros((HEAD_DIM, wq), F32))

    def pair(u, state):
        a = 2 * u
        sb_ref[...] = scores(tile_at(a + 1))
        state, p = _softmax_step(state, sa_ref[...], pv_dot(tile_at(jnp.maximum(a - 1, 0)), pb_ref))
        pa_ref[...] = p
        sa_ref[...] = scores(tile_at(a + 2))
        state, p = _softmax_step(state, sb_ref[...], pv_dot(tile_at(a), pa_ref))
        pb_ref[...] = p
        return state

    n_pair = (n_vis - 1) // 2
    state = lax.fori_loop(0, n_pair, pair, state)
    x = 2 * n_pair
    y = jnp.minimum(x + 1, n_vis - 1)
    tx, ty = tile_at(x), tile_at(y)
    sb_ref[...] = scores(ty)
    state, p = _softmax_step(state, jnp.where(causal(tx), sa_ref[...], NEG),
                             pv_dot(tile_at(jnp.maximum(x - 1, 0)), pb_ref))
    pa_ref[...] = p
    ty_mask = jnp.where(x + 1 < n_vis, ty, jd + 1)
    (_, l_s, acc_s), p = _softmax_step(state, jnp.where(causal(ty_mask), sb_ref[...], NEG), pv_dot(tx, pa_ref))
    acc_s = acc_s + jnp.dot(vt_tile(ty), p, preferred_element_type=F32)

    o_s = acc_s * (1.0 / l_s)
    gw = jax.nn.sigmoid(gt_ref[0])
    outs = []
    for r in range(GROUP):
        sl = slice(r * TQ, (r + 1) * TQ)
        o = (gw[r:r + 1, :] * o_c[:, sl] + gw[GROUP + r:GROUP + r + 1, :] * o_s[:, sl]
             + gw[2 * GROUP + r:2 * GROUP + r + 1, :] * o_w[:, sl])
        ms = jnp.mean(o * o, axis=0, keepdims=True)
        outs.append(o * lax.rsqrt(ms + EPS) * gout_ref[r])
    o_ref[...] = jnp.concatenate(outs, axis=0).T.astype(BF16)


def _nsa(qt, gt, kcp, vct, ks, kw, vst, vwt, qaug, gout_b, ovl_t, grp, batch, seq):
    tps = NSA_TILES_PER_STEP
    nq = seq // (TQ * tps)
    n_cmp = kcp.shape[1]
    n_slc = seq // SLC_BLOCK
    t_tokens = batch * seq
    per_b_chunks = seq // LANES
    return pl.pallas_call(
        functools.partial(_nsa_body, n_cmp=n_cmp),
        grid=(batch, KV_GROUPS, nq),
        in_specs=[
            pl.BlockSpec((tps, 256, LANES), lambda b, g, q: (b * nq + q, g, 0)),
            pl.BlockSpec((tps, 16, LANES), lambda b, g, q: (b * nq + q, g, 0)),
            pl.BlockSpec((1, n_cmp, LANES), lambda b, g, q: (b * KV_GROUPS + g, 0, 0)),
            pl.BlockSpec((1, 64, n_cmp), lambda b, g, q: (b * KV_GROUPS + g, 0, 0)),
            pl.BlockSpec((seq, LANES), lambda b, g, q: (b, g)),
            pl.BlockSpec((seq, LANES), lambda b, g, q: (b, g)),
            pl.BlockSpec((per_b_chunks, 64, LANES), lambda b, g, q: (b, g, 0)),
            pl.BlockSpec((per_b_chunks, 64, LANES), lambda b, g, q: (b, g, 0)),
            pl.BlockSpec((GROUP, 64, LANES), lambda b, g, q: (g, 0, 0)),
            pl.BlockSpec((GROUP, 64, LANES), lambda b, g, q: (g, 0, 0)),
            pl.BlockSpec((n_slc, n_cmp), lambda b, g, q: (0, 0)),
            pl.BlockSpec(grp.shape, lambda b, g, q: (0, 0)),
        ],
        out_specs=pl.BlockSpec((tps * TQ, 256), lambda b, g, q: (b * nq + q, g)),
        out_shape=jax.ShapeDtypeStruct((t_tokens, 1024), BF16),
        scratch_shapes=[pltpu.VMEM((n_slc, TQ), F32),
                        pltpu.VMEM((TK_SEL, GROUP * TQ), F32), pltpu.VMEM((TK_SEL, GROUP * TQ), F32),
                        pltpu.VMEM((TK_SEL, GROUP * TQ), BF16), pltpu.VMEM((TK_SEL, GROUP * TQ), BF16)] * tps,
        compiler_params=_params(("parallel", "parallel", "arbitrary"), 40),
        name="nsa_attention",
    )(qt, gt, kcp, vct, ks, kw, vst, vwt, qaug, gout_b, ovl_t, grp)


def _ret_body(q_ref, k_ref, v_ref, g_ref, dm_ref, kd_ref, qd_ref, cd_ref, go_ref, o_ref, st_ref):
    @pl.when(pl.program_id(2) == 0)
    def _():
        st_ref[...] = jnp.zeros_like(st_ref)

    c_ = RET_CHUNK
    slices = [slice(c * c_, (c + 1) * c_) for c in range(RET_TILE // c_)]
    atts, kvs = [], []
    for sl in slices:
        k = k_ref[sl, :]
        atts.append((lax.dot_general(q_ref[sl, :], k, _NT, preferred_element_type=F32) * dm_ref[0]).astype(BF16))
        kdec = (k.astype(F32) * kd_ref[0]).astype(BF16)
        kvs.append(lax.dot_general(kdec, v_ref[sl, :], _TN, preferred_element_type=F32))
    states = [st_ref[...]]
    for kv in kvs:
        states.append(states[-1] * cd_ref[0] + kv)
    st_ref[...] = states[-1]
    for sl, att, state in zip(slices, atts, states):
        o = (jnp.dot(att, v_ref[sl, :], preferred_element_type=F32)
             + qd_ref[0] * jnp.dot(q_ref[sl, :], state.astype(BF16), preferred_element_type=F32))
        mu = jnp.mean(o, axis=-1, keepdims=True)
        oc = o - mu
        var = jnp.mean(oc * oc, axis=-1, keepdims=True)
        y = oc * lax.rsqrt(var + EPS) * go_ref[0, 0:1, :]
        gate = g_ref[sl, :].astype(F32)
        o_ref[sl, :] = (gate * jax.nn.sigmoid(gate) * y).astype(BF16)


def _retention(q_r, k_r, v_r, g_r, dm, kd, qd, cd, go, batch, seq):
    t_tokens = batch * seq
    nc = seq // RET_TILE
    tok = lambda b, h, c: (b * nc + c, h)
    per_h = lambda b, h, c: (h, 0, 0)
    sq = (1, RET_DIM, RET_DIM)
    return pl.pallas_call(
        _ret_body,
        grid=(batch, RET_HEADS, nc),
        in_specs=[pl.BlockSpec((RET_TILE, RET_DIM), tok)] * 4
        + [pl.BlockSpec(sq, per_h)] * 4 + [pl.BlockSpec((1, 8, RET_DIM), per_h)],
        out_specs=pl.BlockSpec((RET_TILE, RET_DIM), tok),
        out_shape=jax.ShapeDtypeStruct((t_tokens, RET_HEADS * RET_DIM), BF16),
        scratch_shapes=[pltpu.VMEM((RET_DIM, RET_DIM), F32)],
        compiler_params=_params(("parallel", "parallel", "arbitrary"), 32),
        name="retention",
    )(q_r, k_r, v_r, g_r, dm, kd, qd, cd, go)


def _pow2_scale(magnitude):
    return jnp.exp2(jnp.floor(jnp.log2(FP8_TARGET / jnp.maximum(magnitude, 1e-30))))


def _mid_body(on_ref, or_ref, x_ref, g1_ref, sc_ref, sh_ref, gn_ref, wo_ref, wqt_ref, sk_ref, ps_ref,
              x1_ref, h2t_ref, st_ref, scl_ref):
    acc = (jnp.dot(on_ref[...], wo_ref[0:1024, :], preferred_element_type=F32)
           + jnp.dot(or_ref[...], wo_ref[1024:2048, :], preferred_element_type=F32))
    x1 = x_ref[...] + g1_ref[0] * acc
    x1_ref[...] = x1
    ms = jnp.mean(x1 * x1, axis=-1, keepdims=True)
    h2 = x1 * lax.rsqrt(ms + EPS) * gn_ref[...]
    h2 = h2 * (1.0 + sc_ref[0]) + sh_ref[0]

    h2_t = h2.T
    amax = jnp.max(jnp.max(jnp.abs(h2_t), axis=0, keepdims=True), axis=1, keepdims=True)
    s_h = _pow2_scale(amax)
    h2t_ref[...] = (h2_t * s_h).astype(FP8)
    norm = jnp.sqrt(jnp.sum(h2_t * h2_t, axis=0, keepdims=True))
    s_c = _pow2_scale(COEF_BOUND_FACTOR * ps_ref[2:3, 0:1] * norm)
    scl_ref[...] = jnp.concatenate(
        [jnp.broadcast_to(ps_ref[0:1, 0:1] / s_h, s_c.shape), s_c, ps_ref[1:2, 0:1] / s_c,
         jnp.zeros((5, s_c.shape[1]), F32)], axis=0)
    h2 = h2.astype(BF16)
    qt = lax.dot_general(wqt_ref[...], h2, _NT, preferred_element_type=F32).astype(BF16)
    for hp in range(2 * PEER_HEADS):
        st_ref[hp] = jnp.dot(sk_ref[hp], qt[hp * 128:(hp + 1) * 128, :], preferred_element_type=F32)


def _mid(o_nsa, o_ret, x2, gate1, scale2, shift2, g_ffn, w_out, wq_t, sub_keys, peer_scales, seq):
    t_tokens = x2.shape[0]
    tm = TM_PROJ
    tps = seq // tm
    row = lambda i: (i, 0)
    per_b = lambda i: (i // tps, 0, 0)
    return pl.pallas_call(
        _mid_body,
        grid=(t_tokens // tm,),
        in_specs=[pl.BlockSpec((tm, 1024), row),
                  pl.BlockSpec((tm, 1024), row),
                  pl.BlockSpec((tm, D_MODEL), row),
                  pl.BlockSpec((1, 1, D_MODEL), per_b),
                  pl.BlockSpec((1, 1, D_MODEL), per_b),
                  pl.BlockSpec((1, 1, D_MODEL), per_b),
                  _resident((1, D_MODEL), lambda i: (0, 0)),
                  _resident((D_MODEL, D_MODEL), lambda i: (0, 0)),
                  _resident((D_MODEL, D_MODEL), lambda i: (0, 0)),
                  _resident((2 * PEER_HEADS, PEER_KEYS, 128), lambda i: (0, 0, 0)),
                  _resident((8, LANES), lambda i: (0, 0))],
        out_specs=(pl.BlockSpec((tm, D_MODEL), row),
                   pl.BlockSpec((D_MODEL, tm), lambda i: (0, i)),
                   pl.BlockSpec((2 * PEER_HEADS, PEER_KEYS, tm), lambda i: (0, 0, i)),
                   pl.BlockSpec((8, tm), lambda i: (0, i))),
        out_shape=(jax.ShapeDtypeStruct((t_tokens, D_MODEL), F32),
                   jax.ShapeDtypeStruct((D_MODEL, t_tokens), FP8),
                   jax.ShapeDtypeStruct((2 * PEER_HEADS, PEER_KEYS, t_tokens), F32),
                   jax.ShapeDtypeStruct((8, t_tokens), F32)),
        compiler_params=_params(("parallel",), 48),
        name="outproj_peerq",
    )(o_nsa, o_ret, x2, gate1, scale2, shift2, g_ffn, w_out, wq_t, sub_keys, peer_scales)


def _top16(s, break_ties):
    n_rows, n = s.shape
    io = lax.broadcasted_iota(jnp.int32, (n_rows, n), 0)
    a_io = lax.broadcasted_iota(jnp.int32, (PEER_TOPK, n), 0)
    rank = jnp.full((n_rows, n), PEER_TOPK, jnp.int32)
    vals = jnp.zeros((PEER_TOPK, n), F32)
    for a in range(PEER_TOPK):
        mx = jnp.max(s, axis=0, keepdims=True)
        hit = s == mx
        if break_ties:
            hit = io == jnp.min(jnp.where(hit, io, n_rows), axis=0, keepdims=True)
        rank = jnp.where(hit, a, rank)
        s = jnp.where(hit, -jnp.inf, s)
        vals = jnp.where(a_io == a, mx, vals)
    return vals, rank


def _peer_select_body(s_ref, scl_ref, l_ref, w1_ref, r2_ref, w2_ref):
    n = s_ref.shape[2]

    def select(hh, break_ties):
        s1 = s_ref[2 * hh]
        s2 = s_ref[2 * hh + 1]
        v1, rank1 = _top16(s1, break_ties)
        v2, rank2 = _top16(s2, break_ties)
        a_io = lax.broadcasted_iota(jnp.int32, (PEER_TOPK, n), 0)
        cnt = jnp.zeros((PEER_TOPK, n), jnp.int32)
        cur = v1 + v2[0:1, :]
        top = v1[0:1, :] + v2[0:1, :]
        z = jnp.zeros((1, n), F32)
        for _ in range(PEER_TOPK):
            mx = jnp.max(cur, axis=0, keepdims=True)
            aidx = jnp.min(jnp.where(cur == mx, a_io, PEER_TOPK), axis=0, keepdims=True)
            hit = a_io == aidx
            cnt = cnt + hit.astype(jnp.int32)
            nxt = jnp.sum(jnp.where(hit, cnt, 0), axis=0, keepdims=True)
            nv = jnp.max(jnp.where(a_io == nxt, v2, -jnp.inf), axis=0, keepdims=True)
            cur = jnp.where(hit, v1 + nv, cur)
            z = z + jnp.exp(mx - top)
        cnt_b = cnt.astype(F32).astype(BF16)
        rank_b = rank1.astype(F32).astype(BF16)
        lrow = jnp.zeros(s1.shape, BF16)
        for a in range(PEER_TOPK):
            lrow = jnp.where(rank_b == a, jnp.broadcast_to(cnt_b[a:a + 1, :], s1.shape), lrow)
        l_ref[hh] = lrow.astype(F32)
        w1_ref[hh] = jnp.exp(s1 - v1[0:1, :])
        r2_ref[hh] = rank2.astype(F32).astype(BF16)
        w2_ref[hh] = (jnp.exp(s2 - v2[0:1, :]) * (scl_ref[1:2, :] / z)).astype(BF16)
        return (jnp.sum((rank1 < PEER_TOPK).astype(jnp.int32), axis=0, keepdims=True)
                + jnp.sum((rank2 < PEER_TOPK).astype(jnp.int32), axis=0, keepdims=True))

    marked = functools.reduce(jnp.maximum, [select(hh, False) for hh in range(SEL_HEADS_PER_STEP)])

    @pl.when(jnp.max(marked) != 2 * PEER_TOPK)
    def _():
        for hh in range(SEL_HEADS_PER_STEP):
            select(hh, True)


def _peer_select(st, scl):
    t_tokens = st.shape[2]
    tm = TM_SEL
    shp = jax.ShapeDtypeStruct((PEER_HEADS, PEER_KEYS, t_tokens), F32)
    shp_b = jax.ShapeDtypeStruct((PEER_HEADS, PEER_KEYS, t_tokens), BF16)
    hps = SEL_HEADS_PER_STEP
    spec = pl.BlockSpec((hps, PEER_KEYS, tm), lambda i, h: (h, 0, i))
    return pl.pallas_call(
        _peer_select_body,
        grid=(t_tokens // tm, PEER_HEADS // hps),
        in_specs=[pl.BlockSpec((2 * hps, PEER_KEYS, tm), lambda i, h: (h, 0, i)),
                  pl.BlockSpec((8, tm), lambda i, h: (0, i))],
        out_specs=(spec, spec, spec, spec),
        out_shape=(shp, shp, shp_b, shp_b),
        compiler_params=_params(("parallel", "parallel"), 32),
        name="peer_select",
    )(st, scl)


def _transpose_body(v_ref, s_ref, o_ref):
    o_ref[...] = (v_ref[...].T * s_ref[0:1, 0:1]).astype(FP8)


def _transpose_fp8(v, scale_tile):
    n, d = v.shape
    tn = 512
    return pl.pallas_call(
        _transpose_body,
        grid=(n // tn,),
        in_specs=[pl.BlockSpec((tn, d), lambda i: (i, 0)),
                  pl.BlockSpec((8, LANES), lambda i: (0, 0))],
        out_specs=pl.BlockSpec((d, tn), lambda i: (0, i)),
        out_shape=jax.ShapeDtypeStruct((d, n), FP8),
        compiler_params=_params(("parallel",), 32),
        name="transpose_v",
    )(v, scale_tile)


def _peer_expert_body(h2t_ref, u_ref, vt_ref, l_ref, w1_ref, r2_ref, w2_ref, scl_ref,
                      o_ref, ce_ref, co_ref, *, steps_per_tile):
    g = pl.program_id(0)
    sw = MXU_COLS
    strips = [slice(c * sw, (c + 1) * sw) for c in range(TM_PEER // sw)]
    n_piece = TE_PEER // PEER_KEYS
    blk = D_MODEL // n_piece

    @pl.when(g == 0)
    def _():
        co_ref[...] = jnp.zeros_like(co_ref)

    @pl.when((g == 0) | ((g - 1) % steps_per_tile == 0))
    def _():
        o_ref[...] = jnp.zeros_like(o_ref)

    def run(c_new, c_old):
        def piece(j, carry):
            r0 = pl.multiple_of(j * PEER_KEYS, PEER_KEYS)
            d0 = pl.multiple_of(j * blk, blk)
            for ls in strips:
                coef = None
                for h in range(PEER_HEADS):
                    lrow = jnp.broadcast_to(l_ref[h, pl.ds(j, 1), ls], (16, sw)).astype(BF16)
                    w1row = jnp.broadcast_to(w1_ref[h, pl.ds(j, 1), ls], (16, sw)).astype(BF16)
                    lrow = jnp.tile(lrow, (PEER_KEYS // 16, 1))
                    w1row = jnp.tile(w1row, (PEER_KEYS // 16, 1))
                    term = jnp.where(r2_ref[h, :, ls] < lrow, w2_ref[h, :, ls] * w1row, jnp.zeros((), BF16))
                    coef = term if coef is None else coef + term
                a_t = jnp.dot(u_ref[pl.ds(r0, PEER_KEYS), :], h2t_ref[:, ls], preferred_element_type=F32)
                a_scale = jnp.broadcast_to(scl_ref[0:1, ls], (16, sw)).astype(BF16)
                act = _gelu(a_t.astype(BF16) * jnp.tile(a_scale, (PEER_KEYS // 16, 1)))
                c_new[pl.ds(r0, PEER_KEYS), ls] = (coef * act).astype(FP8)
                o_ref[pl.ds(d0, blk), ls] += jnp.dot(vt_ref[pl.ds(d0, blk), :], c_old[:, ls],
                                                     preferred_element_type=F32)
            return carry

        lax.fori_loop(0, n_piece, piece, 0, unroll=8)

    @pl.when(g % 2 == 0)
    def _():
        run(ce_ref, co_ref)

    @pl.when(g % 2 == 1)
    def _():
        run(co_ref, ce_ref)


def _peer_expert(h2t, u_b, v_t, lrow, w1, r2, w2, scl):
    t_tokens = h2t.shape[1]
    tm, te = TM_PEER, TE_PEER
    n_piece = te // PEER_KEYS
    ne = PEER_EXPERTS // te
    n_steps = (t_tokens // tm) * ne
    cur = lambda g: jnp.minimum(g, n_steps - 1)
    prev = lambda g: jnp.maximum(g - 1, 0)
    row_spec = pl.BlockSpec((PEER_HEADS, n_piece, tm), lambda g: (0, cur(g) % ne, cur(g) // ne))
    full_spec = pl.BlockSpec((PEER_HEADS, PEER_KEYS, tm), lambda g: (0, 0, cur(g) // ne))
    return pl.pallas_call(
        functools.partial(_peer_expert_body, steps_per_tile=ne),
        grid=(n_steps + 1,),
        in_specs=[pl.BlockSpec((D_MODEL, tm), lambda g: (0, cur(g) // ne)),
                  pl.BlockSpec((te, D_MODEL), lambda g: (cur(g) % ne, 0)),
                  pl.BlockSpec((D_MODEL, te), lambda g: (0, prev(g) % ne)),
                  row_spec, row_spec, full_spec, full_spec,
                  pl.BlockSpec((8, tm), lambda g: (0, cur(g) // ne))],
        out_specs=pl.BlockSpec((D_MODEL, tm), lambda g: (0, prev(g) // ne)),
        out_shape=jax.ShapeDtypeStruct((D_MODEL, t_tokens), F32),
        scratch_shapes=[pltpu.VMEM((te, tm), FP8), pltpu.VMEM((te, tm), FP8)],
        compiler_params=_params(("arbitrary",), 52),
        name="peer_experts",
    )(h2t, u_b, v_t, lrow, w1, r2, w2, scl)


def _final_body(x1_ref, pt_ref, scl_ref, g2_ref, gn_ref, o_ref, *, apply_norm):
    peer = (pt_ref[...] * scl_ref[2:3, :]).T
    y = x1_ref[...] + g2_ref[0] * peer
    if apply_norm:
        ms = jnp.mean(y * y, axis=-1, keepdims=True)
        y = y * lax.rsqrt(ms + EPS) * gn_ref[...]
    o_ref[...] = y


def _final(x1, peer_t, scl, gate2, g_final, seq, apply_norm):
    t_tokens = x1.shape[0]
    tm = TM_PROJ
    tps = seq // tm
    return pl.pallas_call(
        functools.partial(_final_body, apply_norm=apply_norm),
        grid=(t_tokens // tm,),
        in_specs=[pl.BlockSpec((tm, D_MODEL), lambda i: (i, 0)),
                  pl.BlockSpec((D_MODEL, tm), lambda i: (0, i)),
                  pl.BlockSpec((8, tm), lambda i: (0, i)),
                  pl.BlockSpec((1, 1, D_MODEL), lambda i: (i // tps, 0, 0)),
                  pl.BlockSpec((1, D_MODEL), lambda i: (0, 0))],
        out_specs=pl.BlockSpec((tm, D_MODEL), lambda i: (i, 0)),
        out_shape=jax.ShapeDtypeStruct((t_tokens, D_MODEL), F32),
        compiler_params=_params(("parallel",), 32),
        name="final_norm",
    )(x1, peer_t, scl, gate2, g_final)


def _split_cols(a, sizes):
    out, acc = [], 0
    for s in sizes:
        out.append(a[:, acc:acc + s])
        acc += s
    return out


def _inproj_weights(w_in):
    kvw = KV_GROUPS * HEAD_DIM
    sizes = (1024,) + (kvw,) * 6 + (3 * N_HEADS, 1024, 1024, 1024, 1024)
    q_a, k_c, v_c, k_s, v_s, k_w, v_w, g_a, q_r, k_r, v_r, g_r = _split_cols(w_in, sizes)
    d = w_in.shape[0]

    def grp(a, g):
        return a[:, g * HEAD_DIM:(g + 1) * HEAD_DIM]

    cv = [jnp.concatenate([grp(k_c, g), grp(v_c, g)], axis=1) for g in range(KV_GROUPS)]
    w_std = jnp.concatenate(cv + [k_s, k_w, q_r, k_r, v_r, g_r], axis=1).astype(BF16)
    gcols = []
    for g in range(KV_GROUPS):
        for br in range(3):
            for r in range(GROUP):
                c = (g * GROUP + r) * 3 + br
                gcols.append(g_a[:, c:c + 1])
        gcols.append(jnp.zeros((d, 4), w_in.dtype))
    w_tr = jnp.concatenate([q_a, v_s, v_w] + gcols, axis=1).T.astype(BF16)
    return w_std, w_tr


def _compress_weights(pe_k, pe_v, k_w1, k_w2, v_w1, v_w2):
    half = CMP_LEN // 2

    def w1_half(w1k, w1v, lo):
        a = w1k.reshape(CMP_LEN, HEAD_DIM, CMP_HIDDEN)[lo:lo + half]
        b = w1v.reshape(CMP_LEN, HEAD_DIM, CMP_HIDDEN)[lo:lo + half]
        za = jnp.zeros_like(a)
        top = jnp.concatenate([a, za], axis=2)
        bot = jnp.concatenate([za, b], axis=2)
        return jnp.concatenate([top, bot], axis=1).reshape(half * 2 * HEAD_DIM, 2 * CMP_HIDDEN).astype(BF16)

    def pe_half(lo):
        row = jnp.concatenate([pe_k[lo:lo + half], pe_v[lo:lo + half]], axis=1).reshape(1, -1)
        return jnp.broadcast_to(row, (8, row.shape[1])).astype(BF16)

    wa = w1_half(k_w1, v_w1, 0)
    wb = w1_half(k_w1, v_w1, half)
    w2k = jnp.zeros((2 * CMP_HIDDEN, LANES), F32).at[:CMP_HIDDEN, :HEAD_DIM].set(k_w2).astype(BF16)
    w2vt = jnp.zeros((HEAD_DIM, 2 * CMP_HIDDEN), F32).at[:, CMP_HIDDEN:].set(v_w2.T).astype(BF16)
    return wa, wb, pe_half(0), pe_half(half), w2k, w2vt


def _nsa_constants(seq):
    slopes = jnp.exp2(-8.0 * (jnp.arange(N_HEADS, dtype=F32) + 1.0) / N_HEADS) * LOG2E
    s_hi = slopes.astype(BF16)
    s_lo = (slopes - s_hi.astype(F32)).astype(BF16)
    rows = jnp.zeros((N_HEADS, HEAD_DIM), BF16)
    rows = rows.at[:, 0].set(s_hi).at[:, 1].set(s_hi).at[:, 2].set(s_lo).at[:, 3].set(s_lo)
    qaug = jnp.broadcast_to(rows[:, :, None], (N_HEADS, HEAD_DIM, LANES))
    n_rows = seq // CMP_STRIDE
    n_slc = seq // SLC_BLOCK
    start = np.arange(n_rows)[:, None] * CMP_STRIDE
    end = start + CMP_LEN - 1
    blk = np.arange(n_slc)[None, :] * SLC_BLOCK
    ovl = ((start < blk + SLC_BLOCK) & (end >= blk)).astype(np.float32)
    per_tile = TK_SEL // SLC_BLOCK
    grp = (np.arange(n_slc)[None, :] // per_tile == np.arange(n_slc // per_tile)[:, None])
    return qaug, jnp.asarray(ovl.T, BF16), jnp.asarray(grp.astype(np.float32), BF16)


def _retention_constants():
    h, c = RET_HEADS, RET_CHUNK
    lg = jnp.log1p(-jnp.exp2(-5.0 - jnp.arange(h, dtype=F32)))
    pos = jnp.arange(c, dtype=F32)
    diff = pos[:, None] - pos[None, :]
    scale = RET_DIM ** -0.5
    dm = jnp.where(diff >= 0, jnp.exp(lg[:, None, None] * jnp.maximum(diff, 0.0)), 0.0) * scale
    k_decay = jnp.exp(lg[:, None] * (c - 1.0 - pos)) * scale
    q_decay = jnp.exp(lg[:, None] * (pos + 1.0))
    chunk_decay = jnp.exp(lg * c)
    kd = jnp.broadcast_to(k_decay[:, :, None], (h, c, RET_DIM))
    qd = jnp.broadcast_to(q_decay[:, :, None], (h, c, RET_DIM))
    cd = jnp.broadcast_to(chunk_decay[:, None, None], (h, RET_DIM, RET_DIM))
    return dm, kd, qd, cd


def kernel(x, c, w_ada, b_ada, g_norm_mix, g_norm_ffn, g_norm_final, w_in, cmp_pe_k, cmp_pe_v,
           cmp_k_w1, cmp_k_w2, cmp_v_w1, cmp_v_w2, g_nsa_out, g_ret_out, w_out,
           peer_w_q, peer_sub_keys, peer_u, peer_v):
    batch, seq, d = x.shape
    depth = w_ada.shape[0]
    t_tokens = batch * seq
    xf = x.reshape(t_tokens, d)
    c_pad = jnp.zeros((8, d), F32).at[:batch].set(c)
    qaug, ovl_t, grp = _nsa_constants(seq)
    dm, kd, qd, cd = _retention_constants()

    for l in range(depth):
        mod = _adaln(c_pad, w_ada[l], b_ada[l][None, :])[:batch].reshape(batch, 6, 1, d)
        shift1, scale1, gate1, shift2, scale2, gate2 = (mod[:, k] for k in range(6))

        w_std, w_tr = _inproj_weights(w_in[l])
        (cv, ks, kw, q_r, k_r, v_r, g_r, qt, vst, vwt, gt) = _inproj(
            xf, scale1, shift1, g_norm_mix[l][None, :], w_std, w_tr, seq)

        cv4 = cv.reshape(KV_GROUPS, batch, seq // CMP_STRIDE, CMP_STRIDE * LANES)
        kcp, vct = _compress(cv4, *_compress_weights(cmp_pe_k[l], cmp_pe_v[l], cmp_k_w1[l], cmp_k_w2[l],
                                                     cmp_v_w1[l], cmp_v_w2[l]))
        gout_b = jnp.broadcast_to(g_nsa_out[l][:, :, None], (N_HEADS, HEAD_DIM, LANES))
        o_nsa = _nsa(qt, gt, kcp, vct, ks, kw, vst, vwt, qaug, gout_b, ovl_t, grp, batch, seq)

        go = jnp.broadcast_to(g_ret_out[l][:, None, :], (RET_HEADS, 8, RET_DIM))
        o_ret = _retention(q_r, k_r, v_r, g_r, dm, kd, qd, cd, go, batch, seq)

        sub_keys = peer_sub_keys[l].reshape(2 * PEER_HEADS, PEER_KEYS, -1).astype(BF16)
        s_u = _pow2_scale(jnp.max(jnp.abs(peer_u[l])))
        s_v = _pow2_scale(jnp.max(jnp.abs(peer_v[l])))
        u_norm = jnp.sqrt(jnp.max(jnp.sum(jnp.square(peer_u[l]), axis=1)))
        peer_scales = jnp.broadcast_to(
            jnp.stack([1.0 / s_u, 1.0 / s_v, u_norm] + [jnp.zeros((), F32)] * 5)[:, None], (8, LANES))
        x1, h2t, st, scl = _mid(o_nsa, o_ret, xf, gate1, scale2, shift2, g_norm_ffn[l][None, :],
                                w_out[l].astype(BF16), peer_w_q[l].T.astype(BF16), sub_keys, peer_scales, seq)

        lrow, w1, r2, w2 = _peer_select(st, scl)
        peer_t = _peer_expert(h2t, (peer_u[l] * s_u).astype(FP8),
                              _transpose_fp8(peer_v[l], jnp.full((8, LANES), s_v, F32)),
                              lrow, w1, r2, w2, scl)
        xf = _final(x1, peer_t, scl, gate2, g_norm_final[None, :], seq, apply_norm=(l == depth - 1))
    return xf.reshape(batch, seq, d)
```

```python
import functools
import math

import numpy as np
import jax
import jax.numpy as jnp
from jax import lax
from jax.experimental import pallas as pl
from jax.experimental.pallas import tpu as pltpu

F32 = jnp.float32
BF16 = jnp.bfloat16
FP8 = jnp.float8_e4m3fn
FP8_TARGET = 224.0

D_MODEL = 2048
N_HEADS = 16
HEAD_DIM = 64
KV_GROUPS = 4
GROUP = 4
CMP_LEN = 32
CMP_STRIDE = 16
CMP_HIDDEN = 128
SLC_BLOCK = 64
SLC_TOPK = 16
WINDOW = 512
NEG = -1e30
RET_HEADS = 8
RET_DIM = 128
RET_CHUNK = 128
PEER_HEADS = 8
PEER_KEYS = 128
PEER_EXPERTS = PEER_KEYS * PEER_KEYS
PEER_TOPK = 16
COEF_BOUND_FACTOR = 1.25 * PEER_HEADS
EPS = 1e-6
LOG2E = 1.4426950408889634

LANES = 128
MXU_COLS = 256
TQ = 128
NSA_TILES_PER_STEP = 4
TK_SEL = 256
TM_PROJ = 256
TM_PEER = 512
TE_PEER = 1024
TM_SEL = 256
SEL_HEADS_PER_STEP = 4
RET_TILE = 512

_NT = (((1,), (1,)), ((), ()))
_TN = (((0,), (0,)), ((), ()))


def _params(sem, vmem_mb):
    return pltpu.CompilerParams(dimension_semantics=sem, vmem_limit_bytes=vmem_mb * 1024 * 1024)


def _resident(shape, index_map):
    return pl.BlockSpec(shape, index_map, pipeline_mode=pl.Buffered(1))


def _gelu(x):
    return jax.nn.gelu(x)


def _adaln_body(c_ref, w_ref, b_ref, o_ref):
    c = c_ref[...]
    act = (c * jax.nn.sigmoid(c)).astype(BF16)
    o_ref[...] = jnp.dot(act, w_ref[...].astype(BF16), preferred_element_type=F32) + b_ref[...]


def _adaln(c_pad, w, b):
    n = w.shape[1]
    tn = 1536
    return pl.pallas_call(
        _adaln_body,
        grid=(n // tn,),
        in_specs=[pl.BlockSpec((8, D_MODEL), lambda j: (0, 0)),
                  pl.BlockSpec((D_MODEL, tn), lambda j: (0, j)),
                  pl.BlockSpec((1, tn), lambda j: (0, j))],
        out_specs=pl.BlockSpec((8, tn), lambda j: (0, j)),
        out_shape=jax.ShapeDtypeStruct((8, n), F32),
        compiler_params=_params(("arbitrary",), 40),
        name="adaln",
    )(c_pad, w, b)


STD_COLS = 512 + 256 * 2 + 1024 * 4
TR_ROWS = 1024 + 256 + 256 + 64


def _inproj_body(x_ref, sc_ref, sh_ref, gn_ref, wstd_ref, wt_ref,
                 cv_ref, ks_ref, kw_ref, qr_ref, kr_ref, vr_ref, gr_ref,
                 qt_ref, vst_ref, vwt_ref, gt_ref, *, tiles_per_seq):
    tm = TM_PROJ
    i = pl.program_id(0)
    x = x_ref[...]
    ms = jnp.mean(x * x, axis=-1, keepdims=True)
    h = x * lax.rsqrt(ms + EPS) * gn_ref[...]
    h = h * (1.0 + sc_ref[0]) + sh_ref[0]
    hb = h.astype(BF16)

    def std(a, b):
        return jnp.dot(hb, wstd_ref[:, a:b], preferred_element_type=F32)

    y = std(0, 512)
    for g in range(KV_GROUPS):
        cv_ref[g] = y[:, g * LANES:(g + 1) * LANES].astype(BF16)

    t = (i % tiles_per_seq) * tm + lax.broadcasted_iota(jnp.int32, (tm, LANES), 0)
    lane = lax.broadcasted_iota(jnp.int32, (tm, LANES), 1)
    pos_hi = ((t >> 6) << 6).astype(F32)
    pos_lo = (t & 63).astype(F32)
    aug = jnp.where((lane == 64) | (lane == 66), pos_hi,
                    jnp.where((lane == 65) | (lane == 67), pos_lo, 0.0))
    for ref, off in ((ks_ref, 512), (kw_ref, 768)):
        y = std(off, off + 256)
        for g in range(KV_GROUPS):
            pair = y[:, (g // 2) * LANES:(g // 2 + 1) * LANES]
            keys = pair if g % 2 == 0 else pltpu.roll(pair, HEAD_DIM, 1)
            ref[:, g * LANES:(g + 1) * LANES] = jnp.where(lane < HEAD_DIM, keys, aug).astype(BF16)

    for ref, off in ((qr_ref, 1024), (kr_ref, 2048), (vr_ref, 3072), (gr_ref, 4096)):
        ref[...] = std(off, off + 1024).astype(BF16)

    def tr(a, b):
        return lax.dot_general(wt_ref[a:b, :], hb, _NT, preferred_element_type=F32)

    qt = tr(0, 1024) * (HEAD_DIM ** -0.5 * LOG2E)
    vst = tr(1024, 1280)
    vwt = tr(1280, 1536)
    gt = tr(1536, 1600)
    for c in range(tm // LANES):
        sl = slice(c * LANES, (c + 1) * LANES)
        qt_ref[c] = qt[:, sl].astype(BF16)
        vst_ref[c] = vst[:, sl].astype(BF16)
        vwt_ref[c] = vwt[:, sl].astype(BF16)
        gt_ref[c] = gt[:, sl]


def _inproj(x2, scale1, shift1, g_mix, w_std, w_tr, seq):
    t_tokens = x2.shape[0]
    tm = TM_PROJ
    tps = seq // tm
    nt = t_tokens // tm
    c = tm // LANES
    row = lambda i: (i, 0)
    per_b = lambda i: (i // tps, 0, 0)
    out_shape = (
        jax.ShapeDtypeStruct((KV_GROUPS, t_tokens, LANES), BF16),
        jax.ShapeDtypeStruct((t_tokens, 512), BF16),
        jax.ShapeDtypeStruct((t_tokens, 512), BF16),
        jax.ShapeDtypeStruct((t_tokens, 1024), BF16),
        jax.ShapeDtypeStruct((t_tokens, 1024), BF16),
        jax.ShapeDtypeStruct((t_tokens, 1024), BF16),
        jax.ShapeDtypeStruct((t_tokens, 1024), BF16),
        jax.ShapeDtypeStruct((t_tokens // LANES, 1024, LANES), BF16),
        jax.ShapeDtypeStruct((t_tokens // LANES, 256, LANES), BF16),
        jax.ShapeDtypeStruct((t_tokens // LANES, 256, LANES), BF16),
        jax.ShapeDtypeStruct((t_tokens // LANES, 64, LANES), F32),
    )
    out_specs = (
        pl.BlockSpec((KV_GROUPS, tm, LANES), lambda i: (0, i, 0)),
        pl.BlockSpec((tm, 512), row),
        pl.BlockSpec((tm, 512), row),
        pl.BlockSpec((tm, 1024), row),
        pl.BlockSpec((tm, 1024), row),
        pl.BlockSpec((tm, 1024), row),
        pl.BlockSpec((tm, 1024), row),
        pl.BlockSpec((c, 1024, LANES), lambda i: (i, 0, 0)),
        pl.BlockSpec((c, 256, LANES), lambda i: (i, 0, 0)),
        pl.BlockSpec((c, 256, LANES), lambda i: (i, 0, 0)),
        pl.BlockSpec((c, 64, LANES), lambda i: (i, 0, 0)),
    )
    return pl.pallas_call(
        functools.partial(_inproj_body, tiles_per_seq=tps),
        grid=(nt,),
        in_specs=[pl.BlockSpec((tm, D_MODEL), row),
                  pl.BlockSpec((1, 1, D_MODEL), per_b),
                  pl.BlockSpec((1, 1, D_MODEL), per_b),
                  _resident((1, D_MODEL), lambda i: (0, 0)),
                  _resident((D_MODEL, STD_COLS), lambda i: (0, 0)),
                  _resident((TR_ROWS, D_MODEL), lambda i: (0, 0))],
        out_specs=out_specs,
        out_shape=out_shape,
        compiler_params=_params(("parallel",), 56),
        name="inproj",
    )(x2, scale1, shift1, g_mix, w_std, w_tr)


def _compress_body(x_ref, wa_ref, wb_ref, pea_ref, peb_ref, w2k_ref, w2vt_ref, kcp_ref, vct_ref):
    x = x_ref[0, 0]
    n_rows = x.shape[0]
    p = jnp.dot(x, wa_ref[...], preferred_element_type=F32)
    q = jnp.dot(x, wb_ref[...], preferred_element_type=F32)
    pe = (jnp.dot(pea_ref[...], wa_ref[...], preferred_element_type=F32)
          + jnp.dot(peb_ref[...], wb_ref[...], preferred_element_type=F32))[0:1, :]
    pre = p + pltpu.roll(q, n_rows - 1, 0) + pe
    hid = _gelu(pre).astype(BF16)
    kc = jnp.dot(hid, w2k_ref[...], preferred_element_type=F32)
    n = lax.broadcasted_iota(jnp.int32, (n_rows, LANES), 0)
    lane = lax.broadcasted_iota(jnp.int32, (n_rows, LANES), 1)
    ce = n * CMP_STRIDE + (CMP_LEN - 1)
    ce_hi = ((ce >> 6) << 6).astype(F32)
    ce_lo = (ce & 63).astype(F32)
    aug = jnp.where((lane == 64) | (lane == 66), ce_hi,
                    jnp.where((lane == 65) | (lane == 67), ce_lo, 0.0))
    kcp_ref[0] = (kc + aug).astype(BF16)
    vct_ref[0] = lax.dot_general(w2vt_ref[...], hid, _NT, preferred_element_type=F32).astype(BF16)


def _compress(cv4, wa, wb, pea, peb, w2k, w2vt):
    g_, b_, n_rows, _ = cv4.shape
    const2 = lambda n: (0, 0)
    return pl.pallas_call(
        _compress_body,
        grid=(b_ * g_,),
        in_specs=[pl.BlockSpec((1, 1, n_rows, 2048), lambda n: (n % KV_GROUPS, n // KV_GROUPS, 0, 0)),
                  pl.BlockSpec((2048, 256), const2),
                  pl.BlockSpec((2048, 256), const2),
                  pl.BlockSpec((8, 2048), const2),
                  pl.BlockSpec((8, 2048), const2),
                  pl.BlockSpec((256, LANES), const2),
                  pl.BlockSpec((64, 256), const2)],
        out_specs=(pl.BlockSpec((1, n_rows, LANES), lambda n: (n, 0, 0)),
                   pl.BlockSpec((1, 64, n_rows), lambda n: (n, 0, 0))),
        out_shape=(jax.ShapeDtypeStruct((b_ * g_, n_rows, LANES), BF16),
                   jax.ShapeDtypeStruct((b_ * g_, 64, n_rows), BF16)),
        compiler_params=_params(("parallel",), 32),
        name="nsa_compress",
    )(cv4, wa, wb, pea, peb, w2k, w2vt)


def _softmax_step(state, s, pv_prev):
    m_i, l_i, acc = state
    m_new = jnp.maximum(m_i, jnp.max(s, axis=0, keepdims=True))
    alpha = jnp.exp2(m_i - m_new)
    p = jnp.exp2(s - m_new)
    l_new = alpha * l_i + jnp.sum(p, axis=0, keepdims=True)
    return (m_new, l_new, (acc + pv_prev) * alpha), p.astype(BF16)


def _nsa_body(qt_ref, gt_ref, kcp_ref, vct_ref, ks_ref, kw_ref, vst_ref, vwt_ref,
              qaug_ref, gout_ref, ovl_ref, grp_ref, o_ref, *scratch, n_cmp):
    per_tile = len(scratch) // NSA_TILES_PER_STEP
    for sub in range(NSA_TILES_PER_STEP):
        _nsa_tile(pl.program_id(2) * NSA_TILES_PER_STEP + sub,
                  qt_ref.at[pl.ds(sub, 1)], gt_ref.at[pl.ds(sub, 1)], kcp_ref, vct_ref, ks_ref, kw_ref,
                  vst_ref, vwt_ref, qaug_ref, gout_ref, ovl_ref, grp_ref,
                  o_ref.at[pl.ds(sub * TQ, TQ)], *scratch[sub * per_tile:(sub + 1) * per_tile], n_cmp=n_cmp)


def _nsa_tile(qi, qt_ref, gt_ref, kcp_ref, vct_ref, ks_ref, kw_ref, vst_ref, vwt_ref,
              qaug_ref, gout_ref, ovl_ref, grp_ref, o_ref, selb_ref, sa_ref, sb_ref, pa_ref, pb_ref, *, n_cmp):
    t0 = qi * TQ
    wq = GROUP * TQ

    qp = jnp.concatenate(
        [jnp.concatenate([qt_ref[0, r * HEAD_DIM:(r + 1) * HEAD_DIM, :], qaug_ref[r]], axis=0)
         for r in range(GROUP)], axis=1)

    lane_q = lax.broadcasted_iota(jnp.int32, (1, wq), 1) & (TQ - 1)
    n_win = WINDOW // TQ + 1
    k_lo = pl.multiple_of(jnp.maximum(t0 - WINDOW, 0), TQ)

    s = jnp.dot(kcp_ref[0], qp, preferred_element_type=F32)
    s_w = jnp.dot(kw_ref[pl.ds(k_lo, n_win * TQ), :], qp, preferred_element_type=F32)
    n_io = lax.broadcasted_iota(jnp.int32, (n_cmp, wq), 0)
    tl = lax.broadcasted_iota(jnp.int32, (n_cmp, wq), 1) & (TQ - 1)
    valid = (n_io * CMP_STRIDE + (CMP_LEN - 1)) <= (t0 + tl)
    s = jnp.where(valid, s, NEG)
    m = jnp.maximum(jnp.max(s, axis=0, keepdims=True), 0.5 * NEG)
    p = jnp.exp2(s - m)
    l = jnp.sum(p, axis=0, keepdims=True)
    pn = p * (1.0 / jnp.maximum(l, 1e-30))
    o_c = jnp.dot(vct_ref[0], pn.astype(BF16), preferred_element_type=F32)

    ps = pn[:, 0:TQ] + pn[:, TQ:2 * TQ] + pn[:, 2 * TQ:3 * TQ] + pn[:, 3 * TQ:4 * TQ]
    hi = ps.astype(BF16)
    r1 = ps - hi.astype(F32)
    mid = r1.astype(BF16)
    lo = (r1 - mid.astype(F32)).astype(BF16)
    ovl = ovl_ref[...]
    imp = (jnp.dot(ovl, hi, preferred_element_type=F32)
           + jnp.dot(ovl, mid, preferred_element_type=F32)
           + jnp.dot(ovl, lo, preferred_element_type=F32))

    n_slc = imp.shape[0]
    m_io = lax.broadcasted_iota(jnp.int32, (n_slc, TQ), 0)
    q_io = lax.broadcasted_iota(jnp.int32, (n_slc, TQ), 1)
    back = ((t0 + q_io) >> 6) - m_io
    valid_s = back >= 0
    forced = valid_s & ((m_io == 0) | (back < 2))
    w = jnp.where(forced, -jnp.inf, jnp.where(valid_s, imp, -1.0))
    selb = jnp.where(forced, 0.0, NEG)

    def pick(carry, lanes=None):
        w, selb = carry
        mx = jnp.max(w, axis=0, keepdims=True)
        idx = jnp.min(jnp.where(w == mx, m_io, n_slc), axis=0, keepdims=True)
        hit = m_io == idx
        if lanes is not None:
            hit = hit & lanes
        return jnp.where(hit, -jnp.inf, w), jnp.where(hit, 0.0, selb)

    carry = (w, selb)
    for _ in range(SLC_TOPK - 3):
        carry = pick(carry)
    carry = pick(carry, lanes=(t0 + q_io) < 2 * SLC_BLOCK)
    _, selb = pick(carry, lanes=(t0 + q_io) < SLC_BLOCK)
    selb_ref[...] = selb

    per_blk = TK_SEL // SLC_BLOCK

    def scores(j):
        k0 = pl.multiple_of(j * TK_SEL, TK_SEL)
        bias = jnp.concatenate(
            [jnp.broadcast_to(selb_ref[pl.ds(j * per_blk + u, 1), :], (SLC_BLOCK, TQ)) for u in range(per_blk)],
            axis=0)
        bias = jnp.concatenate([bias] * GROUP, axis=1)
        return jnp.dot(ks_ref[pl.ds(k0, TK_SEL), :], qp, preferred_element_type=F32) + bias

    def vt_tile(j):
        return jnp.concatenate([vst_ref[2 * j], vst_ref[2 * j + 1]], axis=1)

    def causal(j):
        kr = lax.broadcasted_iota(jnp.int32, (TK_SEL, wq), 0)
        return kr <= (t0 - j * TK_SEL) + lane_q

    def pv_dot(j, p_ref):
        return jnp.dot(vt_tile(j), p_ref[...], preferred_element_type=F32)

    jd = qi // (TK_SEL // TQ)

    sa_ref[...] = scores(0)
    pb_ref[...] = jnp.zeros_like(pb_ref)
    sel01 = jnp.where(selb == 0.0, 1.0, 0.0).astype(BF16)
    tile_any = jnp.max(jnp.dot(grp_ref[...], sel01, preferred_element_type=F32), axis=1, keepdims=True)
    j_io = lax.broadcasted_iota(jnp.int32, tile_any.shape, 0)
    q_lo_v = jnp.max(jnp.where((tile_any == 0.0) & (j_io <= jd), j_io, -1), axis=0, keepdims=True) + 1
    n_pre_v = jnp.max(jnp.where((tile_any > 0.0) & (j_io < q_lo_v), j_io, -1), axis=0, keepdims=True) + 1

    d = (t0 - k_lo) + lane_q - lax.broadcasted_iota(jnp.int32, (n_win * TQ, wq), 0)
    s_w = jnp.where(lax.bitcast_convert_type(d, jnp.uint32) < WINDOW, s_w, NEG)
    p = jnp.exp2(s_w - jnp.max(s_w, axis=0, keepdims=True))
    l_w = jnp.sum(p, axis=0, keepdims=True)
    vt_w = jnp.concatenate([vwt_ref[k_lo // TQ + u] for u in range(n_win)], axis=1)
    o_w = jnp.dot(vt_w, p.astype(BF16), preferred_element_type=F32) * (1.0 / l_w)

    q_lo = q_lo_v[0, 0]
    n_pre = n_pre_v[0, 0]
    n_vis = n_pre + jd - q_lo + 1

    def tile_at(pos):
        return jnp.where(pos < n_pre, pos, pos - n_pre + q_lo)

    state = (jnp.full((1, wq), NEG, F32), jnp.zeros((1, wq), F32), jnp.zeros((HEAD_DIM, wq), F32))

    def pair(u, state):
        a = 2 * u
        sb_ref[...] = scores(tile_at(a + 1))
        state, p = _softmax_step(state, sa_ref[...], pv_dot(tile_at(jnp.maximum(a - 1, 0)), pb_ref))
        pa_ref[...] = p
        sa_ref[...] = scores(tile_at(a + 2))
        state, p = _softmax_step(state, sb_ref[...], pv_dot(tile_at(a), pa_ref))
        pb_ref[...] = p
        return state

    n_pair = (n_vis - 1) // 2
    state = lax.fori_loop(0, n_pair, pair, state)
    x = 2 * n_pair
    y = jnp.minimum(x + 1, n_vis - 1)
    tx, ty = tile_at(x), tile_at(y)
    sb_ref[...] = scores(ty)
    state, p = _softmax_step(state, jnp.where(causal(tx), sa_ref[...], NEG),
                             pv_dot(tile_at(jnp.maximum(x - 1, 0)), pb_ref))
    pa_ref[...] = p
    ty_mask = jnp.where(x + 1 < n_vis, ty, jd + 1)
    (_, l_s, acc_s), p = _softmax_step(state, jnp.where(causal(ty_mask), sb_ref[...], NEG), pv_dot(tx, pa_ref))
    acc_s = acc_s + jnp.dot(vt_tile(ty), p, preferred_element_type=F32)

    o_s = acc_s * (1.0 / l_s)
    gw = jax.nn.sigmoid(gt_ref[0])
    outs = []
    for r in range(GROUP):
        sl = slice(r * TQ, (r + 1) * TQ)
        o = (gw[r:r + 1, :] * o_c[:, sl] + gw[GROUP + r:GROUP + r + 1, :] * o_s[:, sl]
             + gw[2 * GROUP + r:2 * GROUP + r + 1, :] * o_w[:, sl])
        ms = jnp.mean(o * o, axis=0, keepdims=True)
        outs.append(o * lax.rsqrt(ms + EPS) * gout_ref[r])
    o_ref[...] = jnp.concatenate(outs, axis=0).T.astype(BF16)


def _nsa(qt, gt, kcp, vct, ks, kw, vst, vwt, qaug, gout_b, ovl_t, grp, batch, seq):
    tps = NSA_TILES_PER_STEP
    nq = seq // (TQ * tps)
    n_cmp = kcp.shape[1]
    n_slc = seq // SLC_BLOCK
    t_tokens = batch * seq
    per_b_chunks = seq // LANES
    return pl.pallas_call(
        functools.partial(_nsa_body, n_cmp=n_cmp),
        grid=(batch, KV_GROUPS, nq),
        in_specs=[
            pl.BlockSpec((tps, 256, LANES), lambda b, g, q: (b * nq + q, g, 0)),
            pl.BlockSpec((tps, 16, LANES), lambda b, g, q: (b * nq + q, g, 0)),
            pl.BlockSpec((1, n_cmp, LANES), lambda b, g, q: (b * KV_GROUPS + g, 0, 0)),
            pl.BlockSpec((1, 64, n_cmp), lambda b, g, q: (b * KV_GROUPS + g, 0, 0)),
            pl.BlockSpec((seq, LANES), lambda b, g, q: (b, g)),
            pl.BlockSpec((seq, LANES), lambda b, g, q: (b, g)),
            pl.BlockSpec((per_b_chunks, 64, LANES), lambda b, g, q: (b, g, 0)),
            pl.BlockSpec((per_b_chunks, 64, LANES), lambda b, g, q: (b, g, 0)),
            pl.BlockSpec((GROUP, 64, LANES), lambda b, g, q: (g, 0, 0)),
            pl.BlockSpec((GROUP, 64, LANES), lambda b, g, q: (g, 0, 0)),
            pl.BlockSpec((n_slc, n_cmp), lambda b, g, q: (0, 0)),
            pl.BlockSpec(grp.shape, lambda b, g, q: (0, 0)),
        ],
        out_specs=pl.BlockSpec((tps * TQ, 256), lambda b, g, q: (b * nq + q, g)),
        out_shape=jax.ShapeDtypeStruct((t_tokens, 1024), BF16),
        scratch_shapes=[pltpu.VMEM((n_slc, TQ), F32),
                        pltpu.VMEM((TK_SEL, GROUP * TQ), F32), pltpu.VMEM((TK_SEL, GROUP * TQ), F32),
                        pltpu.VMEM((TK_SEL, GROUP * TQ), BF16), pltpu.VMEM((TK_SEL, GROUP * TQ), BF16)] * tps,
        compiler_params=_params(("parallel", "parallel", "arbitrary"), 40),
        name="nsa_attention",
    )(qt, gt, kcp, vct, ks, kw, vst, vwt, qaug, gout_b, ovl_t, grp)


def _ret_body(q_ref, k_ref, v_ref, g_ref, dm_ref, kd_ref, qd_ref, cd_ref, go_ref, o_ref, st_ref):
    @pl.when(pl.program_id(2) == 0)
    def _():
        st_ref[...] = jnp.zeros_like(st_ref)

    c_ = RET_CHUNK
    slices = [slice(c * c_, (c + 1) * c_) for c in range(RET_TILE // c_)]
    atts, kvs = [], []
    for sl in slices:
        k = k_ref[sl, :]
        atts.append((lax.dot_general(q_ref[sl, :], k, _NT, preferred_element_type=F32) * dm_ref[0]).astype(BF16))
        kdec = (k.astype(F32) * kd_ref[0]).astype(BF16)
        kvs.append(lax.dot_general(kdec, v_ref[sl, :], _TN, preferred_element_type=F32))
    states = [st_ref[...]]
    for kv in kvs:
        states.append(states[-1] * cd_ref[0] + kv)
    st_ref[...] = states[-1]
    for sl, att, state in zip(slices, atts, states):
        o = (jnp.dot(att, v_ref[sl, :], preferred_element_type=F32)
             + qd_ref[0] * jnp.dot(q_ref[sl, :], state.astype(BF16), preferred_element_type=F32))
        mu = jnp.mean(o, axis=-1, keepdims=True)
        oc = o - mu
        var = jnp.mean(oc * oc, axis=-1, keepdims=True)
        y = oc * lax.rsqrt(var + EPS) * go_ref[0, 0:1, :]
        gate = g_ref[sl, :].astype(F32)
        o_ref[sl, :] = (gate * jax.nn.sigmoid(gate) * y).astype(BF16)


def _retention(q_r, k_r, v_r, g_r, dm, kd, qd, cd, go, batch, seq):
    t_tokens = batch * seq
    nc = seq // RET_TILE
    tok = lambda b, h, c: (b * nc + c, h)
    per_h = lambda b, h, c: (h, 0, 0)
    sq = (1, RET_DIM, RET_DIM)
    return pl.pallas_call(
        _ret_body,
        grid=(batch, RET_HEADS, nc),
        in_specs=[pl.BlockSpec((RET_TILE, RET_DIM), tok)] * 4
        + [pl.BlockSpec(sq, per_h)] * 4 + [pl.BlockSpec((1, 8, RET_DIM), per_h)],
        out_specs=pl.BlockSpec((RET_TILE, RET_DIM), tok),
        out_shape=jax.ShapeDtypeStruct((t_tokens, RET_HEADS * RET_DIM), BF16),
        scratch_shapes=[pltpu.VMEM((RET_DIM, RET_DIM), F32)],
        compiler_params=_params(("parallel", "parallel", "arbitrary"), 32),
        name="retention",
    )(q_r, k_r, v_r, g_r, dm, kd, qd, cd, go)


def _pow2_scale(magnitude):
    return jnp.exp2(jnp.floor(jnp.log2(FP8_TARGET / jnp.maximum(magnitude, 1e-30))))


def _mid_body(on_ref, or_ref, x_ref, g1_ref, sc_ref, sh_ref, gn_ref, wo_ref, wqt_ref, sk_ref, ps_ref,
              x1_ref, h2t_ref, st_ref, scl_ref):
    acc = (jnp.dot(on_ref[...], wo_ref[0:1024, :], preferred_element_type=F32)
           + jnp.dot(or_ref[...], wo_ref[1024:2048, :], preferred_element_type=F32))
    x1 = x_ref[...] + g1_ref[0] * acc
    x1_ref[...] = x1
    ms = jnp.mean(x1 * x1, axis=-1, keepdims=True)
    h2 = x1 * lax.rsqrt(ms + EPS) * gn_ref[...]
    h2 = h2 * (1.0 + sc_ref[0]) + sh_ref[0]

    h2_t = h2.T
    amax = jnp.max(jnp.max(jnp.abs(h2_t), axis=0, keepdims=True), axis=1, keepdims=True)
    s_h = _pow2_scale(amax)
    h2t_ref[...] = (h2_t * s_h).astype(FP8)
    norm = jnp.sqrt(jnp.sum(h2_t * h2_t, axis=0, keepdims=True))
    s_c = _pow2_scale(COEF_BOUND_FACTOR * ps_ref[2:3, 0:1] * norm)
    scl_ref[...] = jnp.concatenate(
        [jnp.broadcast_to(ps_ref[0:1, 0:1] / s_h, s_c.shape), s_c, ps_ref[1:2, 0:1] / s_c,
         jnp.zeros((5, s_c.shape[1]), F32)], axis=0)
    h2 = h2.astype(BF16)
    qt = lax.dot_general(wqt_ref[...], h2, _NT, preferred_element_type=F32).astype(BF16)
    for hp in range(2 * PEER_HEADS):
        st_ref[hp] = jnp.dot(sk_ref[hp], qt[hp * 128:(hp + 1) * 128, :], preferred_element_type=F32)


def _mid(o_nsa, o_ret, x2, gate1, scale2, shift2, g_ffn, w_out, wq_t, sub_keys, peer_scales, seq):
    t_tokens = x2.shape[0]
    tm = TM_PROJ
    tps = seq // tm
    row = lambda i: (i, 0)
    per_b = lambda i: (i // tps, 0, 0)
    return pl.pallas_call(
        _mid_body,
        grid=(t_tokens // tm,),
        in_specs=[pl.BlockSpec((tm, 1024), row),
                  pl.BlockSpec((tm, 1024), row),
                  pl.BlockSpec((tm, D_MODEL), row),
                  pl.BlockSpec((1, 1, D_MODEL), per_b),
                  pl.BlockSpec((1, 1, D_MODEL), per_b),
                  pl.BlockSpec((1, 1, D_MODEL), per_b),
                  _resident((1, D_MODEL), lambda i: (0, 0)),
                  _resident((D_MODEL, D_MODEL), lambda i: (0, 0)),
                  _resident((D_MODEL, D_MODEL), lambda i: (0, 0)),
                  _resident((2 * PEER_HEADS, PEER_KEYS, 128), lambda i: (0, 0, 0)),
                  _resident((8, LANES), lambda i: (0, 0))],
        out_specs=(pl.BlockSpec((tm, D_MODEL), row),
                   pl.BlockSpec((D_MODEL, tm), lambda i: (0, i)),
                   pl.BlockSpec((2 * PEER_HEADS, PEER_KEYS, tm), lambda i: (0, 0, i)),
                   pl.BlockSpec((8, tm), lambda i: (0, i))),
        out_shape=(jax.ShapeDtypeStruct((t_tokens, D_MODEL), F32),
                   jax.ShapeDtypeStruct((D_MODEL, t_tokens), FP8),
                   jax.ShapeDtypeStruct((2 * PEER_HEADS, PEER_KEYS, t_tokens), F32),
                   jax.ShapeDtypeStruct((8, t_tokens), F32)),
        compiler_params=_params(("parallel",), 48),
        name="outproj_peerq",
    )(o_nsa, o_ret, x2, gate1, scale2, shift2, g_ffn, w_out, wq_t, sub_keys, peer_scales)


def _top16(s, break_ties):
    n_rows, n = s.shape
    io = lax.broadcasted_iota(jnp.int32, (n_rows, n), 0)
    a_io = lax.broadcasted_iota(jnp.int32, (PEER_TOPK, n), 0)
    rank = jnp.full((n_rows, n), PEER_TOPK, jnp.int32)
    vals = jnp.zeros((PEER_TOPK, n), F32)
    for a in range(PEER_TOPK):
        mx = jnp.max(s, axis=0, keepdims=True)
        hit = s == mx
        if break_ties:
            hit = io == jnp.min(jnp.where(hit, io, n_rows), axis=0, keepdims=True)
        rank = jnp.where(hit, a, rank)
        s = jnp.where(hit, -jnp.inf, s)
        vals = jnp.where(a_io == a, mx, vals)
    return vals, rank


def _peer_select_body(s_ref, scl_ref, l_ref, w1_ref, r2_ref, w2_ref):
    n = s_ref.shape[2]

    def select(hh, break_ties):
        s1 = s_ref[2 * hh]
        s2 = s_ref[2 * hh + 1]
        v1, rank1 = _top16(s1, break_ties)
        v2, rank2 = _top16(s2, break_ties)
        a_io = lax.broadcasted_iota(jnp.int32, (PEER_TOPK, n), 0)
        cnt = jnp.zeros((PEER_TOPK, n), jnp.int32)
        cur = v1 + v2[0:1, :]
        top = v1[0:1, :] + v2[0:1, :]
        z = jnp.zeros((1, n), F32)
        for _ in range(PEER_TOPK):
            mx = jnp.max(cur, axis=0, keepdims=True)
            aidx = jnp.min(jnp.where(cur == mx, a_io, PEER_TOPK), axis=0, keepdims=True)
            hit = a_io == aidx
            cnt = cnt + hit.astype(jnp.int32)
            nxt = jnp.sum(jnp.where(hit, cnt, 0), axis=0, keepdims=True)
            nv = jnp.max(jnp.where(a_io == nxt, v2, -jnp.inf), axis=0, keepdims=True)
            cur = jnp.where(hit, v1 + nv, cur)
            z = z + jnp.exp(mx - top)
        cnt_b = cnt.astype(F32).astype(BF16)
        rank_b = rank1.astype(F32).astype(BF16)
        lrow = jnp.zeros(s1.shape, BF16)
        for a in range(PEER_TOPK):
            lrow = jnp.where(rank_b == a, jnp.broadcast_to(cnt_b[a:a + 1, :], s1.shape), lrow)
        l_ref[hh] = lrow.astype(F32)
        w1_ref[hh] = jnp.exp(s1 - v1[0:1, :])
        r2_ref[hh] = rank2.astype(F32).astype(BF16)
        w2_ref[hh] = (jnp.exp(s2 - v2[0:1, :]) * (scl_ref[1:2, :] / z)).astype(BF16)
        return (jnp.sum((rank1 < PEER_TOPK).astype(jnp.int32), axis=0, keepdims=True)
                + jnp.sum((rank2 < PEER_TOPK).astype(jnp.int32), axis=0, keepdims=True))

    marked = functools.reduce(jnp.maximum, [select(hh, False) for hh in range(SEL_HEADS_PER_STEP)])

    @pl.when(jnp.max(marked) != 2 * PEER_TOPK)
    def _():
        for hh in range(SEL_HEADS_PER_STEP):
            select(hh, True)


def _peer_select(st, scl):
    t_tokens = st.shape[2]
    tm = TM_SEL
    shp = jax.ShapeDtypeStruct((PEER_HEADS, PEER_KEYS, t_tokens), F32)
    shp_b = jax.ShapeDtypeStruct((PEER_HEADS, PEER_KEYS, t_tokens), BF16)
    hps = SEL_HEADS_PER_STEP
    spec = pl.BlockSpec((hps, PEER_KEYS, tm), lambda i, h: (h, 0, i))
    return pl.pallas_call(
        _peer_select_body,
        grid=(t_tokens // tm, PEER_HEADS // hps),
        in_specs=[pl.BlockSpec((2 * hps, PEER_KEYS, tm), lambda i, h: (h, 0, i)),
                  pl.BlockSpec((8, tm), lambda i, h: (0, i))],
        out_specs=(spec, spec, spec, spec),
        out_shape=(shp, shp, shp_b, shp_b),
        compiler_params=_params(("parallel", "parallel"), 32),
        name="peer_select",
    )(st, scl)


def _transpose_body(v_ref, s_ref, o_ref):
    o_ref[...] = (v_ref[...].T * s_ref[0:1, 0:1]).astype(FP8)


def _transpose_fp8(v, scale_tile):
    n, d = v.shape
    tn = 512
    return pl.pallas_call(
        _transpose_body,
        grid=(n // tn,),
        in_specs=[pl.BlockSpec((tn, d), lambda i: (i, 0)),
                  pl.BlockSpec((8, LANES), lambda i: (0, 0))],
        out_specs=pl.BlockSpec((d, tn), lambda i: (0, i)),
        out_shape=jax.ShapeDtypeStruct((d, n), FP8),
        compiler_params=_params(("parallel",), 32),
        name="transpose_v",
    )(v, scale_tile)


def _peer_expert_body(h2t_ref, u_ref, vt_ref, l_ref, w1_ref, r2_ref, w2_ref, scl_ref,
                      o_ref, ce_ref, co_ref, *, steps_per_tile):
    g = pl.program_id(0)
    sw = MXU_COLS
    strips = [slice(c * sw, (c + 1) * sw) for c in range(TM_PEER // sw)]
    n_piece = TE_PEER // PEER_KEYS
    blk = D_MODEL // n_piece

    @pl.when(g == 0)
    def _():
        co_ref[...] = jnp.zeros_like(co_ref)

    @pl.when((g == 0) | ((g - 1) % steps_per_tile == 0))
    def _():
        o_ref[...] = jnp.zeros_like(o_ref)

    def run(c_new, c_old):
        def piece(j, carry):
            r0 = pl.multiple_of(j * PEER_KEYS, PEER_KEYS)
            d0 = pl.multiple_of(j * blk, blk)
            for ls in strips:
                coef = None
                for h in range(PEER_HEADS):
                    lrow = jnp.broadcast_to(l_ref[h, pl.ds(j, 1), ls], (16, sw)).astype(BF16)
                    w1row = jnp.broadcast_to(w1_ref[h, pl.ds(j, 1), ls], (16, sw)).astype(BF16)
                    lrow = jnp.tile(lrow, (PEER_KEYS // 16, 1))
                    w1row = jnp.tile(w1row, (PEER_KEYS // 16, 1))
                    term = jnp.where(r2_ref[h, :, ls] < lrow, w2_ref[h, :, ls] * w1row, jnp.zeros((), BF16))
                    coef = term if coef is None else coef + term
                a_t = jnp.dot(u_ref[pl.ds(r0, PEER_KEYS), :], h2t_ref[:, ls], preferred_element_type=F32)
                a_scale = jnp.broadcast_to(scl_ref[0:1, ls], (16, sw)).astype(BF16)
                act = _gelu(a_t.astype(BF16) * jnp.tile(a_scale, (PEER_KEYS // 16, 1)))
                c_new[pl.ds(r0, PEER_KEYS), ls] = (coef * act).astype(FP8)
                o_ref[pl.ds(d0, blk), ls] += jnp.dot(vt_ref[pl.ds(d0, blk), :], c_old[:, ls],
                                                     preferred_element_type=F32)
            return carry

        lax.fori_loop(0, n_piece, piece, 0, unroll=8)

    @pl.when(g % 2 == 0)
    def _():
        run(ce_ref, co_ref)

    @pl.when(g % 2 == 1)
    def _():
        run(co_ref, ce_ref)


def _peer_expert(h2t, u_b, v_t, lrow, w1, r2, w2, scl):
    t_tokens = h2t.shape[1]
    tm, te = TM_PEER, TE_PEER
    n_piece = te // PEER_KEYS
    ne = PEER_EXPERTS // te
    n_steps = (t_tokens // tm) * ne
    cur = lambda g: jnp.minimum(g, n_steps - 1)
    prev = lambda g: jnp.maximum(g - 1, 0)
    row_spec = pl.BlockSpec((PEER_HEADS, n_piece, tm), lambda g: (0, cur(g) % ne, cur(g) // ne))
    full_spec = pl.BlockSpec((PEER_HEADS, PEER_KEYS, tm), lambda g: (0, 0, cur(g) // ne))
    return pl.pallas_call(
        functools.partial(_peer_expert_body, steps_per_tile=ne),
        grid=(n_steps + 1,),
        in_specs=[pl.BlockSpec((D_MODEL, tm), lambda g: (0, cur(g) // ne)),
                  pl.BlockSpec((te, D_MODEL), lambda g: (cur(g) % ne, 0)),
                  pl.BlockSpec((D_MODEL, te), lambda g: (0, prev(g) % ne)),
                  row_spec, row_spec, full_spec, full_spec,
                  pl.BlockSpec((8, tm), lambda g: (0, cur(g) // ne))],
        out_specs=pl.BlockSpec((D_MODEL, tm), lambda g: (0, prev(g) // ne)),
        out_shape=jax.ShapeDtypeStruct((D_MODEL, t_tokens), F32),
        scratch_shapes=[pltpu.VMEM((te, tm), FP8), pltpu.VMEM((te, tm), FP8)],
        compiler_params=_params(("arbitrary",), 52),
        name="peer_experts",
    )(h2t, u_b, v_t, lrow, w1, r2, w2, scl)


def _final_body(x1_ref, pt_ref, scl_ref, g2_ref, gn_ref, o_ref, *, apply_norm):
    peer = (pt_ref[...] * scl_ref[2:3, :]).T
    y = x1_ref[...] + g2_ref[0] * peer
    if apply_norm:
        ms = jnp.mean(y * y, axis=-1, keepdims=True)
        y = y * lax.rsqrt(ms + EPS) * gn_ref[...]
    o_ref[...] = y


def _final(x1, peer_t, scl, gate2, g_final, seq, apply_norm):
    t_tokens = x1.shape[0]
    tm = TM_PROJ
    tps = seq // tm
    return pl.pallas_call(
        functools.partial(_final_body, apply_norm=apply_norm),
        grid=(t_tokens // tm,),
        in_specs=[pl.BlockSpec((tm, D_MODEL), lambda i: (i, 0)),
                  pl.BlockSpec((D_MODEL, tm), lambda i: (0, i)),
                  pl.BlockSpec((8, tm), lambda i: (0, i)),
                  pl.BlockSpec((1, 1, D_MODEL), lambda i: (i // tps, 0, 0)),
                  pl.BlockSpec((1, D_MODEL), lambda i: (0, 0))],
        out_specs=pl.BlockSpec((tm, D_MODEL), lambda i: (i, 0)),
        out_shape=jax.ShapeDtypeStruct((t_tokens, D_MODEL), F32),
        compiler_params=_params(("parallel",), 32),
        name="final_norm",
    )(x1, peer_t, scl, gate2, g_final)


def _split_cols(a, sizes):
    out, acc = [], 0
    for s in sizes:
        out.append(a[:, acc:acc + s])
        acc += s
    return out


def _inproj_weights(w_in):
    kvw = KV_GROUPS * HEAD_DIM
    sizes = (1024,) + (kvw,) * 6 + (3 * N_HEADS, 1024, 1024, 1024, 1024)
    q_a, k_c, v_c, k_s, v_s, k_w, v_w, g_a, q_r, k_r, v_r, g_r = _split_cols(w_in, sizes)
    d = w_in.shape[0]

    def grp(a, g):
        return a[:, g * HEAD_DIM:(g + 1) * HEAD_DIM]

    cv = [jnp.concatenate([grp(k_c, g), grp(v_c, g)], axis=1) for g in range(KV_GROUPS)]
    w_std = jnp.concatenate(cv + [k_s, k_w, q_r, k_r, v_r, g_r], axis=1).astype(BF16)
    gcols = []
    for g in range(KV_GROUPS):
        for br in range(3):
            for r in range(GROUP):
                c = (g * GROUP + r) * 3 + br
                gcols.append(g_a[:, c:c + 1])
        gcols.append(jnp.zeros((d, 4), w_in.dtype))
    w_tr = jnp.concatenate([q_a, v_s, v_w] + gcols, axis=1).T.astype(BF16)
    return w_std, w_tr


def _compress_weights(pe_k, pe_v, k_w1, k_w2, v_w1, v_w2):
    half = CMP_LEN // 2

    def w1_half(w1k, w1v, lo):
        a = w1k.reshape(CMP_LEN, HEAD_DIM, CMP_HIDDEN)[lo:lo + half]
        b = w1v.reshape(CMP_LEN, HEAD_DIM, CMP_HIDDEN)[lo:lo + half]
        za = jnp.zeros_like(a)
        top = jnp.concatenate([a, za], axis=2)
        bot = jnp.concatenate([za, b], axis=2)
        return jnp.concatenate([top, bot], axis=1).reshape(half * 2 * HEAD_DIM, 2 * CMP_HIDDEN).astype(BF16)

    def pe_half(lo):
        row = jnp.concatenate([pe_k[lo:lo + half], pe_v[lo:lo + half]], axis=1).reshape(1, -1)
        return jnp.broadcast_to(row, (8, row.shape[1])).astype(BF16)

    wa = w1_half(k_w1, v_w1, 0)
    wb = w1_half(k_w1, v_w1, half)
    w2k = jnp.zeros((2 * CMP_HIDDEN, LANES), F32).at[:CMP_HIDDEN, :HEAD_DIM].set(k_w2).astype(BF16)
    w2vt = jnp.zeros((HEAD_DIM, 2 * CMP_HIDDEN), F32).at[:, CMP_HIDDEN:].set(v_w2.T).astype(BF16)
    return wa, wb, pe_half(0), pe_half(half), w2k, w2vt


def _nsa_constants(seq):
    slopes = jnp.exp2(-8.0 * (jnp.arange(N_HEADS, dtype=F32) + 1.0) / N_HEADS) * LOG2E
    s_hi = slopes.astype(BF16)
    s_lo = (slopes - s_hi.astype(F32)).astype(BF16)
    rows = jnp.zeros((N_HEADS, HEAD_DIM), BF16)
    rows = rows.at[:, 0].set(s_hi).at[:, 1].set(s_hi).at[:, 2].set(s_lo).at[:, 3].set(s_lo)
    qaug = jnp.broadcast_to(rows[:, :, None], (N_HEADS, HEAD_DIM, LANES))
    n_rows = seq // CMP_STRIDE
    n_slc = seq // SLC_BLOCK
    start = np.arange(n_rows)[:, None] * CMP_STRIDE
    end = start + CMP_LEN - 1
    blk = np.arange(n_slc)[None, :] * SLC_BLOCK
    ovl = ((start < blk + SLC_BLOCK) & (end >= blk)).astype(np.float32)
    per_tile = TK_SEL // SLC_BLOCK
    grp = (np.arange(n_slc)[None, :] // per_tile == np.arange(n_slc // per_tile)[:, None])
    return qaug, jnp.asarray(ovl.T, BF16), jnp.asarray(grp.astype(np.float32), BF16)


def _retention_constants():
    h, c = RET_HEADS, RET_CHUNK
    lg = jnp.log1p(-jnp.exp2(-5.0 - jnp.arange(h, dtype=F32)))
    pos = jnp.arange(c, dtype=F32)
    diff = pos[:, None] - pos[None, :]
    scale = RET_DIM ** -0.5
    dm = jnp.where(diff >= 0, jnp.exp(lg[:, None, None] * jnp.maximum(diff, 0.0)), 0.0) * scale
    k_decay = jnp.exp(lg[:, None] * (c - 1.0 - pos)) * scale
    q_decay = jnp.exp(lg[:, None] * (pos + 1.0))
    chunk_decay = jnp.exp(lg * c)
    kd = jnp.broadcast_to(k_decay[:, :, None], (h, c, RET_DIM))
    qd = jnp.broadcast_to(q_decay[:, :, None], (h, c, RET_DIM))
    cd = jnp.broadcast_to(chunk_decay[:, None, None], (h, RET_DIM, RET_DIM))
    return dm, kd, qd, cd


def kernel(x, c, w_ada, b_ada, g_norm_mix, g_norm_ffn, g_norm_final, w_in, cmp_pe_k, cmp_pe_v,
           cmp_k_w1, cmp_k_w2, cmp_v_w1, cmp_v_w2, g_nsa_out, g_ret_out, w_out,
           peer_w_q, peer_sub_keys, peer_u, peer_v):
    batch, seq, d = x.shape
    depth = w_ada.shape[0]
    t_tokens = batch * seq
    xf = x.reshape(t_tokens, d)
    c_pad = jnp.zeros((8, d), F32).at[:batch].set(c)
    qaug, ovl_t, grp = _nsa_constants(seq)
    dm, kd, qd, cd = _retention_constants()

    for l in range(depth):
        mod = _adaln(c_pad, w_ada[l], b_ada[l][None, :])[:batch].reshape(batch, 6, 1, d)
        shift1, scale1, gate1, shift2, scale2, gate2 = (mod[:, k] for k in range(6))

        w_std, w_tr = _inproj_weights(w_in[l])
        (cv, ks, kw, q_r, k_r, v_r, g_r, qt, vst, vwt, gt) = _inproj(
            xf, scale1, shift1, g_norm_mix[l][None, :], w_std, w_tr, seq)

        cv4 = cv.reshape(KV_GROUPS, batch, seq // CMP_STRIDE, CMP_STRIDE * LANES)
        kcp, vct = _compress(cv4, *_compress_weights(cmp_pe_k[l], cmp_pe_v[l], cmp_k_w1[l], cmp_k_w2[l],
                                                     cmp_v_w1[l], cmp_v_w2[l]))
        gout_b = jnp.broadcast_to(g_nsa_out[l][:, :, None], (N_HEADS, HEAD_DIM, LANES))
        o_nsa = _nsa(qt, gt, kcp, vct, ks, kw, vst, vwt, qaug, gout_b, ovl_t, grp, batch, seq)

        go = jnp.broadcast_to(g_ret_out[l][:, None, :], (RET_HEADS, 8, RET_DIM))
        o_ret = _retention(q_r, k_r, v_r, g_r, dm, kd, qd, cd, go, batch, seq)

        sub_keys = peer_sub_keys[l].reshape(2 * PEER_HEADS, PEER_KEYS, -1).astype(BF16)
        s_u = _pow2_scale(jnp.max(jnp.abs(peer_u[l])))
        s_v = _pow2_scale(jnp.max(jnp.abs(peer_v[l])))
        u_norm = jnp.sqrt(jnp.max(jnp.sum(jnp.square(peer_u[l]), axis=1)))
        peer_scales = jnp.broadcast_to(
            jnp.stack([1.0 / s_u, 1.0 / s_v, u_norm] + [jnp.zeros((), F32)] * 5)[:, None], (8, LANES))
        x1, h2t, st, scl = _mid(o_nsa, o_ret, xf, gate1, scale2, shift2, g_norm_ffn[l][None, :],
                                w_out[l].astype(BF16), peer_w_q[l].T.astype(BF16), sub_keys, peer_scales, seq)

        lrow, w1, r2, w2 = _peer_select(st, scl)
        peer_t = _peer_expert(h2t, (peer_u[l] * s_u).astype(FP8),
                              _transpose_fp8(peer_v[l], jnp.full((8, LANES), s_v, F32)),
                              lrow, w1, r2, w2, scl)
        xf = _final(x1, peer_t, scl, gate2, g_norm_final[None, :], seq, apply_norm=(l == depth - 1))
    return xf.reshape(batch, seq, d)
```

```python
import functools
import math

import numpy as np
import jax
import jax.numpy as jnp
from jax import lax
from jax.experimental import pallas as pl
from jax.experimental.pallas import tpu as pltpu

F32 = jnp.float32
BF16 = jnp.bfloat16
FP8 = jnp.float8_e4m3fn
FP8_TARGET = 224.0

D_MODEL = 2048
N_HEADS = 16
HEAD_DIM = 64
KV_GROUPS = 4
GROUP = 4
CMP_LEN = 32
CMP_STRIDE = 16
CMP_HIDDEN = 128
SLC_BLOCK = 64
SLC_TOPK = 16
WINDOW = 512
NEG = -1e30
RET_HEADS = 8
RET_DIM = 128
RET_CHUNK = 128
PEER_HEADS = 8
PEER_KEYS = 128
PEER_EXPERTS = PEER_KEYS * PEER_KEYS
PEER_TOPK = 16
COEF_BOUND_FACTOR = 1.25 * PEER_HEADS
EPS = 1e-6
LOG2E = 1.4426950408889634

LANES = 128
MXU_COLS = 256
TQ = 128
NSA_TILES_PER_STEP = 4
TK_SEL = 256
TM_PROJ = 256
TM_PEER = 512
TE_PEER = 1024
TM_SEL = 256
SEL_HEADS_PER_STEP = 4
RET_TILE = 1024

_NT = (((1,), (1,)), ((), ()))
_TN = (((0,), (0,)), ((), ()))


def _params(sem, vmem_mb):
    return pltpu.CompilerParams(dimension_semantics=sem, vmem_limit_bytes=vmem_mb * 1024 * 1024)


def _resident(shape, index_map):
    return pl.BlockSpec(shape, index_map, pipeline_mode=pl.Buffered(1))


def _gelu(x):
    return jax.nn.gelu(x)


def _adaln_body(c_ref, w_ref, b_ref, o_ref):
    c = c_ref[...]
    act = (c * jax.nn.sigmoid(c)).astype(BF16)
    o_ref[...] = jnp.dot(act, w_ref[...].astype(BF16), preferred_element_type=F32) + b_ref[...]


def _adaln(c_pad, w, b):
    n = w.shape[1]
    tn = 1536
    return pl.pallas_call(
        _adaln_body,
        grid=(n // tn,),
        in_specs=[pl.BlockSpec((8, D_MODEL), lambda j: (0, 0)),
                  pl.BlockSpec((D_MODEL, tn), lambda j: (0, j)),
                  pl.BlockSpec((1, tn), lambda j: (0, j))],
        out_specs=pl.BlockSpec((8, tn), lambda j: (0, j)),
        out_shape=jax.ShapeDtypeStruct((8, n), F32),
        compiler_params=_params(("arbitrary",), 40),
        name="adaln",
    )(c_pad, w, b)


STD_COLS = 512 + 256 * 2 + 1024 * 4
TR_ROWS = 1024 + 256 + 256 + 64


def _inproj_body(x_ref, sc_ref, sh_ref, gn_ref, wstd_ref, wt_ref,
                 cv_ref, ks_ref, kw_ref, qr_ref, kr_ref, vr_ref, gr_ref,
                 qt_ref, vst_ref, vwt_ref, gt_ref, *, tiles_per_seq):
    tm = TM_PROJ
    i = pl.program_id(0)
    x = x_ref[...]
    ms = jnp.mean(x * x, axis=-1, keepdims=True)
    h = x * lax.rsqrt(ms + EPS) * gn_ref[...]
    h = h * (1.0 + sc_ref[0]) + sh_ref[0]
    hb = h.astype(BF16)

    def std(a, b):
        return jnp.dot(hb, wstd_ref[:, a:b], preferred_element_type=F32)

    y = std(0, 512)
    for g in range(KV_GROUPS):
        cv_ref[g] = y[:, g * LANES:(g + 1) * LANES].astype(BF16)

    t = (i % tiles_per_seq) * tm + lax.broadcasted_iota(jnp.int32, (tm, LANES), 0)
    lane = lax.broadcasted_iota(jnp.int32, (tm, LANES), 1)
    pos_hi = ((t >> 6) << 6).astype(F32)
    pos_lo = (t & 63).astype(F32)
    aug = jnp.where((lane == 64) | (lane == 66), pos_hi,
                    jnp.where((lane == 65) | (lane == 67), pos_lo, 0.0))
    for ref, off in ((ks_ref, 512), (kw_ref, 768)):
        y = std(off, off + 256)
        for g in range(KV_GROUPS):
            pair = y[:, (g // 2) * LANES:(g // 2 + 1) * LANES]
            keys = pair if g % 2 == 0 else pltpu.roll(pair, HEAD_DIM, 1)
            ref[:, g * LANES:(g + 1) * LANES] = jnp.where(lane < HEAD_DIM, keys, aug).astype(BF16)

    for ref, off in ((qr_ref, 1024), (kr_ref, 2048), (vr_ref, 3072), (gr_ref, 4096)):
        ref[...] = std(off, off + 1024).astype(BF16)

    def tr(a, b):
        return lax.dot_general(wt_ref[a:b, :], hb, _NT, preferred_element_type=F32)

    qt = tr(0, 1024) * (HEAD_DIM ** -0.5 * LOG2E)
    vst = tr(1024, 1280)
    vwt = tr(1280, 1536)
    gt = tr(1536, 1600)
    for c in range(tm // LANES):
        sl = slice(c * LANES, (c + 1) * LANES)
        qt_ref[c] = qt[:, sl].astype(BF16)
        vst_ref[c] = vst[:, sl].astype(BF16)
        vwt_ref[c] = vwt[:, sl].astype(BF16)
        gt_ref[c] = gt[:, sl]


def _inproj(x2, scale1, shift1, g_mix, w_std, w_tr, seq):
    t_tokens = x2.shape[0]
    tm = TM_PROJ
    tps = seq // tm
    nt = t_tokens // tm
    c = tm // LANES
    row = lambda i: (i, 0)
    per_b = lambda i: (i // tps, 0, 0)
    out_shape = (
        jax.ShapeDtypeStruct((KV_GROUPS, t_tokens, LANES), BF16),
        jax.ShapeDtypeStruct((t_tokens, 512), BF16),
        jax.ShapeDtypeStruct((t_tokens, 512), BF16),
        jax.ShapeDtypeStruct((t_tokens, 1024), BF16),
        jax.ShapeDtypeStruct((t_tokens, 1024), BF16),
        jax.ShapeDtypeStruct((t_tokens, 1024), BF16),
        jax.ShapeDtypeStruct((t_tokens, 1024), BF16),
        jax.ShapeDtypeStruct((t_tokens // LANES, 1024, LANES), BF16),
        jax.ShapeDtypeStruct((t_tokens // LANES, 256, LANES), BF16),
        jax.ShapeDtypeStruct((t_tokens // LANES, 256, LANES), BF16),
        jax.ShapeDtypeStruct((t_tokens // LANES, 64, LANES), F32),
    )
    out_specs = (
        pl.BlockSpec((KV_GROUPS, tm, LANES), lambda i: (0, i, 0)),
        pl.BlockSpec((tm, 512), row),
        pl.BlockSpec((tm, 512), row),
        pl.BlockSpec((tm, 1024), row),
        pl.BlockSpec((tm, 1024), row),
        pl.BlockSpec((tm, 1024), row),
        pl.BlockSpec((tm, 1024), row),
        pl.BlockSpec((c, 1024, LANES), lambda i: (i, 0, 0)),
        pl.BlockSpec((c, 256, LANES), lambda i: (i, 0, 0)),
        pl.BlockSpec((c, 256, LANES), lambda i: (i, 0, 0)),
        pl.BlockSpec((c, 64, LANES), lambda i: (i, 0, 0)),
    )
    return pl.pallas_call(
        functools.partial(_inproj_body, tiles_per_seq=tps),
        grid=(nt,),
        in_specs=[pl.BlockSpec((tm, D_MODEL), row),
                  pl.BlockSpec((1, 1, D_MODEL), per_b),
                  pl.BlockSpec((1, 1, D_MODEL), per_b),
                  _resident((1, D_MODEL), lambda i: (0, 0)),
                  _resident((D_MODEL, STD_COLS), lambda i: (0, 0)),
                  _resident((TR_ROWS, D_MODEL), lambda i: (0, 0))],
        out_specs=out_specs,
        out_shape=out_shape,
        compiler_params=_params(("parallel",), 56),
        name="inproj",
    )(x2, scale1, shift1, g_mix, w_std, w_tr)


def _compress_body(x_ref, wa_ref, wb_ref, pea_ref, peb_ref, w2k_ref, w2vt_ref, kcp_ref, vct_ref):
    x = x_ref[0, 0]
    n_rows = x.shape[0]
    p = jnp.dot(x, wa_ref[...], preferred_element_type=F32)
    q = jnp.dot(x, wb_ref[...], preferred_element_type=F32)
    pe = (jnp.dot(pea_ref[...], wa_ref[...], preferred_element_type=F32)
          + jnp.dot(peb_ref[...], wb_ref[...], preferred_element_type=F32))[0:1, :]
    pre = p + pltpu.roll(q, n_rows - 1, 0) + pe
    hid = _gelu(pre).astype(BF16)
    kc = jnp.dot(hid, w2k_ref[...], preferred_element_type=F32)
    n = lax.broadcasted_iota(jnp.int32, (n_rows, LANES), 0)
    lane = lax.broadcasted_iota(jnp.int32, (n_rows, LANES), 1)
    ce = n * CMP_STRIDE + (CMP_LEN - 1)
    ce_hi = ((ce >> 6) << 6).astype(F32)
    ce_lo = (ce & 63).astype(F32)
    aug = jnp.where((lane == 64) | (lane == 66), ce_hi,
                    jnp.where((lane == 65) | (lane == 67), ce_lo, 0.0))
    kcp_ref[0] = (kc + aug).astype(BF16)
    vct_ref[0] = lax.dot_general(w2vt_ref[...], hid, _NT, preferred_element_type=F32).astype(BF16)


def _compress(cv4, wa, wb, pea, peb, w2k, w2vt):
    g_, b_, n_rows, _ = cv4.shape
    const2 = lambda n: (0, 0)
    return pl.pallas_call(
        _compress_body,
        grid=(b_ * g_,),
        in_specs=[pl.BlockSpec((1, 1, n_rows, 2048), lambda n: (n % KV_GROUPS, n // KV_GROUPS, 0, 0)),
                  pl.BlockSpec((2048, 256), const2),
                  pl.BlockSpec((2048, 256), const2),
                  pl.BlockSpec((8, 2048), const2),
                  pl.BlockSpec((8, 2048), const2),
                  pl.BlockSpec((256, LANES), const2),
                  pl.BlockSpec((64, 256), const2)],
        out_specs=(pl.BlockSpec((1, n_rows, LANES), lambda n: (n, 0, 0)),
                   pl.BlockSpec((1, 64, n_rows), lambda n: (n, 0, 0))),
        out_shape=(jax.ShapeDtypeStruct((b_ * g_, n_rows, LANES), BF16),
                   jax.ShapeDtypeStruct((b_ * g_, 64, n_rows), BF16)),
        compiler_params=_params(("parallel",), 32),
        name="nsa_compress",
    )(cv4, wa, wb, pea, peb, w2k, w2vt)


def _softmax_step(state, s, pv_prev):
    m_i, l_i, acc = state
    m_new = jnp.maximum(m_i, jnp.max(s, axis=0, keepdims=True))
    alpha = jnp.exp2(m_i - m_new)
    p = jnp.exp2(s - m_new)
    l_new = alpha * l_i + jnp.sum(p, axis=0, keepdims=True)
    return (m_new, l_new, (acc + pv_prev) * alpha), p.astype(BF16)


def _nsa_body(qt_ref, gt_ref, kcp_ref, vct_ref, ks_ref, kw_ref, vst_ref, vwt_ref,
              qaug_ref, gout_ref, ovl_ref, grp_ref, o_ref, *scratch, n_cmp):
    per_tile = len(scratch) // NSA_TILES_PER_STEP
    for sub in range(NSA_TILES_PER_STEP):
        _nsa_tile(pl.program_id(2) * NSA_TILES_PER_STEP + sub,
                  qt_ref.at[pl.ds(sub, 1)], gt_ref.at[pl.ds(sub, 1)], kcp_ref, vct_ref, ks_ref, kw_ref,
                  vst_ref, vwt_ref, qaug_ref, gout_ref, ovl_ref, grp_ref,
                  o_ref.at[pl.ds(sub * TQ, TQ)], *scratch[sub * per_tile:(sub + 1) * per_tile], n_cmp=n_cmp)


def _nsa_tile(qi, qt_ref, gt_ref, kcp_ref, vct_ref, ks_ref, kw_ref, vst_ref, vwt_ref,
              qaug_ref, gout_ref, ovl_ref, grp_ref, o_ref, selb_ref, sa_ref, sb_ref, pa_ref, pb_ref, *, n_cmp):
    t0 = qi * TQ
    wq = GROUP * TQ

    qp = jnp.concatenate(
        [jnp.concatenate([qt_ref[0, r * HEAD_DIM:(r + 1) * HEAD_DIM, :], qaug_ref[r]], axis=0)
         for r in range(GROUP)], axis=1)

    lane_q = lax.broadcasted_iota(jnp.int32, (1, wq), 1) & (TQ - 1)
    n_win = WINDOW // TQ + 1
    k_lo = pl.multiple_of(jnp.maximum(t0 - WINDOW, 0), TQ)

    s = jnp.dot(kcp_ref[0], qp, preferred_element_type=F32)
    s_w = jnp.dot(kw_ref[pl.ds(k_lo, n_win * TQ), :], qp, preferred_element_type=F32)
    n_io = lax.broadcasted_iota(jnp.int32, (n_cmp, wq), 0)
    tl = lax.broadcasted_iota(jnp.int32, (n_cmp, wq), 1) & (TQ - 1)
    valid = (n_io * CMP_STRIDE + (CMP_LEN - 1)) <= (t0 + tl)
    s = jnp.where(valid, s, NEG)
    m = jnp.maximum(jnp.max(s, axis=0, keepdims=True), 0.5 * NEG)
    p = jnp.exp2(s - m)
    l = jnp.sum(p, axis=0, keepdims=True)
    pn = p * (1.0 / jnp.maximum(l, 1e-30))
    o_c = jnp.dot(vct_ref[0], pn.astype(BF16), preferred_element_type=F32)

    ps = pn[:, 0:TQ] + pn[:, TQ:2 * TQ] + pn[:, 2 * TQ:3 * TQ] + pn[:, 3 * TQ:4 * TQ]
    hi = ps.astype(BF16)
    r1 = ps - hi.astype(F32)
    mid = r1.astype(BF16)
    lo = (r1 - mid.astype(F32)).astype(BF16)
    ovl = ovl_ref[...]
    imp = (jnp.dot(ovl, hi, preferred_element_type=F32)
           + jnp.dot(ovl, mid, preferred_element_type=F32)
           + jnp.dot(ovl, lo, preferred_element_type=F32))

    n_slc = imp.shape[0]
    m_io = lax.broadcasted_iota(jnp.int32, (n_slc, TQ), 0)
    q_io = lax.broadcasted_iota(jnp.int32, (n_slc, TQ), 1)
    back = ((t0 + q_io) >> 6) - m_io
    valid_s = back >= 0
    forced = valid_s & ((m_io == 0) | (back < 2))
    w = jnp.where(forced, -jnp.inf, jnp.where(valid_s, imp, -1.0))
    selb = jnp.where(forced, 0.0, NEG)

    def pick(carry, lanes=None):
        w, selb = carry
        mx = jnp.max(w, axis=0, keepdims=True)
        idx = jnp.min(jnp.where(w == mx, m_io, n_slc), axis=0, keepdims=True)
        hit = m_io == idx
        if lanes is not None:
            hit = hit & lanes
        return jnp.where(hit, -jnp.inf, w), jnp.where(hit, 0.0, selb)

    carry = (w, selb)
    for _ in range(SLC_TOPK - 3):
        carry = pick(carry)
    carry = pick(carry, lanes=(t0 + q_io) < 2 * SLC_BLOCK)
    _, selb = pick(carry, lanes=(t0 + q_io) < SLC_BLOCK)
    selb_ref[...] = selb

    per_blk = TK_SEL // SLC_BLOCK

    def scores(j):
        k0 = pl.multiple_of(j * TK_SEL, TK_SEL)
        bias = jnp.concatenate(
            [jnp.broadcast_to(selb_ref[pl.ds(j * per_blk + u, 1), :], (SLC_BLOCK, TQ)) for u in range(per_blk)],
            axis=0)
        bias = jnp.concatenate([bias] * GROUP, axis=1)
        return jnp.dot(ks_ref[pl.ds(k0, TK_SEL), :], qp, preferred_element_type=F32) + bias

    def vt_tile(j):
        return jnp.concatenate([vst_ref[2 * j], vst_ref[2 * j + 1]], axis=1)

    def causal(j):
        kr = lax.broadcasted_iota(jnp.int32, (TK_SEL, wq), 0)
        return kr <= (t0 - j * TK_SEL) + lane_q

    def pv_dot(j, p_ref):
        return jnp.dot(vt_tile(j), p_ref[...], preferred_element_type=F32)

    jd = qi // (TK_SEL // TQ)

    sa_ref[...] = scores(0)
    pb_ref[...] = jnp.zeros_like(pb_ref)
    sel01 = jnp.where(selb == 0.0, 1.0, 0.0).astype(BF16)
    tile_any = jnp.max(jnp.dot(grp_ref[...], sel01, preferred_element_type=F32), axis=1, keepdims=True)
    j_io = lax.broadcasted_iota(jnp.int32, tile_any.shape, 0)
    q_lo_v = jnp.max(jnp.where((tile_any == 0.0) & (j_io <= jd), j_io, -1), axis=0, keepdims=True) + 1
    n_pre_v = jnp.max(jnp.where((tile_any > 0.0) & (j_io < q_lo_v), j_io, -1), axis=0, keepdims=True) + 1

    d = (t0 - k_lo) + lane_q - lax.broadcasted_iota(jnp.int32, (n_win * TQ, wq), 0)
    s_w = jnp.where(lax.bitcast_convert_type(d, jnp.uint32) < WINDOW, s_w, NEG)
    p = jnp.exp2(s_w - jnp.max(s_w, axis=0, keepdims=True))
    l_w = jnp.sum(p, axis=0, keepdims=True)
    vt_w = jnp.concatenate([vwt_ref[k_lo // TQ + u] for u in range(n_win)], axis=1)
    o_w = jnp.dot(vt_w, p.astype(BF16), preferred_element_type=F32) * (1.0 / l_w)

    q_lo = q_lo_v[0, 0]
    n_pre = n_pre_v[0, 0]
    n_vis = n_pre + jd - q_lo + 1

    def tile_at(pos):
        return jnp.where(pos < n_pre, pos, pos - n_pre + q_lo)

    state = (jnp.full((1, wq), NEG, F32), jnp.zeros((1, wq), F32), jnp.zeros((HEAD_DIM, wq), F32))

    def pair(u, state):
        a = 2 * u
        sb_ref[...] = scores(tile_at(a + 1))
        state, p = _softmax_step(state, sa_ref[...], pv_dot(tile_at(jnp.maximum(a - 1, 0)), pb_ref))
        pa_ref[...] = p
        sa_ref[...] = scores(tile_at(a + 2))
        state, p = _softmax_step(state, sb_ref[...], pv_dot(tile_at(a), pa_ref))
        pb_ref[...] = p
        return state

    n_pair = (n_vis - 1) // 2
    state = lax.fori_loop(0, n_pair, pair, state)
    x = 2 * n_pair
    y = jnp.minimum(x + 1, n_vis - 1)
    tx, ty = tile_at(x), tile_at(y)
    sb_ref[...] = scores(ty)
    state, p = _softmax_step(state, jnp.where(causal(tx), sa_ref[...], NEG),
                             pv_dot(tile_at(jnp.maximum(x - 1, 0)), pb_ref))
    pa_ref[...] = p
    ty_mask = jnp.where(x + 1 < n_vis, ty, jd + 1)
    (_, l_s, acc_s), p = _softmax_step(state, jnp.where(causal(ty_mask), sb_ref[...], NEG), pv_dot(tx, pa_ref))
    acc_s = acc_s + jnp.dot(vt_tile(ty), p, preferred_element_type=F32)

    o_s = acc_s * (1.0 / l_s)
    gw = jax.nn.sigmoid(gt_ref[0])
    outs = []
    for r in range(GROUP):
        sl = slice(r * TQ, (r + 1) * TQ)
        o = (gw[r:r + 1, :] * o_c[:, sl] + gw[GROUP + r:GROUP + r + 1, :] * o_s[:, sl]
             + gw[2 * GROUP + r:2 * GROUP + r + 1, :] * o_w[:, sl])
        ms = jnp.mean(o * o, axis=0, keepdims=True)
        outs.append(o * lax.rsqrt(ms + EPS) * gout_ref[r])
    o_ref[...] = jnp.concatenate(outs, axis=0).T.astype(BF16)


def _nsa(qt, gt, kcp, vct, ks, kw, vst, vwt, qaug, gout_b, ovl_t, grp, batch, seq):
    tps = NSA_TILES_PER_STEP
    nq = seq // (TQ * tps)
    n_cmp = kcp.shape[1]
    n_slc = seq // SLC_BLOCK
    t_tokens = batch * seq
    per_b_chunks = seq // LANES
    return pl.pallas_call(
        functools.partial(_nsa_body, n_cmp=n_cmp),
        grid=(batch, KV_GROUPS, nq),
        in_specs=[
            pl.BlockSpec((tps, 256, LANES), lambda b, g, q: (b * nq + q, g, 0)),
            pl.BlockSpec((tps, 16, LANES), lambda b, g, q: (b * nq + q, g, 0)),
            pl.BlockSpec((1, n_cmp, LANES), lambda b, g, q: (b * KV_GROUPS + g, 0, 0)),
            pl.BlockSpec((1, 64, n_cmp), lambda b, g, q: (b * KV_GROUPS + g, 0, 0)),
            pl.BlockSpec((seq, LANES), lambda b, g, q: (b, g)),
            pl.BlockSpec((seq, LANES), lambda b, g, q: (b, g)),
            pl.BlockSpec((per_b_chunks, 64, LANES), lambda b, g, q: (b, g, 0)),
            pl.BlockSpec((per_b_chunks, 64, LANES), lambda b, g, q: (b, g, 0)),
            pl.BlockSpec((GROUP, 64, LANES), lambda b, g, q: (g, 0, 0)),
            pl.BlockSpec((GROUP, 64, LANES), lambda b, g, q: (g, 0, 0)),
            pl.BlockSpec((n_slc, n_cmp), lambda b, g, q: (0, 0)),
            pl.BlockSpec(grp.shape, lambda b, g, q: (0, 0)),
        ],
        out_specs=pl.BlockSpec((tps * TQ, 256), lambda b, g, q: (b * nq + q, g)),
        out_shape=jax.ShapeDtypeStruct((t_tokens, 1024), BF16),
        scratch_shapes=[pltpu.VMEM((n_slc, TQ), F32),
                        pltpu.VMEM((TK_SEL, GROUP * TQ), F32), pltpu.VMEM((TK_SEL, GROUP * TQ), F32),
                        pltpu.VMEM((TK_SEL, GROUP * TQ), BF16), pltpu.VMEM((TK_SEL, GROUP * TQ), BF16)] * tps,
        compiler_params=_params(("parallel", "parallel", "arbitrary"), 40),
        name="nsa_attention",
    )(qt, gt, kcp, vct, ks, kw, vst, vwt, qaug, gout_b, ovl_t, grp)


def _ret_body(q_ref, k_ref, v_ref, g_ref, dm_ref, kd_ref, qd_ref, cd_ref, go_ref, o_ref, st_ref):
    @pl.when(pl.program_id(2) == 0)
    def _():
        st_ref[...] = jnp.zeros_like(st_ref)

    c_ = RET_CHUNK
    slices = [slice(c * c_, (c + 1) * c_) for c in range(RET_TILE // c_)]
    atts, kvs = [], []
    for sl in slices:
        k = k_ref[sl, :]
        atts.append((lax.dot_general(q_ref[sl, :], k, _NT, preferred_element_type=F32) * dm_ref[0]).astype(BF16))
        kdec = (k.astype(F32) * kd_ref[0]).astype(BF16)
        kvs.append(lax.dot_general(kdec, v_ref[sl, :], _TN, preferred_element_type=F32))
    states = [st_ref[...]]
    for kv in kvs:
        states.append(states[-1] * cd_ref[0] + kv)
    st_ref[...] = states[-1]
    for sl, att, state in zip(slices, atts, states):
        o = (jnp.dot(att, v_ref[sl, :], preferred_element_type=F32)
             + qd_ref[0] * jnp.dot(q_ref[sl, :], state.astype(BF16), preferred_element_type=F32))
        mu = jnp.mean(o, axis=-1, keepdims=True)
        oc = o - mu
        var = jnp.mean(oc * oc, axis=-1, keepdims=True)
        y = oc * lax.rsqrt(var + EPS) * go_ref[0, 0:1, :]
        gate = g_ref[sl, :].astype(F32)
        o_ref[sl, :] = (gate * jax.nn.sigmoid(gate) * y).astype(BF16)


def _retention(q_r, k_r, v_r, g_r, dm, kd, qd, cd, go, batch, seq):
    t_tokens = batch * seq
    nc = seq // RET_TILE
    tok = lambda b, h, c: (b * nc + c, h)
    per_h = lambda b, h, c: (h, 0, 0)
    sq = (1, RET_DIM, RET_DIM)
    return pl.pallas_call(
        _ret_body,
        grid=(batch, RET_HEADS, nc),
        in_specs=[pl.BlockSpec((RET_TILE, RET_DIM), tok)] * 4
        + [pl.BlockSpec(sq, per_h)] * 4 + [pl.BlockSpec((1, 8, RET_DIM), per_h)],
        out_specs=pl.BlockSpec((RET_TILE, RET_DIM), tok),
        out_shape=jax.ShapeDtypeStruct((t_tokens, RET_HEADS * RET_DIM), BF16),
        scratch_shapes=[pltpu.VMEM((RET_DIM, RET_DIM), F32)],
        compiler_params=_params(("parallel", "parallel", "arbitrary"), 32),
        name="retention",
    )(q_r, k_r, v_r, g_r, dm, kd, qd, cd, go)


def _pow2_scale(magnitude):
    return jnp.exp2(jnp.floor(jnp.log2(FP8_TARGET / jnp.maximum(magnitude, 1e-30))))


def _mid_body(on_ref, or_ref, x_ref, g1_ref, sc_ref, sh_ref, gn_ref, wo_ref, wqt_ref, sk_ref, ps_ref,
              x1_ref, h2t_ref, st_ref, scl_ref):
    acc = (jnp.dot(on_ref[...], wo_ref[0:1024, :], preferred_element_type=F32)
           + jnp.dot(or_ref[...], wo_ref[1024:2048, :], preferred_element_type=F32))
    x1 = x_ref[...] + g1_ref[0] * acc
    x1_ref[...] = x1
    ms = jnp.mean(x1 * x1, axis=-1, keepdims=True)
    h2 = x1 * lax.rsqrt(ms + EPS) * gn_ref[...]
    h2 = h2 * (1.0 + sc_ref[0]) + sh_ref[0]

    h2_t = h2.T
    amax = jnp.max(jnp.max(jnp.abs(h2_t), axis=0, keepdims=True), axis=1, keepdims=True)
    s_h = _pow2_scale(amax)
    h2t_ref[...] = (h2_t * s_h).astype(FP8)
    norm = jnp.sqrt(jnp.sum(h2_t * h2_t, axis=0, keepdims=True))
    s_c = _pow2_scale(COEF_BOUND_FACTOR * ps_ref[2:3, 0:1] * norm)
    scl_ref[...] = jnp.concatenate(
        [jnp.broadcast_to(ps_ref[0:1, 0:1] / s_h, s_c.shape), s_c, ps_ref[1:2, 0:1] / s_c,
         jnp.zeros((5, s_c.shape[1]), F32)], axis=0)
    h2 = h2.astype(BF16)
    qt = lax.dot_general(wqt_ref[...], h2, _NT, preferred_element_type=F32).astype(BF16)
    for hp in range(2 * PEER_HEADS):
        st_ref[hp] = jnp.dot(sk_ref[hp], qt[hp * 128:(hp + 1) * 128, :], preferred_element_type=F32)


def _mid(o_nsa, o_ret, x2, gate1, scale2, shift2, g_ffn, w_out, wq_t, sub_keys, peer_scales, seq):
    t_tokens = x2.shape[0]
    tm = TM_PROJ
    tps = seq // tm
    row = lambda i: (i, 0)
    per_b = lambda i: (i // tps, 0, 0)
    return pl.pallas_call(
        _mid_body,
        grid=(t_tokens // tm,),
        in_specs=[pl.BlockSpec((tm, 1024), row),
                  pl.BlockSpec((tm, 1024), row),
                  pl.BlockSpec((tm, D_MODEL), row),
                  pl.BlockSpec((1, 1, D_MODEL), per_b),
                  pl.BlockSpec((1, 1, D_MODEL), per_b),
                  pl.BlockSpec((1, 1, D_MODEL), per_b),
                  _resident((1, D_MODEL), lambda i: (0, 0)),
                  _resident((D_MODEL, D_MODEL), lambda i: (0, 0)),
                  _resident((D_MODEL, D_MODEL), lambda i: (0, 0)),
                  _resident((2 * PEER_HEADS, PEER_KEYS, 128), lambda i: (0, 0, 0)),
                  _resident((8, LANES), lambda i: (0, 0))],
        out_specs=(pl.BlockSpec((tm, D_MODEL), row),
                   pl.BlockSpec((D_MODEL, tm), lambda i: (0, i)),
                   pl.BlockSpec((2 * PEER_HEADS, PEER_KEYS, tm), lambda i: (0, 0, i)),
                   pl.BlockSpec((8, tm), lambda i: (0, i))),
        out_shape=(jax.ShapeDtypeStruct((t_tokens, D_MODEL), F32),
                   jax.ShapeDtypeStruct((D_MODEL, t_tokens), FP8),
                   jax.ShapeDtypeStruct((2 * PEER_HEADS, PEER_KEYS, t_tokens), F32),
                   jax.ShapeDtypeStruct((8, t_tokens), F32)),
        compiler_params=_params(("parallel",), 48),
        name="outproj_peerq",
    )(o_nsa, o_ret, x2, gate1, scale2, shift2, g_ffn, w_out, wq_t, sub_keys, peer_scales)


def _top16(s, break_ties):
    n_rows, n = s.shape
    io = lax.broadcasted_iota(jnp.int32, (n_rows, n), 0)
    a_io = lax.broadcasted_iota(jnp.int32, (PEER_TOPK, n), 0)
    rank = jnp.full((n_rows, n), PEER_TOPK, jnp.int32)
    vals = jnp.zeros((PEER_TOPK, n), F32)
    for a in range(PEER_TOPK):
        mx = jnp.max(s, axis=0, keepdims=True)
        hit = s == mx
        if break_ties:
            hit = io == jnp.min(jnp.where(hit, io, n_rows), axis=0, keepdims=True)
        rank = jnp.where(hit, a, rank)
        s = jnp.where(hit, -jnp.inf, s)
        vals = jnp.where(a_io == a, mx, vals)
    return vals, rank


def _peer_select_body(s_ref, scl_ref, l_ref, w1_ref, r2_ref, w2_ref):
    n = s_ref.shape[2]

    def select(hh, break_ties):
        s1 = s_ref[2 * hh]
        s2 = s_ref[2 * hh + 1]
        v1, rank1 = _top16(s1, break_ties)
        v2, rank2 = _top16(s2, break_ties)
        a_io = lax.broadcasted_iota(jnp.int32, (PEER_TOPK, n), 0)
        cnt = jnp.zeros((PEER_TOPK, n), jnp.int32)
        cur = v1 + v2[0:1, :]
        top = v1[0:1, :] + v2[0:1, :]
        z = jnp.zeros((1, n), F32)
        for _ in range(PEER_TOPK):
            mx = jnp.max(cur, axis=0, keepdims=True)
            aidx = jnp.min(jnp.where(cur == mx, a_io, PEER_TOPK), axis=0, keepdims=True)
            hit = a_io == aidx
            cnt = cnt + hit.astype(jnp.int32)
            nxt = jnp.sum(jnp.where(hit, cnt, 0), axis=0, keepdims=True)
            nv = jnp.max(jnp.where(a_io == nxt, v2, -jnp.inf), axis=0, keepdims=True)
            cur = jnp.where(hit, v1 + nv, cur)
            z = z + jnp.exp(mx - top)
        cnt_b = cnt.astype(F32).astype(BF16)
        rank_b = rank1.astype(F32).astype(BF16)
        lrow = jnp.zeros(s1.shape, BF16)
        for a in range(PEER_TOPK):
            lrow = jnp.where(rank_b == a, jnp.broadcast_to(cnt_b[a:a + 1, :], s1.shape), lrow)
        l_ref[hh] = lrow.astype(F32)
        w1_ref[hh] = jnp.exp(s1 - v1[0:1, :])
        r2_ref[hh] = rank2.astype(F32).astype(BF16)
        w2_ref[hh] = (jnp.exp(s2 - v2[0:1, :]) * (scl_ref[1:2, :] / z)).astype(BF16)
        return (jnp.sum((rank1 < PEER_TOPK).astype(jnp.int32), axis=0, keepdims=True)
                + jnp.sum((rank2 < PEER_TOPK).astype(jnp.int32), axis=0, keepdims=True))

    marked = functools.reduce(jnp.maximum, [select(hh, False) for hh in range(SEL_HEADS_PER_STEP)])

    @pl.when(jnp.max(marked) != 2 * PEER_TOPK)
    def _():
        for hh in range(SEL_HEADS_PER_STEP):
            select(hh, True)


def _peer_select(st, scl):
    t_tokens = st.shape[2]
    tm = TM_SEL
    shp = jax.ShapeDtypeStruct((PEER_HEADS, PEER_KEYS, t_tokens), F32)
    shp_b = jax.ShapeDtypeStruct((PEER_HEADS, PEER_KEYS, t_tokens), BF16)
    hps = SEL_HEADS_PER_STEP
    spec = pl.BlockSpec((hps, PEER_KEYS, tm), lambda i, h: (h, 0, i))
    return pl.pallas_call(
        _peer_select_body,
        grid=(t_tokens // tm, PEER_HEADS // hps),
        in_specs=[pl.BlockSpec((2 * hps, PEER_KEYS, tm), lambda i, h: (h, 0, i)),
                  pl.BlockSpec((8, tm), lambda i, h: (0, i))],
        out_specs=(spec, spec, spec, spec),
        out_shape=(shp, shp, shp_b, shp_b),
        compiler_params=_params(("parallel", "parallel"), 32),
        name="peer_select",
    )(st, scl)


def _transpose_body(v_ref, s_ref, o_ref):
    o_ref[...] = (v_ref[...].T * s_ref[0:1, 0:1]).astype(FP8)


def _transpose_fp8(v, scale_tile):
    n, d = v.shape
    tn = 512
    return pl.pallas_call(
        _transpose_body,
        grid=(n // tn,),
        in_specs=[pl.BlockSpec((tn, d), lambda i: (i, 0)),
                  pl.BlockSpec((8, LANES), lambda i: (0, 0))],
        out_specs=pl.BlockSpec((d, tn), lambda i: (0, i)),
        out_shape=jax.ShapeDtypeStruct((d, n), FP8),
        compiler_params=_params(("parallel",), 32),
        name="transpose_v",
    )(v, scale_tile)


def _peer_expert_body(h2t_ref, u_ref, vt_ref, l_ref, w1_ref, r2_ref, w2_ref, scl_ref,
                      o_ref, ce_ref, co_ref, *, steps_per_tile):
    g = pl.program_id(0)
    sw = MXU_COLS
    strips = [slice(c * sw, (c + 1) * sw) for c in range(TM_PEER // sw)]
    n_piece = TE_PEER // PEER_KEYS
    blk = D_MODEL // n_piece

    @pl.when(g == 0)
    def _():
        co_ref[...] = jnp.zeros_like(co_ref)

    @pl.when((g == 0) | ((g - 1) % steps_per_tile == 0))
    def _():
        o_ref[...] = jnp.zeros_like(o_ref)

    def run(c_new, c_old):
        def piece(j, carry):
            r0 = pl.multiple_of(j * PEER_KEYS, PEER_KEYS)
            d0 = pl.multiple_of(j * blk, blk)
            for ls in strips:
                coef = None
                for h in range(PEER_HEADS):
                    lrow = jnp.broadcast_to(l_ref[h, pl.ds(j, 1), ls], (16, sw)).astype(BF16)
                    w1row = jnp.broadcast_to(w1_ref[h, pl.ds(j, 1), ls], (16, sw)).astype(BF16)
                    lrow = jnp.tile(lrow, (PEER_KEYS // 16, 1))
                    w1row = jnp.tile(w1row, (PEER_KEYS // 16, 1))
                    term = jnp.where(r2_ref[h, :, ls] < lrow, w2_ref[h, :, ls] * w1row, jnp.zeros((), BF16))
                    coef = term if coef is None else coef + term
                a_t = jnp.dot(u_ref[pl.ds(r0, PEER_KEYS), :], h2t_ref[:, ls], preferred_element_type=F32)
                a_scale = jnp.broadcast_to(scl_ref[0:1, ls], (16, sw)).astype(BF16)
                act = _gelu(a_t.astype(BF16) * jnp.tile(a_scale, (PEER_KEYS // 16, 1)))
                c_new[pl.ds(r0, PEER_KEYS), ls] = (coef * act).astype(FP8)
                o_ref[pl.ds(d0, blk), ls] += jnp.dot(vt_ref[pl.ds(d0, blk), :], c_old[:, ls],
                                                     preferred_element_type=F32)
            return carry

        lax.fori_loop(0, n_piece, piece, 0, unroll=8)

    @pl.when(g % 2 == 0)
    def _():
        run(ce_ref, co_ref)

    @pl.when(g % 2 == 1)
    def _():
        run(co_ref, ce_ref)


def _peer_expert(h2t, u_b, v_t, lrow, w1, r2, w2, scl):
    t_tokens = h2t.shape[1]
    tm, te = TM_PEER, TE_PEER
    n_piece = te // PEER_KEYS
    ne = PEER_EXPERTS // te
    n_steps = (t_tokens // tm) * ne
    cur = lambda g: jnp.minimum(g, n_steps - 1)
    prev = lambda g: jnp.maximum(g - 1, 0)
    row_spec = pl.BlockSpec((PEER_HEADS, n_piece, tm), lambda g: (0, cur(g) % ne, cur(g) // ne))
    full_spec = pl.BlockSpec((PEER_HEADS, PEER_KEYS, tm), lambda g: (0, 0, cur(g) // ne))
    return pl.pallas_call(
        functools.partial(_peer_expert_body, steps_per_tile=ne),
        grid=(n_steps + 1,),
        in_specs=[pl.BlockSpec((D_MODEL, tm), lambda g: (0, cur(g) // ne)),
                  pl.BlockSpec((te, D_MODEL), lambda g: (cur(g) % ne, 0)),
                  pl.BlockSpec((D_MODEL, te), lambda g: (0, prev(g) % ne)),
                  row_spec, row_spec, full_spec, full_spec,
                  pl.BlockSpec((8, tm), lambda g: (0, cur(g) // ne))],
        out_specs=pl.BlockSpec((D_MODEL, tm), lambda g: (0, prev(g) // ne)),
        out_shape=jax.ShapeDtypeStruct((D_MODEL, t_tokens), F32),
        scratch_shapes=[pltpu.VMEM((te, tm), FP8), pltpu.VMEM((te, tm), FP8)],
        compiler_params=_params(("arbitrary",), 52),
        name="peer_experts",
    )(h2t, u_b, v_t, lrow, w1, r2, w2, scl)


def _final_body(x1_ref, pt_ref, scl_ref, g2_ref, gn_ref, o_ref, *, apply_norm):
    peer = (pt_ref[...] * scl_ref[2:3, :]).T
    y = x1_ref[...] + g2_ref[0] * peer
    if apply_norm:
        ms = jnp.mean(y * y, axis=-1, keepdims=True)
        y = y * lax.rsqrt(ms + EPS) * gn_ref[...]
    o_ref[...] = y


def _final(x1, peer_t, scl, gate2, g_final, seq, apply_norm):
    t_tokens = x1.shape[0]
    tm = TM_PROJ
    tps = seq // tm
    return pl.pallas_call(
        functools.partial(_final_body, apply_norm=apply_norm),
        grid=(t_tokens // tm,),
        in_specs=[pl.BlockSpec((tm, D_MODEL), lambda i: (i, 0)),
                  pl.BlockSpec((D_MODEL, tm), lambda i: (0, i)),
                  pl.BlockSpec((8, tm), lambda i: (0, i)),
                  pl.BlockSpec((1, 1, D_MODEL), lambda i: (i // tps, 0, 0)),
                  pl.BlockSpec((1, D_MODEL), lambda i: (0, 0))],
        out_specs=pl.BlockSpec((tm, D_MODEL), lambda i: (i, 0)),
        out_shape=jax.ShapeDtypeStruct((t_tokens, D_MODEL), F32),
        compiler_params=_params(("parallel",), 32),
        name="final_norm",
    )(x1, peer_t, scl, gate2, g_final)


def _split_cols(a, sizes):
    out, acc = [], 0
    for s in sizes:
        out.append(a[:, acc:acc + s])
        acc += s
    return out


def _inproj_weights(w_in):
    kvw = KV_GROUPS * HEAD_DIM
    sizes = (1024,) + (kvw,) * 6 + (3 * N_HEADS, 1024, 1024, 1024, 1024)
    q_a, k_c, v_c, k_s, v_s, k_w, v_w, g_a, q_r, k_r, v_r, g_r = _split_cols(w_in, sizes)
    d = w_in.shape[0]

    def grp(a, g):
        return a[:, g * HEAD_DIM:(g + 1) * HEAD_DIM]

    cv = [jnp.concatenate([grp(k_c, g), grp(v_c, g)], axis=1) for g in range(KV_GROUPS)]
    w_std = jnp.concatenate(cv + [k_s, k_w, q_r, k_r, v_r, g_r], axis=1).astype(BF16)
    gcols = []
    for g in range(KV_GROUPS):
        for br in range(3):
            for r in range(GROUP):
                c = (g * GROUP + r) * 3 + br
                gcols.append(g_a[:, c:c + 1])
        gcols.append(jnp.zeros((d, 4), w_in.dtype))
    w_tr = jnp.concatenate([q_a, v_s, v_w] + gcols, axis=1).T.astype(BF16)
    return w_std, w_tr


def _compress_weights(pe_k, pe_v, k_w1, k_w2, v_w1, v_w2):
    half = CMP_LEN // 2

    def w1_half(w1k, w1v, lo):
        a = w1k.reshape(CMP_LEN, HEAD_DIM, CMP_HIDDEN)[lo:lo + half]
        b = w1v.reshape(CMP_LEN, HEAD_DIM, CMP_HIDDEN)[lo:lo + half]
        za = jnp.zeros_like(a)
        top = jnp.concatenate([a, za], axis=2)
        bot = jnp.concatenate([za, b], axis=2)
        return jnp.concatenate([top, bot], axis=1).reshape(half * 2 * HEAD_DIM, 2 * CMP_HIDDEN).astype(BF16)

    def pe_half(lo):
        row = jnp.concatenate([pe_k[lo:lo + half], pe_v[lo:lo + half]], axis=1).reshape(1, -1)
        return jnp.broadcast_to(row, (8, row.shape[1])).astype(BF16)

    wa = w1_half(k_w1, v_w1, 0)
    wb = w1_half(k_w1, v_w1, half)
    w2k = jnp.zeros((2 * CMP_HIDDEN, LANES), F32).at[:CMP_HIDDEN, :HEAD_DIM].set(k_w2).astype(BF16)
    w2vt = jnp.zeros((HEAD_DIM, 2 * CMP_HIDDEN), F32).at[:, CMP_HIDDEN:].set(v_w2.T).astype(BF16)
    return wa, wb, pe_half(0), pe_half(half), w2k, w2vt


def _nsa_constants(seq):
    slopes = jnp.exp2(-8.0 * (jnp.arange(N_HEADS, dtype=F32) + 1.0) / N_HEADS) * LOG2E
    s_hi = slopes.astype(BF16)
    s_lo = (slopes - s_hi.astype(F32)).astype(BF16)
    rows = jnp.zeros((N_HEADS, HEAD_DIM), BF16)
    rows = rows.at[:, 0].set(s_hi).at[:, 1].set(s_hi).at[:, 2].set(s_lo).at[:, 3].set(s_lo)
    qaug = jnp.broadcast_to(rows[:, :, None], (N_HEADS, HEAD_DIM, LANES))
    n_rows = seq // CMP_STRIDE
    n_slc = seq // SLC_BLOCK
    start = np.arange(n_rows)[:, None] * CMP_STRIDE
    end = start + CMP_LEN - 1
    blk = np.arange(n_slc)[None, :] * SLC_BLOCK
    ovl = ((start < blk + SLC_BLOCK) & (end >= blk)).astype(np.float32)
    per_tile = TK_SEL // SLC_BLOCK
    grp = (np.arange(n_slc)[None, :] // per_tile == np.arange(n_slc // per_tile)[:, None])
    return qaug, jnp.asarray(ovl.T, BF16), jnp.asarray(grp.astype(np.float32), BF16)


def _retention_constants():
    h, c = RET_HEADS, RET_CHUNK
    lg = jnp.log1p(-jnp.exp2(-5.0 - jnp.arange(h, dtype=F32)))
    pos = jnp.arange(c, dtype=F32)
    diff = pos[:, None] - pos[None, :]
    scale = RET_DIM ** -0.5
    dm = jnp.where(diff >= 0, jnp.exp(lg[:, None, None] * jnp.maximum(diff, 0.0)), 0.0) * scale
    k_decay = jnp.exp(lg[:, None] * (c - 1.0 - pos)) * scale
    q_decay = jnp.exp(lg[:, None] * (pos + 1.0))
    chunk_decay = jnp.exp(lg * c)
    kd = jnp.broadcast_to(k_decay[:, :, None], (h, c, RET_DIM))
    qd = jnp.broadcast_to(q_decay[:, :, None], (h, c, RET_DIM))
    cd = jnp.broadcast_to(chunk_decay[:, None, None], (h, RET_DIM, RET_DIM))
    return dm, kd, qd, cd


def kernel(x, c, w_ada, b_ada, g_norm_mix, g_norm_ffn, g_norm_final, w_in, cmp_pe_k, cmp_pe_v,
           cmp_k_w1, cmp_k_w2, cmp_v_w1, cmp_v_w2, g_nsa_out, g_ret_out, w_out,
           peer_w_q, peer_sub_keys, peer_u, peer_v):
    batch, seq, d = x.shape
    depth = w_ada.shape[0]
    t_tokens = batch * seq
    xf = x.reshape(t_tokens, d)
    c_pad = jnp.zeros((8, d), F32).at[:batch].set(c)
    qaug, ovl_t, grp = _nsa_constants(seq)
    dm, kd, qd, cd = _retention_constants()

    for l in range(depth):
        mod = _adaln(c_pad, w_ada[l], b_ada[l][None, :])[:batch].reshape(batch, 6, 1, d)
        shift1, scale1, gate1, shift2, scale2, gate2 = (mod[:, k] for k in range(6))

        w_std, w_tr = _inproj_weights(w_in[l])
        (cv, ks, kw, q_r, k_r, v_r, g_r, qt, vst, vwt, gt) = _inproj(
            xf, scale1, shift1, g_norm_mix[l][None, :], w_std, w_tr, seq)

        cv4 = cv.reshape(KV_GROUPS, batch, seq // CMP_STRIDE, CMP_STRIDE * LANES)
        kcp, vct = _compress(cv4, *_compress_weights(cmp_pe_k[l], cmp_pe_v[l], cmp_k_w1[l], cmp_k_w2[l],
                                                     cmp_v_w1[l], cmp_v_w2[l]))
        gout_b = jnp.broadcast_to(g_nsa_out[l][:, :, None], (N_HEADS, HEAD_DIM, LANES))
        o_nsa = _nsa(qt, gt, kcp, vct, ks, kw, vst, vwt, qaug, gout_b, ovl_t, grp, batch, seq)

        go = jnp.broadcast_to(g_ret_out[l][:, None, :], (RET_HEADS, 8, RET_DIM))
        o_ret = _retention(q_r, k_r, v_r, g_r, dm, kd, qd, cd, go, batch, seq)

        sub_keys = peer_sub_keys[l].reshape(2 * PEER_HEADS, PEER_KEYS, -1).astype(BF16)
        s_u = _pow2_scale(jnp.max(jnp.abs(peer_u[l])))
        s_v = _pow2_scale(jnp.max(jnp.abs(peer_v[l])))
        u_norm = jnp.sqrt(jnp.max(jnp.sum(jnp.square(peer_u[l]), axis=1)))
        peer_scales = jnp.broadcast_to(
            jnp.stack([1.0 / s_u, 1.0 / s_v, u_norm] + [jnp.zeros((), F32)] * 5)[:, None], (8, LANES))
        x1, h2t, st, scl = _mid(o_nsa, o_ret, xf, gate1, scale2, shift2, g_norm_ffn[l][None, :],
                                w_out[l].astype(BF16), peer_w_q[l].T.astype(BF16), sub_keys, peer_scales, seq)

        lrow, w1, r2, w2 = _peer_select(st, scl)
        peer_t = _peer_expert(h2t, (peer_u[l] * s_u).astype(FP8),
                              _transpose_fp8(peer_v[l], jnp.full((8, LANES), s_v, F32)),
                              lrow, w1, r2, w2, scl)
        xf = _final(x1, peer_t, scl, gate2, g_norm_final[None, :], seq, apply_norm=(l == depth - 1))
    return xf.reshape(batch, seq, d)
```

```python
import functools
import math

import numpy as np
import jax
import jax.numpy as jnp
from jax import lax
from jax.experimental import pallas as pl
from jax.experimental.pallas import tpu as pltpu

F32 = jnp.float32
BF16 = jnp.bfloat16
FP8 = jnp.float8_e4m3fn
FP8_TARGET = 224.0

D_MODEL = 2048
N_HEADS = 16
HEAD_DIM = 64
KV_GROUPS = 4
GROUP = 4
CMP_LEN = 32
CMP_STRIDE = 16
CMP_HIDDEN = 128
SLC_BLOCK = 64
SLC_TOPK = 16
WINDOW = 512
NEG = -1e30
RET_HEADS = 8
RET_DIM = 128
RET_CHUNK = 128
PEER_HEADS = 8
PEER_KEYS = 128
PEER_EXPERTS = PEER_KEYS * PEER_KEYS
PEER_TOPK = 16
COEF_BOUND_FACTOR = 1.25 * PEER_HEADS
EPS = 1e-6
LOG2E = 1.4426950408889634

LANES = 128
MXU_COLS = 256
TQ = 128
NSA_TILES_PER_STEP = 4
TK_SEL = 256
TM_PROJ = 256
TM_PEER = 512
TE_PEER = 1024
TM_SEL = 256
SEL_HEADS_PER_STEP = 4
RET_TILE = 2048

_NT = (((1,), (1,)), ((), ()))
_TN = (((0,), (0,)), ((), ()))


def _params(sem, vmem_mb):
    return pltpu.CompilerParams(dimension_semantics=sem, vmem_limit_bytes=vmem_mb * 1024 * 1024)


def _resident(shape, index_map):
    return pl.BlockSpec(shape, index_map, pipeline_mode=pl.Buffered(1))


def _gelu(x):
    return jax.nn.gelu(x)


def _adaln_body(c_ref, w_ref, b_ref, o_ref):
    c = c_ref[...]
    act = (c * jax.nn.sigmoid(c)).astype(BF16)
    o_ref[...] = jnp.dot(act, w_ref[...].astype(BF16), preferred_element_type=F32) + b_ref[...]


def _adaln(c_pad, w, b):
    n = w.shape[1]
    tn = 1536
    return pl.pallas_call(
        _adaln_body,
        grid=(n // tn,),
        in_specs=[pl.BlockSpec((8, D_MODEL), lambda j: (0, 0)),
                  pl.BlockSpec((D_MODEL, tn), lambda j: (0, j)),
                  pl.BlockSpec((1, tn), lambda j: (0, j))],
        out_specs=pl.BlockSpec((8, tn), lambda j: (0, j)),
        out_shape=jax.ShapeDtypeStruct((8, n), F32),
        compiler_params=_params(("arbitrary",), 40),
        name="adaln",
    )(c_pad, w, b)


STD_COLS = 512 + 256 * 2 + 1024 * 4
TR_ROWS = 1024 + 256 + 256 + 64


def _inproj_body(x_ref, sc_ref, sh_ref, gn_ref, wstd_ref, wt_ref,
                 cv_ref, ks_ref, kw_ref, qr_ref, kr_ref, vr_ref, gr_ref,
                 qt_ref, vst_ref, vwt_ref, gt_ref, *, tiles_per_seq):
    tm = TM_PROJ
    i = pl.program_id(0)
    x = x_ref[...]
    ms = jnp.mean(x * x, axis=-1, keepdims=True)
    h = x * lax.rsqrt(ms + EPS) * gn_ref[...]
    h = h * (1.0 + sc_ref[0]) + sh_ref[0]
    hb = h.astype(BF16)

    def std(a, b):
        return jnp.dot(hb, wstd_ref[:, a:b], preferred_element_type=F32)

    y = std(0, 512)
    for g in range(KV_GROUPS):
        cv_ref[g] = y[:, g * LANES:(g + 1) * LANES].astype(BF16)

    t = (i % tiles_per_seq) * tm + lax.broadcasted_iota(jnp.int32, (tm, LANES), 0)
    lane = lax.broadcasted_iota(jnp.int32, (tm, LANES), 1)
    pos_hi = ((t >> 6) << 6).astype(F32)
    pos_lo = (t & 63).astype(F32)
    aug = jnp.where((lane == 64) | (lane == 66), pos_hi,
                    jnp.where((lane == 65) | (lane == 67), pos_lo, 0.0))
    for ref, off in ((ks_ref, 512), (kw_ref, 768)):
        y = std(off, off + 256)
        for g in range(KV_GROUPS):
            pair = y[:, (g // 2) * LANES:(g // 2 + 1) * LANES]
            keys = pair if g % 2 == 0 else pltpu.roll(pair, HEAD_DIM, 1)
            ref[:, g * LANES:(g + 1) * LANES] = jnp.where(lane < HEAD_DIM, keys, aug).astype(BF16)

    for ref, off in ((qr_ref, 1024), (kr_ref, 2048), (vr_ref, 3072), (gr_ref, 4096)):
        ref[...] = std(off, off + 1024).astype(BF16)

    def tr(a, b):
        return lax.dot_general(wt_ref[a:b, :], hb, _NT, preferred_element_type=F32)

    qt = tr(0, 1024) * (HEAD_DIM ** -0.5 * LOG2E)
    vst = tr(1024, 1280)
    vwt = tr(1280, 1536)
    gt = tr(1536, 1600)
    for c in range(tm // LANES):
        sl = slice(c * LANES, (c + 1) * LANES)
        qt_ref[c] = qt[:, sl].astype(BF16)
        vst_ref[c] = vst[:, sl].astype(BF16)
        vwt_ref[c] = vwt[:, sl].astype(BF16)
        gt_ref[c] = gt[:, sl]


def _inproj(x2, scale1, shift1, g_mix, w_std, w_tr, seq):
    t_tokens = x2.shape[0]
    tm = TM_PROJ
    tps = seq // tm
    nt = t_tokens // tm
    c = tm // LANES
    row = lambda i: (i, 0)
    per_b = lambda i: (i // tps, 0, 0)
    out_shape = (
        jax.ShapeDtypeStruct((KV_GROUPS, t_tokens, LANES), BF16),
        jax.ShapeDtypeStruct((t_tokens, 512), BF16),
        jax.ShapeDtypeStruct((t_tokens, 512), BF16),
        jax.ShapeDtypeStruct((t_tokens, 1024), BF16),
        jax.ShapeDtypeStruct((t_tokens, 1024), BF16),
        jax.ShapeDtypeStruct((t_tokens, 1024), BF16),
        jax.ShapeDtypeStruct((t_tokens, 1024), BF16),
        jax.ShapeDtypeStruct((t_tokens // LANES, 1024, LANES), BF16),
        jax.ShapeDtypeStruct((t_tokens // LANES, 256, LANES), BF16),
        jax.ShapeDtypeStruct((t_tokens // LANES, 256, LANES), BF16),
        jax.ShapeDtypeStruct((t_tokens // LANES, 64, LANES), F32),
    )
    out_specs = (
        pl.BlockSpec((KV_GROUPS, tm, LANES), lambda i: (0, i, 0)),
        pl.BlockSpec((tm, 512), row),
        pl.BlockSpec((tm, 512), row),
        pl.BlockSpec((tm, 1024), row),
        pl.BlockSpec((tm, 1024), row),
        pl.BlockSpec((tm, 1024), row),
        pl.BlockSpec((tm, 1024), row),
        pl.BlockSpec((c, 1024, LANES), lambda i: (i, 0, 0)),
        pl.BlockSpec((c, 256, LANES), lambda i: (i, 0, 0)),
        pl.BlockSpec((c, 256, LANES), lambda i: (i, 0, 0)),
        pl.BlockSpec((c, 64, LANES), lambda i: (i, 0, 0)),
    )
    return pl.pallas_call(
        functools.partial(_inproj_body, tiles_per_seq=tps),
        grid=(nt,),
        in_specs=[pl.BlockSpec((tm, D_MODEL), row),
                  pl.BlockSpec((1, 1, D_MODEL), per_b),
                  pl.BlockSpec((1, 1, D_MODEL), per_b),
                  _resident((1, D_MODEL), lambda i: (0, 0)),
                  _resident((D_MODEL, STD_COLS), lambda i: (0, 0)),
                  _resident((TR_ROWS, D_MODEL), lambda i: (0, 0))],
        out_specs=out_specs,
        out_shape=out_shape,
        compiler_params=_params(("parallel",), 56),
        name="inproj",
    )(x2, scale1, shift1, g_mix, w_std, w_tr)


def _compress_body(x_ref, wa_ref, wb_ref, pea_ref, peb_ref, w2k_ref, w2vt_ref, kcp_ref, vct_ref):
    x = x_ref[0, 0]
    n_rows = x.shape[0]
    p = jnp.dot(x, wa_ref[...], preferred_element_type=F32)
    q = jnp.dot(x, wb_ref[...], preferred_element_type=F32)
    pe = (jnp.dot(pea_ref[...], wa_ref[...], preferred_element_type=F32)
          + jnp.dot(peb_ref[...], wb_ref[...], preferred_element_type=F32))[0:1, :]
    pre = p + pltpu.roll(q, n_rows - 1, 0) + pe
    hid = _gelu(pre).astype(BF16)
    kc = jnp.dot(hid, w2k_ref[...], preferred_element_type=F32)
    n = lax.broadcasted_iota(jnp.int32, (n_rows, LANES), 0)
    lane = lax.broadcasted_iota(jnp.int32, (n_rows, LANES), 1)
    ce = n * CMP_STRIDE + (CMP_LEN - 1)
    ce_hi = ((ce >> 6) << 6).astype(F32)
    ce_lo = (ce & 63).astype(F32)
    aug = jnp.where((lane == 64) | (lane == 66), ce_hi,
                    jnp.where((lane == 65) | (lane == 67), ce_lo, 0.0))
    kcp_ref[0] = (kc + aug).astype(BF16)
    vct_ref[0] = lax.dot_general(w2vt_ref[...], hid, _NT, preferred_element_type=F32).astype(BF16)


def _compress(cv4, wa, wb, pea, peb, w2k, w2vt):
    g_, b_, n_rows, _ = cv4.shape
    const2 = lambda n: (0, 0)
    return pl.pallas_call(
        _compress_body,
        grid=(b_ * g_,),
        in_specs=[pl.BlockSpec((1, 1, n_rows, 2048), lambda n: (n % KV_GROUPS, n // KV_GROUPS, 0, 0)),
                  pl.BlockSpec((2048, 256), const2),
                  pl.BlockSpec((2048, 256), const2),
                  pl.BlockSpec((8, 2048), const2),
                  pl.BlockSpec((8, 2048), const2),
                  pl.BlockSpec((256, LANES), const2),
                  pl.BlockSpec((64, 256), const2)],
        out_specs=(pl.BlockSpec((1, n_rows, LANES), lambda n: (n, 0, 0)),
                   pl.BlockSpec((1, 64, n_rows), lambda n: (n, 0, 0))),
        out_shape=(jax.ShapeDtypeStruct((b_ * g_, n_rows, LANES), BF16),
                   jax.ShapeDtypeStruct((b_ * g_, 64, n_rows), BF16)),
        compiler_params=_params(("parallel",), 32),
        name="nsa_compress",
    )(cv4, wa, wb, pea, peb, w2k, w2vt)


def _softmax_step(state, s, pv_prev):
    m_i, l_i, acc = state
    m_new = jnp.maximum(m_i, jnp.max(s, axis=0, keepdims=True))
    alpha = jnp.exp2(m_i - m_new)
    p = jnp.exp2(s - m_new)
    l_new = alpha * l_i + jnp.sum(p, axis=0, keepdims=True)
    return (m_new, l_new, (acc + pv_prev) * alpha), p.astype(BF16)


def _nsa_body(qt_ref, gt_ref, kcp_ref, vct_ref, ks_ref, kw_ref, vst_ref, vwt_ref,
              qaug_ref, gout_ref, ovl_ref, grp_ref, o_ref, *scratch, n_cmp):
    per_tile = len(scratch) // NSA_TILES_PER_STEP
    for sub in range(NSA_TILES_PER_STEP):
        _nsa_tile(pl.program_id(2) * NSA_TILES_PER_STEP + sub,
                  qt_ref.at[pl.ds(sub, 1)], gt_ref.at[pl.ds(sub, 1)], kcp_ref, vct_ref, ks_ref, kw_ref,
                  vst_ref, vwt_ref, qaug_ref, gout_ref, ovl_ref, grp_ref,
                  o_ref.at[pl.ds(sub * TQ, TQ)], *scratch[sub * per_tile:(sub + 1) * per_tile], n_cmp=n_cmp)


def _nsa_tile(qi, qt_ref, gt_ref, kcp_ref, vct_ref, ks_ref, kw_ref, vst_ref, vwt_ref,
              qaug_ref, gout_ref, ovl_ref, grp_ref, o_ref, selb_ref, sa_ref, sb_ref, pa_ref, pb_ref, *, n_cmp):
    t0 = qi * TQ
    wq = GROUP * TQ

    qp = jnp.concatenate(
        [jnp.concatenate([qt_ref[0, r * HEAD_DIM:(r + 1) * HEAD_DIM, :], qaug_ref[r]], axis=0)
         for r in range(GROUP)], axis=1)

    lane_q = lax.broadcasted_iota(jnp.int32, (1, wq), 1) & (TQ - 1)
    n_win = WINDOW // TQ + 1
    k_lo = pl.multiple_of(jnp.maximum(t0 - WINDOW, 0), TQ)

    s = jnp.dot(kcp_ref[0], qp, preferred_element_type=F32)
    s_w = jnp.dot(kw_ref[pl.ds(k_lo, n_win * TQ), :], qp, preferred_element_type=F32)
    n_io = lax.broadcasted_iota(jnp.int32, (n_cmp, wq), 0)
    tl = lax.broadcasted_iota(jnp.int32, (n_cmp, wq), 1) & (TQ - 1)
    valid = (n_io * CMP_STRIDE + (CMP_LEN - 1)) <= (t0 + tl)
    s = jnp.where(valid, s, NEG)
    m = jnp.maximum(jnp.max(s, axis=0, keepdims=True), 0.5 * NEG)
    p = jnp.exp2(s - m)
    l = jnp.sum(p, axis=0, keepdims=True)
    pn = p * (1.0 / jnp.maximum(l, 1e-30))
    o_c = jnp.dot(vct_ref[0], pn.astype(BF16), preferred_element_type=F32)

    ps = pn[:, 0:TQ] + pn[:, TQ:2 * TQ] + pn[:, 2 * TQ:3 * TQ] + pn[:, 3 * TQ:4 * TQ]
    hi = ps.astype(BF16)
    r1 = ps - hi.astype(F32)
    mid = r1.astype(BF16)
    lo = (r1 - mid.astype(F32)).astype(BF16)
    ovl = ovl_ref[...]
    imp = (jnp.dot(ovl, hi, preferred_element_type=F32)
           + jnp.dot(ovl, mid, preferred_element_type=F32)
           + jnp.dot(ovl, lo, preferred_element_type=F32))

    n_slc = imp.shape[0]
    m_io = lax.broadcasted_iota(jnp.int32, (n_slc, TQ), 0)
    q_io = lax.broadcasted_iota(jnp.int32, (n_slc, TQ), 1)
    back = ((t0 + q_io) >> 6) - m_io
    valid_s = back >= 0
    forced = valid_s & ((m_io == 0) | (back < 2))
    w = jnp.where(forced, -jnp.inf, jnp.where(valid_s, imp, -1.0))
    selb = jnp.where(forced, 0.0, NEG)

    def pick(carry, lanes=None):
        w, selb = carry
        mx = jnp.max(w, axis=0, keepdims=True)
        idx = jnp.min(jnp.where(w == mx, m_io, n_slc), axis=0, keepdims=True)
        hit = m_io == idx
        if lanes is not None:
            hit = hit & lanes
        return jnp.where(hit, -jnp.inf, w), jnp.where(hit, 0.0, selb)

    carry = (w, selb)
    for _ in range(SLC_TOPK - 3):
        carry = pick(carry)
    carry = pick(carry, lanes=(t0 + q_io) < 2 * SLC_BLOCK)
    _, selb = pick(carry, lanes=(t0 + q_io) < SLC_BLOCK)
    selb_ref[...] = selb

    per_blk = TK_SEL // SLC_BLOCK

    def scores(j):
        k0 = pl.multiple_of(j * TK_SEL, TK_SEL)
        bias = jnp.concatenate(
            [jnp.broadcast_to(selb_ref[pl.ds(j * per_blk + u, 1), :], (SLC_BLOCK, TQ)) for u in range(per_blk)],
            axis=0)
        bias = jnp.concatenate([bias] * GROUP, axis=1)
        return jnp.dot(ks_ref[pl.ds(k0, TK_SEL), :], qp, preferred_element_type=F32) + bias

    def vt_tile(j):
        return jnp.concatenate([vst_ref[2 * j], vst_ref[2 * j + 1]], axis=1)

    def causal(j):
        kr = lax.broadcasted_iota(jnp.int32, (TK_SEL, wq), 0)
        return kr <= (t0 - j * TK_SEL) + lane_q

    def pv_dot(j, p_ref):
        return jnp.dot(vt_tile(j), p_ref[...], preferred_element_type=F32)

    jd = qi // (TK_SEL // TQ)

    sa_ref[...] = scores(0)
    pb_ref[...] = jnp.zeros_like(pb_ref)
    sel01 = jnp.where(selb == 0.0, 1.0, 0.0).astype(BF16)
    tile_any = jnp.max(jnp.dot(grp_ref[...], sel01, preferred_element_type=F32), axis=1, keepdims=True)
    j_io = lax.broadcasted_iota(jnp.int32, tile_any.shape, 0)
    q_lo_v = jnp.max(jnp.where((tile_any == 0.0) & (j_io <= jd), j_io, -1), axis=0, keepdims=True) + 1
    n_pre_v = jnp.max(jnp.where((tile_any > 0.0) & (j_io < q_lo_v), j_io, -1), axis=0, keepdims=True) + 1

    d = (t0 - k_lo) + lane_q - lax.broadcasted_iota(jnp.int32, (n_win * TQ, wq), 0)
    s_w = jnp.where(lax.bitcast_convert_type(d, jnp.uint32) < WINDOW, s_w, NEG)
    p = jnp.exp2(s_w - jnp.max(s_w, axis=0, keepdims=True))
    l_w = jnp.sum(p, axis=0, keepdims=True)
    vt_w = jnp.concatenate([vwt_ref[k_lo // TQ + u] for u in range(n_win)], axis=1)
    o_w = jnp.dot(vt_w, p.astype(BF16), preferred_element_type=F32) * (1.0 / l_w)

    q_lo = q_lo_v[0, 0]
    n_pre = n_pre_v[0, 0]
    n_vis = n_pre + jd - q_lo + 1

    def tile_at(pos):
        return jnp.where(pos < n_pre, pos, pos - n_pre + q_lo)

    state = (jnp.full((1, wq), NEG, F32), jnp.zeros((1, wq), F32), jnp.zeros((HEAD_DIM, wq), F32))

    def pair(u, state):
        a = 2 * u
        sb_ref[...] = scores(tile_at(a + 1))
        state, p = _softmax_step(state, sa_ref[...], pv_dot(tile_at(jnp.maximum(a - 1, 0)), pb_ref))
        pa_ref[...] = p
        sa_ref[...] = scores(tile_at(a + 2))
        state, p = _softmax_step(state, sb_ref[...], pv_dot(tile_at(a), pa_ref))
        pb_ref[...] = p
        return state

    n_pair = (n_vis - 1) // 2
    state = lax.fori_loop(0, n_pair, pair, state)
    x = 2 * n_pair
    y = jnp.minimum(x + 1, n_vis - 1)
    tx, ty = tile_at(x), tile_at(y)
    sb_ref[...] = scores(ty)
    state, p = _softmax_step(state, jnp.where(causal(tx), sa_ref[...], NEG),
                             pv_dot(tile_at(jnp.maximum(x - 1, 0)), pb_ref))
    pa_ref[...] = p
    ty_mask = jnp.where(x + 1 < n_vis, ty, jd + 1)
    (_, l_s, acc_s), p = _softmax_step(state, jnp.where(causal(ty_mask), sb_ref[...], NEG), pv_dot(tx, pa_ref))
    acc_s = acc_s + jnp.dot(vt_tile(ty), p, preferred_element_type=F32)

    o_s = acc_s * (1.0 / l_s)
    gw = jax.nn.sigmoid(gt_ref[0])
    outs = []
    for r in range(GROUP):
        sl = slice(r * TQ, (r + 1) * TQ)
        o = (gw[r:r + 1, :] * o_c[:, sl] + gw[GROUP + r:GROUP + r + 1, :] * o_s[:, sl]
             + gw[2 * GROUP + r:2 * GROUP + r + 1, :] * o_w[:, sl])
        ms = jnp.mean(o * o, axis=0, keepdims=True)
        outs.append(o * lax.rsqrt(ms + EPS) * gout_ref[r])
    o_ref[...] = jnp.concatenate(outs, axis=0).T.astype(BF16)


def _nsa(qt, gt, kcp, vct, ks, kw, vst, vwt, qaug, gout_b, ovl_t, grp, batch, seq):
    tps = NSA_TILES_PER_STEP
    nq = seq // (TQ * tps)
    n_cmp = kcp.shape[1]
    n_slc = seq // SLC_BLOCK
    t_tokens = batch * seq
    per_b_chunks = seq // LANES
    return pl.pallas_call(
        functools.partial(_nsa_body, n_cmp=n_cmp),
        grid=(batch, KV_GROUPS, nq),
        in_specs=[
            pl.BlockSpec((tps, 256, LANES), lambda b, g, q: (b * nq + q, g, 0)),
            pl.BlockSpec((tps, 16, LANES), lambda b, g, q: (b * nq + q, g, 0)),
            pl.BlockSpec((1, n_cmp, LANES), lambda b, g, q: (b * KV_GROUPS + g, 0, 0)),
            pl.BlockSpec((1, 64, n_cmp), lambda b, g, q: (b * KV_GROUPS + g, 0, 0)),
            pl.BlockSpec((seq, LANES), lambda b, g, q: (b, g)),
            pl.BlockSpec((seq, LANES), lambda b, g, q: (b, g)),
            pl.BlockSpec((per_b_chunks, 64, LANES), lambda b, g, q: (b, g, 0)),
            pl.BlockSpec((per_b_chunks, 64, LANES), lambda b, g, q: (b, g, 0)),
            pl.BlockSpec((GROUP, 64, LANES), lambda b, g, q: (g, 0, 0)),
            pl.BlockSpec((GROUP, 64, LANES), lambda b, g, q: (g, 0, 0)),
            pl.BlockSpec((n_slc, n_cmp), lambda b, g, q: (0, 0)),
            pl.BlockSpec(grp.shape, lambda b, g, q: (0, 0)),
        ],
        out_specs=pl.BlockSpec((tps * TQ, 256), lambda b, g, q: (b * nq + q, g)),
        out_shape=jax.ShapeDtypeStruct((t_tokens, 1024), BF16),
        scratch_shapes=[pltpu.VMEM((n_slc, TQ), F32),
                        pltpu.VMEM((TK_SEL, GROUP * TQ), F32), pltpu.VMEM((TK_SEL, GROUP * TQ), F32),
                        pltpu.VMEM((TK_SEL, GROUP * TQ), BF16), pltpu.VMEM((TK_SEL, GROUP * TQ), BF16)] * tps,
        compiler_params=_params(("parallel", "parallel", "arbitrary"), 40),
        name="nsa_attention",
    )(qt, gt, kcp, vct, ks, kw, vst, vwt, qaug, gout_b, ovl_t, grp)


def _ret_body(q_ref, k_ref, v_ref, g_ref, dm_ref, kd_ref, qd_ref, cd_ref, go_ref, o_ref, st_ref):
    @pl.when(pl.program_id(2) == 0)
    def _():
        st_ref[...] = jnp.zeros_like(st_ref)

    c_ = RET_CHUNK
    slices = [slice(c * c_, (c + 1) * c_) for c in range(RET_TILE // c_)]
    atts, kvs = [], []
    for sl in slices:
        k = k_ref[sl, :]
        atts.append((lax.dot_general(q_ref[sl, :], k, _NT, preferred_element_type=F32) * dm_ref[0]).astype(BF16))
        kdec = (k.astype(F32) * kd_ref[0]).astype(BF16)
        kvs.append(lax.dot_general(kdec, v_ref[sl, :], _TN, preferred_element_type=F32))
    states = [st_ref[...]]
    for kv in kvs:
        states.append(states[-1] * cd_ref[0] + kv)
    st_ref[...] = states[-1]
    for sl, att, state in zip(slices, atts, states):
        o = (jnp.dot(att, v_ref[sl, :], preferred_element_type=F32)
             + qd_ref[0] * jnp.dot(q_ref[sl, :], state.astype(BF16), preferred_element_type=F32))
        mu = jnp.mean(o, axis=-1, keepdims=True)
        oc = o - mu
        var = jnp.mean(oc * oc, axis=-1, keepdims=True)
        y = oc * lax.rsqrt(var + EPS) * go_ref[0, 0:1, :]
        gate = g_ref[sl, :].astype(F32)
        o_ref[sl, :] = (gate * jax.nn.sigmoid(gate) * y).astype(BF16)


def _retention(q_r, k_r, v_r, g_r, dm, kd, qd, cd, go, batch, seq):
    t_tokens = batch * seq
    nc = seq // RET_TILE
    tok = lambda b, h, c: (b * nc + c, h)
    per_h = lambda b, h, c: (h, 0, 0)
    sq = (1, RET_DIM, RET_DIM)
    return pl.pallas_call(
        _ret_body,
        grid=(batch, RET_HEADS, nc),
        in_specs=[pl.BlockSpec((RET_TILE, RET_DIM), tok)] * 4
        + [pl.BlockSpec(sq, per_h)] * 4 + [pl.BlockSpec((1, 8, RET_DIM), per_h)],
        out_specs=pl.BlockSpec((RET_TILE, RET_DIM), tok),
        out_shape=jax.ShapeDtypeStruct((t_tokens, RET_HEADS * RET_DIM), BF16),
        scratch_shapes=[pltpu.VMEM((RET_DIM, RET_DIM), F32)],
        compiler_params=_params(("parallel", "parallel", "arbitrary"), 32),
        name="retention",
    )(q_r, k_r, v_r, g_r, dm, kd, qd, cd, go)


def _pow2_scale(magnitude):
    return jnp.exp2(jnp.floor(jnp.log2(FP8_TARGET / jnp.maximum(magnitude, 1e-30))))


def _mid_body(on_ref, or_ref, x_ref, g1_ref, sc_ref, sh_ref, gn_ref, wo_ref, wqt_ref, sk_ref, ps_ref,
              x1_ref, h2t_ref, st_ref, scl_ref):
    acc = (jnp.dot(on_ref[...], wo_ref[0:1024, :], preferred_element_type=F32)
           + jnp.dot(or_ref[...], wo_ref[1024:2048, :], preferred_element_type=F32))
    x1 = x_ref[...] + g1_ref[0] * acc
    x1_ref[...] = x1
    ms = jnp.mean(x1 * x1, axis=-1, keepdims=True)
    h2 = x1 * lax.rsqrt(ms + EPS) * gn_ref[...]
    h2 = h2 * (1.0 + sc_ref[0]) + sh_ref[0]

    h2_t = h2.T
    amax = jnp.max(jnp.max(jnp.abs(h2_t), axis=0, keepdims=True), axis=1, keepdims=True)
    s_h = _pow2_scale(amax)
    h2t_ref[...] = (h2_t * s_h).astype(FP8)
    norm = jnp.sqrt(jnp.sum(h2_t * h2_t, axis=0, keepdims=True))
    s_c = _pow2_scale(COEF_BOUND_FACTOR * ps_ref[2:3, 0:1] * norm)
    scl_ref[...] = jnp.concatenate(
        [jnp.broadcast_to(ps_ref[0:1, 0:1] / s_h, s_c.shape), s_c, ps_ref[1:2, 0:1] / s_c,
         jnp.zeros((5, s_c.shape[1]), F32)], axis=0)
    h2 = h2.astype(BF16)
    qt = lax.dot_general(wqt_ref[...], h2, _NT, preferred_element_type=F32).astype(BF16)
    for hp in range(2 * PEER_HEADS):
        st_ref[hp] = jnp.dot(sk_ref[hp], qt[hp * 128:(hp + 1) * 128, :], preferred_element_type=F32)


def _mid(o_nsa, o_ret, x2, gate1, scale2, shift2, g_ffn, w_out, wq_t, sub_keys, peer_scales, seq):
    t_tokens = x2.shape[0]
    tm = TM_PROJ
    tps = seq // tm
    row = lambda i: (i, 0)
    per_b = lambda i: (i // tps, 0, 0)
    return pl.pallas_call(
        _mid_body,
        grid=(t_tokens // tm,),
        in_specs=[pl.BlockSpec((tm, 1024), row),
                  pl.BlockSpec((tm, 1024), row),
                  pl.BlockSpec((tm, D_MODEL), row),
                  pl.BlockSpec((1, 1, D_MODEL), per_b),
                  pl.BlockSpec((1, 1, D_MODEL), per_b),
                  pl.BlockSpec((1, 1, D_MODEL), per_b),
                  _resident((1, D_MODEL), lambda i: (0, 0)),
                  _resident((D_MODEL, D_MODEL), lambda i: (0, 0)),
                  _resident((D_MODEL, D_MODEL), lambda i: (0, 0)),
                  _resident((2 * PEER_HEADS, PEER_KEYS, 128), lambda i: (0, 0, 0)),
                  _resident((8, LANES), lambda i: (0, 0))],
        out_specs=(pl.BlockSpec((tm, D_MODEL), row),
                   pl.BlockSpec((D_MODEL, tm), lambda i: (0, i)),
                   pl.BlockSpec((2 * PEER_HEADS, PEER_KEYS, tm), lambda i: (0, 0, i)),
                   pl.BlockSpec((8, tm), lambda i: (0, i))),
        out_shape=(jax.ShapeDtypeStruct((t_tokens, D_MODEL), F32),
                   jax.ShapeDtypeStruct((D_MODEL, t_tokens), FP8),
                   jax.ShapeDtypeStruct((2 * PEER_HEADS, PEER_KEYS, t_tokens), F32),
                   jax.ShapeDtypeStruct((8, t_tokens), F32)),
        compiler_params=_params(("parallel",), 48),
        name="outproj_peerq",
    )(o_nsa, o_ret, x2, gate1, scale2, shift2, g_ffn, w_out, wq_t, sub_keys, peer_scales)


def _top16(s, break_ties):
    n_rows, n = s.shape
    io = lax.broadcasted_iota(jnp.int32, (n_rows, n), 0)
    a_io = lax.broadcasted_iota(jnp.int32, (PEER_TOPK, n), 0)
    rank = jnp.full((n_rows, n), PEER_TOPK, jnp.int32)
    vals = jnp.zeros((PEER_TOPK, n), F32)
    for a in range(PEER_TOPK):
        mx = jnp.max(s, axis=0, keepdims=True)
        hit = s == mx
        if break_ties:
            hit = io == jnp.min(jnp.where(hit, io, n_rows), axis=0, keepdims=True)
        rank = jnp.where(hit, a, rank)
        s = jnp.where(hit, -jnp.inf, s)
        vals = jnp.where(a_io == a, mx, vals)
    return vals, rank


def _peer_select_body(s_ref, scl_ref, l_ref, w1_ref, r2_ref, w2_ref):
    n = s_ref.shape[2]

    def select(hh, break_ties):
        s1 = s_ref[2 * hh]
        s2 = s_ref[2 * hh + 1]
        v1, rank1 = _top16(s1, break_ties)
        v2, rank2 = _top16(s2, break_ties)
        a_io = lax.broadcasted_iota(jnp.int32, (PEER_TOPK, n), 0)
        cnt = jnp.zeros((PEER_TOPK, n), jnp.int32)
        cur = v1 + v2[0:1, :]
        top = v1[0:1, :] + v2[0:1, :]
        z = jnp.zeros((1, n), F32)
        for _ in range(PEER_TOPK):
            mx = jnp.max(cur, axis=0, keepdims=True)
            aidx = jnp.min(jnp.where(cur == mx, a_io, PEER_TOPK), axis=0, keepdims=True)
            hit = a_io == aidx
            cnt = cnt + hit.astype(jnp.int32)
            nxt = jnp.sum(jnp.where(hit, cnt, 0), axis=0, keepdims=True)
            nv = jnp.max(jnp.where(a_io == nxt, v2, -jnp.inf), axis=0, keepdims=True)
            cur = jnp.where(hit, v1 + nv, cur)
            z = z + jnp.exp(mx - top)
        cnt_b = cnt.astype(F32).astype(BF16)
        rank_b = rank1.astype(F32).astype(BF16)
        lrow = jnp.zeros(s1.shape, BF16)
        for a in range(PEER_TOPK):
            lrow = jnp.where(rank_b == a, jnp.broadcast_to(cnt_b[a:a + 1, :], s1.shape), lrow)
        l_ref[hh] = lrow.astype(F32)
        w1_ref[hh] = jnp.exp(s1 - v1[0:1, :])
        r2_ref[hh] = rank2.astype(F32).astype(BF16)
        w2_ref[hh] = (jnp.exp(s2 - v2[0:1, :]) * (scl_ref[1:2, :] / z)).astype(BF16)
        return (jnp.sum((rank1 < PEER_TOPK).astype(jnp.int32), axis=0, keepdims=True)
                + jnp.sum((rank2 < PEER_TOPK).astype(jnp.int32), axis=0, keepdims=True))

    marked = functools.reduce(jnp.maximum, [select(hh, False) for hh in range(SEL_HEADS_PER_STEP)])

    @pl.when(jnp.max(marked) != 2 * PEER_TOPK)
    def _():
        for hh in range(SEL_HEADS_PER_STEP):
            select(hh, True)


def _peer_select(st, scl):
    t_tokens = st.shape[2]
    tm = TM_SEL
    shp = jax.ShapeDtypeStruct((PEER_HEADS, PEER_KEYS, t_tokens), F32)
    shp_b = jax.ShapeDtypeStruct((PEER_HEADS, PEER_KEYS, t_tokens), BF16)
    hps = SEL_HEADS_PER_STEP
    spec = pl.BlockSpec((hps, PEER_KEYS, tm), lambda i, h: (h, 0, i))
    return pl.pallas_call(
        _peer_select_body,
        grid=(t_tokens // tm, PEER_HEADS // hps),
        in_specs=[pl.BlockSpec((2 * hps, PEER_KEYS, tm), lambda i, h: (h, 0, i)),
                  pl.BlockSpec((8, tm), lambda i, h: (0, i))],
        out_specs=(spec, spec, spec, spec),
        out_shape=(shp, shp, shp_b, shp_b),
        compiler_params=_params(("parallel", "parallel"), 32),
        name="peer_select",
    )(st, scl)


def _transpose_body(v_ref, s_ref, o_ref):
    o_ref[...] = (v_ref[...].T * s_ref[0:1, 0:1]).astype(FP8)


def _transpose_fp8(v, scale_tile):
    n, d = v.shape
    tn = 512
    return pl.pallas_call(
        _transpose_body,
        grid=(n // tn,),
        in_specs=[pl.BlockSpec((tn, d), lambda i: (i, 0)),
                  pl.BlockSpec((8, LANES), lambda i: (0, 0))],
        out_specs=pl.BlockSpec((d, tn), lambda i: (0, i)),
        out_shape=jax.ShapeDtypeStruct((d, n), FP8),
        compiler_params=_params(("parallel",), 32),
        name="transpose_v",
    )(v, scale_tile)


def _peer_expert_body(h2t_ref, u_ref, vt_ref, l_ref, w1_ref, r2_ref, w2_ref, scl_ref,
                      o_ref, ce_ref, co_ref, *, steps_per_tile):
    g = pl.program_id(0)
    sw = MXU_COLS
    strips = [slice(c * sw, (c + 1) * sw) for c in range(TM_PEER // sw)]
    n_piece = TE_PEER // PEER_KEYS
    blk = D_MODEL // n_piece

    @pl.when(g == 0)
    def _():
        co_ref[...] = jnp.zeros_like(co_ref)

    @pl.when((g == 0) | ((g - 1) % steps_per_tile == 0))
    def _():
        o_ref[...] = jnp.zeros_like(o_ref)

    def run(c_new, c_old):
        def piece(j, carry):
            r0 = pl.multiple_of(j * PEER_KEYS, PEER_KEYS)
            d0 = pl.multiple_of(j * blk, blk)
            for ls in strips:
                coef = None
                for h in range(PEER_HEADS):
                    lrow = jnp.broadcast_to(l_ref[h, pl.ds(j, 1), ls], (16, sw)).astype(BF16)
                    w1row = jnp.broadcast_to(w1_ref[h, pl.ds(j, 1), ls], (16, sw)).astype(BF16)
                    lrow = jnp.tile(lrow, (PEER_KEYS // 16, 1))
                    w1row = jnp.tile(w1row, (PEER_KEYS // 16, 1))
                    term = jnp.where(r2_ref[h, :, ls] < lrow, w2_ref[h, :, ls] * w1row, jnp.zeros((), BF16))
                    coef = term if coef is None else coef + term
                a_t = jnp.dot(u_ref[pl.ds(r0, PEER_KEYS), :], h2t_ref[:, ls], preferred_element_type=F32)
                a_scale = jnp.broadcast_to(scl_ref[0:1, ls], (16, sw)).astype(BF16)
                act = _gelu(a_t.astype(BF16) * jnp.tile(a_scale, (PEER_KEYS // 16, 1)))
                c_new[pl.ds(r0, PEER_KEYS), ls] = (coef * act).astype(FP8)
                o_ref[pl.ds(d0, blk), ls] += jnp.dot(vt_ref[pl.ds(d0, blk), :], c_old[:, ls],
                                                     preferred_element_type=F32)
            return carry

        lax.fori_loop(0, n_piece, piece, 0, unroll=8)

    @pl.when(g % 2 == 0)
    def _():
        run(ce_ref, co_ref)

    @pl.when(g % 2 == 1)
    def _():
        run(co_ref, ce_ref)


def _peer_expert(h2t, u_b, v_t, lrow, w1, r2, w2, scl):
    t_tokens = h2t.shape[1]
    tm, te = TM_PEER, TE_PEER
    n_piece = te // PEER_KEYS
    ne = PEER_EXPERTS // te
    n_steps = (t_tokens // tm) * ne
    cur = lambda g: jnp.minimum(g, n_steps - 1)
    prev = lambda g: jnp.maximum(g - 1, 0)
    row_spec = pl.BlockSpec((PEER_HEADS, n_piece, tm), lambda g: (0, cur(g) % ne, cur(g) // ne))
    full_spec = pl.BlockSpec((PEER_HEADS, PEER_KEYS, tm), lambda g: (0, 0, cur(g) // ne))
    return pl.pallas_call(
        functools.partial(_peer_expert_body, steps_per_tile=ne),
        grid=(n_steps + 1,),
        in_specs=[pl.BlockSpec((D_MODEL, tm), lambda g: (0, cur(g) // ne)),
                  pl.BlockSpec((te, D_MODEL), lambda g: (cur(g) % ne, 0)),
                  pl.BlockSpec((D_MODEL, te), lambda g: (0, prev(g) % ne)),
                  row_spec, row_spec, full_spec, full_spec,
                  pl.BlockSpec((8, tm), lambda g: (0, cur(g) // ne))],
        out_specs=pl.BlockSpec((D_MODEL, tm), lambda g: (0, prev(g) // ne)),
        out_shape=jax.ShapeDtypeStruct((D_MODEL, t_tokens), F32),
        scratch_shapes=[pltpu.VMEM((te, tm), FP8), pltpu.VMEM((te, tm), FP8)],
        compiler_params=_params(("arbitrary",), 52),
        name="peer_experts",
    )(h2t, u_b, v_t, lrow, w1, r2, w2, scl)


def _final_body(x1_ref, pt_ref, scl_ref, g2_ref, gn_ref, o_ref, *, apply_norm):
    peer = (pt_ref[...] * scl_ref[2:3, :]).T
    y = x1_ref[...] + g2_ref[0] * peer
    if apply_norm:
        ms = jnp.mean(y * y, axis=-1, keepdims=True)
        y = y * lax.rsqrt(ms + EPS) * gn_ref[...]
    o_ref[...] = y


def _final(x1, peer_t, scl, gate2, g_final, seq, apply_norm):
    t_tokens = x1.shape[0]
    tm = TM_PROJ
    tps = seq // tm
    return pl.pallas_call(
        functools.partial(_final_body, apply_norm=apply_norm),
        grid=(t_tokens // tm,),
        in_specs=[pl.BlockSpec((tm, D_MODEL), lambda i: (i, 0)),
                  pl.BlockSpec((D_MODEL, tm), lambda i: (0, i)),
                  pl.BlockSpec((8, tm), lambda i: (0, i)),
                  pl.BlockSpec((1, 1, D_MODEL), lambda i: (i // tps, 0, 0)),
                  pl.BlockSpec((1, D_MODEL), lambda i: (0, 0))],
        out_specs=pl.BlockSpec((tm, D_MODEL), lambda i: (i, 0)),
        out_shape=jax.ShapeDtypeStruct((t_tokens, D_MODEL), F32),
        compiler_params=_params(("parallel",), 32),
        name="final_norm",
    )(x1, peer_t, scl, gate2, g_final)


def _split_cols(a, sizes):
    out, acc = [], 0
    for s in sizes:
        out.append(a[:, acc:acc + s])
        acc += s
    return out


def _inproj_weights(w_in):
    kvw = KV_GROUPS * HEAD_DIM
    sizes = (1024,) + (kvw,) * 6 + (3 * N_HEADS, 1024, 1024, 1024, 1024)
    q_a, k_c, v_c, k_s, v_s, k_w, v_w, g_a, q_r, k_r, v_r, g_r = _split_cols(w_in, sizes)
    d = w_in.shape[0]

    def grp(a, g):
        return a[:, g * HEAD_DIM:(g + 1) * HEAD_DIM]

    cv = [jnp.concatenate([grp(k_c, g), grp(v_c, g)], axis=1) for g in range(KV_GROUPS)]
    w_std = jnp.concatenate(cv + [k_s, k_w, q_r, k_r, v_r, g_r], axis=1).astype(BF16)
    gcols = []
    for g in range(KV_GROUPS):
        for br in range(3):
            for r in range(GROUP):
                c = (g * GROUP + r) * 3 + br
                gcols.append(g_a[:, c:c + 1])
        gcols.append(jnp.zeros((d, 4), w_in.dtype))
    w_tr = jnp.concatenate([q_a, v_s, v_w] + gcols, axis=1).T.astype(BF16)
    return w_std, w_tr


def _compress_weights(pe_k, pe_v, k_w1, k_w2, v_w1, v_w2):
    half = CMP_LEN // 2

    def w1_half(w1k, w1v, lo):
        a = w1k.reshape(CMP_LEN, HEAD_DIM, CMP_HIDDEN)[lo:lo + half]
        b = w1v.reshape(CMP_LEN, HEAD_DIM, CMP_HIDDEN)[lo:lo + half]
        za = jnp.zeros_like(a)
        top = jnp.concatenate([a, za], axis=2)
        bot = jnp.concatenate([za, b], axis=2)
        return jnp.concatenate([top, bot], axis=1).reshape(half * 2 * HEAD_DIM, 2 * CMP_HIDDEN).astype(BF16)

    def pe_half(lo):
        row = jnp.concatenate([pe_k[lo:lo + half], pe_v[lo:lo + half]], axis=1).reshape(1, -1)
        return jnp.broadcast_to(row, (8, row.shape[1])).astype(BF16)

    wa = w1_half(k_w1, v_w1, 0)
    wb = w1_half(k_w1, v_w1, half)
    w2k = jnp.zeros((2 * CMP_HIDDEN, LANES), F32).at[:CMP_HIDDEN, :HEAD_DIM].set(k_w2).astype(BF16)
    w2vt = jnp.zeros((HEAD_DIM, 2 * CMP_HIDDEN), F32).at[:, CMP_HIDDEN:].set(v_w2.T).astype(BF16)
    return wa, wb, pe_half(0), pe_half(half), w2k, w2vt


def _nsa_constants(seq):
    slopes = jnp.exp2(-8.0 * (jnp.arange(N_HEADS, dtype=F32) + 1.0) / N_HEADS) * LOG2E
    s_hi = slopes.astype(BF16)
    s_lo = (slopes - s_hi.astype(F32)).astype(BF16)
    rows = jnp.zeros((N_HEADS, HEAD_DIM), BF16)
    rows = rows.at[:, 0].set(s_hi).at[:, 1].set(s_hi).at[:, 2].set(s_lo).at[:, 3].set(s_lo)
    qaug = jnp.broadcast_to(rows[:, :, None], (N_HEADS, HEAD_DIM, LANES))
    n_rows = seq // CMP_STRIDE
    n_slc = seq // SLC_BLOCK
    start = np.arange(n_rows)[:, None] * CMP_STRIDE
    end = start + CMP_LEN - 1
    blk = np.arange(n_slc)[None, :] * SLC_BLOCK
    ovl = ((start < blk + SLC_BLOCK) & (end >= blk)).astype(np.float32)
    per_tile = TK_SEL // SLC_BLOCK
    grp = (np.arange(n_slc)[None, :] // per_tile == np.arange(n_slc // per_tile)[:, None])
    return qaug, jnp.asarray(ovl.T, BF16), jnp.asarray(grp.astype(np.float32), BF16)


def _retention_constants():
    h, c = RET_HEADS, RET_CHUNK
    lg = jnp.log1p(-jnp.exp2(-5.0 - jnp.arange(h, dtype=F32)))
    pos = jnp.arange(c, dtype=F32)
    diff = pos[:, None] - pos[None, :]
    scale = RET_DIM ** -0.5
    dm = jnp.where(diff >= 0, jnp.exp(lg[:, None, None] * jnp.maximum(diff, 0.0)), 0.0) * scale
    k_decay = jnp.exp(lg[:, None] * (c - 1.0 - pos)) * scale
    q_decay = jnp.exp(lg[:, None] * (pos + 1.0))
    chunk_decay = jnp.exp(lg * c)
    kd = jnp.broadcast_to(k_decay[:, :, None], (h, c, RET_DIM))
    qd = jnp.broadcast_to(q_decay[:, :, None], (h, c, RET_DIM))
    cd = jnp.broadcast_to(chunk_decay[:, None, None], (h, RET_DIM, RET_DIM))
    return dm, kd, qd, cd


def kernel(x, c, w_ada, b_ada, g_norm_mix, g_norm_ffn, g_norm_final, w_in, cmp_pe_k, cmp_pe_v,
           cmp_k_w1, cmp_k_w2, cmp_v_w1, cmp_v_w2, g_nsa_out, g_ret_out, w_out,
           peer_w_q, peer_sub_keys, peer_u, peer_v):
    batch, seq, d = x.shape
    depth = w_ada.shape[0]
    t_tokens = batch * seq
    xf = x.reshape(t_tokens, d)
    c_pad = jnp.zeros((8, d), F32).at[:batch].set(c)
    qaug, ovl_t, grp = _nsa_constants(seq)
    dm, kd, qd, cd = _retention_constants()

    for l in range(depth):
        mod = _adaln(c_pad, w_ada[l], b_ada[l][None, :])[:batch].reshape(batch, 6, 1, d)
        shift1, scale1, gate1, shift2, scale2, gate2 = (mod[:, k] for k in range(6))

        w_std, w_tr = _inproj_weights(w_in[l])
        (cv, ks, kw, q_r, k_r, v_r, g_r, qt, vst, vwt, gt) = _inproj(
            xf, scale1, shift1, g_norm_mix[l][None, :], w_std, w_tr, seq)

        cv4 = cv.reshape(KV_GROUPS, batch, seq // CMP_STRIDE, CMP_STRIDE * LANES)
        kcp, vct = _compress(cv4, *_compress_weights(cmp_pe_k[l], cmp_pe_v[l], cmp_k_w1[l], cmp_k_w2[l],
                                                     cmp_v_w1[l], cmp_v_w2[l]))
        gout_b = jnp.broadcast_to(g_nsa_out[l][:, :, None], (N_HEADS, HEAD_DIM, LANES))
        o_nsa = _nsa(qt, gt, kcp, vct, ks, kw, vst, vwt, qaug, gout_b, ovl_t, grp, batch, seq)

        go = jnp.broadcast_to(g_ret_out[l][:, None, :], (RET_HEADS, 8, RET_DIM))
        o_ret = _retention(q_r, k_r, v_r, g_r, dm, kd, qd, cd, go, batch, seq)

        sub_keys = peer_sub_keys[l].reshape(2 * PEER_HEADS, PEER_KEYS, -1).astype(BF16)
        s_u = _pow2_scale(jnp.max(jnp.abs(peer_u[l])))
        s_v = _pow2_scale(jnp.max(jnp.abs(peer_v[l])))
        u_norm = jnp.sqrt(jnp.max(jnp.sum(jnp.square(peer_u[l]), axis=1)))
        peer_scales = jnp.broadcast_to(
            jnp.stack([1.0 / s_u, 1.0 / s_v, u_norm] + [jnp.zeros((), F32)] * 5)[:, None], (8, LANES))
        x1, h2t, st, scl = _mid(o_nsa, o_ret, xf, gate1, scale2, shift2, g_norm_ffn[l][None, :],
                                w_out[l].astype(BF16), peer_w_q[l].T.astype(BF16), sub_keys, peer_scales, seq)

        lrow, w1, r2, w2 = _peer_select(st, scl)
        peer_t = _peer_expert(h2t, (peer_u[l] * s_u).astype(FP8),
                              _transpose_fp8(peer_v[l], jnp.full((8, LANES), s_v, F32)),
                              lrow, w1, r2, w2, scl)
        xf = _final(x1, peer_t, scl, gate2, g_norm_final[None, :], seq, apply_norm=(l == depth - 1))
    return xf.reshape(batch, seq, d)
```
